```python
import jax, jax.numpy as jnp
from jax import lax
import numpy as np

D_MODEL = 1024
BATCH = 8
SEQ = 4096
DEPTH = 2

N_META = 16
N_A_LAYERS = DEPTH // 2
N_B_LAYERS = DEPTH - N_A_LAYERS
POOL_WINDOWS = (2, 4, 8, 16)
N_POOL_GROUPS = len(POOL_WINDOWS)
POOL_GROUP_DIM = D_MODEL // N_POOL_GROUPS
N_HEADS = 16
HEAD_DIM = D_MODEL // N_HEADS
Q_BLOCK = 128
D_FF = ((8 * D_MODEL // 3 + 127) // 128) * 128
CONV_WIDTH = 3
RMS_EPS = 1e-6

kernel_name = "yoco_pool_stickbreak_convffn"


def rms_norm(x, g):
    xf = x.astype(jnp.float32)
    y = xf * lax.rsqrt(jnp.mean(xf * xf, axis=-1, keepdims=True) + RMS_EPS)
    return (y * g.astype(jnp.float32)).astype(x.dtype)


def multiscale_pool(h, w_groups, scale):
    b, l, d = h.shape
    hf = h.astype(jnp.float32)
    csum = jnp.concatenate([jnp.zeros((b, 1, d), jnp.float32), jnp.cumsum(hf, axis=1)], axis=1)
    hg = hf.reshape(b, l, N_POOL_GROUPS, POOL_GROUP_DIM)
    cg = csum.reshape(b, l + 1, N_POOL_GROUPS, POOL_GROUP_DIM)
    t = jnp.arange(l)
    diffs = []
    for g, w in enumerate(POOL_WINDOWS):
        lo = jnp.maximum(t + 1 - w, 0)
        count = (t + 1 - lo).astype(jnp.float32)
        cgg = cg[:, :, g]
        window_sum = cgg[:, 1:] - cgg[:, lo]
        diffs.append(window_sum / count[None, :, None] - hg[:, :, g])
    diff = jnp.stack(diffs, axis=2).astype(h.dtype)
    y = jnp.einsum('blgc,gcd->blgd', diff, w_groups).reshape(b, l, d)
    return y * scale


def causal_dwconv(u, w, bias):
    l = u.shape[1]
    up = jnp.pad(u, ((0, 0), (CONV_WIDTH - 1, 0), (0, 0)))
    out = bias + w[0] * up[:, 0:l]
    for k in range(1, CONV_WIDTH):
        out = out + w[k] * up[:, k:k + l]
    return out


def conv_ffn(h, w_up, conv_w, conv_b, w_down):
    u = causal_dwconv(h @ w_up, conv_w, conv_b)
    gate, val = jnp.split(u, 2, axis=-1)
    return (jax.nn.silu(gate) * val) @ w_down


def shared_kv(h, kv_norm, w_kv):
    b, l, _ = h.shape
    kv = rms_norm(h, kv_norm) @ w_kv
    k, v = jnp.split(kv, 2, axis=-1)
    k = k.reshape(b, l, N_HEADS, HEAD_DIM).transpose(0, 2, 1, 3)
    v = v.reshape(b, l, N_HEADS, HEAD_DIM).transpose(0, 2, 1, 3)
    return k, v


def stick_breaking_block(q_blk, pos_q, k, v):
    z = jnp.einsum('bhqd,bhsd->bhqs', q_blk, k).astype(jnp.float32) * (HEAD_DIM ** -0.5)
    pos_k = jnp.arange(k.shape[2])
    mask = pos_k[None, :] < pos_q[:, None]
    log_beta = jax.nn.log_sigmoid(z)
    log_1m_beta = jnp.where(mask, jax.nn.log_sigmoid(-z), 0.0)
    later = lax.cumsum(log_1m_beta, axis=3, reverse=True) - log_1m_beta
    a = jnp.where(mask, jnp.exp(log_beta + later), 0.0)
    return jnp.einsum('bhqs,bhsd->bhqd', a.astype(v.dtype), v)


def stick_breaking_attention(h, w_q, k, v, w_o):
    b, l, d = h.shape
    n_real = l - N_META
    n_blk = n_real // Q_BLOCK
    q = (h @ w_q).reshape(b, l, N_HEADS, HEAD_DIM).transpose(0, 2, 1, 3)
    o_meta = stick_breaking_block(q[:, :, :N_META], jnp.arange(N_META),
                                  k[:, :, :N_META], v[:, :, :N_META])
    q_real = q[:, :, N_META:].reshape(b, N_HEADS, n_blk, Q_BLOCK, HEAD_DIM).transpose(2, 0, 1, 3, 4)
    pos_real = (N_META + jnp.arange(n_real)).reshape(n_blk, Q_BLOCK)
    o_real = lax.map(lambda args: stick_breaking_block(args[0], args[1], k, v), (q_real, pos_real))
    o_real = o_real.transpose(1, 2, 0, 3, 4).reshape(b, N_HEADS, n_real, HEAD_DIM)
    o = jnp.concatenate([o_meta, o_real], axis=2).transpose(0, 2, 1, 3).reshape(b, l, d)
    return o @ w_o


def _fwd_setup_inputs(seed: int = 0) -> dict:
    key = jax.random.key(seed)
    ks = jax.random.split(key, 16)
    f32 = jnp.float32
    nrm = lambda k, shape, s: jax.random.normal(k, shape, f32) * s
    return {
        "x": nrm(ks[0], (BATCH, SEQ, D_MODEL), 1.0),
        "meta_tokens": nrm(ks[1], (N_META, D_MODEL), 1.0),
        "mix_norm": 1.0 + nrm(ks[2], (DEPTH, D_MODEL), 0.05),
        "ffn_norm": 1.0 + nrm(ks[3], (DEPTH, D_MODEL), 0.05),
        "pool_w": nrm(ks[4], (N_A_LAYERS, N_POOL_GROUPS, POOL_GROUP_DIM, POOL_GROUP_DIM), POOL_GROUP_DIM ** -0.5),
        "pool_scale": 1.0 + nrm(ks[5], (N_A_LAYERS, D_MODEL), 0.1),
        "kv_norm": 1.0 + nrm(ks[6], (D_MODEL,), 0.05),
        "w_kv": nrm(ks[7], (D_MODEL, 2 * D_MODEL), D_MODEL ** -0.5),
        "w_q": nrm(ks[8], (N_B_LAYERS, D_MODEL, D_MODEL), D_MODEL ** -0.5),
        "w_o": nrm(ks[9], (N_B_LAYERS, D_MODEL, D_MODEL), D_MODEL ** -0.5),
        "ffn_w_up": nrm(ks[10], (DEPTH, D_MODEL, 2 * D_FF), D_MODEL ** -0.5),
        "ffn_conv_w": nrm(ks[11], (DEPTH, CONV_WIDTH, 2 * D_FF), CONV_WIDTH ** -0.5),
        "ffn_conv_b": nrm(ks[12], (DEPTH, 2 * D_FF), 0.01),
        "ffn_w_down": nrm(ks[13], (DEPTH, D_FF, D_MODEL), D_FF ** -0.5),
        "final_norm": 1.0 + nrm(ks[14], (D_MODEL,), 0.05),
    }


def _fwd_reference(x, meta_tokens, mix_norm, ffn_norm, pool_w, pool_scale, kv_norm, w_kv,
              w_q, w_o, ffn_w_up, ffn_conv_w, ffn_conv_b, ffn_w_down, final_norm):
    b = x.shape[0]
    meta = jnp.broadcast_to(meta_tokens[None].astype(x.dtype), (b, N_META, D_MODEL))
    h = jnp.concatenate([meta, x], axis=1)
    k = v = None
    for layer in range(DEPTH):
        if layer < N_A_LAYERS:
            h = h + multiscale_pool(rms_norm(h, mix_norm[layer]), pool_w[layer], pool_scale[layer])
        else:
            if layer == N_A_LAYERS:
                k, v = shared_kv(h, kv_norm, w_kv)
            j = layer - N_A_LAYERS
            h = h + stick_breaking_attention(rms_norm(h, mix_norm[layer]), w_q[j], k, v, w_o[j])
        h = h + conv_ffn(rms_norm(h, ffn_norm[layer]), ffn_w_up[layer], ffn_conv_w[layer],
                         ffn_conv_b[layer], ffn_w_down[layer])
    return rms_norm(h, final_norm)[:, N_META:]


import jax as _jax
import jax.numpy as _jnp

TWIN_FORMAT = 'train_step'
FWD_PARAMS = ['x', 'meta_tokens', 'mix_norm', 'ffn_norm', 'pool_w', 'pool_scale', 'kv_norm', 'w_kv', 'w_q', 'w_o', 'ffn_w_up', 'ffn_conv_w', 'ffn_conv_b', 'ffn_w_down', 'final_norm']
TWIN_WEIGHTS = ['meta_tokens', 'mix_norm', 'ffn_norm', 'pool_w', 'pool_scale', 'kv_norm', 'w_kv', 'w_q', 'w_o', 'ffn_w_up', 'ffn_conv_w', 'ffn_conv_b', 'ffn_w_down', 'final_norm']
TWIN_DIFF_INPUT = 'x'
TWIN_INPUTS = ['x', 'meta_tokens', 'mix_norm', 'ffn_norm', 'pool_w', 'pool_scale', 'kv_norm', 'w_kv', 'w_q', 'w_o', 'ffn_w_up', 'ffn_conv_w', 'ffn_conv_b', 'ffn_w_down', 'final_norm', 'loss_target', 'm_meta_tokens', 'm_mix_norm', 'm_ffn_norm', 'm_pool_w', 'm_pool_scale', 'm_kv_norm', 'm_w_kv', 'm_w_q', 'm_w_o', 'm_ffn_w_up', 'm_ffn_conv_w', 'm_ffn_conv_b', 'm_ffn_w_down', 'm_final_norm', 'v_meta_tokens', 'v_mix_norm', 'v_ffn_norm', 'v_pool_w', 'v_pool_scale', 'v_kv_norm', 'v_w_kv', 'v_w_q', 'v_w_o', 'v_ffn_w_up', 'v_ffn_conv_w', 'v_ffn_conv_b', 'v_ffn_w_down', 'v_final_norm']
TWIN_OUTPUTS = ['loss', 'grad_x', 'grad_meta_tokens', 'grad_mix_norm', 'grad_ffn_norm', 'grad_pool_w', 'grad_pool_scale', 'grad_kv_norm', 'grad_w_kv', 'grad_w_q', 'grad_w_o', 'grad_ffn_w_up', 'grad_ffn_conv_w', 'grad_ffn_conv_b', 'grad_ffn_w_down', 'grad_final_norm', 'delta_meta_tokens', 'delta_mix_norm', 'delta_ffn_norm', 'delta_pool_w', 'delta_pool_scale', 'delta_kv_norm', 'delta_w_kv', 'delta_w_q', 'delta_w_o', 'delta_ffn_w_up', 'delta_ffn_conv_w', 'delta_ffn_conv_b', 'delta_ffn_w_down', 'delta_final_norm', 'new_m_meta_tokens', 'new_m_mix_norm', 'new_m_ffn_norm', 'new_m_pool_w', 'new_m_pool_scale', 'new_m_kv_norm', 'new_m_w_kv', 'new_m_w_q', 'new_m_w_o', 'new_m_ffn_w_up', 'new_m_ffn_conv_w', 'new_m_ffn_conv_b', 'new_m_ffn_w_down', 'new_m_final_norm', 'new_v_meta_tokens', 'new_v_mix_norm', 'new_v_ffn_norm', 'new_v_pool_w', 'new_v_pool_scale', 'new_v_kv_norm', 'new_v_w_kv', 'new_v_w_q', 'new_v_w_o', 'new_v_ffn_w_up', 'new_v_ffn_conv_w', 'new_v_ffn_conv_b', 'new_v_ffn_w_down', 'new_v_final_norm']
TWIN_LEAF_KINDS = {'loss': 'loss', 'grad_x': 'grad_x', 'grad_meta_tokens': 'grad_w', 'grad_mix_norm': 'grad_w', 'grad_ffn_norm': 'grad_w', 'grad_pool_w': 'grad_w', 'grad_pool_scale': 'grad_w', 'grad_kv_norm': 'grad_w', 'grad_w_kv': 'grad_w', 'grad_w_q': 'grad_w', 'grad_w_o': 'grad_w', 'grad_ffn_w_up': 'grad_w', 'grad_ffn_conv_w': 'grad_w', 'grad_ffn_conv_b': 'grad_w', 'grad_ffn_w_down': 'grad_w', 'grad_final_norm': 'grad_w', 'delta_meta_tokens': 'delta_w', 'delta_mix_norm': 'delta_w', 'delta_ffn_norm': 'delta_w', 'delta_pool_w': 'delta_w', 'delta_pool_scale': 'delta_w', 'delta_kv_norm': 'delta_w', 'delta_w_kv': 'delta_w', 'delta_w_q': 'delta_w', 'delta_w_o': 'delta_w', 'delta_ffn_w_up': 'delta_w', 'delta_ffn_conv_w': 'delta_w', 'delta_ffn_conv_b': 'delta_w', 'delta_ffn_w_down': 'delta_w', 'delta_final_norm': 'delta_w', 'new_m_meta_tokens': 'new_m', 'new_m_mix_norm': 'new_m', 'new_m_ffn_norm': 'new_m', 'new_m_pool_w': 'new_m', 'new_m_pool_scale': 'new_m', 'new_m_kv_norm': 'new_m', 'new_m_w_kv': 'new_m', 'new_m_w_q': 'new_m', 'new_m_w_o': 'new_m', 'new_m_ffn_w_up': 'new_m', 'new_m_ffn_conv_w': 'new_m', 'new_m_ffn_conv_b': 'new_m', 'new_m_ffn_w_down': 'new_m', 'new_m_final_norm': 'new_m', 'new_v_meta_tokens': 'new_v', 'new_v_mix_norm': 'new_v', 'new_v_ffn_norm': 'new_v', 'new_v_pool_w': 'new_v', 'new_v_pool_scale': 'new_v', 'new_v_kv_norm': 'new_v', 'new_v_w_kv': 'new_v', 'new_v_w_q': 'new_v', 'new_v_w_o': 'new_v', 'new_v_ffn_w_up': 'new_v', 'new_v_ffn_conv_w': 'new_v', 'new_v_ffn_conv_b': 'new_v', 'new_v_ffn_w_down': 'new_v', 'new_v_final_norm': 'new_v'}


def _forward(args):
    return _fwd_reference(*[args[k] for k in FWD_PARAMS])


def _output_shape():
    def fwd():
        inp = _fwd_setup_inputs(0)
        return _fwd_reference(*[inp[k] for k in FWD_PARAMS])
    out = _jax.eval_shape(fwd)
    return out.shape, out.dtype

N_MICROBATCH = 1
ADAM_LR = 0.001
ADAM_B1 = 0.9
ADAM_B2 = 0.999
ADAM_EPS = 1e-08
ADAM_WD = 0.01
ADAM_STEP = 10
PER_EXAMPLE_BATCH_AXIS = {'x': 0, 'loss_target': 0}
SHARED_INPUTS = []
_WEIGHT_DTYPES = {'meta_tokens': _jnp.float32, 'mix_norm': _jnp.float32, 'ffn_norm': _jnp.float32, 'pool_w': _jnp.float32, 'pool_scale': _jnp.float32, 'kv_norm': _jnp.float32, 'w_kv': _jnp.float32, 'w_q': _jnp.float32, 'w_o': _jnp.float32, 'ffn_w_up': _jnp.float32, 'ffn_conv_w': _jnp.float32, 'ffn_conv_b': _jnp.float32, 'ffn_w_down': _jnp.float32, 'final_norm': _jnp.float32}
MOMENT_SCALE = {'meta_tokens': 3.502251e-03, 'mix_norm': 1.234098e-01, 'ffn_norm': 1.081539e-01, 'pool_w': 1.512458e-01, 'pool_scale': 8.072223e-01, 'kv_norm': 8.262666e-02, 'w_kv': 5.926991e-02, 'w_q': 3.636003e-02, 'w_o': 7.582509e-02, 'ffn_w_up': 4.582212e-02, 'ffn_conv_w': 4.620482e-02, 'ffn_conv_b': 4.558464e-02, 'ffn_w_down': 7.519181e-02, 'final_norm': 3.219251e+01}


def _to_microbatches(a, axis):
    t = _jnp.moveaxis(a, axis, 0)
    t = t.reshape((N_MICROBATCH, t.shape[0] // N_MICROBATCH) + t.shape[1:])
    return _jnp.moveaxis(t, 1, axis + 1)


def setup_inputs(seed: int = 0) -> dict:
    inp = _fwd_setup_inputs(seed)
    key = _jax.random.fold_in(_jax.random.key(seed), 7919)
    shape, _ = _output_shape()
    out = dict(inp)
    out["loss_target"] = _jax.random.normal(_jax.random.fold_in(key, 0), shape, _jnp.float32)
    for i, name in enumerate(TWIN_WEIGHTS):
        w = inp[name].astype(_jnp.float32)
        if MOMENT_SCALE is None:
            s = _jnp.sqrt(_jnp.mean(_jnp.square(w)) + 1e-30)
        else:
            s = MOMENT_SCALE[name]
        km, kv = _jax.random.split(_jax.random.fold_in(key, i + 1))
        out[name] = w
        out["m_" + name] = s * _jax.random.normal(km, w.shape, _jnp.float32)
        out["v_" + name] = (s * s) * _jax.random.uniform(kv, w.shape, _jnp.float32, 0.5, 1.5)
    if N_MICROBATCH > 1:
        for name, axis in PER_EXAMPLE_BATCH_AXIS.items():
            out[name] = _to_microbatches(out[name], axis)
    return {'x': out['x'], 'meta_tokens': out['meta_tokens'], 'mix_norm': out['mix_norm'], 'ffn_norm': out['ffn_norm'], 'pool_w': out['pool_w'], 'pool_scale': out['pool_scale'], 'kv_norm': out['kv_norm'], 'w_kv': out['w_kv'], 'w_q': out['w_q'], 'w_o': out['w_o'], 'ffn_w_up': out['ffn_w_up'], 'ffn_conv_w': out['ffn_conv_w'], 'ffn_conv_b': out['ffn_conv_b'], 'ffn_w_down': out['ffn_w_down'], 'final_norm': out['final_norm'], 'loss_target': out['loss_target'], 'm_meta_tokens': out['m_meta_tokens'], 'm_mix_norm': out['m_mix_norm'], 'm_ffn_norm': out['m_ffn_norm'], 'm_pool_w': out['m_pool_w'], 'm_pool_scale': out['m_pool_scale'], 'm_kv_norm': out['m_kv_norm'], 'm_w_kv': out['m_w_kv'], 'm_w_q': out['m_w_q'], 'm_w_o': out['m_w_o'], 'm_ffn_w_up': out['m_ffn_w_up'], 'm_ffn_conv_w': out['m_ffn_conv_w'], 'm_ffn_conv_b': out['m_ffn_conv_b'], 'm_ffn_w_down': out['m_ffn_w_down'], 'm_final_norm': out['m_final_norm'], 'v_meta_tokens': out['v_meta_tokens'], 'v_mix_norm': out['v_mix_norm'], 'v_ffn_norm': out['v_ffn_norm'], 'v_pool_w': out['v_pool_w'], 'v_pool_scale': out['v_pool_scale'], 'v_kv_norm': out['v_kv_norm'], 'v_w_kv': out['v_w_kv'], 'v_w_q': out['v_w_q'], 'v_w_o': out['v_w_o'], 'v_ffn_w_up': out['v_ffn_w_up'], 'v_ffn_conv_w': out['v_ffn_conv_w'], 'v_ffn_conv_b': out['v_ffn_conv_b'], 'v_ffn_w_down': out['v_ffn_w_down'], 'v_final_norm': out['v_final_norm']}


def _loss(weights, diff, rest, loss_target):
    with _jax.named_scope("forward"):
        args = {**rest, TWIN_DIFF_INPUT: diff, **{k: w.astype(_WEIGHT_DTYPES[k]) for k, w in weights.items()}}
        y = _forward(args)
    with _jax.named_scope("loss_head"):
        err = _jnp.square(y.astype(_jnp.float32) - loss_target)
        return 0.5 * _jnp.sum(_jnp.mean(err, axis=-1)) if err.ndim else 0.5 * err


def _adamw(w, g, m, v):
    m = ADAM_B1 * m + (1.0 - ADAM_B1) * g
    v = ADAM_B2 * v + (1.0 - ADAM_B2) * _jnp.square(g)
    m_hat = m / (1.0 - ADAM_B1 ** ADAM_STEP)
    v_hat = v / (1.0 - ADAM_B2 ** ADAM_STEP)
    delta = -ADAM_LR * (m_hat / (_jnp.sqrt(v_hat) + ADAM_EPS) + ADAM_WD * w)
    return delta, m, v


def reference(x, meta_tokens, mix_norm, ffn_norm, pool_w, pool_scale, kv_norm, w_kv, w_q, w_o, ffn_w_up, ffn_conv_w, ffn_conv_b, ffn_w_down, final_norm, loss_target, m_meta_tokens, m_mix_norm, m_ffn_norm, m_pool_w, m_pool_scale, m_kv_norm, m_w_kv, m_w_q, m_w_o, m_ffn_w_up, m_ffn_conv_w, m_ffn_conv_b, m_ffn_w_down, m_final_norm, v_meta_tokens, v_mix_norm, v_ffn_norm, v_pool_w, v_pool_scale, v_kv_norm, v_w_kv, v_w_q, v_w_o, v_ffn_w_up, v_ffn_conv_w, v_ffn_conv_b, v_ffn_w_down, v_final_norm):
    given = dict(x=x, meta_tokens=meta_tokens, mix_norm=mix_norm, ffn_norm=ffn_norm, pool_w=pool_w, pool_scale=pool_scale, kv_norm=kv_norm, w_kv=w_kv, w_q=w_q, w_o=w_o, ffn_w_up=ffn_w_up, ffn_conv_w=ffn_conv_w, ffn_conv_b=ffn_conv_b, ffn_w_down=ffn_w_down, final_norm=final_norm, loss_target=loss_target, m_meta_tokens=m_meta_tokens, m_mix_norm=m_mix_norm, m_ffn_norm=m_ffn_norm, m_pool_w=m_pool_w, m_pool_scale=m_pool_scale, m_kv_norm=m_kv_norm, m_w_kv=m_w_kv, m_w_q=m_w_q, m_w_o=m_w_o, m_ffn_w_up=m_ffn_w_up, m_ffn_conv_w=m_ffn_conv_w, m_ffn_conv_b=m_ffn_conv_b, m_ffn_w_down=m_ffn_w_down, m_final_norm=m_final_norm, v_meta_tokens=v_meta_tokens, v_mix_norm=v_mix_norm, v_ffn_norm=v_ffn_norm, v_pool_w=v_pool_w, v_pool_scale=v_pool_scale, v_kv_norm=v_kv_norm, v_w_kv=v_w_kv, v_w_q=v_w_q, v_w_o=v_w_o, v_ffn_w_up=v_ffn_w_up, v_ffn_conv_w=v_ffn_conv_w, v_ffn_conv_b=v_ffn_conv_b, v_ffn_w_down=v_ffn_w_down, v_final_norm=v_final_norm)
    weights = {n: given[n] for n in TWIN_WEIGHTS}
    shared = {n: given[n] for n in SHARED_INPUTS}
    per_example = {n: given[n] for n in ['x']}
    grad_fn = _jax.value_and_grad(_loss, argnums=(0, 1))

    def one_microbatch(ex, loss_target):
        ex = dict(ex)
        diff = ex.pop(TWIN_DIFF_INPUT)
        return grad_fn(weights, diff, {**shared, **ex}, loss_target)

    if N_MICROBATCH == 1:
        loss, (grad_w, grad_x) = one_microbatch(per_example, given["loss_target"])
    else:
        def body(carry, xs):
            loss_sum, grad_sum = carry
            l_k, (gw_k, gx_k) = one_microbatch(xs[0], xs[1])
            with _jax.named_scope("update"):
                return (loss_sum + l_k, _jax.tree.map(_jnp.add, grad_sum, gw_k)), gx_k

        init = (_jnp.zeros((), _jnp.float32), _jax.tree.map(_jnp.zeros_like, weights))
        (loss, grad_w), grad_x = _jax.lax.scan(body, init, (per_example, given["loss_target"]))
    with _jax.named_scope("update"):
        delta_w, new_m, new_v = {}, {}, {}
        for n in TWIN_WEIGHTS:
            delta_w[n], new_m[n], new_v[n] = _adamw(weights[n], grad_w[n], given["m_" + n], given["v_" + n])
    return (loss, grad_x, *[grad_w[n] for n in TWIN_WEIGHTS], *[delta_w[n] for n in TWIN_WEIGHTS],
            *[new_m[n] for n in TWIN_WEIGHTS], *[new_v[n] for n in TWIN_WEIGHTS])
```

```python
import functools

import jax
import jax.numpy as jnp
from jax import lax
from jax.experimental import pallas as pl
from jax.experimental.pallas import tpu as pltpu

F32 = jnp.float32
BF16 = jnp.bfloat16
SDS = jax.ShapeDtypeStruct

N_DEV = 8
N_META = 16
HEAD_DIM = 64
HEAD_PAIRS = 8
RMS_EPS = 1e-6
POOL_WINDOWS = (2, 4, 8, 16)
POOL_C = 256
POOL_HALO = 16
CONV_HALO = 8
ROW_TILE = 384
ATT_BLK = 128
VMEM_LIMIT = 56 * 1024 * 1024

ADAM_LR = 0.001
ADAM_B1 = 0.9
ADAM_B2 = 0.999
ADAM_EPS = 1e-08
ADAM_WD = 0.01
ADAM_STEP = 10

MESH_AXES = ("x", "y", "c")
NN = (((1,), (0,)), ((), ()))
NT = (((1,), (1,)), ((), ()))
TN = (((0,), (0,)), ((), ()))


def _cp(n_axes):
    return pltpu.CompilerParams(dimension_semantics=("arbitrary",) * n_axes, vmem_limit_bytes=VMEM_LIMIT)


def _dot(a, b, dims=NN):
    return lax.dot_general(a, b, dims, preferred_element_type=F32)


def _rstd(x):
    return lax.rsqrt(jnp.mean(x * x, axis=-1, keepdims=True) + RMS_EPS)


def _row_tile(rows, cap=512):
    if rows <= cap:
        return rows
    best = 8
    for t in range(8, cap + 1, 8):
        if rows % t == 0:
            best = t
    assert rows % best == 0
    return best


def _rms_fwd(h, gains, name):
    lp, d = h.shape
    k = gains.shape[0]
    tm = ROW_TILE

    def body(h_ref, g_ref, *o_refs):
        x = h_ref[...]
        u = x * _rstd(x)
        for j in range(k):
            o_refs[j][...] = (u * g_ref[j:j + 1, :]).astype(BF16)

    row = pl.BlockSpec((tm, d), lambda i: (i, 0))
    return pl.pallas_call(
        body, name=name, grid=(lp // tm,),
        in_specs=[row, pl.BlockSpec((k, d), lambda i: (0, 0))],
        out_specs=[row] * k, out_shape=[SDS((lp, d), BF16)] * k,
        compiler_params=_cp(1))(h, gains)


def _rms_bwd(h, gains, dns, dh_in, name):
    lp, d = h.shape
    k = gains.shape[0]
    tm = ROW_TILE

    def body(h_ref, g_ref, *refs):
        dn_refs, dh_ref, dho_ref, dg_ref = refs[:k], refs[k], refs[k + 1], refs[k + 2]
        i = pl.program_id(0)
        x = h_ref[...]
        r = _rstd(x)
        u = x * r
        du = jnp.zeros_like(x)
        rows = []
        for j in range(k):
            dn = dn_refs[j][...]
            du = du + dn * g_ref[j:j + 1, :]
            rows.append(jnp.sum(dn * u, axis=0, keepdims=True))
        dx = r * (du - u * jnp.mean(du * u, axis=-1, keepdims=True))
        dho_ref[...] = dh_ref[...] + dx

        @pl.when(i == 0)
        def _():
            for j in range(k):
                dg_ref[j:j + 1, :] = rows[j]

        @pl.when(i > 0)
        def _():
            for j in range(k):
                dg_ref[j:j + 1, :] += rows[j]

    row = pl.BlockSpec((tm, d), lambda i: (i, 0))
    vec = pl.BlockSpec((k, d), lambda i: (0, 0))
    return pl.pallas_call(
        body, name=name, grid=(lp // tm,),
        in_specs=[row, vec] + [row] * k + [row],
        out_specs=[row, vec], out_shape=[SDS((lp, d), F32), SDS((k, d), F32)],
        compiler_params=_cp(1))(h, gains, *dns, dh_in)


def _loss_bwd(h, gain, target, n_real, name):
    lp, d = h.shape
    tm = ROW_TILE

    def body(h_ref, g_ref, t_ref, dh_ref, loss_ref, dg_ref):
        i = pl.program_id(0)
        x = h_ref[...]
        g = g_ref[...]
        r = _rstd(x)
        u = x * r
        row = i * tm + lax.broadcasted_iota(jnp.int32, (tm, 1), 0)
        valid = (row >= N_META) & (row < N_META + n_real)
        e = jnp.where(valid, u * g - t_ref[...], 0.0)
        part = 0.5 * jnp.sum(jnp.sum(e * e, axis=-1, keepdims=True), axis=0, keepdims=True) * (1.0 / d)
        dy = e * (1.0 / d)
        du = dy * g
        dh_ref[...] = r * (du - u * jnp.mean(du * u, axis=-1, keepdims=True))
        dgp = jnp.sum(dy * u, axis=0, keepdims=True)

        @pl.when(i == 0)
        def _():
            loss_ref[...] = jnp.broadcast_to(part, (8, 128))
            dg_ref[...] = dgp

        @pl.when(i > 0)
        def _():
            loss_ref[...] += jnp.broadcast_to(part, (8, 128))
            dg_ref[...] += dgp

    row = pl.BlockSpec((tm, d), lambda i: (i, 0))
    vec = pl.BlockSpec((1, d), lambda i: (0, 0))
    return pl.pallas_call(
        body, name=name, grid=(lp // tm,),
        in_specs=[row, vec, row],
        out_specs=[row, pl.BlockSpec((8, 128), lambda i: (0, 0)), vec],
        out_shape=[SDS((lp, d), F32), SDS((8, 128), F32), SDS((1, d), F32)],
        compiler_params=_cp(1))(h, gain, target)


def _pool_fwd(h, gain, w, scale, name):
    lp, d = h.shape
    tm = ROW_TILE
    hb = POOL_HALO

    def body(h_ref, halo_ref, g_ref, w_ref, s_ref, h1_ref, diff_ref):
        i = pl.program_id(0)
        g = g_ref[...]
        x = h_ref[...]
        n = x * _rstd(x) * g
        xh = halo_ref[...]
        nh = jnp.where(i > 0, xh * _rstd(xh) * g, 0.0)
        cur = jnp.concatenate([nh, n], axis=0)
        pos = i * tm + lax.broadcasted_iota(jnp.int32, (tm, 1), 0)
        for gi, win in enumerate(POOL_WINDOWS):
            if gi > 0:
                cur = cur[:, POOL_C:]
            cur = cur + pltpu.roll(cur, win // 2, 0)
            c0 = gi * POOL_C
            count = jnp.minimum(pos + 1, win).astype(F32)
            diff = cur[hb:, :POOL_C] / count - n[:, c0:c0 + POOL_C]
            diff = diff.astype(BF16)
            y = _dot(diff, w_ref[gi])
            h1_ref[:, c0:c0 + POOL_C] = x[:, c0:c0 + POOL_C] + y * s_ref[:, c0:c0 + POOL_C]
            diff_ref[:, c0:c0 + POOL_C] = diff

    row = pl.BlockSpec((tm, d), lambda i: (i, 0))
    halo = pl.BlockSpec((hb, d), lambda i: (jnp.maximum(i * (tm // hb) - 1, 0), 0))
    vec = pl.BlockSpec((1, d), lambda i: (0, 0))
    return pl.pallas_call(
        body, name=name, grid=(lp // tm,),
        in_specs=[row, halo, vec, pl.BlockSpec(w.shape, lambda i: (0, 0, 0)), vec],
        out_specs=[row, row], out_shape=[SDS((lp, d), F32), SDS((lp, d), BF16)],
        compiler_params=_cp(1))(h, h, gain, w, scale)


def _pool_bwd(h, gain, w, scale, diff, dh1, name):
    lp, d = h.shape
    tm = ROW_TILE
    hb = POOL_HALO
    nblk = lp // tm
    ext = tm + hb

    def body(h_ref, g_ref, w_ref, s_ref, diff_ref, dh_ref, dhn_ref, dh0_ref, dw_ref, ds_ref, dg_ref):
        i = pl.program_id(0)
        g = g_ref[...]
        x = h_ref[...]
        r = _rstd(x)
        u = x * r
        dh = dh_ref[...]
        dhn = jnp.where(i < nblk - 1, dhn_ref[...], 0.0)
        dyp = jnp.concatenate([dh, dhn], axis=0) * s_ref[...]
        pos = i * tm + lax.broadcasted_iota(jnp.int32, (ext, 1), 0)
        dn_parts, dw_parts, ds_parts = [], [], []
        for gi, win in enumerate(POOL_WINDOWS):
            c0 = gi * POOL_C
            wg = w_ref[gi]
            dyp_g = dyp[:, c0:c0 + POOL_C].astype(BF16)
            dd = _dot(dyp_g, wg, NT)
            dfg = diff_ref[:, c0:c0 + POOL_C]
            dw_parts.append(_dot(dfg, dyp_g[:tm], TN))
            ds_parts.append(jnp.sum(dh[:, c0:c0 + POOL_C] * _dot(dfg, wg), axis=0, keepdims=True))
            count = jnp.minimum(pos + 1, win).astype(F32)
            cur = dd / count
            sh = 1
            while sh < win:
                cur = cur + pltpu.roll(cur, ext - sh, 0)
                sh *= 2
            dn_parts.append(cur[:tm] - dd[:tm])
        dn = jnp.concatenate(dn_parts, axis=1)
        du = dn * g
        dh0_ref[...] = dh + r * (du - u * jnp.mean(du * u, axis=-1, keepdims=True))
        dgp = jnp.sum(dn * u, axis=0, keepdims=True)
        dsp = jnp.concatenate(ds_parts, axis=1)

        @pl.when(i == 0)
        def _():
            for gi in range(len(POOL_WINDOWS)):
                dw_ref[gi] = dw_parts[gi]
            ds_ref[...] = dsp
            dg_ref[...] = dgp

        @pl.when(i > 0)
        def _():
            for gi in range(len(POOL_WINDOWS)):
                dw_ref[gi] += dw_parts[gi]
            ds_ref[...] += dsp
            dg_ref[...] += dgp

    row = pl.BlockSpec((tm, d), lambda i: (i, 0))
    nxt = pl.BlockSpec((hb, d), lambda i: (jnp.minimum((i + 1) * (tm // hb), lp // hb - 1), 0))
    vec = pl.BlockSpec((1, d), lambda i: (0, 0))
    wsp = pl.BlockSpec(w.shape, lambda i: (0, 0, 0))
    return pl.pallas_call(
        body, name=name, grid=(nblk,),
        in_specs=[row, vec, wsp, vec, row, row, nxt],
        out_specs=[row, wsp, vec, vec],
        out_shape=[SDS((lp, d), F32), SDS(w.shape, F32), SDS((1, d), F32), SDS((1, d), F32)],
        compiler_params=_cp(1))(h, gain, w, scale, diff, dh1, dh1)


def _conv_taps(x, halo, first):
    ext = jnp.concatenate([jnp.where(first, 0.0, halo), x], axis=0)
    return pltpu.roll(ext, 1, 0)[CONV_HALO:], pltpu.roll(ext, 2, 0)[CONV_HALO:]


def _ffn_specs(tm, c, lp):
    blk = pl.BlockSpec((2, 1, tm, c), lambda g, i: (0, g, i, 0))
    halo = pl.BlockSpec((2, 1, CONV_HALO, c), lambda g, i: (0, g, jnp.maximum(i * (tm // CONV_HALO) - 1, 0), 0))
    cw = pl.BlockSpec((2, 1, 3, c), lambda g, i: (0, g, 0, 0))
    cb = pl.BlockSpec((2, 1, 1, c), lambda g, i: (0, g, 0, 0))
    return blk, halo, cw, cb


def _ffn_act_fwd(up4, cw4, cb4, name):
    _, ng, lp, c = up4.shape
    tm = ROW_TILE

    def body(up_ref, halo_ref, cw_ref, cb_ref, act_ref):
        first = pl.program_id(1) == 0
        u = []
        for half in range(2):
            x = up_ref[half, 0]
            xm1, xm2 = _conv_taps(x, halo_ref[half, 0], first)
            u.append(cb_ref[half, 0] + cw_ref[half, 0, 0:1, :] * xm2 + cw_ref[half, 0, 1:2, :] * xm1
                     + cw_ref[half, 0, 2:3, :] * x)
        gate, val = u
        sig = 1.0 / (1.0 + jnp.exp(-gate))
        act_ref[0] = (gate * sig * val).astype(BF16)

    blk, halo, cw, cb = _ffn_specs(tm, c, lp)
    return pl.pallas_call(
        body, name=name, grid=(ng, lp // tm),
        in_specs=[blk, halo, cw, cb],
        out_specs=pl.BlockSpec((1, tm, c), lambda g, i: (g, i, 0)),
        out_shape=SDS((ng, lp, c), BF16), compiler_params=_cp(2))(up4, up4, cw4, cb4)


def _ffn_act_bwd(up4, cw4, cb4, dact, name):
    _, ng, lp, c = up4.shape
    tm = ROW_TILE

    def body(up_ref, halo_ref, cw_ref, cb_ref, da_ref, du_ref, dcw_ref, dcb_ref):
        i = pl.program_id(1)
        first = i == 0
        u, taps = [], []
        for half in range(2):
            x = up_ref[half, 0]
            xm1, xm2 = _conv_taps(x, halo_ref[half, 0], first)
            u.append(cb_ref[half, 0] + cw_ref[half, 0, 0:1, :] * xm2 + cw_ref[half, 0, 1:2, :] * xm1
                     + cw_ref[half, 0, 2:3, :] * x)
            taps.append((xm2, xm1, x))
        gate, val = u
        sig = 1.0 / (1.0 + jnp.exp(-gate))
        da = da_ref[0]
        dus = (da * val * (sig * (1.0 + gate * (1.0 - sig))), da * (gate * sig))
        sums = []
        for half in range(2):
            du_ref[half, 0] = dus[half]
            sums.append([jnp.sum(dus[half] * t, axis=0, keepdims=True) for t in taps[half]]
                        + [jnp.sum(dus[half], axis=0, keepdims=True)])

        @pl.when(first)
        def _():
            for half in range(2):
                for k in range(3):
                    dcw_ref[half, 0, k:k + 1, :] = sums[half][k]
                dcb_ref[half, 0] = sums[half][3]

        @pl.when(i > 0)
        def _():
            for half in range(2):
                for k in range(3):
                    dcw_ref[half, 0, k:k + 1, :] += sums[half][k]
                dcb_ref[half, 0] += sums[half][3]

    blk, halo, cw, cb = _ffn_specs(tm, c, lp)
    return pl.pallas_call(
        body, name=name, grid=(ng, lp // tm),
        in_specs=[blk, halo, cw, cb, pl.BlockSpec((1, tm, c), lambda g, i: (g, i, 0))],
        out_specs=[blk, cw, cb],
        out_shape=[SDS(up4.shape, F32), SDS(cw4.shape, F32), SDS(cb4.shape, F32)],
        compiler_params=_cp(2))(up4, up4, cw4, cb4, dact)


def _conv_bwd(du, cw, name):
    ng, lp, c = du.shape
    tm = ROW_TILE
    nblk = lp // tm
    ext = tm + CONV_HALO

    def body(du_ref, nxt_ref, cw_ref, out_ref):
        i = pl.program_id(1)
        x = du_ref[0]
        full = jnp.concatenate([x, jnp.where(i < nblk - 1, nxt_ref[0], 0.0)], axis=0)
        xp1 = pltpu.roll(full, ext - 1, 0)[:tm]
        xp2 = pltpu.roll(full, ext - 2, 0)[:tm]
        out_ref[0] = (cw_ref[0, 2:3, :] * x + cw_ref[0, 1:2, :] * xp1 + cw_ref[0, 0:1, :] * xp2).astype(BF16)

    blk = pl.BlockSpec((1, tm, c), lambda g, i: (g, i, 0))
    nxt = pl.BlockSpec((1, CONV_HALO, c),
                       lambda g, i: (g, jnp.minimum((i + 1) * (tm // CONV_HALO), lp // CONV_HALO - 1), 0))
    return pl.pallas_call(
        body, name=name, grid=(ng, nblk),
        in_specs=[blk, nxt, pl.BlockSpec((1, 3, c), lambda g, i: (g, 0, 0))],
        out_specs=blk, out_shape=SDS((ng, lp, c), BF16), compiler_params=_cp(2))(du, du, cw)


def _mm_group(a, b, dims, out_dtype, name):
    m, k = a.shape
    ng = b.shape[0]
    n = b.shape[2] if dims == NN else b.shape[1]
    tm = ROW_TILE

    def body(a_ref, b_ref, o_ref):
        o_ref[0] = _dot(a_ref[...].astype(BF16), b_ref[0], dims).astype(out_dtype)

    return pl.pallas_call(
        body, name=name, grid=(ng, m // tm),
        in_specs=[pl.BlockSpec((tm, k), lambda g, i: (i, 0)),
                  pl.BlockSpec((1,) + b.shape[1:], lambda g, i: (g, 0, 0))],
        out_specs=pl.BlockSpec((1, tm, n), lambda g, i: (g, i, 0)),
        out_shape=SDS((ng, m, n), out_dtype), compiler_params=_cp(2))(a, b)


def _mm_reduce(a, b, dims, res, name):
    ng, m, k = a.shape
    n = b.shape[2] if dims == NN else b.shape[1]
    tm = ROW_TILE
    has_res = res is not None

    def body(a_ref, b_ref, *refs):
        o_ref, acc_ref = refs[-2], refs[-1]
        g = pl.program_id(1)
        p = _dot(a_ref[0].astype(BF16), b_ref[0], dims)

        @pl.when(g == 0)
        def _():
            acc_ref[...] = p + refs[0][...] if has_res else p

        @pl.when(g > 0)
        def _():
            acc_ref[...] += p

        @pl.when(g == ng - 1)
        def _():
            o_ref[...] = acc_ref[...]

    row = pl.BlockSpec((tm, n), lambda i, g: (i, 0))
    return pl.pallas_call(
        body, name=name, grid=(m // tm, ng),
        in_specs=[pl.BlockSpec((1, tm, k), lambda i, g: (g, i, 0)),
                  pl.BlockSpec((1,) + b.shape[1:], lambda i, g: (g, 0, 0))] + ([row] if has_res else []),
        out_specs=row, out_shape=SDS((m, n), F32),
        scratch_shapes=[pltpu.VMEM((tm, n), F32)], compiler_params=_cp(2))(a, b, *([res] if has_res else []))


def _mm_tn(a, b, name):
    ga, m, ka = a.shape
    gb, _, n = b.shape
    ng = max(ga, gb)
    tk = ROW_TILE
    nk = m // tk

    def body(a_ref, b_ref, o_ref, acc_ref):
        s = pl.program_id(1)
        p = _dot(a_ref[0].astype(BF16), b_ref[0].astype(BF16), TN)

        @pl.when(s == 0)
        def _():
            acc_ref[...] = p

        @pl.when(s > 0)
        def _():
            acc_ref[...] += p

        @pl.when(s == nk - 1)
        def _():
            o_ref[0] = acc_ref[...].astype(BF16)

    return pl.pallas_call(
        body, name=name, grid=(ng, nk),
        in_specs=[pl.BlockSpec((1, tk, ka), (lambda g, s: (g, s, 0)) if ga > 1 else (lambda g, s: (0, s, 0))),
                  pl.BlockSpec((1, tk, n), (lambda g, s: (g, s, 0)) if gb > 1 else (lambda g, s: (0, s, 0)))],
        out_specs=pl.BlockSpec((1, ka, n), lambda g, s: (g, 0, 0)),
        out_shape=SDS((ng, ka, n), BF16),
        scratch_shapes=[pltpu.VMEM((ka, n), F32)], compiler_params=_cp(2))(a, b)


def _pair_tri(kind):
    r = jnp.arange(2 * ATT_BLK)[:, None]
    c = jnp.arange(2 * ATT_BLK)[None, :]
    same = (r < ATT_BLK) == (c < ATT_BLK)
    rel = {"after": r > c, "upto": r <= c, "before": r < c}[kind]
    return (same & rel).astype(BF16)


def _scan_dot(x, tri):
    hi = x.astype(BF16)
    lo = (x - hi.astype(F32)).astype(BF16)
    return _dot(hi, tri) + _dot(lo, tri)


def _split_heads(blk, lane_a):
    zero = jnp.zeros_like(blk)
    return jnp.concatenate([jnp.where(lane_a, blk, zero), jnp.where(lane_a, zero, blk)], axis=0)


def _log_sigmoids(z):
    lb = jnp.minimum(z, 0.0) - jnp.log1p(jnp.exp(-jnp.abs(z)))
    return lb, lb - z


def _diag_valid():
    t = lax.broadcasted_iota(jnp.int32, (ATT_BLK, 2 * ATT_BLK), 0)
    s = lax.broadcasted_iota(jnp.int32, (ATT_BLK, 2 * ATT_BLK), 1)
    return jnp.where(s >= ATT_BLK, s - ATT_BLK, s) < t


def _halves(x):
    return x[:, :ATT_BLK], x[:, ATT_BLK:]


def _attn_specs(lp):
    bq = ATT_BLK
    qblk = pl.BlockSpec((bq, bq), lambda p, i: (i, p))
    kblk = pl.BlockSpec((1, lp, bq), lambda p, i: (p // 2, 0, p % 2))
    vblk = pl.BlockSpec((1, lp, bq), lambda p, i: (HEAD_PAIRS // 2 + p // 2, 0, p % 2))
    tri = pl.BlockSpec((2 * bq, 2 * bq), lambda p, i: (0, 0))
    return qblk, kblk, vblk, tri


def _attn_fwd(q, kv, name):
    lp, d = q.shape
    bq = ATT_BLK

    def body(q_ref, k_ref, v_ref, tri_ref, o_ref, t_ref):
        i = pl.program_id(1)
        qs = q_ref[...] * (HEAD_DIM ** -0.5)
        lane_a = lax.broadcasted_iota(jnp.int32, (1, bq), 1) < HEAD_DIM
        tri = tri_ref[...]

        def step(j, carry, diag):
            oacc, ca, cb = carry
            rows = pl.ds(pl.multiple_of(j * bq, bq), bq)
            kk = _split_heads(k_ref[0, rows, :], lane_a)
            vv = _split_heads(v_ref[0, rows, :], lane_a)
            lb, l1m = _log_sigmoids(_dot(qs, kk, NT))
            if diag:
                valid = _diag_valid()
                l1m = jnp.where(valid, l1m, 0.0)
            ex = lb + _scan_dot(l1m, tri)
            exa, exb = _halves(ex)
            a = jnp.concatenate([jnp.exp(exa + ca), jnp.exp(exb + cb)], axis=1)
            if diag:
                a = jnp.where(valid, a, 0.0)
            oacc = oacc + _dot(a.astype(BF16), vv)
            la, lb_ = _halves(l1m)
            return (oacc, ca + jnp.sum(la, axis=1, keepdims=True), cb + jnp.sum(lb_, axis=1, keepdims=True))

        carry = (jnp.zeros((bq, bq), F32), jnp.zeros((bq, 1), F32), jnp.zeros((bq, 1), F32))
        carry = step(i, carry, True)
        oacc, ca, cb = lax.fori_loop(0, i, lambda jj, c: step(i - 1 - jj, c, False), carry)
        o_ref[...] = oacc.astype(BF16)
        t_ref[...] = jnp.where(lane_a, ca, cb)

    qblk, kblk, vblk, tri = _attn_specs(lp)
    return pl.pallas_call(
        body, name=name, grid=(HEAD_PAIRS, lp // bq),
        in_specs=[qblk, kblk, vblk, tri],
        out_specs=[qblk, qblk], out_shape=[SDS((lp, d), BF16), SDS((lp, d), F32)],
        compiler_params=_cp(2))(q, kv, kv, _pair_tri("after"))


def _attn_bwd(q, kv, do, tot, name):
    lp, d = q.shape
    bq = ATT_BLK
    scale = HEAD_DIM ** -0.5

    def body(q_ref, k_ref, v_ref, do_ref, t_ref, upto_ref, before_ref, dq_ref, dk_ref, dv_ref):
        i = pl.program_id(1)

        @pl.when(i == 0)
        def _():
            dk_ref[...] = jnp.zeros_like(dk_ref)
            dv_ref[...] = jnp.zeros_like(dv_ref)

        qs = q_ref[...] * scale
        do_blk = do_ref[...]
        lane_a = lax.broadcasted_iota(jnp.int32, (1, bq), 1) < HEAD_DIM
        tot_blk = t_ref[...]
        ta = jnp.max(jnp.where(lane_a, tot_blk, -jnp.inf), axis=1, keepdims=True)
        tb = jnp.max(jnp.where(lane_a, -jnp.inf, tot_blk), axis=1, keepdims=True)
        upto = upto_ref[...]
        before = before_ref[...]

        def step(j, carry, diag):
            dq, pa, pb, ea, eb = carry
            rows = pl.ds(pl.multiple_of(j * bq, bq), bq)
            kk = _split_heads(k_ref[0, rows, :], lane_a)
            vv = _split_heads(v_ref[0, rows, :], lane_a)
            lb, l1m = _log_sigmoids(_dot(qs, kk, NT))
            if diag:
                valid = _diag_valid()
                l1m = jnp.where(valid, l1m, 0.0)
            inca, incb = _halves(_scan_dot(l1m, upto))
            lba, lbb = _halves(lb)
            a = jnp.concatenate([jnp.exp(lba + ((ta - pa) - inca)), jnp.exp(lbb + ((tb - pb) - incb))], axis=1)
            if diag:
                a = jnp.where(valid, a, 0.0)
            e = a * _dot(do_blk, vv, NT)
            sa, sb = _halves(_scan_dot(e, before))
            e_before = jnp.concatenate([sa + ea, sb + eb], axis=1)
            sig = jnp.exp(lb)
            dz = e - sig * (e + e_before)
            if diag:
                dz = jnp.where(valid, dz, 0.0)
            dzb = dz.astype(BF16)
            dq = dq + _dot(dzb, kk)
            rk = _dot(dzb, qs, TN)
            rv = _dot(a.astype(BF16), do_blk, TN)
            dk_ref[0, rows, :] += jnp.where(lane_a, rk[:bq], rk[bq:])
            dv_ref[0, rows, :] += jnp.where(lane_a, rv[:bq], rv[bq:])
            la, lb_ = _halves(l1m)
            e_a, e_b = _halves(e)
            return (dq, pa + jnp.sum(la, axis=1, keepdims=True), pb + jnp.sum(lb_, axis=1, keepdims=True),
                    ea + jnp.sum(e_a, axis=1, keepdims=True), eb + jnp.sum(e_b, axis=1, keepdims=True))

        zcol = jnp.zeros((bq, 1), F32)
        carry = (jnp.zeros((bq, bq), F32), zcol, zcol, zcol, zcol)
        carry = lax.fori_loop(0, i, lambda j, c: step(j, c, False), carry)
        carry = step(i, carry, True)
        dq_ref[...] = (carry[0] * scale).astype(BF16)

    qblk, kblk, vblk, tri = _attn_specs(lp)
    return pl.pallas_call(
        body, name=name, grid=(HEAD_PAIRS, lp // bq),
        in_specs=[qblk, kblk, vblk, qblk, qblk, tri, tri],
        out_specs=[qblk, kblk, kblk],
        out_shape=[SDS((lp, d), BF16), SDS((HEAD_PAIRS // 2, lp, 2 * bq), F32), SDS((HEAD_PAIRS // 2, lp, 2 * bq), F32)],
        compiler_params=_cp(2))(q, kv, kv, do, tot, _pair_tri("upto"), _pair_tri("before"))


def _mesh_pos():
    return lax.axis_index("x"), lax.axis_index("y"), lax.axis_index("c")


def _flip(pos, r):
    x, y, c = pos
    return (1 - x if r & 4 else x, 1 - y if r & 2 else y, 1 - c if r & 1 else c)


def _dev_index(pos):
    return 4 * pos[0] + 2 * pos[1] + pos[2]


def _all_gather(x, name):
    shape = x.shape

    def body(x_ref, out_ref, send_sems, recv_sems, local_sem):
        me = _mesh_pos()
        sibling = _flip(me, 1)
        others = [_flip(me, 4), _flip(me, 2), _flip(me, 6)]

        def copy(k, block, to, src=None):
            slab = out_ref.at[_dev_index(block)]
            return pltpu.make_async_remote_copy(
                src_ref=slab if src is None else src, dst_ref=slab,
                send_sem=send_sems.at[k], recv_sem=recv_sems.at[k],
                device_id=to, device_id_type=pl.DeviceIdType.MESH)

        mine = pltpu.make_async_copy(x_ref, out_ref.at[_dev_index(me)], local_sem)
        mine.start()
        first = [copy(0, me, sibling, src=x_ref)] + [copy(1 + j, me, o, src=x_ref) for j, o in enumerate(others)]
        for cp in first:
            cp.start()
        passed = [copy(4 + j, o, sibling) for j, o in enumerate(others)]
        for j, o in enumerate(others):
            copy(1 + j, o, me).wait_recv()
            passed[j].start()
        copy(0, sibling, me).wait_recv()
        for j, o in enumerate(others):
            copy(4 + j, _flip(o, 1), me).wait_recv()
        for cp in first + passed:
            cp.wait_send()
        mine.wait()

    return pl.pallas_call(
        body, name=name, out_shape=SDS((N_DEV,) + shape, x.dtype),
        in_specs=[pl.BlockSpec(memory_space=pl.ANY)], out_specs=pl.BlockSpec(memory_space=pl.ANY),
        scratch_shapes=[pltpu.SemaphoreType.DMA((7,)), pltpu.SemaphoreType.DMA((7,)), pltpu.SemaphoreType.DMA(())],
    )(x)


def _scatter_exchange(p, name):
    def body(p_ref, out_ref, send_sems, recv_sems, local_sem):
        me = _mesh_pos()
        mine = pltpu.make_async_copy(p_ref.at[_dev_index(me)], out_ref.at[_dev_index(me)], local_sem)
        mine.start()
        copies = []
        for r in range(1, N_DEV):
            peer = _flip(me, r)
            copies.append(pltpu.make_async_remote_copy(
                src_ref=p_ref.at[_dev_index(peer)], dst_ref=out_ref.at[_dev_index(me)],
                send_sem=send_sems.at[r - 1], recv_sem=recv_sems.at[r - 1],
                device_id=peer, device_id_type=pl.DeviceIdType.MESH))
            copies[-1].start()
        for r in range(1, N_DEV):
            peer = _flip(me, r)
            landed = out_ref.at[_dev_index(peer)]
            pltpu.make_async_remote_copy(
                src_ref=landed, dst_ref=landed, send_sem=send_sems.at[r - 1], recv_sem=recv_sems.at[r - 1],
                device_id=peer, device_id_type=pl.DeviceIdType.MESH).wait_recv()
        for cp in copies:
            cp.wait_send()
        mine.wait()

    return pl.pallas_call(
        body, name=name, out_shape=SDS(p.shape, p.dtype),
        in_specs=[pl.BlockSpec(memory_space=pl.ANY)], out_specs=pl.BlockSpec(memory_space=pl.ANY),
        scratch_shapes=[pltpu.SemaphoreType.DMA((7,)), pltpu.SemaphoreType.DMA((7,)), pltpu.SemaphoreType.DMA(())],
    )(p)


def _sum_slabs(a, name):
    n, rows, cols = a.shape
    tr = _row_tile(rows, 64 if a.dtype == BF16 else 1024)

    def body(a_ref, o_ref):
        acc = a_ref[0].astype(F32)
        for k in range(1, n):
            acc = acc + a_ref[k].astype(F32)
        o_ref[...] = acc

    return pl.pallas_call(
        body, name=name, grid=(rows // tr,),
        in_specs=[pl.BlockSpec((n, tr, cols), lambda i: (0, i, 0))],
        out_specs=pl.BlockSpec((tr, cols), lambda i: (i, 0)),
        out_shape=SDS((rows, cols), F32), compiler_params=_cp(1))(a)


def _adamw(w, g, m, v, name):
    rows, cols = w.shape
    tr = _row_tile(rows, 352)

    def body(w_ref, g_ref, m_ref, v_ref, d_ref, mo_ref, vo_ref):
        g_ = g_ref[...]
        m_ = ADAM_B1 * m_ref[...] + (1.0 - ADAM_B1) * g_
        v_ = ADAM_B2 * v_ref[...] + (1.0 - ADAM_B2) * (g_ * g_)
        m_hat = m_ / (1.0 - ADAM_B1 ** ADAM_STEP)
        v_hat = v_ / (1.0 - ADAM_B2 ** ADAM_STEP)
        d_ref[...] = -ADAM_LR * (m_hat / (jnp.sqrt(v_hat) + ADAM_EPS) + ADAM_WD * w_ref[...])
        mo_ref[...] = m_
        vo_ref[...] = v_

    blk = pl.BlockSpec((tr, cols), lambda i: (i, 0))
    return pl.pallas_call(
        body, name=name, grid=(rows // tr,),
        in_specs=[blk] * 4, out_specs=[blk] * 3, out_shape=[SDS((rows, cols), F32)] * 3,
        compiler_params=_cp(1))(w, g, m, v)


def _ffn_fwd(h, gain, w_up, cw4, cb4, w_down4, tag):
    (n2,) = _rms_fwd(h, gain, f"ffn_norm_{tag}")
    up = _mm_group(n2, w_up, NN, F32, f"ffn_up_{tag}")
    up4 = up.reshape((2, 4) + up.shape[1:])
    act = _ffn_act_fwd(up4, cw4, cb4, f"ffn_act_{tag}")
    out = _mm_reduce(act, w_down4, NN, h, f"ffn_down_{tag}")
    return out, (n2, up4, act)


def _ffn_bwd(h, gain, w_up, cw4, cb4, w_down4, saved, dh, tag):
    n2, up4, act = saved
    dact = _mm_group(dh, w_down4, NT, F32, f"ffn_dact_{tag}")
    d_w_down = _mm_tn(act, dh[None], f"ffn_dwdown_{tag}")
    du4, dcw4, dcb4 = _ffn_act_bwd(up4, cw4, cb4, dact, f"ffn_dgate_{tag}")
    du = du4.reshape((8,) + du4.shape[2:])
    dup = _conv_bwd(du, cw4.reshape((8,) + cw4.shape[2:]), f"ffn_dconv_{tag}")
    dn2 = _mm_reduce(dup, w_up, NT, None, f"ffn_dnorm_{tag}")
    d_w_up = _mm_tn(n2[None], dup, f"ffn_dwup_{tag}")
    dh_in, dgain = _rms_bwd(h, gain, [dn2], dh, f"ffn_dh_{tag}")
    return dh_in, dgain, d_w_up, d_w_down, dcw4, dcb4


def kernel(x, meta_tokens, mix_norm, ffn_norm, pool_w, pool_scale, kv_norm, w_kv, w_q, w_o, ffn_w_up, ffn_conv_w, ffn_conv_b, ffn_w_down, final_norm, loss_target, m_meta_tokens, m_mix_norm, m_ffn_norm, m_pool_w, m_pool_scale, m_kv_norm, m_w_kv, m_w_q, m_w_o, m_ffn_w_up, m_ffn_conv_w, m_ffn_conv_b, m_ffn_w_down, m_final_norm, v_meta_tokens, v_mix_norm, v_ffn_norm, v_pool_w, v_pool_scale, v_kv_norm, v_w_kv, v_w_q, v_w_o, v_ffn_w_up, v_ffn_conv_w, v_ffn_conv_b, v_ffn_w_down, v_final_norm):
    seq, d = x.shape[1], x.shape[2]
    n_tok = N_META + seq
    lp = -(-n_tok // ROW_TILE) * ROW_TILE
    fc = ffn_w_up.shape[2]
    me = _dev_index(_mesh_pos())

    big_parts = [pool_w, w_kv, w_q, w_o, ffn_w_up, ffn_w_down]
    big_rows = [p.size // d for p in big_parts]
    big_off = [sum(big_rows[:k]) for k in range(len(big_parts) + 1)]
    local_big = jnp.concatenate([p.reshape(-1, d) for p in big_parts], axis=0).astype(BF16)
    gb = _all_gather(local_big, "gather_matrices")

    def big(k):
        return gb[:, big_off[k]:big_off[k + 1]]

    pw = big(0).reshape(N_DEV, 4, POOL_C // N_DEV, POOL_C).transpose(1, 0, 2, 3).reshape(4, POOL_C, POOL_C)
    wkv = big(1).reshape(N_DEV, d, 2 * d // N_DEV)
    wq = big(2).reshape(1, d, d)
    wo = big(3).reshape(1, d, d)
    wup = big(4).reshape(N_DEV, 2, d, fc)
    wdn = big(5).reshape(N_DEV, 2, fc // 2, d)
    wup_l = [wup[:, l] for l in range(2)]
    wdn_l = [wdn[:, l].reshape(4, fc, d) for l in range(2)]

    small_parts = [meta_tokens, pool_scale, ffn_conv_w]
    small_rows = [p.size // 128 for p in small_parts]
    small_pad = -sum(small_rows) % 8
    local_small = jnp.concatenate([p.reshape(-1, 128) for p in small_parts] + [jnp.zeros((small_pad, 128), F32)], axis=0)
    gs = _all_gather(local_small, "gather_vectors")
    r0, r1, r2 = small_rows[0], small_rows[0] + small_rows[1], sum(small_rows)
    meta_full = gs[:, :r0].transpose(1, 0, 2).reshape(N_META, d)
    pscale = gs[:, r0:r1].reshape(1, d)
    cw = gs[:, r1:r2].reshape(N_DEV, 2, 3, fc)
    cw4_l = [cw[:, l].reshape(2, 4, 3, fc) for l in range(2)]
    cb4_l = [ffn_conv_b[l].reshape(2, 4, 1, fc) for l in range(2)]

    h0 = jnp.concatenate([meta_full, x[0], jnp.zeros((lp - n_tok, d), F32)], axis=0)
    h1, diff = _pool_fwd(h0, mix_norm[0:1], pw, pscale, "pool_fwd")
    h2, saved0 = _ffn_fwd(h1, ffn_norm[0:1], wup_l[0], cw4_l[0], cb4_l[0], wdn_l[0], "0")
    gains_b = jnp.stack([kv_norm, mix_norm[1]], axis=0)
    kvn, n3 = _rms_fwd(h2, gains_b, "attn_norms")
    kv = _mm_group(kvn, wkv, NN, BF16, "kv_proj")
    q = _mm_group(n3, wq, NN, BF16, "q_proj")[0]
    o, tot = _attn_fwd(q, kv, "attn_fwd")
    h3 = _mm_reduce(o[None], wo, NN, h2, "o_proj")
    h4, saved1 = _ffn_fwd(h3, ffn_norm[1:2], wup_l[1], cw4_l[1], cb4_l[1], wdn_l[1], "1")
    target = jnp.pad(loss_target[0], ((N_META, lp - n_tok), (0, 0)))
    dh4, loss_blk, dg_final = _loss_bwd(h4, final_norm[None], target, seq, "loss")
    loss = lax.psum(loss_blk[0, 0], MESH_AXES)

    dh3, dg_ffn1, d_wup1, d_wdn1, dcw4_1, dcb4_1 = _ffn_bwd(
        h3, ffn_norm[1:2], wup_l[1], cw4_l[1], cb4_l[1], wdn_l[1], saved1, dh4, "1")
    d_o = _mm_group(dh3, wo, NT, BF16, "o_proj_dx")[0]
    d_wo = _mm_tn(o[None], dh3[None], "o_proj_dw")
    dq, dk, dv = _attn_bwd(q, kv, d_o, tot, "attn_bwd")
    dn3 = _mm_group(dq, wq, NT, F32, "q_proj_dx")[0]
    d_wq = _mm_tn(n3[None], dq[None], "q_proj_dw")
    dkv = jnp.concatenate([dk, dv], axis=0).astype(BF16)
    dkvn = _mm_reduce(dkv, wkv, NT, None, "kv_proj_dx")
    d_wkv = _mm_tn(kvn[None], dkv, "kv_proj_dw")
    dh2, dg_b = _rms_bwd(h2, gains_b, [dkvn, dn3], dh3, "attn_norms_bwd")
    dh1, dg_ffn0, d_wup0, d_wdn0, dcw4_0, dcb4_0 = _ffn_bwd(
        h1, ffn_norm[0:1], wup_l[0], cw4_l[0], cb4_l[0], wdn_l[0], saved0, dh2, "0")
    dh0, d_pw, d_pscale, dg_mix0 = _pool_bwd(h0, mix_norm[0:1], pw, pscale, diff, dh1, "pool_bwd")
    grad_x = dh0[N_META:n_tok][None]

    d_pw8 = d_pw.reshape(4, N_DEV, POOL_C // N_DEV, POOL_C).transpose(1, 0, 2, 3).reshape(N_DEV, -1, d).astype(BF16)
    d_wup8 = jnp.stack([d_wup0, d_wup1], axis=1).reshape(N_DEV, -1, d)
    d_wdn8 = jnp.stack([d_wdn0.reshape(N_DEV, fc // 2, d), d_wdn1.reshape(N_DEV, fc // 2, d)], axis=1).reshape(N_DEV, -1, d)
    partial_big = jnp.concatenate(
        [d_pw8, d_wkv.reshape(N_DEV, -1, d), d_wq.reshape(N_DEV, -1, d), d_wo.reshape(N_DEV, -1, d), d_wup8, d_wdn8], axis=1)
    g_big = _sum_slabs(_scatter_exchange(partial_big, "scatter_matrices"), "sum_matrices")

    rep_parts = [jnp.concatenate([dg_mix0, dg_b[1:2]], axis=0), jnp.concatenate([dg_ffn0, dg_ffn1], axis=0),
                 dg_b[0:1], dg_final, jnp.stack([dcb4_0.reshape(-1), dcb4_1.reshape(-1)], axis=0)]
    rep_shapes = [mix_norm.shape, ffn_norm.shape, kv_norm.shape, final_norm.shape, ffn_conv_b.shape]
    rep_rows = [p.size // 128 for p in rep_parts]
    d_meta8 = dh0[:N_META].reshape(N_META, N_DEV, d // N_DEV).transpose(1, 0, 2).reshape(N_DEV, -1, 128)
    d_cw8 = jnp.stack([dcw4_0.reshape(N_DEV, 3, fc), dcw4_1.reshape(N_DEV, 3, fc)], axis=1).reshape(N_DEV, -1, 128)
    shard_parts = jnp.concatenate([d_meta8, d_pscale.reshape(N_DEV, 1, 128), d_cw8], axis=1)
    n_rep = sum(rep_rows)
    partial_small = jnp.concatenate([p.reshape(-1, 128) for p in rep_parts] + [shard_parts.reshape(-1, 128)], axis=0)
    g_small = _sum_slabs(_all_gather(partial_small, "gather_vector_grads"), "sum_vectors")
    g_rep = [g_small[sum(rep_rows[:k]):sum(rep_rows[:k + 1])].reshape(s) for k, s in enumerate(rep_shapes)]
    g_shard = lax.dynamic_index_in_dim(g_small[n_rep:].reshape(N_DEV, -1, 128), me, 0, keepdims=False)
    g_meta = g_shard[:r0].reshape(meta_tokens.shape)
    g_pscale = g_shard[r0:r1].reshape(pool_scale.shape)
    g_cw = g_shard[r1:r2].reshape(ffn_conv_w.shape)

    grads = {
        "meta_tokens": g_meta, "mix_norm": g_rep[0], "ffn_norm": g_rep[1],
        "pool_w": g_big[big_off[0]:big_off[1]].reshape(pool_w.shape), "pool_scale": g_pscale, "kv_norm": g_rep[2],
        "w_kv": g_big[big_off[1]:big_off[2]].reshape(w_kv.shape), "w_q": g_big[big_off[2]:big_off[3]].reshape(w_q.shape),
        "w_o": g_big[big_off[3]:big_off[4]].reshape(w_o.shape),
        "ffn_w_up": g_big[big_off[4]:big_off[5]].reshape(ffn_w_up.shape), "ffn_conv_w": g_cw, "ffn_conv_b": g_rep[4],
        "ffn_w_down": g_big[big_off[5]:big_off[6]].reshape(ffn_w_down.shape), "final_norm": g_rep[3],
    }
    names = list(grads)
    weights = dict(zip(names, [meta_tokens, mix_norm, ffn_norm, pool_w, pool_scale, kv_norm, w_kv, w_q, w_o,
                               ffn_w_up, ffn_conv_w, ffn_conv_b, ffn_w_down, final_norm]))
    mom1 = dict(zip(names, [m_meta_tokens, m_mix_norm, m_ffn_norm, m_pool_w, m_pool_scale, m_kv_norm, m_w_kv, m_w_q,
                            m_w_o, m_ffn_w_up, m_ffn_conv_w, m_ffn_conv_b, m_ffn_w_down, m_final_norm]))
    mom2 = dict(zip(names, [v_meta_tokens, v_mix_norm, v_ffn_norm, v_pool_w, v_pool_scale, v_kv_norm, v_w_kv, v_w_q,
                            v_w_o, v_ffn_w_up, v_ffn_conv_w, v_ffn_conv_b, v_ffn_w_down, v_final_norm]))

    delta, new_m, new_v = {}, {}, {}
    matrices = ["pool_w", "w_kv", "w_q", "w_o", "ffn_w_up", "ffn_w_down"]
    for n in matrices:
        shape = weights[n].shape
        flat = (-1, shape[-1])
        dl, nm, nv = _adamw(weights[n].reshape(flat), grads[n].reshape(flat), mom1[n].reshape(flat),
                            mom2[n].reshape(flat), "adamw_" + n)
        delta[n], new_m[n], new_v[n] = dl.reshape(shape), nm.reshape(shape), nv.reshape(shape)
    vectors = [n for n in names if n not in matrices]
    vec_rows = [weights[n].size // 128 for n in vectors]
    vec_pad = -sum(vec_rows) % 8

    def pack(tree):
        return jnp.concatenate([tree[n].reshape(-1, 128) for n in vectors] + [jnp.ones((vec_pad, 128), F32)], axis=0)

    outs = _adamw(pack(weights), pack(grads), pack(mom1), pack(mom2), "adamw_vectors")
    for tree, packed in zip((delta, new_m, new_v), outs):
        for k, n in enumerate(vectors):
            tree[n] = packed[sum(vec_rows[:k]):sum(vec_rows[:k + 1])].reshape(weights[n].shape)

    return (loss, grad_x, *[grads[n] for n in names], *[delta[n] for n in names],
            *[new_m[n] for n in names], *[new_v[n] for n in names])
```

```python
import functools

import jax
import jax.numpy as jnp
from jax import lax
from jax.experimental import pallas as pl
from jax.experimental.pallas import tpu as pltpu

F32 = jnp.float32
BF16 = jnp.bfloat16
SDS = jax.ShapeDtypeStruct

N_DEV = 8
N_META = 16
HEAD_DIM = 64
HEAD_PAIRS = 8
RMS_EPS = 1e-6
POOL_WINDOWS = (2, 4, 8, 16)
POOL_C = 256
POOL_HALO = 16
CONV_HALO = 8
ROW_TILE = 384
ATT_BLK = 128
ATT_Q = ROW_TILE
ATT_UNROLL = ATT_Q // ATT_BLK
VMEM_LIMIT = 56 * 1024 * 1024

ADAM_LR = 0.001
ADAM_B1 = 0.9
ADAM_B2 = 0.999
ADAM_EPS = 1e-08
ADAM_WD = 0.01
ADAM_STEP = 10

MESH_AXES = ("x", "y", "c")
NN = (((1,), (0,)), ((), ()))
NT = (((1,), (1,)), ((), ()))
TN = (((0,), (0,)), ((), ()))


def _cp(n_axes):
    return pltpu.CompilerParams(dimension_semantics=("arbitrary",) * n_axes, vmem_limit_bytes=VMEM_LIMIT)


def _dot(a, b, dims=NN):
    return lax.dot_general(a, b, dims, preferred_element_type=F32)


def _rstd(x):
    return lax.rsqrt(jnp.mean(x * x, axis=-1, keepdims=True) + RMS_EPS)


def _row_tile(rows, cap=512):
    if rows <= cap:
        return rows
    best = 8
    for t in range(8, cap + 1, 8):
        if rows % t == 0:
            best = t
    assert rows % best == 0
    return best


def _rms_fwd(h, gains, name):
    lp, d = h.shape
    k = gains.shape[0]
    tm = ROW_TILE

    def body(h_ref, g_ref, *o_refs):
        x = h_ref[...]
        u = x * _rstd(x)
        for j in range(k):
            o_refs[j][...] = (u * g_ref[j:j + 1, :]).astype(BF16)

    row = pl.BlockSpec((tm, d), lambda i: (i, 0))
    return pl.pallas_call(
        body, name=name, grid=(lp // tm,),
        in_specs=[row, pl.BlockSpec((k, d), lambda i: (0, 0))],
        out_specs=[row] * k, out_shape=[SDS((lp, d), BF16)] * k,
        compiler_params=_cp(1))(h, gains)


def _rms_bwd(h, gains, dns, dh_in, name):
    lp, d = h.shape
    k = gains.shape[0]
    tm = ROW_TILE

    def body(h_ref, g_ref, *refs):
        dn_refs, dh_ref, dho_ref, dg_ref = refs[:k], refs[k], refs[k + 1], refs[k + 2]
        i = pl.program_id(0)
        x = h_ref[...]
        r = _rstd(x)
        u = x * r
        du = jnp.zeros_like(x)
        rows = []
        for j in range(k):
            dn = dn_refs[j][...]
            du = du + dn * g_ref[j:j + 1, :]
            rows.append(jnp.sum(dn * u, axis=0, keepdims=True))
        dx = r * (du - u * jnp.mean(du * u, axis=-1, keepdims=True))
        dho_ref[...] = dh_ref[...] + dx

        @pl.when(i == 0)
        def _():
            for j in range(k):
                dg_ref[j:j + 1, :] = rows[j]

        @pl.when(i > 0)
        def _():
            for j in range(k):
                dg_ref[j:j + 1, :] += rows[j]

    row = pl.BlockSpec((tm, d), lambda i: (i, 0))
    vec = pl.BlockSpec((k, d), lambda i: (0, 0))
    return pl.pallas_call(
        body, name=name, grid=(lp // tm,),
        in_specs=[row, vec] + [row] * k + [row],
        out_specs=[row, vec], out_shape=[SDS((lp, d), F32), SDS((k, d), F32)],
        compiler_params=_cp(1))(h, gains, *dns, dh_in)


def _loss_bwd(h, gain, target, n_real, name):
    lp, d = h.shape
    tm = ROW_TILE

    def body(h_ref, g_ref, t_ref, dh_ref, loss_ref, dg_ref):
        i = pl.program_id(0)
        x = h_ref[...]
        g = g_ref[...]
        r = _rstd(x)
        u = x * r
        row = i * tm + lax.broadcasted_iota(jnp.int32, (tm, 1), 0)
        valid = (row >= N_META) & (row < N_META + n_real)
        e = jnp.where(valid, u * g - t_ref[...], 0.0)
        part = 0.5 * jnp.sum(jnp.sum(e * e, axis=-1, keepdims=True), axis=0, keepdims=True) * (1.0 / d)
        dy = e * (1.0 / d)
        du = dy * g
        dh_ref[...] = r * (du - u * jnp.mean(du * u, axis=-1, keepdims=True))
        dgp = jnp.sum(dy * u, axis=0, keepdims=True)

        @pl.when(i == 0)
        def _():
            loss_ref[...] = jnp.broadcast_to(part, (8, 128))
            dg_ref[...] = dgp

        @pl.when(i > 0)
        def _():
            loss_ref[...] += jnp.broadcast_to(part, (8, 128))
            dg_ref[...] += dgp

    row = pl.BlockSpec((tm, d), lambda i: (i, 0))
    vec = pl.BlockSpec((1, d), lambda i: (0, 0))
    return pl.pallas_call(
        body, name=name, grid=(lp // tm,),
        in_specs=[row, vec, row],
        out_specs=[row, pl.BlockSpec((8, 128), lambda i: (0, 0)), vec],
        out_shape=[SDS((lp, d), F32), SDS((8, 128), F32), SDS((1, d), F32)],
        compiler_params=_cp(1))(h, gain, target)


def _pool_fwd(h, gain, w, scale, name):
    lp, d = h.shape
    tm = ROW_TILE
    hb = POOL_HALO

    def body(h_ref, halo_ref, g_ref, w_ref, s_ref, h1_ref, diff_ref):
        i = pl.program_id(0)
        g = g_ref[...]
        x = h_ref[...]
        n = x * _rstd(x) * g
        xh = halo_ref[...]
        nh = jnp.where(i > 0, xh * _rstd(xh) * g, 0.0)
        cur = jnp.concatenate([nh, n], axis=0)
        pos = i * tm + lax.broadcasted_iota(jnp.int32, (tm, 1), 0)
        for gi, win in enumerate(POOL_WINDOWS):
            if gi > 0:
                cur = cur[:, POOL_C:]
            cur = cur + pltpu.roll(cur, win // 2, 0)
            c0 = gi * POOL_C
            count = jnp.minimum(pos + 1, win).astype(F32)
            diff = cur[hb:, :POOL_C] / count - n[:, c0:c0 + POOL_C]
            diff = diff.astype(BF16)
            y = _dot(diff, w_ref[gi])
            h1_ref[:, c0:c0 + POOL_C] = x[:, c0:c0 + POOL_C] + y * s_ref[:, c0:c0 + POOL_C]
            diff_ref[:, c0:c0 + POOL_C] = diff

    row = pl.BlockSpec((tm, d), lambda i: (i, 0))
    halo = pl.BlockSpec((hb, d), lambda i: (jnp.maximum(i * (tm // hb) - 1, 0), 0))
    vec = pl.BlockSpec((1, d), lambda i: (0, 0))
    return pl.pallas_call(
        body, name=name, grid=(lp // tm,),
        in_specs=[row, halo, vec, pl.BlockSpec(w.shape, lambda i: (0, 0, 0)), vec],
        out_specs=[row, row], out_shape=[SDS((lp, d), F32), SDS((lp, d), BF16)],
        compiler_params=_cp(1))(h, h, gain, w, scale)


def _pool_bwd(h, gain, w, scale, diff, dh1, name):
    lp, d = h.shape
    tm = ROW_TILE
    hb = POOL_HALO
    nblk = lp // tm
    ext = tm + hb

    def body(h_ref, g_ref, w_ref, s_ref, diff_ref, dh_ref, dhn_ref, dh0_ref, dw_ref, ds_ref, dg_ref):
        i = pl.program_id(0)
        g = g_ref[...]
        x = h_ref[...]
        r = _rstd(x)
        u = x * r
        dh = dh_ref[...]
        dhn = jnp.where(i < nblk - 1, dhn_ref[...], 0.0)
        dyp = jnp.concatenate([dh, dhn], axis=0) * s_ref[...]
        pos = i * tm + lax.broadcasted_iota(jnp.int32, (ext, 1), 0)
        dn_parts, dw_parts, ds_parts = [], [], []
        for gi, win in enumerate(POOL_WINDOWS):
            c0 = gi * POOL_C
            wg = w_ref[gi]
            dyp_g = dyp[:, c0:c0 + POOL_C].astype(BF16)
            dd = _dot(dyp_g, wg, NT)
            dfg = diff_ref[:, c0:c0 + POOL_C]
            dw_parts.append(_dot(dfg, dyp_g[:tm], TN))
            ds_parts.append(jnp.sum(dh[:, c0:c0 + POOL_C] * _dot(dfg, wg), axis=0, keepdims=True))
            count = jnp.minimum(pos + 1, win).astype(F32)
            cur = dd / count
            sh = 1
            while sh < win:
                cur = cur + pltpu.roll(cur, ext - sh, 0)
                sh *= 2
            dn_parts.append(cur[:tm] - dd[:tm])
        dn = jnp.concatenate(dn_parts, axis=1)
        du = dn * g
        dh0_ref[...] = dh + r * (du - u * jnp.mean(du * u, axis=-1, keepdims=True))
        dgp = jnp.sum(dn * u, axis=0, keepdims=True)
        dsp = jnp.concatenate(ds_parts, axis=1)

        @pl.when(i == 0)
        def _():
            for gi in range(len(POOL_WINDOWS)):
                dw_ref[gi] = dw_parts[gi]
            ds_ref[...] = dsp
            dg_ref[...] = dgp

        @pl.when(i > 0)
        def _():
            for gi in range(len(POOL_WINDOWS)):
                dw_ref[gi] += dw_parts[gi]
            ds_ref[...] += dsp
            dg_ref[...] += dgp

    row = pl.BlockSpec((tm, d), lambda i: (i, 0))
    nxt = pl.BlockSpec((hb, d), lambda i: (jnp.minimum((i + 1) * (tm // hb), lp // hb - 1), 0))
    vec = pl.BlockSpec((1, d), lambda i: (0, 0))
    wsp = pl.BlockSpec(w.shape, lambda i: (0, 0, 0))
    return pl.pallas_call(
        body, name=name, grid=(nblk,),
        in_specs=[row, vec, wsp, vec, row, row, nxt],
        out_specs=[row, wsp, vec, vec],
        out_shape=[SDS((lp, d), F32), SDS(w.shape, F32), SDS((1, d), F32), SDS((1, d), F32)],
        compiler_params=_cp(1))(h, gain, w, scale, diff, dh1, dh1)


def _conv_taps(x, halo, first):
    ext = jnp.concatenate([jnp.where(first, 0.0, halo), x], axis=0)
    return pltpu.roll(ext, 1, 0)[CONV_HALO:], pltpu.roll(ext, 2, 0)[CONV_HALO:]


def _ffn_specs(tm, c, lp):
    blk = pl.BlockSpec((2, 1, tm, c), lambda g, i: (0, g, i, 0))
    halo = pl.BlockSpec((2, 1, CONV_HALO, c), lambda g, i: (0, g, jnp.maximum(i * (tm // CONV_HALO) - 1, 0), 0))
    cw = pl.BlockSpec((2, 1, 3, c), lambda g, i: (0, g, 0, 0))
    cb = pl.BlockSpec((2, 1, 1, c), lambda g, i: (0, g, 0, 0))
    return blk, halo, cw, cb


def _ffn_act_fwd(up4, cw4, cb4, name):
    _, ng, lp, c = up4.shape
    tm = ROW_TILE

    def body(up_ref, halo_ref, cw_ref, cb_ref, act_ref):
        first = pl.program_id(1) == 0
        u = []
        for half in range(2):
            x = up_ref[half, 0]
            xm1, xm2 = _conv_taps(x, halo_ref[half, 0], first)
            u.append(cb_ref[half, 0] + cw_ref[half, 0, 0:1, :] * xm2 + cw_ref[half, 0, 1:2, :] * xm1
                     + cw_ref[half, 0, 2:3, :] * x)
        gate, val = u
        sig = 1.0 / (1.0 + jnp.exp(-gate))
        act_ref[0] = (gate * sig * val).astype(BF16)

    blk, halo, cw, cb = _ffn_specs(tm, c, lp)
    return pl.pallas_call(
        body, name=name, grid=(ng, lp // tm),
        in_specs=[blk, halo, cw, cb],
        out_specs=pl.BlockSpec((1, tm, c), lambda g, i: (g, i, 0)),
        out_shape=SDS((ng, lp, c), BF16), compiler_params=_cp(2))(up4, up4, cw4, cb4)


def _ffn_act_bwd(up4, cw4, cb4, dact, name):
    _, ng, lp, c = up4.shape
    tm = ROW_TILE

    def body(up_ref, halo_ref, cw_ref, cb_ref, da_ref, du_ref, dcw_ref, dcb_ref):
        i = pl.program_id(1)
        first = i == 0
        u, taps = [], []
        for half in range(2):
            x = up_ref[half, 0]
            xm1, xm2 = _conv_taps(x, halo_ref[half, 0], first)
            u.append(cb_ref[half, 0] + cw_ref[half, 0, 0:1, :] * xm2 + cw_ref[half, 0, 1:2, :] * xm1
                     + cw_ref[half, 0, 2:3, :] * x)
            taps.append((xm2, xm1, x))
        gate, val = u
        sig = 1.0 / (1.0 + jnp.exp(-gate))
        da = da_ref[0]
        dus = (da * val * (sig * (1.0 + gate * (1.0 - sig))), da * (gate * sig))
        sums = []
        for half in range(2):
            du_ref[half, 0] = dus[half]
            sums.append([jnp.sum(dus[half] * t, axis=0, keepdims=True) for t in taps[half]]
                        + [jnp.sum(dus[half], axis=0, keepdims=True)])

        @pl.when(first)
        def _():
            for half in range(2):
                for k in range(3):
                    dcw_ref[half, 0, k:k + 1, :] = sums[half][k]
                dcb_ref[half, 0] = sums[half][3]

        @pl.when(i > 0)
        def _():
            for half in range(2):
                for k in range(3):
                    dcw_ref[half, 0, k:k + 1, :] += sums[half][k]
                dcb_ref[half, 0] += sums[half][3]

    blk, halo, cw, cb = _ffn_specs(tm, c, lp)
    return pl.pallas_call(
        body, name=name, grid=(ng, lp // tm),
        in_specs=[blk, halo, cw, cb, pl.BlockSpec((1, tm, c), lambda g, i: (g, i, 0))],
        out_specs=[blk, cw, cb],
        out_shape=[SDS(up4.shape, F32), SDS(cw4.shape, F32), SDS(cb4.shape, F32)],
        compiler_params=_cp(2))(up4, up4, cw4, cb4, dact)


def _conv_bwd(du, cw, name):
    ng, lp, c = du.shape
    tm = ROW_TILE
    nblk = lp // tm
    ext = tm + CONV_HALO

    def body(du_ref, nxt_ref, cw_ref, out_ref):
        i = pl.program_id(1)
        x = du_ref[0]
        full = jnp.concatenate([x, jnp.where(i < nblk - 1, nxt_ref[0], 0.0)], axis=0)
        xp1 = pltpu.roll(full, ext - 1, 0)[:tm]
        xp2 = pltpu.roll(full, ext - 2, 0)[:tm]
        out_ref[0] = (cw_ref[0, 2:3, :] * x + cw_ref[0, 1:2, :] * xp1 + cw_ref[0, 0:1, :] * xp2).astype(BF16)

    blk = pl.BlockSpec((1, tm, c), lambda g, i: (g, i, 0))
    nxt = pl.BlockSpec((1, CONV_HALO, c),
                       lambda g, i: (g, jnp.minimum((i + 1) * (tm // CONV_HALO), lp // CONV_HALO - 1), 0))
    return pl.pallas_call(
        body, name=name, grid=(ng, nblk),
        in_specs=[blk, nxt, pl.BlockSpec((1, 3, c), lambda g, i: (g, 0, 0))],
        out_specs=blk, out_shape=SDS((ng, lp, c), BF16), compiler_params=_cp(2))(du, du, cw)


def _mm_group(a, b, dims, out_dtype, name):
    m, k = a.shape
    ng = b.shape[0]
    n = b.shape[2] if dims == NN else b.shape[1]
    tm = ROW_TILE

    def body(a_ref, b_ref, o_ref):
        o_ref[0] = _dot(a_ref[...].astype(BF16), b_ref[0], dims).astype(out_dtype)

    return pl.pallas_call(
        body, name=name, grid=(ng, m // tm),
        in_specs=[pl.BlockSpec((tm, k), lambda g, i: (i, 0)),
                  pl.BlockSpec((1,) + b.shape[1:], lambda g, i: (g, 0, 0))],
        out_specs=pl.BlockSpec((1, tm, n), lambda g, i: (g, i, 0)),
        out_shape=SDS((ng, m, n), out_dtype), compiler_params=_cp(2))(a, b)


def _mm_reduce(a, b, dims, res, name):
    ng, m, k = a.shape
    n = b.shape[2] if dims == NN else b.shape[1]
    tm = ROW_TILE
    has_res = res is not None

    def body(a_ref, b_ref, *refs):
        o_ref, acc_ref = refs[-2], refs[-1]
        g = pl.program_id(1)
        p = _dot(a_ref[0].astype(BF16), b_ref[0], dims)

        @pl.when(g == 0)
        def _():
            acc_ref[...] = p + refs[0][...] if has_res else p

        @pl.when(g > 0)
        def _():
            acc_ref[...] += p

        @pl.when(g == ng - 1)
        def _():
            o_ref[...] = acc_ref[...]

    row = pl.BlockSpec((tm, n), lambda i, g: (i, 0))
    return pl.pallas_call(
        body, name=name, grid=(m // tm, ng),
        in_specs=[pl.BlockSpec((1, tm, k), lambda i, g: (g, i, 0)),
                  pl.BlockSpec((1,) + b.shape[1:], lambda i, g: (g, 0, 0))] + ([row] if has_res else []),
        out_specs=row, out_shape=SDS((m, n), F32),
        scratch_shapes=[pltpu.VMEM((tm, n), F32)], compiler_params=_cp(2))(a, b, *([res] if has_res else []))


def _mm_tn(a, b, name):
    ga, m, ka = a.shape
    gb, _, n = b.shape
    ng = max(ga, gb)
    tk = ROW_TILE
    nk = m // tk

    def body(a_ref, b_ref, o_ref, acc_ref):
        s = pl.program_id(1)
        p = _dot(a_ref[0].astype(BF16), b_ref[0].astype(BF16), TN)

        @pl.when(s == 0)
        def _():
            acc_ref[...] = p

        @pl.when(s > 0)
        def _():
            acc_ref[...] += p

        @pl.when(s == nk - 1)
        def _():
            o_ref[0] = acc_ref[...].astype(BF16)

    return pl.pallas_call(
        body, name=name, grid=(ng, nk),
        in_specs=[pl.BlockSpec((1, tk, ka), (lambda g, s: (g, s, 0)) if ga > 1 else (lambda g, s: (0, s, 0))),
                  pl.BlockSpec((1, tk, n), (lambda g, s: (g, s, 0)) if gb > 1 else (lambda g, s: (0, s, 0)))],
        out_specs=pl.BlockSpec((1, ka, n), lambda g, s: (g, 0, 0)),
        out_shape=SDS((ng, ka, n), BF16),
        scratch_shapes=[pltpu.VMEM((ka, n), F32)], compiler_params=_cp(2))(a, b)


def _pair_tri(kind):
    r = jnp.arange(2 * ATT_BLK)[:, None]
    c = jnp.arange(2 * ATT_BLK)[None, :]
    same = (r < ATT_BLK) == (c < ATT_BLK)
    rel = {"after": r > c, "upto": r <= c, "before": r < c}[kind]
    return (same & rel).astype(BF16)


def _scan_dot(x, tri):
    hi = x.astype(BF16)
    lo = (x - hi.astype(F32)).astype(BF16)
    return _dot(hi, tri) + _dot(lo, tri)


def _split_heads(blk, lane_a):
    zero = jnp.zeros_like(blk)
    return jnp.concatenate([jnp.where(lane_a, blk, zero), jnp.where(lane_a, zero, blk)], axis=0)


def _log_sigmoids(z):
    lb = jnp.minimum(z, 0.0) - jnp.log1p(jnp.exp(-jnp.abs(z)))
    return lb, lb - z


def _visible(qi, j):
    t = qi * ATT_Q + lax.broadcasted_iota(jnp.int32, (ATT_Q, 2 * ATT_BLK), 0)
    s = j * ATT_BLK + (lax.broadcasted_iota(jnp.int32, (ATT_Q, 2 * ATT_BLK), 1) & (ATT_BLK - 1))
    return s < t


def _halves(x):
    return x[:, :ATT_BLK], x[:, ATT_BLK:]


def _rowsum(x):
    return jnp.sum(x, axis=1, keepdims=True)


def _attn_specs(lp):
    bk = ATT_BLK
    qblk = pl.BlockSpec((ATT_Q, bk), lambda p, i: (i, p))
    kblk = pl.BlockSpec((1, lp, bk), lambda p, i: (p // 2, 0, p % 2))
    vblk = pl.BlockSpec((1, lp, bk), lambda p, i: (HEAD_PAIRS // 2 + p // 2, 0, p % 2))
    tri = pl.BlockSpec((2 * bk, 2 * bk), lambda p, i: (0, 0))
    return qblk, kblk, vblk, tri


def _attn_fwd(q, kv, name):
    lp, d = q.shape
    bk = ATT_BLK

    def body(q_ref, k_ref, v_ref, tri_ref, o_ref, t_ref):
        qi = pl.program_id(1)
        qs = q_ref[...] * (HEAD_DIM ** -0.5)
        lane_a = lax.broadcasted_iota(jnp.int32, (1, bk), 1) < HEAD_DIM
        tri = tri_ref[...]

        def step(j, carry, masked):
            oacc, ca, cb = carry
            rows = pl.ds(pl.multiple_of(j * bk, bk), bk)
            kk = _split_heads(k_ref[0, rows, :], lane_a)
            vv = _split_heads(v_ref[0, rows, :], lane_a)
            lb, l1m = _log_sigmoids(_dot(qs, kk, NT))
            if masked:
                valid = _visible(qi, j)
                l1m = jnp.where(valid, l1m, 0.0)
            ex = lb + _scan_dot(l1m, tri)
            exa, exb = _halves(ex)
            a = jnp.concatenate([jnp.exp(exa + ca), jnp.exp(exb + cb)], axis=1)
            if masked:
                a = jnp.where(valid, a, 0.0)
            oacc = oacc + _dot(a.astype(BF16), vv)
            la, lb_ = _halves(l1m)
            return oacc, ca + _rowsum(la), cb + _rowsum(lb_)

        carry = (jnp.zeros((ATT_Q, bk), F32), jnp.zeros((ATT_Q, 1), F32), jnp.zeros((ATT_Q, 1), F32))
        top = (qi + 1) * ATT_UNROLL - 1
        for u in range(ATT_UNROLL):
            carry = step(top - u, carry, True)

        def group(g, c):
            for u in range(ATT_UNROLL):
                c = step(top - (g + 1) * ATT_UNROLL - u, c, False)
            return c

        oacc, ca, cb = lax.fori_loop(0, qi, group, carry)
        o_ref[...] = oacc.astype(BF16)
        t_ref[...] = jnp.where(lane_a, ca, cb)

    qblk, kblk, vblk, tri = _attn_specs(lp)
    return pl.pallas_call(
        body, name=name, grid=(HEAD_PAIRS, lp // ATT_Q),
        in_specs=[qblk, kblk, vblk, tri],
        out_specs=[qblk, qblk], out_shape=[SDS((lp, d), BF16), SDS((lp, d), F32)],
        compiler_params=_cp(2))(q, kv, kv, _pair_tri("after"))


def _attn_bwd(q, kv, do, tot, name):
    lp, d = q.shape
    bk = ATT_BLK
    scale = HEAD_DIM ** -0.5

    def body(q_ref, k_ref, v_ref, do_ref, t_ref, upto_ref, before_ref, dq_ref, dk_ref, dv_ref):
        qi = pl.program_id(1)

        @pl.when(qi == 0)
        def _():
            dk_ref[...] = jnp.zeros_like(dk_ref)
            dv_ref[...] = jnp.zeros_like(dv_ref)

        qs = q_ref[...] * scale
        do_blk = do_ref[...]
        lane_a = lax.broadcasted_iota(jnp.int32, (1, bk), 1) < HEAD_DIM
        tot_blk = t_ref[...]
        ta = jnp.max(jnp.where(lane_a, tot_blk, -jnp.inf), axis=1, keepdims=True)
        tb = jnp.max(jnp.where(lane_a, -jnp.inf, tot_blk), axis=1, keepdims=True)
        upto = upto_ref[...]
        before = before_ref[...]

        def step(j, carry, masked):
            dq, pa, pb, ea, eb = carry
            rows = pl.ds(pl.multiple_of(j * bk, bk), bk)
            kk = _split_heads(k_ref[0, rows, :], lane_a)
            vv = _split_heads(v_ref[0, rows, :], lane_a)
            lb, l1m = _log_sigmoids(_dot(qs, kk, NT))
            if masked:
                valid = _visible(qi, j)
                l1m = jnp.where(valid, l1m, 0.0)
            inca, incb = _halves(_scan_dot(l1m, upto))
            lba, lbb = _halves(lb)
            a = jnp.concatenate([jnp.exp(lba + ((ta - pa) - inca)), jnp.exp(lbb + ((tb - pb) - incb))], axis=1)
            if masked:
                a = jnp.where(valid, a, 0.0)
            e = a * _dot(do_blk, vv, NT)
            sa, sb = _halves(_scan_dot(e, before))
            e_before = jnp.concatenate([sa + ea, sb + eb], axis=1)
            dz = e - jnp.exp(lb) * (e + e_before)
            if masked:
                dz = jnp.where(valid, dz, 0.0)
            dzb = dz.astype(BF16)
            dq = dq + _dot(dzb, kk)
            rk = _dot(dzb, qs, TN)
            rv = _dot(a.astype(BF16), do_blk, TN)
            dk_ref[0, rows, :] += jnp.where(lane_a, rk[:bk], rk[bk:])
            dv_ref[0, rows, :] += jnp.where(lane_a, rv[:bk], rv[bk:])
            la, lb_ = _halves(l1m)
            e_a, e_b = _halves(e)
            return dq, pa + _rowsum(la), pb + _rowsum(lb_), ea + _rowsum(e_a), eb + _rowsum(e_b)

        def group(g, c):
            for u in range(ATT_UNROLL):
                c = step(g * ATT_UNROLL + u, c, False)
            return c

        zcol = jnp.zeros((ATT_Q, 1), F32)
        carry = lax.fori_loop(0, qi, group, (jnp.zeros((ATT_Q, bk), F32), zcol, zcol, zcol, zcol))
        for u in range(ATT_UNROLL):
            carry = step(qi * ATT_UNROLL + u, carry, True)
        dq_ref[...] = (carry[0] * scale).astype(BF16)

    qblk, kblk, vblk, tri = _attn_specs(lp)
    return pl.pallas_call(
        body, name=name, grid=(HEAD_PAIRS, lp // ATT_Q),
        in_specs=[qblk, kblk, vblk, qblk, qblk, tri, tri],
        out_specs=[qblk, kblk, kblk],
        out_shape=[SDS((lp, d), BF16), SDS((HEAD_PAIRS // 2, lp, 2 * bk), F32), SDS((HEAD_PAIRS // 2, lp, 2 * bk), F32)],
        compiler_params=_cp(2))(q, kv, kv, do, tot, _pair_tri("upto"), _pair_tri("before"))


def _mesh_pos():
    return lax.axis_index("x"), lax.axis_index("y"), lax.axis_index("c")


def _flip(pos, r):
    x, y, c = pos
    return (1 - x if r & 4 else x, 1 - y if r & 2 else y, 1 - c if r & 1 else c)


def _dev_index(pos):
    return 4 * pos[0] + 2 * pos[1] + pos[2]


def _all_gather(x, name):
    shape = x.shape

    def body(x_ref, out_ref, send_sems, recv_sems, local_sem):
        me = _mesh_pos()
        sibling = _flip(me, 1)
        others = [_flip(me, 4), _flip(me, 2), _flip(me, 6)]

        def copy(k, block, to, src=None):
            slab = out_ref.at[_dev_index(block)]
            return pltpu.make_async_remote_copy(
                src_ref=slab if src is None else src, dst_ref=slab,
                send_sem=send_sems.at[k], recv_sem=recv_sems.at[k],
                device_id=to, device_id_type=pl.DeviceIdType.MESH)

        mine = pltpu.make_async_copy(x_ref, out_ref.at[_dev_index(me)], local_sem)
        mine.start()
        first = [copy(0, me, sibling, src=x_ref)] + [copy(1 + j, me, o, src=x_ref) for j, o in enumerate(others)]
        for cp in first:
            cp.start()
        passed = [copy(4 + j, o, sibling) for j, o in enumerate(others)]
        for j, o in enumerate(others):
            copy(1 + j, o, me).wait_recv()
            passed[j].start()
        copy(0, sibling, me).wait_recv()
        for j, o in enumerate(others):
            copy(4 + j, _flip(o, 1), me).wait_recv()
        for cp in first + passed:
            cp.wait_send()
        mine.wait()

    return pl.pallas_call(
        body, name=name, out_shape=SDS((N_DEV,) + shape, x.dtype),
        in_specs=[pl.BlockSpec(memory_space=pl.ANY)], out_specs=pl.BlockSpec(memory_space=pl.ANY),
        scratch_shapes=[pltpu.SemaphoreType.DMA((7,)), pltpu.SemaphoreType.DMA((7,)), pltpu.SemaphoreType.DMA(())],
    )(x)


def _scatter_exchange(p, name):
    def body(p_ref, out_ref, send_sems, recv_sems, local_sem):
        me = _mesh_pos()
        mine = pltpu.make_async_copy(p_ref.at[_dev_index(me)], out_ref.at[_dev_index(me)], local_sem)
        mine.start()
        copies = []
        for r in range(1, N_DEV):
            peer = _flip(me, r)
            copies.append(pltpu.make_async_remote_copy(
                src_ref=p_ref.at[_dev_index(peer)], dst_ref=out_ref.at[_dev_index(me)],
                send_sem=send_sems.at[r - 1], recv_sem=recv_sems.at[r - 1],
                device_id=peer, device_id_type=pl.DeviceIdType.MESH))
            copies[-1].start()
        for r in range(1, N_DEV):
            peer = _flip(me, r)
            landed = out_ref.at[_dev_index(peer)]
            pltpu.make_async_remote_copy(
                src_ref=landed, dst_ref=landed, send_sem=send_sems.at[r - 1], recv_sem=recv_sems.at[r - 1],
                device_id=peer, device_id_type=pl.DeviceIdType.MESH).wait_recv()
        for cp in copies:
            cp.wait_send()
        mine.wait()

    return pl.pallas_call(
        body, name=name, out_shape=SDS(p.shape, p.dtype),
        in_specs=[pl.BlockSpec(memory_space=pl.ANY)], out_specs=pl.BlockSpec(memory_space=pl.ANY),
        scratch_shapes=[pltpu.SemaphoreType.DMA((7,)), pltpu.SemaphoreType.DMA((7,)), pltpu.SemaphoreType.DMA(())],
    )(p)


def _sum_slabs(a, name):
    n, rows, cols = a.shape
    tr = _row_tile(rows, 64 if a.dtype == BF16 else 1024)

    def body(a_ref, o_ref):
        acc = a_ref[0].astype(F32)
        for k in range(1, n):
            acc = acc + a_ref[k].astype(F32)
        o_ref[...] = acc

    return pl.pallas_call(
        body, name=name, grid=(rows // tr,),
        in_specs=[pl.BlockSpec((n, tr, cols), lambda i: (0, i, 0))],
        out_specs=pl.BlockSpec((tr, cols), lambda i: (i, 0)),
        out_shape=SDS((rows, cols), F32), compiler_params=_cp(1))(a)


def _adamw(w, g, m, v, name):
    rows, cols = w.shape
    tr = _row_tile(rows, 352)

    def body(w_ref, g_ref, m_ref, v_ref, d_ref, mo_ref, vo_ref):
        g_ = g_ref[...]
        m_ = ADAM_B1 * m_ref[...] + (1.0 - ADAM_B1) * g_
        v_ = ADAM_B2 * v_ref[...] + (1.0 - ADAM_B2) * (g_ * g_)
        m_hat = m_ / (1.0 - ADAM_B1 ** ADAM_STEP)
        v_hat = v_ / (1.0 - ADAM_B2 ** ADAM_STEP)
        d_ref[...] = -ADAM_LR * (m_hat / (jnp.sqrt(v_hat) + ADAM_EPS) + ADAM_WD * w_ref[...])
        mo_ref[...] = m_
        vo_ref[...] = v_

    blk = pl.BlockSpec((tr, cols), lambda i: (i, 0))
    return pl.pallas_call(
        body, name=name, grid=(rows // tr,),
        in_specs=[blk] * 4, out_specs=[blk] * 3, out_shape=[SDS((rows, cols), F32)] * 3,
        compiler_params=_cp(1))(w, g, m, v)


def _ffn_fwd(h, gain, w_up, cw4, cb4, w_down4, tag):
    (n2,) = _rms_fwd(h, gain, f"ffn_norm_{tag}")
    up = _mm_group(n2, w_up, NN, F32, f"ffn_up_{tag}")
    up4 = up.reshape((2, 4) + up.shape[1:])
    act = _ffn_act_fwd(up4, cw4, cb4, f"ffn_act_{tag}")
    out = _mm_reduce(act, w_down4, NN, h, f"ffn_down_{tag}")
    return out, (n2, up4, act)


def _ffn_bwd(h, gain, w_up, cw4, cb4, w_down4, saved, dh, tag):
    n2, up4, act = saved
    dact = _mm_group(dh, w_down4, NT, F32, f"ffn_dact_{tag}")
    d_w_down = _mm_tn(act, dh[None], f"ffn_dwdown_{tag}")
    du4, dcw4, dcb4 = _ffn_act_bwd(up4, cw4, cb4, dact, f"ffn_dgate_{tag}")
    du = du4.reshape((8,) + du4.shape[2:])
    dup = _conv_bwd(du, cw4.reshape((8,) + cw4.shape[2:]), f"ffn_dconv_{tag}")
    dn2 = _mm_reduce(dup, w_up, NT, None, f"ffn_dnorm_{tag}")
    d_w_up = _mm_tn(n2[None], dup, f"ffn_dwup_{tag}")
    dh_in, dgain = _rms_bwd(h, gain, [dn2], dh, f"ffn_dh_{tag}")
    return dh_in, dgain, d_w_up, d_w_down, dcw4, dcb4


def kernel(x, meta_tokens, mix_norm, ffn_norm, pool_w, pool_scale, kv_norm, w_kv, w_q, w_o, ffn_w_up, ffn_conv_w, ffn_conv_b, ffn_w_down, final_norm, loss_target, m_meta_tokens, m_mix_norm, m_ffn_norm, m_pool_w, m_pool_scale, m_kv_norm, m_w_kv, m_w_q, m_w_o, m_ffn_w_up, m_ffn_conv_w, m_ffn_conv_b, m_ffn_w_down, m_final_norm, v_meta_tokens, v_mix_norm, v_ffn_norm, v_pool_w, v_pool_scale, v_kv_norm, v_w_kv, v_w_q, v_w_o, v_ffn_w_up, v_ffn_conv_w, v_ffn_conv_b, v_ffn_w_down, v_final_norm):
    seq, d = x.shape[1], x.shape[2]
    n_tok = N_META + seq
    lp = -(-n_tok // ROW_TILE) * ROW_TILE
    fc = ffn_w_up.shape[2]
    me = _dev_index(_mesh_pos())

    big_parts = [pool_w, w_kv, w_q, w_o, ffn_w_up, ffn_w_down]
    big_rows = [p.size // d for p in big_parts]
    big_off = [sum(big_rows[:k]) for k in range(len(big_parts) + 1)]
    local_big = jnp.concatenate([p.reshape(-1, d) for p in big_parts], axis=0).astype(BF16)
    gb = _all_gather(local_big, "gather_matrices")

    def big(k):
        return gb[:, big_off[k]:big_off[k + 1]]

    pw = big(0).reshape(N_DEV, 4, POOL_C // N_DEV, POOL_C).transpose(1, 0, 2, 3).reshape(4, POOL_C, POOL_C)
    wkv = big(1).reshape(N_DEV, d, 2 * d // N_DEV)
    wq = big(2).reshape(1, d, d)
    wo = big(3).reshape(1, d, d)
    wup = big(4).reshape(N_DEV, 2, d, fc)
    wdn = big(5).reshape(N_DEV, 2, fc // 2, d)
    wup_l = [wup[:, l] for l in range(2)]
    wdn_l = [wdn[:, l].reshape(4, fc, d) for l in range(2)]

    small_parts = [meta_tokens, pool_scale, ffn_conv_w]
    small_rows = [p.size // 128 for p in small_parts]
    small_pad = -sum(small_rows) % 8
    local_small = jnp.concatenate([p.reshape(-1, 128) for p in small_parts] + [jnp.zeros((small_pad, 128), F32)], axis=0)
    gs = _all_gather(local_small, "gather_vectors")
    r0, r1, r2 = small_rows[0], small_rows[0] + small_rows[1], sum(small_rows)
    meta_full = gs[:, :r0].transpose(1, 0, 2).reshape(N_META, d)
    pscale = gs[:, r0:r1].reshape(1, d)
    cw = gs[:, r1:r2].reshape(N_DEV, 2, 3, fc)
    cw4_l = [cw[:, l].reshape(2, 4, 3, fc) for l in range(2)]
    cb4_l = [ffn_conv_b[l].reshape(2, 4, 1, fc) for l in range(2)]

    h0 = jnp.concatenate([meta_full, x[0], jnp.zeros((lp - n_tok, d), F32)], axis=0)
    h1, diff = _pool_fwd(h0, mix_norm[0:1], pw, pscale, "pool_fwd")
    h2, saved0 = _ffn_fwd(h1, ffn_norm[0:1], wup_l[0], cw4_l[0], cb4_l[0], wdn_l[0], "0")
    gains_b = jnp.stack([kv_norm, mix_norm[1]], axis=0)
    kvn, n3 = _rms_fwd(h2, gains_b, "attn_norms")
    kv = _mm_group(kvn, wkv, NN, BF16, "kv_proj")
    q = _mm_group(n3, wq, NN, BF16, "q_proj")[0]
    o, tot = _attn_fwd(q, kv, "attn_fwd")
    h3 = _mm_reduce(o[None], wo, NN, h2, "o_proj")
    h4, saved1 = _ffn_fwd(h3, ffn_norm[1:2], wup_l[1], cw4_l[1], cb4_l[1], wdn_l[1], "1")
    target = jnp.pad(loss_target[0], ((N_META, lp - n_tok), (0, 0)))
    dh4, loss_blk, dg_final = _loss_bwd(h4, final_norm[None], target, seq, "loss")
    loss = lax.psum(loss_blk[0, 0], MESH_AXES)

    dh3, dg_ffn1, d_wup1, d_wdn1, dcw4_1, dcb4_1 = _ffn_bwd(
        h3, ffn_norm[1:2], wup_l[1], cw4_l[1], cb4_l[1], wdn_l[1], saved1, dh4, "1")
    d_o = _mm_group(dh3, wo, NT, BF16, "o_proj_dx")[0]
    d_wo = _mm_tn(o[None], dh3[None], "o_proj_dw")
    dq, dk, dv = _attn_bwd(q, kv, d_o, tot, "attn_bwd")
    dn3 = _mm_group(dq, wq, NT, F32, "q_proj_dx")[0]
    d_wq = _mm_tn(n3[None], dq[None], "q_proj_dw")
    dkv = jnp.concatenate([dk, dv], axis=0).astype(BF16)
    dkvn = _mm_reduce(dkv, wkv, NT, None, "kv_proj_dx")
    d_wkv = _mm_tn(kvn[None], dkv, "kv_proj_dw")
    dh2, dg_b = _rms_bwd(h2, gains_b, [dkvn, dn3], dh3, "attn_norms_bwd")
    dh1, dg_ffn0, d_wup0, d_wdn0, dcw4_0, dcb4_0 = _ffn_bwd(
        h1, ffn_norm[0:1], wup_l[0], cw4_l[0], cb4_l[0], wdn_l[0], saved0, dh2, "0")
    dh0, d_pw, d_pscale, dg_mix0 = _pool_bwd(h0, mix_norm[0:1], pw, pscale, diff, dh1, "pool_bwd")
    grad_x = dh0[N_META:n_tok][None]

    d_pw8 = d_pw.reshape(4, N_DEV, POOL_C // N_DEV, POOL_C).transpose(1, 0, 2, 3).reshape(N_DEV, -1, d).astype(BF16)
    d_wup8 = jnp.stack([d_wup0, d_wup1], axis=1).reshape(N_DEV, -1, d)
    d_wdn8 = jnp.stack([d_wdn0.reshape(N_DEV, fc // 2, d), d_wdn1.reshape(N_DEV, fc // 2, d)], axis=1).reshape(N_DEV, -1, d)
    partial_big = jnp.concatenate(
        [d_pw8, d_wkv.reshape(N_DEV, -1, d), d_wq.reshape(N_DEV, -1, d), d_wo.reshape(N_DEV, -1, d), d_wup8, d_wdn8], axis=1)
    g_big = _sum_slabs(_scatter_exchange(partial_big, "scatter_matrices"), "sum_matrices")

    rep_parts = [jnp.concatenate([dg_mix0, dg_b[1:2]], axis=0), jnp.concatenate([dg_ffn0, dg_ffn1], axis=0),
                 dg_b[0:1], dg_final, jnp.stack([dcb4_0.reshape(-1), dcb4_1.reshape(-1)], axis=0)]
    rep_shapes = [mix_norm.shape, ffn_norm.shape, kv_norm.shape, final_norm.shape, ffn_conv_b.shape]
    rep_rows = [p.size // 128 for p in rep_parts]
    d_meta8 = dh0[:N_META].reshape(N_META, N_DEV, d // N_DEV).transpose(1, 0, 2).reshape(N_DEV, -1, 128)
    d_cw8 = jnp.stack([dcw4_0.reshape(N_DEV, 3, fc), dcw4_1.reshape(N_DEV, 3, fc)], axis=1).reshape(N_DEV, -1, 128)
    shard_parts = jnp.concatenate([d_meta8, d_pscale.reshape(N_DEV, 1, 128), d_cw8], axis=1)
    n_rep = sum(rep_rows)
    partial_small = jnp.concatenate([p.reshape(-1, 128) for p in rep_parts] + [shard_parts.reshape(-1, 128)], axis=0)
    g_small = _sum_slabs(_all_gather(partial_small, "gather_vector_grads"), "sum_vectors")
    g_rep = [g_small[sum(rep_rows[:k]):sum(rep_rows[:k + 1])].reshape(s) for k, s in enumerate(rep_shapes)]
    g_shard = lax.dynamic_index_in_dim(g_small[n_rep:].reshape(N_DEV, -1, 128), me, 0, keepdims=False)
    g_meta = g_shard[:r0].reshape(meta_tokens.shape)
    g_pscale = g_shard[r0:r1].reshape(pool_scale.shape)
    g_cw = g_shard[r1:r2].reshape(ffn_conv_w.shape)

    grads = {
        "meta_tokens": g_meta, "mix_norm": g_rep[0], "ffn_norm": g_rep[1],
        "pool_w": g_big[big_off[0]:big_off[1]].reshape(pool_w.shape), "pool_scale": g_pscale, "kv_norm": g_rep[2],
        "w_kv": g_big[big_off[1]:big_off[2]].reshape(w_kv.shape), "w_q": g_big[big_off[2]:big_off[3]].reshape(w_q.shape),
        "w_o": g_big[big_off[3]:big_off[4]].reshape(w_o.shape),
        "ffn_w_up": g_big[big_off[4]:big_off[5]].reshape(ffn_w_up.shape), "ffn_conv_w": g_cw, "ffn_conv_b": g_rep[4],
        "ffn_w_down": g_big[big_off[5]:big_off[6]].reshape(ffn_w_down.shape), "final_norm": g_rep[3],
    }
    names = list(grads)
    weights = dict(zip(names, [meta_tokens, mix_norm, ffn_norm, pool_w, pool_scale, kv_norm, w_kv, w_q, w_o,
                               ffn_w_up, ffn_conv_w, ffn_conv_b, ffn_w_down, final_norm]))
    mom1 = dict(zip(names, [m_meta_tokens, m_mix_norm, m_ffn_norm, m_pool_w, m_pool_scale, m_kv_norm, m_w_kv, m_w_q,
                            m_w_o, m_ffn_w_up, m_ffn_conv_w, m_ffn_conv_b, m_ffn_w_down, m_final_norm]))
    mom2 = dict(zip(names, [v_meta_tokens, v_mix_norm, v_ffn_norm, v_pool_w, v_pool_scale, v_kv_norm, v_w_kv, v_w_q,
                            v_w_o, v_ffn_w_up, v_ffn_conv_w, v_ffn_conv_b, v_ffn_w_down, v_final_norm]))

    delta, new_m, new_v = {}, {}, {}
    matrices = ["pool_w", "w_kv", "w_q", "w_o", "ffn_w_up", "ffn_w_down"]
    for n in matrices:
        shape = weights[n].shape
        flat = (-1, shape[-1])
        dl, nm, nv = _adamw(weights[n].reshape(flat), grads[n].reshape(flat), mom1[n].reshape(flat),
                            mom2[n].reshape(flat), "adamw_" + n)
        delta[n], new_m[n], new_v[n] = dl.reshape(shape), nm.reshape(shape), nv.reshape(shape)
    vectors = [n for n in names if n not in matrices]
    vec_rows = [weights[n].size // 128 for n in vectors]
    vec_pad = -sum(vec_rows) % 8

    def pack(tree):
        return jnp.concatenate([tree[n].reshape(-1, 128) for n in vectors] + [jnp.ones((vec_pad, 128), F32)], axis=0)

    outs = _adamw(pack(weights), pack(grads), pack(mom1), pack(mom2), "adamw_vectors")
    for tree, packed in zip((delta, new_m, new_v), outs):
        for k, n in enumerate(vectors):
            tree[n] = packed[sum(vec_rows[:k]):sum(vec_rows[:k + 1])].reshape(weights[n].shape)

    return (loss, grad_x, *[grads[n] for n in names], *[delta[n] for n in names],
            *[new_m[n] for n in names], *[new_v[n] for n in names])
```

```python
import functools

import jax
import jax.numpy as jnp
from jax import lax
from jax.experimental import pallas as pl
from jax.experimental.pallas import tpu as pltpu

F32 = jnp.float32
BF16 = jnp.bfloat16
SDS = jax.ShapeDtypeStruct

N_DEV = 8
N_META = 16
HEAD_DIM = 64
HEAD_PAIRS = 8
RMS_EPS = 1e-6
LOG2_E = 1.4426950408889634
POOL_WINDOWS = (2, 4, 8, 16)
POOL_C = 256
POOL_HALO = 16
CONV_HALO = 8
ROW_TILE = 384
MM_ROWS_MAX = 1408
ATT_BLK = 128
ATT_Q = ROW_TILE
ATT_UNROLL = ATT_Q // ATT_BLK
VMEM_LIMIT = 56 * 1024 * 1024

ADAM_LR = 0.001
ADAM_B1 = 0.9
ADAM_B2 = 0.999
ADAM_EPS = 1e-08
ADAM_WD = 0.01
ADAM_STEP = 10

MESH_AXES = ("x", "y", "c")
NN = (((1,), (0,)), ((), ()))
NT = (((1,), (1,)), ((), ()))
TN = (((0,), (0,)), ((), ()))


def _cp(n_axes):
    return pltpu.CompilerParams(dimension_semantics=("arbitrary",) * n_axes, vmem_limit_bytes=VMEM_LIMIT)


def _dot(a, b, dims=NN):
    return lax.dot_general(a, b, dims, preferred_element_type=F32)


def _rstd(x):
    return lax.rsqrt(jnp.mean(x * x, axis=-1, keepdims=True) + RMS_EPS)


def _row_tile(rows, cap=512):
    if rows <= cap:
        return rows
    best = 8
    for t in range(8, cap + 1, 8):
        if rows % t == 0:
            best = t
    assert rows % best == 0
    return best


def _rms_fwd(h, gains, name):
    lp, d = h.shape
    k = gains.shape[0]
    tm = ROW_TILE

    def body(h_ref, g_ref, *o_refs):
        x = h_ref[...]
        u = x * _rstd(x)
        for j in range(k):
            o_refs[j][...] = (u * g_ref[j:j + 1, :]).astype(BF16)

    row = pl.BlockSpec((tm, d), lambda i: (i, 0))
    return pl.pallas_call(
        body, name=name, grid=(lp // tm,),
        in_specs=[row, pl.BlockSpec((k, d), lambda i: (0, 0))],
        out_specs=[row] * k, out_shape=[SDS((lp, d), BF16)] * k,
        compiler_params=_cp(1))(h, gains)


def _rms_bwd(h, gains, dns, dh_in, name):
    lp, d = h.shape
    k = gains.shape[0]
    tm = ROW_TILE

    def body(h_ref, g_ref, *refs):
        dn_refs, dh_ref, dho_ref, dg_ref = refs[:k], refs[k], refs[k + 1], refs[k + 2]
        i = pl.program_id(0)
        x = h_ref[...]
        r = _rstd(x)
        u = x * r
        du = jnp.zeros_like(x)
        rows = []
        for j in range(k):
            dn = dn_refs[j][...]
            du = du + dn * g_ref[j:j + 1, :]
            rows.append(jnp.sum(dn * u, axis=0, keepdims=True))
        dx = r * (du - u * jnp.mean(du * u, axis=-1, keepdims=True))
        dho_ref[...] = dh_ref[...] + dx

        @pl.when(i == 0)
        def _():
            for j in range(k):
                dg_ref[j:j + 1, :] = rows[j]

        @pl.when(i > 0)
        def _():
            for j in range(k):
                dg_ref[j:j + 1, :] += rows[j]

    row = pl.BlockSpec((tm, d), lambda i: (i, 0))
    vec = pl.BlockSpec((k, d), lambda i: (0, 0))
    return pl.pallas_call(
        body, name=name, grid=(lp // tm,),
        in_specs=[row, vec] + [row] * k + [row],
        out_specs=[row, vec], out_shape=[SDS((lp, d), F32), SDS((k, d), F32)],
        compiler_params=_cp(1))(h, gains, *dns, dh_in)


def _loss_bwd(h, gain, target, n_real, name):
    lp, d = h.shape
    tm = ROW_TILE

    def body(h_ref, g_ref, t_ref, dh_ref, loss_ref, dg_ref):
        i = pl.program_id(0)
        x = h_ref[...]
        g = g_ref[...]
        r = _rstd(x)
        u = x * r
        row = i * tm + lax.broadcasted_iota(jnp.int32, (tm, 1), 0)
        valid = (row >= N_META) & (row < N_META + n_real)
        e = jnp.where(valid, u * g - t_ref[...], 0.0)
        part = 0.5 * jnp.sum(jnp.sum(e * e, axis=-1, keepdims=True), axis=0, keepdims=True) * (1.0 / d)
        dy = e * (1.0 / d)
        du = dy * g
        dh_ref[...] = r * (du - u * jnp.mean(du * u, axis=-1, keepdims=True))
        dgp = jnp.sum(dy * u, axis=0, keepdims=True)

        @pl.when(i == 0)
        def _():
            loss_ref[...] = jnp.broadcast_to(part, (8, 128))
            dg_ref[...] = dgp

        @pl.when(i > 0)
        def _():
            loss_ref[...] += jnp.broadcast_to(part, (8, 128))
            dg_ref[...] += dgp

    row = pl.BlockSpec((tm, d), lambda i: (i, 0))
    vec = pl.BlockSpec((1, d), lambda i: (0, 0))
    return pl.pallas_call(
        body, name=name, grid=(lp // tm,),
        in_specs=[row, vec, row],
        out_specs=[row, pl.BlockSpec((8, 128), lambda i: (0, 0)), vec],
        out_shape=[SDS((lp, d), F32), SDS((8, 128), F32), SDS((1, d), F32)],
        compiler_params=_cp(1))(h, gain, target)


def _pool_fwd(h, gain, w, scale, name):
    lp, d = h.shape
    tm = ROW_TILE
    hb = POOL_HALO

    def body(h_ref, halo_ref, g_ref, w_ref, s_ref, h1_ref, diff_ref):
        i = pl.program_id(0)
        g = g_ref[...]
        x = h_ref[...]
        n = x * _rstd(x) * g
        xh = halo_ref[...]
        nh = jnp.where(i > 0, xh * _rstd(xh) * g, 0.0)
        cur = jnp.concatenate([nh, n], axis=0)
        pos = i * tm + lax.broadcasted_iota(jnp.int32, (tm, 1), 0)
        for gi, win in enumerate(POOL_WINDOWS):
            if gi > 0:
                cur = cur[:, POOL_C:]
            cur = cur + pltpu.roll(cur, win // 2, 0)
            c0 = gi * POOL_C
            count = jnp.minimum(pos + 1, win).astype(F32)
            diff = cur[hb:, :POOL_C] / count - n[:, c0:c0 + POOL_C]
            diff = diff.astype(BF16)
            y = _dot(diff, w_ref[gi])
            h1_ref[:, c0:c0 + POOL_C] = x[:, c0:c0 + POOL_C] + y * s_ref[:, c0:c0 + POOL_C]
            diff_ref[:, c0:c0 + POOL_C] = diff

    row = pl.BlockSpec((tm, d), lambda i: (i, 0))
    halo = pl.BlockSpec((hb, d), lambda i: (jnp.maximum(i * (tm // hb) - 1, 0), 0))
    vec = pl.BlockSpec((1, d), lambda i: (0, 0))
    return pl.pallas_call(
        body, name=name, grid=(lp // tm,),
        in_specs=[row, halo, vec, pl.BlockSpec(w.shape, lambda i: (0, 0, 0)), vec],
        out_specs=[row, row], out_shape=[SDS((lp, d), F32), SDS((lp, d), BF16)],
        compiler_params=_cp(1))(h, h, gain, w, scale)


def _pool_bwd(h, gain, w, scale, diff, dh1, name):
    lp, d = h.shape
    tm = ROW_TILE
    hb = POOL_HALO
    nblk = lp // tm
    ext = tm + hb

    def body(h_ref, g_ref, w_ref, s_ref, diff_ref, dh_ref, dhn_ref, dh0_ref, dw_ref, ds_ref, dg_ref):
        i = pl.program_id(0)
        g = g_ref[...]
        x = h_ref[...]
        r = _rstd(x)
        u = x * r
        dh = dh_ref[...]
        dhn = jnp.where(i < nblk - 1, dhn_ref[...], 0.0)
        dyp = jnp.concatenate([dh, dhn], axis=0) * s_ref[...]
        pos = i * tm + lax.broadcasted_iota(jnp.int32, (ext, 1), 0)
        dn_parts, dw_parts, ds_parts = [], [], []
        for gi, win in enumerate(POOL_WINDOWS):
            c0 = gi * POOL_C
            wg = w_ref[gi]
            dyp_g = dyp[:, c0:c0 + POOL_C].astype(BF16)
            dd = _dot(dyp_g, wg, NT)
            dfg = diff_ref[:, c0:c0 + POOL_C]
            dw_parts.append(_dot(dfg, dyp_g[:tm], TN))
            ds_parts.append(jnp.sum(dh[:, c0:c0 + POOL_C] * _dot(dfg, wg), axis=0, keepdims=True))
            count = jnp.minimum(pos + 1, win).astype(F32)
            cur = dd / count
            sh = 1
            while sh < win:
                cur = cur + pltpu.roll(cur, ext - sh, 0)
                sh *= 2
            dn_parts.append(cur[:tm] - dd[:tm])
        dn = jnp.concatenate(dn_parts, axis=1)
        du = dn * g
        dh0_ref[...] = dh + r * (du - u * jnp.mean(du * u, axis=-1, keepdims=True))
        dgp = jnp.sum(dn * u, axis=0, keepdims=True)
        dsp = jnp.concatenate(ds_parts, axis=1)

        @pl.when(i == 0)
        def _():
            for gi in range(len(POOL_WINDOWS)):
                dw_ref[gi] = dw_parts[gi]
            ds_ref[...] = dsp
            dg_ref[...] = dgp

        @pl.when(i > 0)
        def _():
            for gi in range(len(POOL_WINDOWS)):
                dw_ref[gi] += dw_parts[gi]
            ds_ref[...] += dsp
            dg_ref[...] += dgp

    row = pl.BlockSpec((tm, d), lambda i: (i, 0))
    nxt = pl.BlockSpec((hb, d), lambda i: (jnp.minimum((i + 1) * (tm // hb), lp // hb - 1), 0))
    vec = pl.BlockSpec((1, d), lambda i: (0, 0))
    wsp = pl.BlockSpec(w.shape, lambda i: (0, 0, 0))
    return pl.pallas_call(
        body, name=name, grid=(nblk,),
        in_specs=[row, vec, wsp, vec, row, row, nxt],
        out_specs=[row, wsp, vec, vec],
        out_shape=[SDS((lp, d), F32), SDS(w.shape, F32), SDS((1, d), F32), SDS((1, d), F32)],
        compiler_params=_cp(1))(h, gain, w, scale, diff, dh1, dh1)


def _conv_taps(x, halo, first):
    ext = jnp.concatenate([jnp.where(first, 0.0, halo), x], axis=0)
    return pltpu.roll(ext, 1, 0)[CONV_HALO:], pltpu.roll(ext, 2, 0)[CONV_HALO:]


def _ffn_specs(tm, c, lp):
    blk = pl.BlockSpec((2, 1, tm, c), lambda g, i: (0, g, i, 0))
    halo = pl.BlockSpec((2, 1, CONV_HALO, c), lambda g, i: (0, g, jnp.maximum(i * (tm // CONV_HALO) - 1, 0), 0))
    cw = pl.BlockSpec((2, 1, 3, c), lambda g, i: (0, g, 0, 0))
    cb = pl.BlockSpec((2, 1, 1, c), lambda g, i: (0, g, 0, 0))
    return blk, halo, cw, cb


def _ffn_act_fwd(up4, cw4, cb4, name):
    _, ng, lp, c = up4.shape
    tm = ROW_TILE

    def body(up_ref, halo_ref, cw_ref, cb_ref, act_ref):
        first = pl.program_id(1) == 0
        u = []
        for half in range(2):
            x = up_ref[half, 0]
            xm1, xm2 = _conv_taps(x, halo_ref[half, 0], first)
            u.append(cb_ref[half, 0] + cw_ref[half, 0, 0:1, :] * xm2 + cw_ref[half, 0, 1:2, :] * xm1
                     + cw_ref[half, 0, 2:3, :] * x)
        gate, val = u
        sig = 1.0 / (1.0 + jnp.exp(-gate))
        act_ref[0] = (gate * sig * val).astype(BF16)

    blk, halo, cw, cb = _ffn_specs(tm, c, lp)
    return pl.pallas_call(
        body, name=name, grid=(ng, lp // tm),
        in_specs=[blk, halo, cw, cb],
        out_specs=pl.BlockSpec((1, tm, c), lambda g, i: (g, i, 0)),
        out_shape=SDS((ng, lp, c), BF16), compiler_params=_cp(2))(up4, up4, cw4, cb4)


def _ffn_act_bwd(up4, cw4, cb4, dact, name):
    _, ng, lp, c = up4.shape
    tm = ROW_TILE

    def body(up_ref, halo_ref, cw_ref, cb_ref, da_ref, du_ref, dcw_ref, dcb_ref):
        i = pl.program_id(1)
        first = i == 0
        u, taps = [], []
        for half in range(2):
            x = up_ref[half, 0]
            xm1, xm2 = _conv_taps(x, halo_ref[half, 0], first)
            u.append(cb_ref[half, 0] + cw_ref[half, 0, 0:1, :] * xm2 + cw_ref[half, 0, 1:2, :] * xm1
                     + cw_ref[half, 0, 2:3, :] * x)
            taps.append((xm2, xm1, x))
        gate, val = u
        sig = 1.0 / (1.0 + jnp.exp(-gate))
        da = da_ref[0]
        dus = (da * val * (sig * (1.0 + gate * (1.0 - sig))), da * (gate * sig))
        sums = []
        for half in range(2):
            du_ref[half, 0] = dus[half]
            sums.append([jnp.sum(dus[half] * t, axis=0, keepdims=True) for t in taps[half]]
                        + [jnp.sum(dus[half], axis=0, keepdims=True)])

        @pl.when(first)
        def _():
            for half in range(2):
                for k in range(3):
                    dcw_ref[half, 0, k:k + 1, :] = sums[half][k]
                dcb_ref[half, 0] = sums[half][3]

        @pl.when(i > 0)
        def _():
            for half in range(2):
                for k in range(3):
                    dcw_ref[half, 0, k:k + 1, :] += sums[half][k]
                dcb_ref[half, 0] += sums[half][3]

    blk, halo, cw, cb = _ffn_specs(tm, c, lp)
    return pl.pallas_call(
        body, name=name, grid=(ng, lp // tm),
        in_specs=[blk, halo, cw, cb, pl.BlockSpec((1, tm, c), lambda g, i: (g, i, 0))],
        out_specs=[blk, cw, cb],
        out_shape=[SDS(up4.shape, F32), SDS(cw4.shape, F32), SDS(cb4.shape, F32)],
        compiler_params=_cp(2))(up4, up4, cw4, cb4, dact)


def _conv_bwd(du, cw, name):
    ng, lp, c = du.shape
    tm = ROW_TILE
    nblk = lp // tm
    ext = tm + CONV_HALO

    def body(du_ref, nxt_ref, cw_ref, out_ref):
        i = pl.program_id(1)
        x = du_ref[0]
        full = jnp.concatenate([x, jnp.where(i < nblk - 1, nxt_ref[0], 0.0)], axis=0)
        xp1 = pltpu.roll(full, ext - 1, 0)[:tm]
        xp2 = pltpu.roll(full, ext - 2, 0)[:tm]
        out_ref[0] = (cw_ref[0, 2:3, :] * x + cw_ref[0, 1:2, :] * xp1 + cw_ref[0, 0:1, :] * xp2).astype(BF16)

    blk = pl.BlockSpec((1, tm, c), lambda g, i: (g, i, 0))
    nxt = pl.BlockSpec((1, CONV_HALO, c),
                       lambda g, i: (g, jnp.minimum((i + 1) * (tm // CONV_HALO), lp // CONV_HALO - 1), 0))
    return pl.pallas_call(
        body, name=name, grid=(ng, nblk),
        in_specs=[blk, nxt, pl.BlockSpec((1, 3, c), lambda g, i: (g, 0, 0))],
        out_specs=blk, out_shape=SDS((ng, lp, c), BF16), compiler_params=_cp(2))(du, du, cw)


def _mm_tile(rows):
    return _row_tile(rows, MM_ROWS_MAX)


def _mm_group(a, b, dims, out_dtype, name):
    m, k = a.shape
    ng = b.shape[0]
    n = b.shape[2] if dims == NN else b.shape[1]
    tm = _mm_tile(m)

    def body(a_ref, b_ref, o_ref):
        o_ref[0] = _dot(a_ref[...].astype(BF16), b_ref[0], dims).astype(out_dtype)

    return pl.pallas_call(
        body, name=name, grid=(ng, m // tm),
        in_specs=[pl.BlockSpec((tm, k), lambda g, i: (i, 0)),
                  pl.BlockSpec((1,) + b.shape[1:], lambda g, i: (g, 0, 0))],
        out_specs=pl.BlockSpec((1, tm, n), lambda g, i: (g, i, 0)),
        out_shape=SDS((ng, m, n), out_dtype), compiler_params=_cp(2))(a, b)


def _mm_reduce(a, b, dims, res, name):
    ng, m, k = a.shape
    n = b.shape[2] if dims == NN else b.shape[1]
    tm = _mm_tile(m)
    has_res = res is not None

    def body(a_ref, b_ref, *refs):
        o_ref, acc_ref = refs[-2], refs[-1]
        g = pl.program_id(1)
        p = _dot(a_ref[0].astype(BF16), b_ref[0], dims)

        @pl.when(g == 0)
        def _():
            acc_ref[...] = p + refs[0][...] if has_res else p

        @pl.when(g > 0)
        def _():
            acc_ref[...] += p

        @pl.when(g == ng - 1)
        def _():
            o_ref[...] = acc_ref[...]

    row = pl.BlockSpec((tm, n), lambda i, g: (i, 0))
    return pl.pallas_call(
        body, name=name, grid=(m // tm, ng),
        in_specs=[pl.BlockSpec((1, tm, k), lambda i, g: (g, i, 0)),
                  pl.BlockSpec((1,) + b.shape[1:], lambda i, g: (g, 0, 0))] + ([row] if has_res else []),
        out_specs=row, out_shape=SDS((m, n), F32),
        scratch_shapes=[pltpu.VMEM((tm, n), F32)], compiler_params=_cp(2))(a, b, *([res] if has_res else []))


def _mm_tn(a, b, name):
    ga, m, ka = a.shape
    gb, _, n = b.shape
    ng = max(ga, gb)
    tk = _mm_tile(m)
    nk = m // tk

    def body(a_ref, b_ref, o_ref, acc_ref):
        s = pl.program_id(1)
        p = _dot(a_ref[0].astype(BF16), b_ref[0].astype(BF16), TN)

        @pl.when(s == 0)
        def _():
            acc_ref[...] = p

        @pl.when(s > 0)
        def _():
            acc_ref[...] += p

        @pl.when(s == nk - 1)
        def _():
            o_ref[0] = acc_ref[...].astype(BF16)

    return pl.pallas_call(
        body, name=name, grid=(ng, nk),
        in_specs=[pl.BlockSpec((1, tk, ka), (lambda g, s: (g, s, 0)) if ga > 1 else (lambda g, s: (0, s, 0))),
                  pl.BlockSpec((1, tk, n), (lambda g, s: (g, s, 0)) if gb > 1 else (lambda g, s: (0, s, 0)))],
        out_specs=pl.BlockSpec((1, ka, n), lambda g, s: (g, 0, 0)),
        out_shape=SDS((ng, ka, n), BF16),
        scratch_shapes=[pltpu.VMEM((ka, n), F32)], compiler_params=_cp(2))(a, b)


def _pair_tri(kind, sign):
    r = jnp.arange(2 * ATT_BLK)[:, None]
    c = jnp.arange(2 * ATT_BLK)[None, :]
    same = (r < ATT_BLK) == (c < ATT_BLK)
    rel = {"from": r >= c, "before": r < c}[kind]
    return ((same & rel) * sign).astype(BF16)


def _scan_dot(x, tri):
    hi = x.astype(BF16)
    lo = (x - hi.astype(F32)).astype(BF16)
    return _dot(hi, tri) + _dot(lo, tri)


def _split_heads(blk, lane_a):
    zero = jnp.zeros_like(blk)
    return jnp.concatenate([jnp.where(lane_a, blk, zero), jnp.where(lane_a, zero, blk)], axis=0)


def _softplus(z):
    return jnp.maximum(z, 0.0) + jnp.log(1.0 + jnp.exp2(jnp.abs(z) * (-LOG2_E)))


def _visible(qi, j):
    t = qi * ATT_Q + lax.broadcasted_iota(jnp.int32, (ATT_Q, 2 * ATT_BLK), 0)
    s = j * ATT_BLK + (lax.broadcasted_iota(jnp.int32, (ATT_Q, 2 * ATT_BLK), 1) & (ATT_BLK - 1))
    return s < t


def _halves(x):
    return x[:, :ATT_BLK], x[:, ATT_BLK:]


def _rowsum(x):
    return jnp.sum(x, axis=1, keepdims=True)


def _attn_specs(lp):
    bk = ATT_BLK
    qblk = pl.BlockSpec((ATT_Q, bk), lambda p, i: (i, p))
    kblk = pl.BlockSpec((1, lp, bk), lambda p, i: (p // 2, 0, p % 2))
    vblk = pl.BlockSpec((1, lp, bk), lambda p, i: (HEAD_PAIRS // 2 + p // 2, 0, p % 2))
    tri = pl.BlockSpec((2 * bk, 2 * bk), lambda p, i: (0, 0))
    return qblk, kblk, vblk, tri


def _attn_fwd(q, kv, name):
    lp, d = q.shape
    bk = ATT_BLK

    def body(q_ref, k_ref, v_ref, tri_ref, o_ref, t_ref):
        qi = pl.program_id(1)
        qs = q_ref[...] * (HEAD_DIM ** -0.5)
        lane_a = lax.broadcasted_iota(jnp.int32, (1, bk), 1) < HEAD_DIM
        tri = tri_ref[...]

        def trip(js, carry, masked):
            oacc, ca, cb = carry
            rows = [pl.ds(pl.multiple_of(j * bk, bk), bk) for j in js]
            zs = [_dot(qs, _split_heads(k_ref[0, r, :], lane_a), NT) for r in rows]
            ms = [_softplus(z) for z in zs]
            if masked:
                ms = [jnp.where(_visible(qi, j), m, 0.0) for j, m in zip(js, ms)]
            ws = [_scan_dot(m, tri) for m in ms]
            for j, r, z, m, w in zip(js, rows, zs, ms, ws):
                exa, exb = _halves(z + w)
                a = jnp.concatenate([jnp.exp(exa - ca), jnp.exp(exb - cb)], axis=1)
                if masked:
                    a = jnp.where(_visible(qi, j), a, 0.0)
                oacc = oacc + _dot(a.astype(BF16), _split_heads(v_ref[0, r, :], lane_a))
                ma, mb = _halves(m)
                ca, cb = ca + _rowsum(ma), cb + _rowsum(mb)
            return oacc, ca, cb

        carry = (jnp.zeros((ATT_Q, bk), F32), jnp.zeros((ATT_Q, 1), F32), jnp.zeros((ATT_Q, 1), F32))
        top = (qi + 1) * ATT_UNROLL - 1
        carry = trip([top - u for u in range(ATT_UNROLL)], carry, True)
        oacc, ca, cb = lax.fori_loop(
            0, qi, lambda g, c: trip([top - (g + 1) * ATT_UNROLL - u for u in range(ATT_UNROLL)], c, False), carry)
        o_ref[...] = oacc.astype(BF16)
        t_ref[...] = jnp.where(lane_a, ca, cb)

    qblk, kblk, vblk, tri = _attn_specs(lp)
    return pl.pallas_call(
        body, name=name, grid=(HEAD_PAIRS, lp // ATT_Q),
        in_specs=[qblk, kblk, vblk, tri],
        out_specs=[qblk, qblk], out_shape=[SDS((lp, d), BF16), SDS((lp, d), F32)],
        compiler_params=_cp(2))(q, kv, kv, _pair_tri("from", -1))


def _attn_bwd(q, kv, do, tot, name):
    lp, d = q.shape
    bk = ATT_BLK
    scale = HEAD_DIM ** -0.5

    def body(q_ref, k_ref, v_ref, do_ref, t_ref, tri_ref, dq_ref, dk_ref, dv_ref):
        qi = pl.program_id(1)

        @pl.when(qi == 0)
        def _():
            dk_ref[...] = jnp.zeros_like(dk_ref)
            dv_ref[...] = jnp.zeros_like(dv_ref)

        qs = q_ref[...] * scale
        do_blk = do_ref[...]
        lane_a = lax.broadcasted_iota(jnp.int32, (1, bk), 1) < HEAD_DIM
        tot_blk = t_ref[...]
        ta = jnp.max(jnp.where(lane_a, tot_blk, -jnp.inf), axis=1, keepdims=True)
        tb = jnp.max(jnp.where(lane_a, -jnp.inf, tot_blk), axis=1, keepdims=True)
        tri = tri_ref[...]

        def trip(js, carry, masked):
            dq, pa, pb, ea, eb = carry
            rows = [pl.ds(pl.multiple_of(j * bk, bk), bk) for j in js]
            kks = [_split_heads(k_ref[0, r, :], lane_a) for r in rows]
            zs = [_dot(qs, kk, NT) for kk in kks]
            das = [_dot(do_blk, _split_heads(v_ref[0, r, :], lane_a), NT) for r in rows]
            ms = [_softplus(z) for z in zs]
            if masked:
                ms = [jnp.where(_visible(qi, j), m, 0.0) for j, m in zip(js, ms)]
            xs = [_scan_dot(m, tri) for m in ms]
            es, a_bf = [], []
            for j, z, m, x, da in zip(js, zs, ms, xs, das):
                xa, xb = _halves(z + x)
                a = jnp.concatenate([jnp.exp(xa + pa), jnp.exp(xb + pb)], axis=1)
                if masked:
                    a = jnp.where(_visible(qi, j), a, 0.0)
                a_bf.append(a.astype(BF16))
                es.append(a * da)
                ma, mb = _halves(m)
                pa, pb = pa + _rowsum(ma), pb + _rowsum(mb)
            ss = [_scan_dot(e, tri) for e in es]
            for j, r, kk, z, m, e, s, ab in zip(js, rows, kks, zs, ms, es, ss, a_bf):
                sa, sb = _halves(s)
                e_before = jnp.concatenate([sa + ea, sb + eb], axis=1)
                dz = e - jnp.exp(z - m) * (e + e_before)
                if masked:
                    dz = jnp.where(_visible(qi, j), dz, 0.0)
                dzb = dz.astype(BF16)
                dq = dq + _dot(dzb, kk)
                rk = _dot(dzb, qs, TN)
                rv = _dot(ab, do_blk, TN)
                dk_ref[0, r, :] += jnp.where(lane_a, rk[:bk], rk[bk:])
                dv_ref[0, r, :] += jnp.where(lane_a, rv[:bk], rv[bk:])
                e_a, e_b = _halves(e)
                ea, eb = ea + _rowsum(e_a), eb + _rowsum(e_b)
            return dq, pa, pb, ea, eb

        zcol = jnp.zeros((ATT_Q, 1), F32)
        carry = lax.fori_loop(
            0, qi, lambda g, c: trip([g * ATT_UNROLL + u for u in range(ATT_UNROLL)], c, False),
            (jnp.zeros((ATT_Q, bk), F32), -ta, -tb, zcol, zcol))
        carry = trip([qi * ATT_UNROLL + u for u in range(ATT_UNROLL)], carry, True)
        dq_ref[...] = (carry[0] * scale).astype(BF16)

    qblk, kblk, vblk, tri = _attn_specs(lp)
    return pl.pallas_call(
        body, name=name, grid=(HEAD_PAIRS, lp // ATT_Q),
        in_specs=[qblk, kblk, vblk, qblk, qblk, tri],
        out_specs=[qblk, kblk, kblk],
        out_shape=[SDS((lp, d), BF16), SDS((HEAD_PAIRS // 2, lp, 2 * bk), F32), SDS((HEAD_PAIRS // 2, lp, 2 * bk), F32)],
        compiler_params=_cp(2))(q, kv, kv, do, tot, _pair_tri("before", 1))


def _mesh_pos():
    return lax.axis_index("x"), lax.axis_index("y"), lax.axis_index("c")


def _flip(pos, r):
    x, y, c = pos
    return (1 - x if r & 4 else x, 1 - y if r & 2 else y, 1 - c if r & 1 else c)


def _dev_index(pos):
    return 4 * pos[0] + 2 * pos[1] + pos[2]


def _all_gather(x, name):
    shape = x.shape

    def body(x_ref, out_ref, send_sems, recv_sems, local_sem):
        me = _mesh_pos()
        sibling = _flip(me, 1)
        others = [_flip(me, 4), _flip(me, 2), _flip(me, 6)]

        def copy(k, block, to, src=None):
            slab = out_ref.at[_dev_index(block)]
            return pltpu.make_async_remote_copy(
                src_ref=slab if src is None else src, dst_ref=slab,
                send_sem=send_sems.at[k], recv_sem=recv_sems.at[k],
                device_id=to, device_id_type=pl.DeviceIdType.MESH)

        mine = pltpu.make_async_copy(x_ref, out_ref.at[_dev_index(me)], local_sem)
        mine.start()
        first = [copy(0, me, sibling, src=x_ref)] + [copy(1 + j, me, o, src=x_ref) for j, o in enumerate(others)]
        for cp in first:
            cp.start()
        passed = [copy(4 + j, o, sibling) for j, o in enumerate(others)]
        for j, o in enumerate(others):
            copy(1 + j, o, me).wait_recv()
            passed[j].start()
        copy(0, sibling, me).wait_recv()
        for j, o in enumerate(others):
            copy(4 + j, _flip(o, 1), me).wait_recv()
        for cp in first + passed:
            cp.wait_send()
        mine.wait()

    return pl.pallas_call(
        body, name=name, out_shape=SDS((N_DEV,) + shape, x.dtype),
        in_specs=[pl.BlockSpec(memory_space=pl.ANY)], out_specs=pl.BlockSpec(memory_space=pl.ANY),
        scratch_shapes=[pltpu.SemaphoreType.DMA((7,)), pltpu.SemaphoreType.DMA((7,)), pltpu.SemaphoreType.DMA(())],
    )(x)


def _scatter_exchange(p, name):
    def body(p_ref, out_ref, send_sems, recv_sems, local_sem):
        me = _mesh_pos()
        mine = pltpu.make_async_copy(p_ref.at[_dev_index(me)], out_ref.at[_dev_index(me)], local_sem)
        mine.start()
        copies = []
        for r in range(1, N_DEV):
            peer = _flip(me, r)
            copies.append(pltpu.make_async_remote_copy(
                src_ref=p_ref.at[_dev_index(peer)], dst_ref=out_ref.at[_dev_index(me)],
                send_sem=send_sems.at[r - 1], recv_sem=recv_sems.at[r - 1],
                device_id=peer, device_id_type=pl.DeviceIdType.MESH))
            copies[-1].start()
        for r in range(1, N_DEV):
            peer = _flip(me, r)
            landed = out_ref.at[_dev_index(peer)]
            pltpu.make_async_remote_copy(
                src_ref=landed, dst_ref=landed, send_sem=send_sems.at[r - 1], recv_sem=recv_sems.at[r - 1],
                device_id=peer, device_id_type=pl.DeviceIdType.MESH).wait_recv()
        for cp in copies:
            cp.wait_send()
        mine.wait()

    return pl.pallas_call(
        body, name=name, out_shape=SDS(p.shape, p.dtype),
        in_specs=[pl.BlockSpec(memory_space=pl.ANY)], out_specs=pl.BlockSpec(memory_space=pl.ANY),
        scratch_shapes=[pltpu.SemaphoreType.DMA((7,)), pltpu.SemaphoreType.DMA((7,)), pltpu.SemaphoreType.DMA(())],
    )(p)


def _sum_slabs(a, name):
    n, rows, cols = a.shape
    tr = _row_tile(rows, 64 if a.dtype == BF16 else 1024)

    def body(a_ref, o_ref):
        acc = a_ref[0].astype(F32)
        for k in range(1, n):
            acc = acc + a_ref[k].astype(F32)
        o_ref[...] = acc

    return pl.pallas_call(
        body, name=name, grid=(rows // tr,),
        in_specs=[pl.BlockSpec((n, tr, cols), lambda i: (0, i, 0))],
        out_specs=pl.BlockSpec((tr, cols), lambda i: (i, 0)),
        out_shape=SDS((rows, cols), F32), compiler_params=_cp(1))(a)


def _adamw(w, g, m, v, name):
    rows, cols = w.shape
    tr = _row_tile(rows, 352)

    def body(w_ref, g_ref, m_ref, v_ref, d_ref, mo_ref, vo_ref):
        g_ = g_ref[...]
        m_ = ADAM_B1 * m_ref[...] + (1.0 - ADAM_B1) * g_
        v_ = ADAM_B2 * v_ref[...] + (1.0 - ADAM_B2) * (g_ * g_)
        m_hat = m_ / (1.0 - ADAM_B1 ** ADAM_STEP)
        v_hat = v_ / (1.0 - ADAM_B2 ** ADAM_STEP)
        d_ref[...] = -ADAM_LR * (m_hat / (jnp.sqrt(v_hat) + ADAM_EPS) + ADAM_WD * w_ref[...])
        mo_ref[...] = m_
        vo_ref[...] = v_

    blk = pl.BlockSpec((tr, cols), lambda i: (i, 0))
    return pl.pallas_call(
        body, name=name, grid=(rows // tr,),
        in_specs=[blk] * 4, out_specs=[blk] * 3, out_shape=[SDS((rows, cols), F32)] * 3,
        compiler_params=_cp(1))(w, g, m, v)


def _ffn_fwd(h, gain, w_up, cw4, cb4, w_down4, tag):
    (n2,) = _rms_fwd(h, gain, f"ffn_norm_{tag}")
    up = _mm_group(n2, w_up, NN, F32, f"ffn_up_{tag}")
    up4 = up.reshape((2, 4) + up.shape[1:])
    act = _ffn_act_fwd(up4, cw4, cb4, f"ffn_act_{tag}")
    out = _mm_reduce(act, w_down4, NN, h, f"ffn_down_{tag}")
    return out, (n2, up4, act)


def _ffn_bwd(h, gain, w_up, cw4, cb4, w_down4, saved, dh, tag):
    n2, up4, act = saved
    dact = _mm_group(dh, w_down4, NT, F32, f"ffn_dact_{tag}")
    d_w_down = _mm_tn(act, dh[None], f"ffn_dwdown_{tag}")
    du4, dcw4, dcb4 = _ffn_act_bwd(up4, cw4, cb4, dact, f"ffn_dgate_{tag}")
    du = du4.reshape((8,) + du4.shape[2:])
    dup = _conv_bwd(du, cw4.reshape((8,) + cw4.shape[2:]), f"ffn_dconv_{tag}")
    dn2 = _mm_reduce(dup, w_up, NT, None, f"ffn_dnorm_{tag}")
    d_w_up = _mm_tn(n2[None], dup, f"ffn_dwup_{tag}")
    dh_in, dgain = _rms_bwd(h, gain, [dn2], dh, f"ffn_dh_{tag}")
    return dh_in, dgain, d_w_up, d_w_down, dcw4, dcb4


def kernel(x, meta_tokens, mix_norm, ffn_norm, pool_w, pool_scale, kv_norm, w_kv, w_q, w_o, ffn_w_up, ffn_conv_w, ffn_conv_b, ffn_w_down, final_norm, loss_target, m_meta_tokens, m_mix_norm, m_ffn_norm, m_pool_w, m_pool_scale, m_kv_norm, m_w_kv, m_w_q, m_w_o, m_ffn_w_up, m_ffn_conv_w, m_ffn_conv_b, m_ffn_w_down, m_final_norm, v_meta_tokens, v_mix_norm, v_ffn_norm, v_pool_w, v_pool_scale, v_kv_norm, v_w_kv, v_w_q, v_w_o, v_ffn_w_up, v_ffn_conv_w, v_ffn_conv_b, v_ffn_w_down, v_final_norm):
    seq, d = x.shape[1], x.shape[2]
    n_tok = N_META + seq
    lp = -(-n_tok // ROW_TILE) * ROW_TILE
    fc = ffn_w_up.shape[2]
    me = _dev_index(_mesh_pos())

    big_parts = [pool_w, w_kv, w_q, w_o, ffn_w_up, ffn_w_down]
    big_rows = [p.size // d for p in big_parts]
    big_off = [sum(big_rows[:k]) for k in range(len(big_parts) + 1)]
    local_big = jnp.concatenate([p.reshape(-1, d) for p in big_parts], axis=0).astype(BF16)
    gb = _all_gather(local_big, "gather_matrices")

    def big(k):
        return gb[:, big_off[k]:big_off[k + 1]]

    pw = big(0).reshape(N_DEV, 4, POOL_C // N_DEV, POOL_C).transpose(1, 0, 2, 3).reshape(4, POOL_C, POOL_C)
    wkv = big(1).reshape(N_DEV, d, 2 * d // N_DEV)
    wq = big(2).reshape(1, d, d)
    wo = big(3).reshape(1, d, d)
    wup = big(4).reshape(N_DEV, 2, d, fc)
    wdn = big(5).reshape(N_DEV, 2, fc // 2, d)
    wup_l = [wup[:, l] for l in range(2)]
    wdn_l = [wdn[:, l].reshape(4, fc, d) for l in range(2)]

    small_parts = [meta_tokens, pool_scale, ffn_conv_w]
    small_rows = [p.size // 128 for p in small_parts]
    small_pad = -sum(small_rows) % 8
    local_small = jnp.concatenate([p.reshape(-1, 128) for p in small_parts] + [jnp.zeros((small_pad, 128), F32)], axis=0)
    gs = _all_gather(local_small, "gather_vectors")
    r0, r1, r2 = small_rows[0], small_rows[0] + small_rows[1], sum(small_rows)
    meta_full = gs[:, :r0].transpose(1, 0, 2).reshape(N_META, d)
    pscale = gs[:, r0:r1].reshape(1, d)
    cw = gs[:, r1:r2].reshape(N_DEV, 2, 3, fc)
    cw4_l = [cw[:, l].reshape(2, 4, 3, fc) for l in range(2)]
    cb4_l = [ffn_conv_b[l].reshape(2, 4, 1, fc) for l in range(2)]

    h0 = jnp.concatenate([meta_full, x[0], jnp.zeros((lp - n_tok, d), F32)], axis=0)
    h1, diff = _pool_fwd(h0, mix_norm[0:1], pw, pscale, "pool_fwd")
    h2, saved0 = _ffn_fwd(h1, ffn_norm[0:1], wup_l[0], cw4_l[0], cb4_l[0], wdn_l[0], "0")
    gains_b = jnp.stack([kv_norm, mix_norm[1]], axis=0)
    kvn, n3 = _rms_fwd(h2, gains_b, "attn_norms")
    kv = _mm_group(kvn, wkv, NN, BF16, "kv_proj")
    q = _mm_group(n3, wq, NN, BF16, "q_proj")[0]
    o, tot = _attn_fwd(q, kv, "attn_fwd")
    h3 = _mm_reduce(o[None], wo, NN, h2, "o_proj")
    h4, saved1 = _ffn_fwd(h3, ffn_norm[1:2], wup_l[1], cw4_l[1], cb4_l[1], wdn_l[1], "1")
    target = jnp.pad(loss_target[0], ((N_META, lp - n_tok), (0, 0)))
    dh4, loss_blk, dg_final = _loss_bwd(h4, final_norm[None], target, seq, "loss")
    loss = lax.psum(loss_blk[0, 0], MESH_AXES)

    dh3, dg_ffn1, d_wup1, d_wdn1, dcw4_1, dcb4_1 = _ffn_bwd(
        h3, ffn_norm[1:2], wup_l[1], cw4_l[1], cb4_l[1], wdn_l[1], saved1, dh4, "1")
    d_o = _mm_group(dh3, wo, NT, BF16, "o_proj_dx")[0]
    d_wo = _mm_tn(o[None], dh3[None], "o_proj_dw")
    dq, dk, dv = _attn_bwd(q, kv, d_o, tot, "attn_bwd")
    dn3 = _mm_group(dq, wq, NT, F32, "q_proj_dx")[0]
    d_wq = _mm_tn(n3[None], dq[None], "q_proj_dw")
    dkv = jnp.concatenate([dk, dv], axis=0).astype(BF16)
    dkvn = _mm_reduce(dkv, wkv, NT, None, "kv_proj_dx")
    d_wkv = _mm_tn(kvn[None], dkv, "kv_proj_dw")
    dh2, dg_b = _rms_bwd(h2, gains_b, [dkvn, dn3], dh3, "attn_norms_bwd")
    dh1, dg_ffn0, d_wup0, d_wdn0, dcw4_0, dcb4_0 = _ffn_bwd(
        h1, ffn_norm[0:1], wup_l[0], cw4_l[0], cb4_l[0], wdn_l[0], saved0, dh2, "0")
    dh0, d_pw, d_pscale, dg_mix0 = _pool_bwd(h0, mix_norm[0:1], pw, pscale, diff, dh1, "pool_bwd")
    grad_x = dh0[N_META:n_tok][None]

    d_pw8 = d_pw.reshape(4, N_DEV, POOL_C // N_DEV, POOL_C).transpose(1, 0, 2, 3).reshape(N_DEV, -1, d).astype(BF16)
    d_wup8 = jnp.stack([d_wup0, d_wup1], axis=1).reshape(N_DEV, -1, d)
    d_wdn8 = jnp.stack([d_wdn0.reshape(N_DEV, fc // 2, d), d_wdn1.reshape(N_DEV, fc // 2, d)], axis=1).reshape(N_DEV, -1, d)
    partial_big = jnp.concatenate(
        [d_pw8, d_wkv.reshape(N_DEV, -1, d), d_wq.reshape(N_DEV, -1, d), d_wo.reshape(N_DEV, -1, d), d_wup8, d_wdn8], axis=1)
    g_big = _sum_slabs(_scatter_exchange(partial_big, "scatter_matrices"), "sum_matrices")

    rep_parts = [jnp.concatenate([dg_mix0, dg_b[1:2]], axis=0), jnp.concatenate([dg_ffn0, dg_ffn1], axis=0),
                 dg_b[0:1], dg_final, jnp.stack([dcb4_0.reshape(-1), dcb4_1.reshape(-1)], axis=0)]
    rep_shapes = [mix_norm.shape, ffn_norm.shape, kv_norm.shape, final_norm.shape, ffn_conv_b.shape]
    rep_rows = [p.size // 128 for p in rep_parts]
    d_meta8 = dh0[:N_META].reshape(N_META, N_DEV, d // N_DEV).transpose(1, 0, 2).reshape(N_DEV, -1, 128)
    d_cw8 = jnp.stack([dcw4_0.reshape(N_DEV, 3, fc), dcw4_1.reshape(N_DEV, 3, fc)], axis=1).reshape(N_DEV, -1, 128)
    shard_parts = jnp.concatenate([d_meta8, d_pscale.reshape(N_DEV, 1, 128), d_cw8], axis=1)
    n_rep = sum(rep_rows)
    partial_small = jnp.concatenate([p.reshape(-1, 128) for p in rep_parts] + [shard_parts.reshape(-1, 128)], axis=0)
    g_small = _sum_slabs(_all_gather(partial_small, "gather_vector_grads"), "sum_vectors")
    g_rep = [g_small[sum(rep_rows[:k]):sum(rep_rows[:k + 1])].reshape(s) for k, s in enumerate(rep_shapes)]
    g_shard = lax.dynamic_index_in_dim(g_small[n_rep:].reshape(N_DEV, -1, 128), me, 0, keepdims=False)
    g_meta = g_shard[:r0].reshape(meta_tokens.shape)
    g_pscale = g_shard[r0:r1].reshape(pool_scale.shape)
    g_cw = g_shard[r1:r2].reshape(ffn_conv_w.shape)

    grads = {
        "meta_tokens": g_meta, "mix_norm": g_rep[0], "ffn_norm": g_rep[1],
        "pool_w": g_big[big_off[0]:big_off[1]].reshape(pool_w.shape), "pool_scale": g_pscale, "kv_norm": g_rep[2],
        "w_kv": g_big[big_off[1]:big_off[2]].reshape(w_kv.shape), "w_q": g_big[big_off[2]:big_off[3]].reshape(w_q.shape),
        "w_o": g_big[big_off[3]:big_off[4]].reshape(w_o.shape),
        "ffn_w_up": g_big[big_off[4]:big_off[5]].reshape(ffn_w_up.shape), "ffn_conv_w": g_cw, "ffn_conv_b": g_rep[4],
        "ffn_w_down": g_big[big_off[5]:big_off[6]].reshape(ffn_w_down.shape), "final_norm": g_rep[3],
    }
    names = list(grads)
    weights = dict(zip(names, [meta_tokens, mix_norm, ffn_norm, pool_w, pool_scale, kv_norm, w_kv, w_q, w_o,
                               ffn_w_up, ffn_conv_w, ffn_conv_b, ffn_w_down, final_norm]))
    mom1 = dict(zip(names, [m_meta_tokens, m_mix_norm, m_ffn_norm, m_pool_w, m_pool_scale, m_kv_norm, m_w_kv, m_w_q,
                            m_w_o, m_ffn_w_up, m_ffn_conv_w, m_ffn_conv_b, m_ffn_w_down, m_final_norm]))
    mom2 = dict(zip(names, [v_meta_tokens, v_mix_norm, v_ffn_norm, v_pool_w, v_pool_scale, v_kv_norm, v_w_kv, v_w_q,
                            v_w_o, v_ffn_w_up, v_ffn_conv_w, v_ffn_conv_b, v_ffn_w_down, v_final_norm]))

    delta, new_m, new_v = {}, {}, {}
    matrices = ["pool_w", "w_kv", "w_q", "w_o", "ffn_w_up", "ffn_w_down"]
    for n in matrices:
        shape = weights[n].shape
        flat = (-1, shape[-1])
        dl, nm, nv = _adamw(weights[n].reshape(flat), grads[n].reshape(flat), mom1[n].reshape(flat),
                            mom2[n].reshape(flat), "adamw_" + n)
        delta[n], new_m[n], new_v[n] = dl.reshape(shape), nm.reshape(shape), nv.reshape(shape)
    vectors = [n for n in names if n not in matrices]
    vec_rows = [weights[n].size // 128 for n in vectors]
    vec_pad = -sum(vec_rows) % 8

    def pack(tree):
        return jnp.concatenate([tree[n].reshape(-1, 128) for n in vectors] + [jnp.ones((vec_pad, 128), F32)], axis=0)

    outs = _adamw(pack(weights), pack(grads), pack(mom1), pack(mom2), "adamw_vectors")
    for tree, packed in zip((delta, new_m, new_v), outs):
        for k, n in enumerate(vectors):
            tree[n] = packed[sum(vec_rows[:k]):sum(vec_rows[:k + 1])].reshape(weights[n].shape)

    return (loss, grad_x, *[grads[n] for n in names], *[delta[n] for n in names],
            *[new_m[n] for n in names], *[new_v[n] for n in names])
```

```python
import jax
import jax.numpy as jnp
from jax import lax
from jax.experimental import pallas as pl
from jax.experimental.pallas import tpu as pltpu

F32 = jnp.float32
BF16 = jnp.bfloat16
SDS = jax.ShapeDtypeStruct

N_DEV = 8
N_META = 16
HEAD_DIM = 64
HEAD_PAIRS = 8
RMS_EPS = 1e-6
LOG2_E = 1.4426950408889634
POOL_WINDOWS = (2, 4, 8, 16)
POOL_C = 256
POOL_HALO = 16
CONV_HALO = 8
ROW_TILE = 384
MM_ROWS_MAX = 1408
SUM_ROWS_MAX = 256
SUM_WHOLE_BYTES = 4 << 20
ATT_BLK = 128
ATT_Q = ROW_TILE
ATT_UNROLL = ATT_Q // ATT_BLK
VMEM_LIMIT = 56 * 1024 * 1024

ADAM_LR = 0.001
ADAM_B1 = 0.9
ADAM_B2 = 0.999
ADAM_EPS = 1e-08
ADAM_WD = 0.01
ADAM_STEP = 10

MESH_AXES = ("x", "y", "c")
NN = (((1,), (0,)), ((), ()))
NT = (((1,), (1,)), ((), ()))
TN = (((0,), (0,)), ((), ()))


def _cp(n_axes):
    return pltpu.CompilerParams(dimension_semantics=("arbitrary",) * n_axes, vmem_limit_bytes=VMEM_LIMIT)


def _dot(a, b, dims=NN):
    return lax.dot_general(a, b, dims, preferred_element_type=F32)


def _rstd(x):
    return lax.rsqrt(jnp.mean(x * x, axis=-1, keepdims=True) + RMS_EPS)


def _row_tile(rows, cap=512, mult=8):
    if rows <= cap:
        return rows
    best = mult
    for t in range(mult, cap + 1, mult):
        if rows % t == 0:
            best = t
    assert rows % best == 0
    return best


def _rms_fwd(h, gains, name):
    lp, d = h.shape
    k = gains.shape[0]
    tm = ROW_TILE

    def body(h_ref, g_ref, *o_refs):
        x = h_ref[...]
        u = x * _rstd(x)
        for j in range(k):
            o_refs[j][...] = (u * g_ref[j:j + 1, :]).astype(BF16)

    row = pl.BlockSpec((tm, d), lambda i: (i, 0))
    return pl.pallas_call(
        body, name=name, grid=(lp // tm,),
        in_specs=[row, pl.BlockSpec((k, d), lambda i: (0, 0))],
        out_specs=[row] * k, out_shape=[SDS((lp, d), BF16)] * k,
        compiler_params=_cp(1))(h, gains)


def _rms_bwd(h, gains, dns, dh_in, name):
    lp, d = h.shape
    k = gains.shape[0]
    tm = ROW_TILE

    def body(h_ref, g_ref, *refs):
        dn_refs, dh_ref, dho_ref, dg_ref = refs[:k], refs[k], refs[k + 1], refs[k + 2]
        i = pl.program_id(0)
        x = h_ref[...]
        r = _rstd(x)
        u = x * r
        du = jnp.zeros_like(x)
        rows = []
        for j in range(k):
            dn = dn_refs[j][...]
            du = du + dn * g_ref[j:j + 1, :]
            rows.append(jnp.sum(dn * u, axis=0, keepdims=True))
        dx = r * (du - u * jnp.mean(du * u, axis=-1, keepdims=True))
        dho_ref[...] = dh_ref[...] + dx

        @pl.when(i == 0)
        def _():
            for j in range(k):
                dg_ref[j:j + 1, :] = rows[j]

        @pl.when(i > 0)
        def _():
            for j in range(k):
                dg_ref[j:j + 1, :] += rows[j]

    row = pl.BlockSpec((tm, d), lambda i: (i, 0))
    vec = pl.BlockSpec((k, d), lambda i: (0, 0))
    return pl.pallas_call(
        body, name=name, grid=(lp // tm,),
        in_specs=[row, vec] + [row] * k + [row],
        out_specs=[row, vec], out_shape=[SDS((lp, d), F32), SDS((k, d), F32)],
        compiler_params=_cp(1))(h, gains, *dns, dh_in)


def _loss_bwd(h, gain, target, n_real, name):
    lp, d = h.shape
    tm = ROW_TILE

    def body(h_ref, g_ref, t_ref, dh_ref, loss_ref, dg_ref):
        i = pl.program_id(0)
        x = h_ref[...]
        g = g_ref[...]
        r = _rstd(x)
        u = x * r
        row = i * tm + lax.broadcasted_iota(jnp.int32, (tm, 1), 0)
        valid = (row >= N_META) & (row < N_META + n_real)
        e = jnp.where(valid, u * g - t_ref[...], 0.0)
        part = 0.5 * jnp.sum(jnp.sum(e * e, axis=-1, keepdims=True), axis=0, keepdims=True) * (1.0 / d)
        dy = e * (1.0 / d)
        du = dy * g
        dh_ref[...] = r * (du - u * jnp.mean(du * u, axis=-1, keepdims=True))
        dgp = jnp.sum(dy * u, axis=0, keepdims=True)

        @pl.when(i == 0)
        def _():
            loss_ref[...] = jnp.broadcast_to(part, (8, 128))
            dg_ref[...] = dgp

        @pl.when(i > 0)
        def _():
            loss_ref[...] += jnp.broadcast_to(part, (8, 128))
            dg_ref[...] += dgp

    row = pl.BlockSpec((tm, d), lambda i: (i, 0))
    vec = pl.BlockSpec((1, d), lambda i: (0, 0))
    return pl.pallas_call(
        body, name=name, grid=(lp // tm,),
        in_specs=[row, vec, row],
        out_specs=[row, pl.BlockSpec((8, 128), lambda i: (0, 0)), vec],
        out_shape=[SDS((lp, d), F32), SDS((8, 128), F32), SDS((1, d), F32)],
        compiler_params=_cp(1))(h, gain, target)


def _pool_fwd(h, gain, w, scale, name):
    lp, d = h.shape
    tm = ROW_TILE
    hb = POOL_HALO

    def body(h_ref, halo_ref, g_ref, w_ref, s_ref, h1_ref, diff_ref):
        i = pl.program_id(0)
        g = g_ref[...]
        x = h_ref[...]
        n = x * _rstd(x) * g
        xh = halo_ref[...]
        nh = jnp.where(i > 0, xh * _rstd(xh) * g, 0.0)
        cur = jnp.concatenate([nh, n], axis=0)
        pos = i * tm + lax.broadcasted_iota(jnp.int32, (tm, 1), 0)
        for gi, win in enumerate(POOL_WINDOWS):
            if gi > 0:
                cur = cur[:, POOL_C:]
            cur = cur + pltpu.roll(cur, win // 2, 0)
            c0 = gi * POOL_C
            count = jnp.minimum(pos + 1, win).astype(F32)
            diff = cur[hb:, :POOL_C] / count - n[:, c0:c0 + POOL_C]
            diff = diff.astype(BF16)
            y = _dot(diff, w_ref[gi])
            h1_ref[:, c0:c0 + POOL_C] = x[:, c0:c0 + POOL_C] + y * s_ref[:, c0:c0 + POOL_C]
            diff_ref[:, c0:c0 + POOL_C] = diff

    row = pl.BlockSpec((tm, d), lambda i: (i, 0))
    halo = pl.BlockSpec((hb, d), lambda i: (jnp.maximum(i * (tm // hb) - 1, 0), 0))
    vec = pl.BlockSpec((1, d), lambda i: (0, 0))
    return pl.pallas_call(
        body, name=name, grid=(lp // tm,),
        in_specs=[row, halo, vec, pl.BlockSpec(w.shape, lambda i: (0, 0, 0)), vec],
        out_specs=[row, row], out_shape=[SDS((lp, d), F32), SDS((lp, d), BF16)],
        compiler_params=_cp(1))(h, h, gain, w, scale)


def _pool_bwd(h, gain, w, scale, diff, dh1, name):
    lp, d = h.shape
    tm = ROW_TILE
    hb = POOL_HALO
    nblk = lp // tm
    ext = tm + hb

    def body(h_ref, g_ref, w_ref, s_ref, diff_ref, dh_ref, dhn_ref, dh0_ref, dw_ref, ds_ref, dg_ref):
        i = pl.program_id(0)
        g = g_ref[...]
        x = h_ref[...]
        r = _rstd(x)
        u = x * r
        dh = dh_ref[...]
        dhn = jnp.where(i < nblk - 1, dhn_ref[...], 0.0)
        dyp = jnp.concatenate([dh, dhn], axis=0) * s_ref[...]
        pos = i * tm + lax.broadcasted_iota(jnp.int32, (ext, 1), 0)
        dn_parts, dw_parts, ds_parts = [], [], []
        for gi, win in enumerate(POOL_WINDOWS):
            c0 = gi * POOL_C
            wg = w_ref[gi]
            dyp_g = dyp[:, c0:c0 + POOL_C].astype(BF16)
            dd = _dot(dyp_g, wg, NT)
            dfg = diff_ref[:, c0:c0 + POOL_C]
            dw_parts.append(_dot(dfg, dyp_g[:tm], TN))
            ds_parts.append(jnp.sum(dh[:, c0:c0 + POOL_C] * _dot(dfg, wg), axis=0, keepdims=True))
            count = jnp.minimum(pos + 1, win).astype(F32)
            cur = dd / count
            sh = 1
            while sh < win:
                cur = cur + pltpu.roll(cur, ext - sh, 0)
                sh *= 2
            dn_parts.append(cur[:tm] - dd[:tm])
        dn = jnp.concatenate(dn_parts, axis=1)
        du = dn * g
        dh0_ref[...] = dh + r * (du - u * jnp.mean(du * u, axis=-1, keepdims=True))
        dgp = jnp.sum(dn * u, axis=0, keepdims=True)
        dsp = jnp.concatenate(ds_parts, axis=1)

        @pl.when(i == 0)
        def _():
            for gi in range(len(POOL_WINDOWS)):
                dw_ref[gi] = dw_parts[gi]
            ds_ref[...] = dsp
            dg_ref[...] = dgp

        @pl.when(i > 0)
        def _():
            for gi in range(len(POOL_WINDOWS)):
                dw_ref[gi] += dw_parts[gi]
            ds_ref[...] += dsp
            dg_ref[...] += dgp

    row = pl.BlockSpec((tm, d), lambda i: (i, 0))
    nxt = pl.BlockSpec((hb, d), lambda i: (jnp.minimum((i + 1) * (tm // hb), lp // hb - 1), 0))
    vec = pl.BlockSpec((1, d), lambda i: (0, 0))
    wsp = pl.BlockSpec(w.shape, lambda i: (0, 0, 0))
    return pl.pallas_call(
        body, name=name, grid=(nblk,),
        in_specs=[row, vec, wsp, vec, row, row, nxt],
        out_specs=[row, wsp, vec, vec],
        out_shape=[SDS((lp, d), F32), SDS(w.shape, F32), SDS((1, d), F32), SDS((1, d), F32)],
        compiler_params=_cp(1))(h, gain, w, scale, diff, dh1, dh1)


def _conv_taps(x, halo, first):
    ext = jnp.concatenate([jnp.where(first, 0.0, halo), x], axis=0)
    return pltpu.roll(ext, 1, 0)[CONV_HALO:], pltpu.roll(ext, 2, 0)[CONV_HALO:]


def _ffn_specs(tm, c, lp):
    blk = pl.BlockSpec((2, 1, tm, c), lambda g, i: (0, g, i, 0))
    halo = pl.BlockSpec((2, 1, CONV_HALO, c), lambda g, i: (0, g, jnp.maximum(i * (tm // CONV_HALO) - 1, 0), 0))
    cw = pl.BlockSpec((2, 1, 3, c), lambda g, i: (0, g, 0, 0))
    cb = pl.BlockSpec((2, 1, 1, c), lambda g, i: (0, g, 0, 0))
    return blk, halo, cw, cb


def _ffn_act_fwd(up4, cw4, cb4, name):
    _, ng, lp, c = up4.shape
    tm = ROW_TILE

    def body(up_ref, halo_ref, cw_ref, cb_ref, act_ref):
        first = pl.program_id(1) == 0
        u = []
        for half in range(2):
            x = up_ref[half, 0]
            xm1, xm2 = _conv_taps(x, halo_ref[half, 0], first)
            u.append(cb_ref[half, 0] + cw_ref[half, 0, 0:1, :] * xm2 + cw_ref[half, 0, 1:2, :] * xm1
                     + cw_ref[half, 0, 2:3, :] * x)
        gate, val = u
        sig = 1.0 / (1.0 + jnp.exp(-gate))
        act_ref[0] = (gate * sig * val).astype(BF16)

    blk, halo, cw, cb = _ffn_specs(tm, c, lp)
    return pl.pallas_call(
        body, name=name, grid=(ng, lp // tm),
        in_specs=[blk, halo, cw, cb],
        out_specs=pl.BlockSpec((1, tm, c), lambda g, i: (g, i, 0)),
        out_shape=SDS((ng, lp, c), BF16), compiler_params=_cp(2))(up4, up4, cw4, cb4)


def _ffn_act_bwd(up4, cw4, cb4, dact, name):
    _, ng, lp, c = up4.shape
    tm = ROW_TILE
    hb = CONV_HALO
    nblk = lp // tm
    ext = tm + hb

    def body(up_ref, prev_ref, next_ref, cw_ref, cb_ref, da_ref, dan_ref, dup_ref, dcw_ref, dcb_ref):
        i = pl.program_id(1)
        first = i == 0
        last = i == nblk - 1
        da = jnp.concatenate([da_ref[0], jnp.where(last, 0.0, dan_ref[0])], axis=0)
        u, taps = [], []
        for half in range(2):
            rows = jnp.concatenate([jnp.where(first, 0.0, prev_ref[half, 0]), up_ref[half, 0],
                                    jnp.where(last, 0.0, next_ref[half, 0])], axis=0)
            x, xm1, xm2 = rows[hb:], pltpu.roll(rows, 1, 0)[hb:], pltpu.roll(rows, 2, 0)[hb:]
            u.append(cb_ref[half, 0] + cw_ref[half, 0, 0:1, :] * xm2 + cw_ref[half, 0, 1:2, :] * xm1
                     + cw_ref[half, 0, 2:3, :] * x)
            taps.append((xm2, xm1, x))
        gate, val = u
        sig = 1.0 / (1.0 + jnp.exp(-gate))
        dus = (da * val * (sig * (1.0 + gate * (1.0 - sig))), da * (gate * sig))
        sums = []
        for half in range(2):
            du = dus[half]
            dup_ref[half, 0] = (cw_ref[half, 0, 2:3, :] * du[:tm] + cw_ref[half, 0, 1:2, :] * pltpu.roll(du, ext - 1, 0)[:tm]
                                + cw_ref[half, 0, 0:1, :] * pltpu.roll(du, ext - 2, 0)[:tm]).astype(BF16)
            sums.append([jnp.sum(du[:tm] * t[:tm], axis=0, keepdims=True) for t in taps[half]]
                        + [jnp.sum(du[:tm], axis=0, keepdims=True)])

        @pl.when(first)
        def _():
            for half in range(2):
                for k in range(3):
                    dcw_ref[half, 0, k:k + 1, :] = sums[half][k]
                dcb_ref[half, 0] = sums[half][3]

        @pl.when(i > 0)
        def _():
            for half in range(2):
                for k in range(3):
                    dcw_ref[half, 0, k:k + 1, :] += sums[half][k]
                dcb_ref[half, 0] += sums[half][3]

    blk, prev, cw, cb = _ffn_specs(tm, c, lp)

    def next_rows(g, i):
        return jnp.minimum((i + 1) * (tm // hb), lp // hb - 1)

    return pl.pallas_call(
        body, name=name, grid=(ng, nblk),
        in_specs=[blk, prev, pl.BlockSpec((2, 1, hb, c), lambda g, i: (0, g, next_rows(g, i), 0)), cw, cb,
                  pl.BlockSpec((1, tm, c), lambda g, i: (g, i, 0)),
                  pl.BlockSpec((1, hb, c), lambda g, i: (g, next_rows(g, i), 0))],
        out_specs=[blk, cw, cb],
        out_shape=[SDS(up4.shape, BF16), SDS(cw4.shape, F32), SDS(cb4.shape, F32)],
        compiler_params=_cp(2))(up4, up4, up4, cw4, cb4, dact, dact)


def _mm_tile(rows):
    return _row_tile(rows, MM_ROWS_MAX)


def _mm_group(a, b, dims, out_dtype, name, b_sel=(1, 0)):
    m, k = a.shape
    stride, offset = b_sel
    ng = b.shape[0] // stride
    n = b.shape[2] if dims == NN else b.shape[1]
    tm = _mm_tile(m)

    def body(a_ref, b_ref, o_ref):
        o_ref[0] = _dot(a_ref[...].astype(BF16), b_ref[0], dims).astype(out_dtype)

    return pl.pallas_call(
        body, name=name, grid=(ng, m // tm),
        in_specs=[pl.BlockSpec((tm, k), lambda g, i: (i, 0)),
                  pl.BlockSpec((1,) + b.shape[1:], lambda g, i: (stride * g + offset, 0, 0))],
        out_specs=pl.BlockSpec((1, tm, n), lambda g, i: (g, i, 0)),
        out_shape=SDS((ng, m, n), out_dtype), compiler_params=_cp(2))(a, b)


def _mm_reduce(a, b, dims, res, name, b_sel=(1, 0)):
    ng, m, k = a.shape
    stride, offset = b_sel
    n = b.shape[2] if dims == NN else b.shape[1]
    tm = _mm_tile(m)
    has_res = res is not None

    def body(a_ref, b_ref, *refs):
        o_ref, acc_ref = refs[-2], refs[-1]
        g = pl.program_id(1)
        p = _dot(a_ref[0].astype(BF16), b_ref[0], dims)

        @pl.when(g == 0)
        def _():
            acc_ref[...] = p + refs[0][...] if has_res else p

        @pl.when(g > 0)
        def _():
            acc_ref[...] += p

        @pl.when(g == ng - 1)
        def _():
            o_ref[...] = acc_ref[...]

    row = pl.BlockSpec((tm, n), lambda i, g: (i, 0))
    return pl.pallas_call(
        body, name=name, grid=(m // tm, ng),
        in_specs=[pl.BlockSpec((1, tm, k), lambda i, g: (g, i, 0)),
                  pl.BlockSpec((1,) + b.shape[1:], lambda i, g: (stride * g + offset, 0, 0))] + ([row] if has_res else []),
        out_specs=row, out_shape=SDS((m, n), F32),
        scratch_shapes=[pltpu.VMEM((tm, n), F32)], compiler_params=_cp(2))(a, b, *([res] if has_res else []))


def _mm_tn(a, b, name):
    ga, m, ka = a.shape
    gb, _, n = b.shape
    ng = max(ga, gb)
    tk = _mm_tile(m)
    nk = m // tk

    def body(a_ref, b_ref, o_ref, acc_ref):
        s = pl.program_id(1)
        p = _dot(a_ref[0].astype(BF16), b_ref[0].astype(BF16), TN)

        @pl.when(s == 0)
        def _():
            acc_ref[...] = p

        @pl.when(s > 0)
        def _():
            acc_ref[...] += p

        @pl.when(s == nk - 1)
        def _():
            o_ref[0] = acc_ref[...].astype(BF16)

    return pl.pallas_call(
        body, name=name, grid=(ng, nk),
        in_specs=[pl.BlockSpec((1, tk, ka), (lambda g, s: (g, s, 0)) if ga > 1 else (lambda g, s: (0, s, 0))),
                  pl.BlockSpec((1, tk, n), (lambda g, s: (g, s, 0)) if gb > 1 else (lambda g, s: (0, s, 0)))],
        out_specs=pl.BlockSpec((1, ka, n), lambda g, s: (g, 0, 0)),
        out_shape=SDS((ng, ka, n), BF16),
        scratch_shapes=[pltpu.VMEM((ka, n), F32)], compiler_params=_cp(2))(a, b)


def _pair_tri(kind, sign):
    r = jnp.arange(2 * ATT_BLK)[:, None]
    c = jnp.arange(2 * ATT_BLK)[None, :]
    same = (r < ATT_BLK) == (c < ATT_BLK)
    rel = {"from": r >= c, "before": r < c}[kind]
    return ((same & rel) * sign).astype(BF16)


def _scan_dot(x, tri):
    hi = x.astype(BF16)
    lo = (x - hi.astype(F32)).astype(BF16)
    return _dot(hi, tri) + _dot(lo, tri)


def _split_heads(blk, lane_a):
    zero = jnp.zeros_like(blk)
    return jnp.concatenate([jnp.where(lane_a, blk, zero), jnp.where(lane_a, zero, blk)], axis=0)


def _softplus(z):
    return jnp.maximum(z, 0.0) + jnp.log(1.0 + jnp.exp2(jnp.abs(z) * (-LOG2_E)))


def _visible(qi, j):
    t = qi * ATT_Q + lax.broadcasted_iota(jnp.int32, (ATT_Q, 2 * ATT_BLK), 0)
    s = j * ATT_BLK + (lax.broadcasted_iota(jnp.int32, (ATT_Q, 2 * ATT_BLK), 1) & (ATT_BLK - 1))
    return s < t


def _halves(x):
    return x[:, :ATT_BLK], x[:, ATT_BLK:]


def _rowsum(x):
    return jnp.sum(x, axis=1, keepdims=True)


def _attn_specs(lp):
    bk = ATT_BLK
    qblk = pl.BlockSpec((ATT_Q, bk), lambda p, i: (i, p))
    kblk = pl.BlockSpec((1, lp, bk), lambda p, i: (p // 2, 0, p % 2))
    vblk = pl.BlockSpec((1, lp, bk), lambda p, i: (HEAD_PAIRS // 2 + p // 2, 0, p % 2))
    tri = pl.BlockSpec((2 * bk, 2 * bk), lambda p, i: (0, 0))
    return qblk, kblk, vblk, tri


def _attn_fwd(q, kv, name):
    lp, d = q.shape
    bk = ATT_BLK

    def body(q_ref, k_ref, v_ref, tri_ref, o_ref, t_ref):
        qi = pl.program_id(1)
        qs = q_ref[...] * (HEAD_DIM ** -0.5)
        lane_a = lax.broadcasted_iota(jnp.int32, (1, bk), 1) < HEAD_DIM
        tri = tri_ref[...]

        def trip(js, carry, masked):
            oacc, ca, cb = carry
            rows = [pl.ds(pl.multiple_of(j * bk, bk), bk) for j in js]
            zs = [_dot(qs, _split_heads(k_ref[0, r, :], lane_a), NT) for r in rows]
            ms = [_softplus(z) for z in zs]
            if masked:
                ms = [jnp.where(_visible(qi, j), m, 0.0) for j, m in zip(js, ms)]
            ws = [_scan_dot(m, tri) for m in ms]
            for j, r, z, m, w in zip(js, rows, zs, ms, ws):
                exa, exb = _halves(z + w)
                a = jnp.concatenate([jnp.exp(exa - ca), jnp.exp(exb - cb)], axis=1)
                if masked:
                    a = jnp.where(_visible(qi, j), a, 0.0)
                oacc = oacc + _dot(a.astype(BF16), _split_heads(v_ref[0, r, :], lane_a))
                ma, mb = _halves(m)
                ca, cb = ca + _rowsum(ma), cb + _rowsum(mb)
            return oacc, ca, cb

        carry = (jnp.zeros((ATT_Q, bk), F32), jnp.zeros((ATT_Q, 1), F32), jnp.zeros((ATT_Q, 1), F32))
        top = (qi + 1) * ATT_UNROLL - 1
        carry = trip([top - u for u in range(ATT_UNROLL)], carry, True)
        oacc, ca, cb = lax.fori_loop(
            0, qi, lambda g, c: trip([top - (g + 1) * ATT_UNROLL - u for u in range(ATT_UNROLL)], c, False), carry)
        o_ref[...] = oacc.astype(BF16)
        t_ref[...] = jnp.where(lane_a, ca, cb)

    qblk, kblk, vblk, tri = _attn_specs(lp)
    return pl.pallas_call(
        body, name=name, grid=(HEAD_PAIRS, lp // ATT_Q),
        in_specs=[qblk, kblk, vblk, tri],
        out_specs=[qblk, qblk], out_shape=[SDS((lp, d), BF16), SDS((lp, d), F32)],
        compiler_params=_cp(2))(q, kv, kv, _pair_tri("from", -1))


def _attn_bwd(q, kv, do, tot, name):
    lp, d = q.shape
    bk = ATT_BLK
    scale = HEAD_DIM ** -0.5

    def body(q_ref, k_ref, v_ref, do_ref, t_ref, tri_ref, dq_ref, dk_ref, dv_ref):
        qi = pl.program_id(1)

        @pl.when(qi == 0)
        def _():
            dk_ref[...] = jnp.zeros_like(dk_ref)
            dv_ref[...] = jnp.zeros_like(dv_ref)

        qs = q_ref[...] * scale
        do_blk = do_ref[...]
        lane_a = lax.broadcasted_iota(jnp.int32, (1, bk), 1) < HEAD_DIM
        tot_blk = t_ref[...]
        ta = jnp.max(jnp.where(lane_a, tot_blk, -jnp.inf), axis=1, keepdims=True)
        tb = jnp.max(jnp.where(lane_a, -jnp.inf, tot_blk), axis=1, keepdims=True)
        tri = tri_ref[...]

        def trip(js, carry, masked):
            dq, pa, pb, ea, eb = carry
            rows = [pl.ds(pl.multiple_of(j * bk, bk), bk) for j in js]
            kks = [_split_heads(k_ref[0, r, :], lane_a) for r in rows]
            zs = [_dot(qs, kk, NT) for kk in kks]
            das = [_dot(do_blk, _split_heads(v_ref[0, r, :], lane_a), NT) for r in rows]
            ms = [_softplus(z) for z in zs]
            if masked:
                ms = [jnp.where(_visible(qi, j), m, 0.0) for j, m in zip(js, ms)]
            xs = [_scan_dot(m, tri) for m in ms]
            es, a_bf = [], []
            for j, z, m, x, da in zip(js, zs, ms, xs, das):
                xa, xb = _halves(z + x)
                a = jnp.concatenate([jnp.exp(xa + pa), jnp.exp(xb + pb)], axis=1)
                if masked:
                    a = jnp.where(_visible(qi, j), a, 0.0)
                a_bf.append(a.astype(BF16))
                es.append(a * da)
                ma, mb = _halves(m)
                pa, pb = pa + _rowsum(ma), pb + _rowsum(mb)
            ss = [_dot(e.astype(BF16), tri) for e in es]
            for j, r, kk, z, m, e, s, ab in zip(js, rows, kks, zs, ms, es, ss, a_bf):
                sa, sb = _halves(s)
                e_before = jnp.concatenate([sa + ea, sb + eb], axis=1)
                dz = e - jnp.exp(z - m) * (e + e_before)
                if masked:
                    dz = jnp.where(_visible(qi, j), dz, 0.0)
                dzb = dz.astype(BF16)
                dq = dq + _dot(dzb, kk)
                rk = _dot(dzb, qs, TN)
                rv = _dot(ab, do_blk, TN)
                dk_ref[0, r, :] += jnp.where(lane_a, rk[:bk], rk[bk:])
                dv_ref[0, r, :] += jnp.where(lane_a, rv[:bk], rv[bk:])
                e_a, e_b = _halves(e)
                ea, eb = ea + _rowsum(e_a), eb + _rowsum(e_b)
            return dq, pa, pb, ea, eb

        zcol = jnp.zeros((ATT_Q, 1), F32)
        carry = lax.fori_loop(
            0, qi, lambda g, c: trip([g * ATT_UNROLL + u for u in range(ATT_UNROLL)], c, False),
            (jnp.zeros((ATT_Q, bk), F32), -ta, -tb, zcol, zcol))
        carry = trip([qi * ATT_UNROLL + u for u in range(ATT_UNROLL)], carry, True)
        dq_ref[...] = (carry[0] * scale).astype(BF16)

    qblk, kblk, vblk, tri = _attn_specs(lp)
    return pl.pallas_call(
        body, name=name, grid=(HEAD_PAIRS, lp // ATT_Q),
        in_specs=[qblk, kblk, vblk, qblk, qblk, tri],
        out_specs=[qblk, kblk, kblk],
        out_shape=[SDS((lp, d), BF16), SDS((HEAD_PAIRS // 2, lp, 2 * bk), F32), SDS((HEAD_PAIRS // 2, lp, 2 * bk), F32)],
        compiler_params=_cp(2))(q, kv, kv, do, tot, _pair_tri("before", 1))


def _mesh_pos():
    return lax.axis_index("x"), lax.axis_index("y"), lax.axis_index("c")


def _flip(pos, r):
    x, y, c = pos
    return (1 - x if r & 4 else x, 1 - y if r & 2 else y, 1 - c if r & 1 else c)


def _dev_index(pos):
    return 4 * pos[0] + 2 * pos[1] + pos[2]


def _all_gather(xs, name):
    n = len(xs)

    def body(*refs):
        x_refs, out_refs = refs[:n], refs[n:2 * n]
        send_sems, recv_sems, local_sems = refs[2 * n:]
        me = _mesh_pos()
        sibling = _flip(me, 1)
        others = [_flip(me, 4), _flip(me, 2), _flip(me, 6)]

        def copy(a, k, block, to, own=False):
            slab = out_refs[a].at[_dev_index(block)]
            return pltpu.make_async_remote_copy(
                src_ref=x_refs[a] if own else slab, dst_ref=slab,
                send_sem=send_sems.at[7 * a + k], recv_sem=recv_sems.at[7 * a + k],
                device_id=to, device_id_type=pl.DeviceIdType.MESH)

        mine = [pltpu.make_async_copy(x_refs[a], out_refs[a].at[_dev_index(me)], local_sems.at[a]) for a in range(n)]
        first = []
        for a in range(n):
            mine[a].start()
            first += [copy(a, 0, me, sibling, own=True)] + [copy(a, 1 + j, me, o, own=True) for j, o in enumerate(others)]
        for cp in first:
            cp.start()
        passed = []
        for a in range(n):
            for j, o in enumerate(others):
                copy(a, 1 + j, o, me).wait_recv()
                passed.append(copy(a, 4 + j, o, sibling))
                passed[-1].start()
        for a in range(n):
            copy(a, 0, sibling, me).wait_recv()
            for j, o in enumerate(others):
                copy(a, 4 + j, _flip(o, 1), me).wait_recv()
        for cp in first + passed:
            cp.wait_send()
        for cp in mine:
            cp.wait()

    hbm = pl.BlockSpec(memory_space=pl.ANY)
    return pl.pallas_call(
        body, name=name, out_shape=[SDS((N_DEV,) + x.shape, x.dtype) for x in xs],
        in_specs=[hbm] * n, out_specs=[hbm] * n,
        scratch_shapes=[pltpu.SemaphoreType.DMA((7 * n,)), pltpu.SemaphoreType.DMA((7 * n,)), pltpu.SemaphoreType.DMA((n,))],
    )(*xs)


def _scatter_to_sibling(ps, name):
    n = len(ps)

    def body(*refs):
        p_refs, out_refs = refs[:n], refs[n:2 * n]
        send_sems, recv_sems = refs[2 * n:]
        me = _mesh_pos()
        sibling = _flip(me, 1)
        copies = []
        for a in range(n):
            for k in range(4):
                copies.append(pltpu.make_async_remote_copy(
                    src_ref=p_refs[a].at[2 * k + sibling[2]], dst_ref=out_refs[a].at[k],
                    send_sem=send_sems.at[4 * a + k], recv_sem=recv_sems.at[4 * a + k],
                    device_id=sibling, device_id_type=pl.DeviceIdType.MESH))
                copies[-1].start()
        for cp in copies:
            cp.wait_recv()
        for cp in copies:
            cp.wait_send()

    hbm = pl.BlockSpec(memory_space=pl.ANY)
    return pl.pallas_call(
        body, name=name, out_shape=[SDS((4,) + p.shape[1:], p.dtype) for p in ps],
        in_specs=[hbm] * n, out_specs=[hbm] * n,
        scratch_shapes=[pltpu.SemaphoreType.DMA((4 * n,)), pltpu.SemaphoreType.DMA((4 * n,))],
    )(*ps)


def _scatter_to_chips(qs, name):
    n = len(qs)

    def body(*refs):
        q_refs, out_refs = refs[:n], refs[n:2 * n]
        send_sems, recv_sems, local_sems = refs[2 * n:]
        me = _mesh_pos()
        my_chip = 2 * me[0] + me[1]
        mine, copies, landed = [], [], []
        for a in range(n):
            mine.append(pltpu.make_async_copy(q_refs[a].at[my_chip], out_refs[a].at[my_chip], local_sems.at[a]))
            mine[-1].start()
            for j, r in enumerate((4, 2, 6)):
                peer = _flip(me, r)
                peer_chip = 2 * peer[0] + peer[1]
                copies.append(pltpu.make_async_remote_copy(
                    src_ref=q_refs[a].at[peer_chip], dst_ref=out_refs[a].at[my_chip],
                    send_sem=send_sems.at[3 * a + j], recv_sem=recv_sems.at[3 * a + j],
                    device_id=peer, device_id_type=pl.DeviceIdType.MESH))
                copies[-1].start()
                slab = out_refs[a].at[peer_chip]
                landed.append(pltpu.make_async_remote_copy(
                    src_ref=slab, dst_ref=slab, send_sem=send_sems.at[3 * a + j], recv_sem=recv_sems.at[3 * a + j],
                    device_id=peer, device_id_type=pl.DeviceIdType.MESH))
        for cp in landed:
            cp.wait_recv()
        for cp in copies:
            cp.wait_send()
        for cp in mine:
            cp.wait()

    hbm = pl.BlockSpec(memory_space=pl.ANY)
    return pl.pallas_call(
        body, name=name, out_shape=[SDS(q.shape, q.dtype) for q in qs],
        in_specs=[hbm] * n, out_specs=[hbm] * n,
        scratch_shapes=[pltpu.SemaphoreType.DMA((3 * n,)), pltpu.SemaphoreType.DMA((3 * n,)), pltpu.SemaphoreType.DMA((n,))],
    )(*qs)


def _pair_sum(p, r, name):
    _, rows, cols = p.shape
    tr = _row_tile(rows, SUM_ROWS_MAX, 16)
    p4 = p.reshape(4, 2, rows, cols)

    def body(p_ref, r_ref, o_ref):
        own = jnp.where(lax.axis_index("c") == 0, p_ref[0, 0].astype(F32), p_ref[0, 1].astype(F32))
        o_ref[0] = (own + r_ref[0].astype(F32)).astype(BF16)

    blk = pl.BlockSpec((1, tr, cols), lambda k, i: (k, i, 0))
    return pl.pallas_call(
        body, name=name, grid=(4, rows // tr),
        in_specs=[pl.BlockSpec((1, 2, tr, cols), lambda k, i: (k, 0, i, 0)), blk],
        out_specs=blk, out_shape=SDS((4, rows, cols), BF16), compiler_params=_cp(2))(p4, r)


def _sum_slabs(a, name):
    n, rows, cols = a.shape
    tr = rows if a.size * a.dtype.itemsize <= SUM_WHOLE_BYTES else _row_tile(rows, SUM_ROWS_MAX, 16)

    def body(a_ref, o_ref):
        acc = a_ref[0].astype(F32)
        for k in range(1, n):
            acc = acc + a_ref[k].astype(F32)
        o_ref[...] = acc

    return pl.pallas_call(
        body, name=name, grid=(rows // tr,),
        in_specs=[pl.BlockSpec((n, tr, cols), lambda i: (0, i, 0))],
        out_specs=pl.BlockSpec((tr, cols), lambda i: (i, 0)),
        out_shape=SDS((rows, cols), F32), compiler_params=_cp(1))(a)


def _reduce_scatter(ps, tag):
    from_sibling = _scatter_to_sibling(ps, f"scatter_sibling_{tag}")
    qs = [_pair_sum(p, r, f"pair_sum_{tag}{a}") for a, (p, r) in enumerate(zip(ps, from_sibling))]
    from_chips = _scatter_to_chips(qs, f"scatter_chips_{tag}")
    return [_sum_slabs(r, f"sum_{tag}{a}") for a, r in enumerate(from_chips)]


def _adamw(w, g, m, v, name):
    rows, cols = w.shape
    tr = _row_tile(rows, 352)

    def body(w_ref, g_ref, m_ref, v_ref, d_ref, mo_ref, vo_ref):
        g_ = g_ref[...]
        m_ = ADAM_B1 * m_ref[...] + (1.0 - ADAM_B1) * g_
        v_ = ADAM_B2 * v_ref[...] + (1.0 - ADAM_B2) * (g_ * g_)
        m_hat = m_ / (1.0 - ADAM_B1 ** ADAM_STEP)
        v_hat = v_ / (1.0 - ADAM_B2 ** ADAM_STEP)
        d_ref[...] = -ADAM_LR * (m_hat / (jnp.sqrt(v_hat) + ADAM_EPS) + ADAM_WD * w_ref[...])
        mo_ref[...] = m_
        vo_ref[...] = v_

    blk = pl.BlockSpec((tr, cols), lambda i: (i, 0))
    return pl.pallas_call(
        body, name=name, grid=(rows // tr,),
        in_specs=[blk] * 4, out_specs=[blk] * 3, out_shape=[SDS((rows, cols), F32)] * 3,
        compiler_params=_cp(1))(w, g, m, v)


def _ffn_fwd(h, gain, w_up, up_sel, cw4, cb4, w_down4, tag):
    (n2,) = _rms_fwd(h, gain, f"ffn_norm_{tag}")
    up = _mm_group(n2, w_up, NN, F32, f"ffn_up_{tag}", up_sel)
    up4 = up.reshape((2, 4) + up.shape[1:])
    act = _ffn_act_fwd(up4, cw4, cb4, f"ffn_act_{tag}")
    out = _mm_reduce(act, w_down4, NN, h, f"ffn_down_{tag}")
    return out, (n2, up4, act)


def _ffn_bwd(h, gain, w_up, up_sel, cw4, cb4, w_down4, saved, dh, tag):
    n2, up4, act = saved
    dact = _mm_group(dh, w_down4, NT, F32, f"ffn_dact_{tag}")
    d_w_down = _mm_tn(act, dh[None], f"ffn_dwdown_{tag}")
    dup4, dcw4, dcb4 = _ffn_act_bwd(up4, cw4, cb4, dact, f"ffn_dgate_{tag}")
    dup = dup4.reshape((8,) + dup4.shape[2:])
    dn2 = _mm_reduce(dup, w_up, NT, None, f"ffn_dnorm_{tag}", up_sel)
    d_w_up = _mm_tn(n2[None], dup, f"ffn_dwup_{tag}")
    dh_in, dgain = _rms_bwd(h, gain, [dn2], dh, f"ffn_dh_{tag}")
    return dh_in, dgain, d_w_up, d_w_down, dcw4, dcb4


def kernel(x, meta_tokens, mix_norm, ffn_norm, pool_w, pool_scale, kv_norm, w_kv, w_q, w_o, ffn_w_up, ffn_conv_w, ffn_conv_b, ffn_w_down, final_norm, loss_target, m_meta_tokens, m_mix_norm, m_ffn_norm, m_pool_w, m_pool_scale, m_kv_norm, m_w_kv, m_w_q, m_w_o, m_ffn_w_up, m_ffn_conv_w, m_ffn_conv_b, m_ffn_w_down, m_final_norm, v_meta_tokens, v_mix_norm, v_ffn_norm, v_pool_w, v_pool_scale, v_kv_norm, v_w_kv, v_w_q, v_w_o, v_ffn_w_up, v_ffn_conv_w, v_ffn_conv_b, v_ffn_w_down, v_final_norm):
    seq, d = x.shape[1], x.shape[2]
    n_tok = N_META + seq
    lp = -(-n_tok // ROW_TILE) * ROW_TILE
    fc = ffn_w_up.shape[2]
    me = _dev_index(_mesh_pos())

    wide_parts = [pool_w, w_kv, w_q, w_o, ffn_w_down[0], ffn_w_down[1]]
    wide_rows = [p.size // d for p in wide_parts]
    wide_off = [sum(wide_rows[:k]) for k in range(len(wide_parts) + 1)]
    local_wide = jnp.concatenate([p.reshape(-1, d) for p in wide_parts], axis=0).astype(BF16)
    local_up = ffn_w_up.reshape(-1, fc).astype(BF16)
    g_wide, g_up = _all_gather([local_wide, local_up], "gather_matrices")

    def wide(k):
        return g_wide[:, wide_off[k]:wide_off[k + 1]]

    pw = wide(0).reshape(N_DEV, 4, POOL_C // N_DEV, POOL_C).transpose(1, 0, 2, 3).reshape(4, POOL_C, POOL_C)
    wkv = wide(1).reshape(N_DEV, d, 2 * d // N_DEV)
    wq = wide(2).reshape(1, d, d)
    wo = wide(3).reshape(1, d, d)
    wdn_l = [wide(4 + l).reshape(4, fc, d) for l in range(2)]
    wup = g_up.reshape(2 * N_DEV, d, fc)
    up_sel = [(2, l) for l in range(2)]

    small_parts = [meta_tokens, pool_scale, ffn_conv_w]
    small_rows = [p.size // 128 for p in small_parts]
    small_pad = -sum(small_rows) % 8
    local_small = jnp.concatenate([p.reshape(-1, 128) for p in small_parts] + [jnp.zeros((small_pad, 128), F32)], axis=0)
    (gs,) = _all_gather([local_small], "gather_vectors")
    r0, r1, r2 = small_rows[0], small_rows[0] + small_rows[1], sum(small_rows)
    meta_full = gs[:, :r0].transpose(1, 0, 2).reshape(N_META, d)
    pscale = gs[:, r0:r1].reshape(1, d)
    cw = gs[:, r1:r2].reshape(N_DEV, 2, 3, fc)
    cw4_l = [cw[:, l].reshape(2, 4, 3, fc) for l in range(2)]
    cb4_l = [ffn_conv_b[l].reshape(2, 4, 1, fc) for l in range(2)]

    h0 = jnp.concatenate([meta_full, x[0], jnp.zeros((lp - n_tok, d), F32)], axis=0)
    h1, diff = _pool_fwd(h0, mix_norm[0:1], pw, pscale, "pool_fwd")
    h2, saved0 = _ffn_fwd(h1, ffn_norm[0:1], wup, up_sel[0], cw4_l[0], cb4_l[0], wdn_l[0], "0")
    gains_b = jnp.stack([kv_norm, mix_norm[1]], axis=0)
    kvn, n3 = _rms_fwd(h2, gains_b, "attn_norms")
    kv = _mm_group(kvn, wkv, NN, BF16, "kv_proj")
    q = _mm_group(n3, wq, NN, BF16, "q_proj")[0]
    o, tot = _attn_fwd(q, kv, "attn_fwd")
    h3 = _mm_reduce(o[None], wo, NN, h2, "o_proj")
    h4, saved1 = _ffn_fwd(h3, ffn_norm[1:2], wup, up_sel[1], cw4_l[1], cb4_l[1], wdn_l[1], "1")
    target = jnp.pad(loss_target[0], ((N_META, lp - n_tok), (0, 0)))
    dh4, loss_blk, dg_final = _loss_bwd(h4, final_norm[None], target, seq, "loss")
    loss = lax.psum(loss_blk[0, 0], MESH_AXES)

    dh3, dg_ffn1, d_wup1, d_wdn1, dcw4_1, dcb4_1 = _ffn_bwd(
        h3, ffn_norm[1:2], wup, up_sel[1], cw4_l[1], cb4_l[1], wdn_l[1], saved1, dh4, "1")
    d_o = _mm_group(dh3, wo, NT, BF16, "o_proj_dx")[0]
    d_wo = _mm_tn(o[None], dh3[None], "o_proj_dw")
    dq, dk, dv = _attn_bwd(q, kv, d_o, tot, "attn_bwd")
    dn3 = _mm_group(dq, wq, NT, F32, "q_proj_dx")[0]
    d_wq = _mm_tn(n3[None], dq[None], "q_proj_dw")
    dkv = jnp.concatenate([dk, dv], axis=0).astype(BF16)
    dkvn = _mm_reduce(dkv, wkv, NT, None, "kv_proj_dx")
    d_wkv = _mm_tn(kvn[None], dkv, "kv_proj_dw")
    dh2, dg_b = _rms_bwd(h2, gains_b, [dkvn, dn3], dh3, "attn_norms_bwd")
    dh1, dg_ffn0, d_wup0, d_wdn0, dcw4_0, dcb4_0 = _ffn_bwd(
        h1, ffn_norm[0:1], wup, up_sel[0], cw4_l[0], cb4_l[0], wdn_l[0], saved0, dh2, "0")
    dh0, d_pw, d_pscale, dg_mix0 = _pool_bwd(h0, mix_norm[0:1], pw, pscale, diff, dh1, "pool_bwd")
    grad_x = dh0[N_META:n_tok][None]

    d_pw8 = d_pw.reshape(4, N_DEV, POOL_C // N_DEV, POOL_C).transpose(1, 0, 2, 3).reshape(N_DEV, -1, d).astype(BF16)
    partial_wide = jnp.concatenate(
        [d_pw8, d_wkv.reshape(N_DEV, -1, d), d_wq.reshape(N_DEV, -1, d), d_wo.reshape(N_DEV, -1, d),
         d_wdn0.reshape(N_DEV, -1, d), d_wdn1.reshape(N_DEV, -1, d)], axis=1)
    partial_up = jnp.concatenate([d_wup0, d_wup1], axis=1)
    s_wide, s_up = _reduce_scatter([partial_wide, partial_up], "matrices")

    rep_parts = [jnp.concatenate([dg_mix0, dg_b[1:2]], axis=0), jnp.concatenate([dg_ffn0, dg_ffn1], axis=0),
                 dg_b[0:1], dg_final, jnp.stack([dcb4_0.reshape(-1), dcb4_1.reshape(-1)], axis=0)]
    rep_shapes = [mix_norm.shape, ffn_norm.shape, kv_norm.shape, final_norm.shape, ffn_conv_b.shape]
    rep_rows = [p.size // 128 for p in rep_parts]
    d_meta8 = dh0[:N_META].reshape(N_META, N_DEV, d // N_DEV).transpose(1, 0, 2).reshape(N_DEV, -1, 128)
    d_cw8 = jnp.stack([dcw4_0.reshape(N_DEV, 3, fc), dcw4_1.reshape(N_DEV, 3, fc)], axis=1).reshape(N_DEV, -1, 128)
    shard_parts = jnp.concatenate([d_meta8, d_pscale.reshape(N_DEV, 1, 128), d_cw8], axis=1)
    n_rep = sum(rep_rows)
    partial_small = jnp.concatenate([p.reshape(-1, 128) for p in rep_parts] + [shard_parts.reshape(-1, 128)], axis=0)
    g_small = _sum_slabs(_all_gather([partial_small], "gather_vector_grads")[0], "sum_vectors")
    g_rep = [g_small[sum(rep_rows[:k]):sum(rep_rows[:k + 1])].reshape(s) for k, s in enumerate(rep_shapes)]
    g_shard = lax.dynamic_index_in_dim(g_small[n_rep:].reshape(N_DEV, -1, 128), me, 0, keepdims=False)
    g_meta = g_shard[:r0].reshape(meta_tokens.shape)
    g_pscale = g_shard[r0:r1].reshape(pool_scale.shape)
    g_cw = g_shard[r1:r2].reshape(ffn_conv_w.shape)

    grads = {
        "meta_tokens": g_meta, "mix_norm": g_rep[0], "ffn_norm": g_rep[1],
        "pool_w": s_wide[wide_off[0]:wide_off[1]].reshape(pool_w.shape), "pool_scale": g_pscale, "kv_norm": g_rep[2],
        "w_kv": s_wide[wide_off[1]:wide_off[2]].reshape(w_kv.shape), "w_q": s_wide[wide_off[2]:wide_off[3]].reshape(w_q.shape),
        "w_o": s_wide[wide_off[3]:wide_off[4]].reshape(w_o.shape),
        "ffn_w_up": s_up.reshape(ffn_w_up.shape), "ffn_conv_w": g_cw, "ffn_conv_b": g_rep[4],
        "ffn_w_down": s_wide[wide_off[4]:wide_off[6]].reshape(ffn_w_down.shape), "final_norm": g_rep[3],
    }
    names = list(grads)
    weights = dict(zip(names, [meta_tokens, mix_norm, ffn_norm, pool_w, pool_scale, kv_norm, w_kv, w_q, w_o,
                               ffn_w_up, ffn_conv_w, ffn_conv_b, ffn_w_down, final_norm]))
    mom1 = dict(zip(names, [m_meta_tokens, m_mix_norm, m_ffn_norm, m_pool_w, m_pool_scale, m_kv_norm, m_w_kv, m_w_q,
                            m_w_o, m_ffn_w_up, m_ffn_conv_w, m_ffn_conv_b, m_ffn_w_down, m_final_norm]))
    mom2 = dict(zip(names, [v_meta_tokens, v_mix_norm, v_ffn_norm, v_pool_w, v_pool_scale, v_kv_norm, v_w_kv, v_w_q,
                            v_w_o, v_ffn_w_up, v_ffn_conv_w, v_ffn_conv_b, v_ffn_w_down, v_final_norm]))

    delta, new_m, new_v = {}, {}, {}
    matrices = ["pool_w", "w_kv", "w_q", "w_o", "ffn_w_up", "ffn_w_down"]
    for n in matrices:
        shape = weights[n].shape
        flat = (-1, shape[-1])
        dl, nm, nv = _adamw(weights[n].reshape(flat), grads[n].reshape(flat), mom1[n].reshape(flat),
                            mom2[n].reshape(flat), "adamw_" + n)
        delta[n], new_m[n], new_v[n] = dl.reshape(shape), nm.reshape(shape), nv.reshape(shape)
    vectors = [n for n in names if n not in matrices]
    vec_rows = [weights[n].size // 128 for n in vectors]
    vec_pad = -sum(vec_rows) % 8

    def pack(tree):
        return jnp.concatenate([tree[n].reshape(-1, 128) for n in vectors] + [jnp.ones((vec_pad, 128), F32)], axis=0)

    outs = _adamw(pack(weights), pack(grads), pack(mom1), pack(mom2), "adamw_vectors")
    for tree, packed in zip((delta, new_m, new_v), outs):
        for k, n in enumerate(vectors):
            tree[n] = packed[sum(vec_rows[:k]):sum(vec_rows[:k + 1])].reshape(weights[n].shape)

    return (loss, grad_x, *[grads[n] for n in names], *[delta[n] for n in names],
            *[new_m[n] for n in names], *[new_v[n] for n in names])
```

```python
from typing import NamedTuple

import jax
import jax.numpy as jnp
from jax import lax
from jax.experimental import pallas as pl
from jax.experimental.pallas import tpu as pltpu

F32 = jnp.float32
BF16 = jnp.bfloat16
SDS = jax.ShapeDtypeStruct

N_DEV = 8
N_META = 16
HEAD_DIM = 64
HEAD_PAIRS = 8
RMS_EPS = 1e-6
LOG2_E = 1.4426950408889634
POOL_WINDOWS = (2, 4, 8, 16)
POOL_C = 256
POOL_HALO = 16
CONV_HALO = 8
ROW_TILE = 384
MM_ROWS_MAX = 1408
SUM_ROWS_MAX = 256
SUM_WHOLE_BYTES = 4 << 20
ATT_BLK = 128
ATT_Q = ROW_TILE
ATT_UNROLL = ATT_Q // ATT_BLK
VMEM_LIMIT = 56 * 1024 * 1024

ADAM_LR = 0.001
ADAM_B1 = 0.9
ADAM_B2 = 0.999
ADAM_EPS = 1e-08
ADAM_WD = 0.01
ADAM_STEP = 10

MESH_AXES = ("x", "y", "c")
NN = (((1,), (0,)), ((), ()))
NT = (((1,), (1,)), ((), ()))
TN = (((0,), (0,)), ((), ()))


def _cp(n_axes):
    return pltpu.CompilerParams(dimension_semantics=("arbitrary",) * n_axes, vmem_limit_bytes=VMEM_LIMIT)


def _dot(a, b, dims=NN):
    return lax.dot_general(a, b, dims, preferred_element_type=F32)


def _rstd(x):
    return lax.rsqrt(jnp.mean(x * x, axis=-1, keepdims=True) + RMS_EPS)


def _row_tile(rows, cap=512, mult=8):
    if rows <= cap:
        return rows
    best = mult
    for t in range(mult, cap + 1, mult):
        if rows % t == 0:
            best = t
    assert rows % best == 0
    return best


def _rms_fwd(h, gains, name):
    lp, d = h.shape
    k = gains.shape[0]
    tm = ROW_TILE

    def body(h_ref, g_ref, *o_refs):
        x = h_ref[...]
        u = x * _rstd(x)
        for j in range(k):
            o_refs[j][...] = (u * g_ref[j:j + 1, :]).astype(BF16)

    row = pl.BlockSpec((tm, d), lambda i: (i, 0))
    return pl.pallas_call(
        body, name=name, grid=(lp // tm,),
        in_specs=[row, pl.BlockSpec((k, d), lambda i: (0, 0))],
        out_specs=[row] * k, out_shape=[SDS((lp, d), BF16)] * k,
        compiler_params=_cp(1))(h, gains)


def _rms_bwd(h, gains, dns, dh_in, name):
    lp, d = h.shape
    k = gains.shape[0]
    tm = ROW_TILE

    def body(h_ref, g_ref, *refs):
        dn_refs, dh_ref, dho_ref, dg_ref = refs[:k], refs[k], refs[k + 1], refs[k + 2]
        i = pl.program_id(0)
        x = h_ref[...]
        r = _rstd(x)
        u = x * r
        du = jnp.zeros_like(x)
        rows = []
        for j in range(k):
            dn = dn_refs[j][...]
            du = du + dn * g_ref[j:j + 1, :]
            rows.append(jnp.sum(dn * u, axis=0, keepdims=True))
        dx = r * (du - u * jnp.mean(du * u, axis=-1, keepdims=True))
        dho_ref[...] = dh_ref[...] + dx

        @pl.when(i == 0)
        def _():
            for j in range(k):
                dg_ref[j:j + 1, :] = rows[j]

        @pl.when(i > 0)
        def _():
            for j in range(k):
                dg_ref[j:j + 1, :] += rows[j]

    row = pl.BlockSpec((tm, d), lambda i: (i, 0))
    vec = pl.BlockSpec((k, d), lambda i: (0, 0))
    return pl.pallas_call(
        body, name=name, grid=(lp // tm,),
        in_specs=[row, vec] + [row] * k + [row],
        out_specs=[row, vec], out_shape=[SDS((lp, d), F32), SDS((k, d), F32)],
        compiler_params=_cp(1))(h, gains, *dns, dh_in)


def _loss_bwd(h, gain, target, n_real, name):
    lp, d = h.shape
    tm = ROW_TILE

    def body(h_ref, g_ref, t_ref, dh_ref, loss_ref, dg_ref):
        i = pl.program_id(0)
        x = h_ref[...]
        g = g_ref[...]
        r = _rstd(x)
        u = x * r
        row = i * tm + lax.broadcasted_iota(jnp.int32, (tm, 1), 0)
        valid = (row >= N_META) & (row < N_META + n_real)
        e = jnp.where(valid, u * g - t_ref[...], 0.0)
        part = 0.5 * jnp.sum(jnp.sum(e * e, axis=-1, keepdims=True), axis=0, keepdims=True) * (1.0 / d)
        dy = e * (1.0 / d)
        du = dy * g
        dh_ref[...] = r * (du - u * jnp.mean(du * u, axis=-1, keepdims=True))
        dgp = jnp.sum(dy * u, axis=0, keepdims=True)

        @pl.when(i == 0)
        def _():
            loss_ref[...] = jnp.broadcast_to(part, (8, 128))
            dg_ref[...] = dgp

        @pl.when(i > 0)
        def _():
            loss_ref[...] += jnp.broadcast_to(part, (8, 128))
            dg_ref[...] += dgp

    row = pl.BlockSpec((tm, d), lambda i: (i, 0))
    vec = pl.BlockSpec((1, d), lambda i: (0, 0))
    return pl.pallas_call(
        body, name=name, grid=(lp // tm,),
        in_specs=[row, vec, row],
        out_specs=[row, pl.BlockSpec((8, 128), lambda i: (0, 0)), vec],
        out_shape=[SDS((lp, d), F32), SDS((8, 128), F32), SDS((1, d), F32)],
        compiler_params=_cp(1))(h, gain, target)


def _pool_fwd(h, gain, w, scale, name):
    lp, d = h.shape
    tm = ROW_TILE
    hb = POOL_HALO

    def body(h_ref, halo_ref, g_ref, w_ref, s_ref, h1_ref, diff_ref):
        i = pl.program_id(0)
        g = g_ref[...]
        x = h_ref[...]
        n = x * _rstd(x) * g
        xh = halo_ref[...]
        nh = jnp.where(i > 0, xh * _rstd(xh) * g, 0.0)
        cur = jnp.concatenate([nh, n], axis=0)
        pos = i * tm + lax.broadcasted_iota(jnp.int32, (tm, 1), 0)
        for gi, win in enumerate(POOL_WINDOWS):
            if gi > 0:
                cur = cur[:, POOL_C:]
            cur = cur + pltpu.roll(cur, win // 2, 0)
            c0 = gi * POOL_C
            count = jnp.minimum(pos + 1, win).astype(F32)
            diff = cur[hb:, :POOL_C] / count - n[:, c0:c0 + POOL_C]
            diff = diff.astype(BF16)
            y = _dot(diff, w_ref[gi])
            h1_ref[:, c0:c0 + POOL_C] = x[:, c0:c0 + POOL_C] + y * s_ref[:, c0:c0 + POOL_C]
            diff_ref[:, c0:c0 + POOL_C] = diff

    row = pl.BlockSpec((tm, d), lambda i: (i, 0))
    halo = pl.BlockSpec((hb, d), lambda i: (jnp.maximum(i * (tm // hb) - 1, 0), 0))
    vec = pl.BlockSpec((1, d), lambda i: (0, 0))
    return pl.pallas_call(
        body, name=name, grid=(lp // tm,),
        in_specs=[row, halo, vec, pl.BlockSpec(w.shape, lambda i: (0, 0, 0)), vec],
        out_specs=[row, row], out_shape=[SDS((lp, d), F32), SDS((lp, d), BF16)],
        compiler_params=_cp(1))(h, h, gain, w, scale)


def _pool_bwd(h, gain, w, scale, diff, dh1, name):
    lp, d = h.shape
    tm = ROW_TILE
    hb = POOL_HALO
    nblk = lp // tm
    ext = tm + hb

    def body(h_ref, g_ref, w_ref, s_ref, diff_ref, dh_ref, dhn_ref, dh0_ref, dw_ref, ds_ref, dg_ref):
        i = pl.program_id(0)
        g = g_ref[...]
        x = h_ref[...]
        r = _rstd(x)
        u = x * r
        dh = dh_ref[...]
        dhn = jnp.where(i < nblk - 1, dhn_ref[...], 0.0)
        dyp = jnp.concatenate([dh, dhn], axis=0) * s_ref[...]
        pos = i * tm + lax.broadcasted_iota(jnp.int32, (ext, 1), 0)
        dn_parts, dw_parts, ds_parts = [], [], []
        for gi, win in enumerate(POOL_WINDOWS):
            c0 = gi * POOL_C
            wg = w_ref[gi]
            dyp_g = dyp[:, c0:c0 + POOL_C].astype(BF16)
            dd = _dot(dyp_g, wg, NT)
            dfg = diff_ref[:, c0:c0 + POOL_C]
            dw_parts.append(_dot(dfg, dyp_g[:tm], TN))
            ds_parts.append(jnp.sum(dh[:, c0:c0 + POOL_C] * _dot(dfg, wg), axis=0, keepdims=True))
            count = jnp.minimum(pos + 1, win).astype(F32)
            cur = dd / count
            sh = 1
            while sh < win:
                cur = cur + pltpu.roll(cur, ext - sh, 0)
                sh *= 2
            dn_parts.append(cur[:tm] - dd[:tm])
        dn = jnp.concatenate(dn_parts, axis=1)
        du = dn * g
        dh0_ref[...] = dh + r * (du - u * jnp.mean(du * u, axis=-1, keepdims=True))
        dgp = jnp.sum(dn * u, axis=0, keepdims=True)
        dsp = jnp.concatenate(ds_parts, axis=1)

        @pl.when(i == 0)
        def _():
            for gi in range(len(POOL_WINDOWS)):
                dw_ref[gi] = dw_parts[gi]
            ds_ref[...] = dsp
            dg_ref[...] = dgp

        @pl.when(i > 0)
        def _():
            for gi in range(len(POOL_WINDOWS)):
                dw_ref[gi] += dw_parts[gi]
            ds_ref[...] += dsp
            dg_ref[...] += dgp

    row = pl.BlockSpec((tm, d), lambda i: (i, 0))
    nxt = pl.BlockSpec((hb, d), lambda i: (jnp.minimum((i + 1) * (tm // hb), lp // hb - 1), 0))
    vec = pl.BlockSpec((1, d), lambda i: (0, 0))
    wsp = pl.BlockSpec(w.shape, lambda i: (0, 0, 0))
    return pl.pallas_call(
        body, name=name, grid=(nblk,),
        in_specs=[row, vec, wsp, vec, row, row, nxt],
        out_specs=[row, wsp, vec, vec],
        out_shape=[SDS((lp, d), F32), SDS(w.shape, F32), SDS((1, d), F32), SDS((1, d), F32)],
        compiler_params=_cp(1))(h, gain, w, scale, diff, dh1, dh1)


def _conv_taps(x, halo, first):
    ext = jnp.concatenate([jnp.where(first, 0.0, halo), x], axis=0)
    return pltpu.roll(ext, 1, 0)[CONV_HALO:], pltpu.roll(ext, 2, 0)[CONV_HALO:]


def _ffn_specs(tm, c, lp):
    blk = pl.BlockSpec((2, 1, tm, c), lambda g, i: (0, g, i, 0))
    halo = pl.BlockSpec((2, 1, CONV_HALO, c), lambda g, i: (0, g, jnp.maximum(i * (tm // CONV_HALO) - 1, 0), 0))
    cw = pl.BlockSpec((2, 1, 3, c), lambda g, i: (0, g, 0, 0))
    cb = pl.BlockSpec((2, 1, 1, c), lambda g, i: (0, g, 0, 0))
    return blk, halo, cw, cb


def _ffn_act_fwd(up4, cw4, cb4, name):
    _, ng, lp, c = up4.shape
    tm = ROW_TILE

    def body(up_ref, halo_ref, cw_ref, cb_ref, act_ref):
        first = pl.program_id(1) == 0
        u = []
        for half in range(2):
            x = up_ref[half, 0]
            xm1, xm2 = _conv_taps(x, halo_ref[half, 0], first)
            u.append(cb_ref[half, 0] + cw_ref[half, 0, 0:1, :] * xm2 + cw_ref[half, 0, 1:2, :] * xm1
                     + cw_ref[half, 0, 2:3, :] * x)
        gate, val = u
        sig = 1.0 / (1.0 + jnp.exp(-gate))
        act_ref[0] = (gate * sig * val).astype(BF16)

    blk, halo, cw, cb = _ffn_specs(tm, c, lp)
    return pl.pallas_call(
        body, name=name, grid=(ng, lp // tm),
        in_specs=[blk, halo, cw, cb],
        out_specs=pl.BlockSpec((1, tm, c), lambda g, i: (g, i, 0)),
        out_shape=SDS((ng, lp, c), BF16), compiler_params=_cp(2))(up4, up4, cw4, cb4)


def _ffn_act_bwd(up4, cw4, cb4, dact, name):
    _, ng, lp, c = up4.shape
    tm = ROW_TILE
    hb = CONV_HALO
    nblk = lp // tm
    ext = tm + hb

    def body(up_ref, prev_ref, next_ref, cw_ref, cb_ref, da_ref, dan_ref, dup_ref, dcw_ref, dcb_ref):
        i = pl.program_id(1)
        first = i == 0
        last = i == nblk - 1
        da = jnp.concatenate([da_ref[0], jnp.where(last, 0.0, dan_ref[0])], axis=0)
        u, taps = [], []
        for half in range(2):
            rows = jnp.concatenate([jnp.where(first, 0.0, prev_ref[half, 0]), up_ref[half, 0],
                                    jnp.where(last, 0.0, next_ref[half, 0])], axis=0)
            x, xm1, xm2 = rows[hb:], pltpu.roll(rows, 1, 0)[hb:], pltpu.roll(rows, 2, 0)[hb:]
            u.append(cb_ref[half, 0] + cw_ref[half, 0, 0:1, :] * xm2 + cw_ref[half, 0, 1:2, :] * xm1
                     + cw_ref[half, 0, 2:3, :] * x)
            taps.append((xm2, xm1, x))
        gate, val = u
        sig = 1.0 / (1.0 + jnp.exp(-gate))
        dus = (da * val * (sig * (1.0 + gate * (1.0 - sig))), da * (gate * sig))
        sums = []
        for half in range(2):
            du = dus[half]
            dup_ref[half, 0] = (cw_ref[half, 0, 2:3, :] * du[:tm] + cw_ref[half, 0, 1:2, :] * pltpu.roll(du, ext - 1, 0)[:tm]
                                + cw_ref[half, 0, 0:1, :] * pltpu.roll(du, ext - 2, 0)[:tm]).astype(BF16)
            sums.append([jnp.sum(du[:tm] * t[:tm], axis=0, keepdims=True) for t in taps[half]]
                        + [jnp.sum(du[:tm], axis=0, keepdims=True)])

        @pl.when(first)
        def _():
            for half in range(2):
                for k in range(3):
                    dcw_ref[half, 0, k:k + 1, :] = sums[half][k]
                dcb_ref[half, 0] = sums[half][3]

        @pl.when(i > 0)
        def _():
            for half in range(2):
                for k in range(3):
                    dcw_ref[half, 0, k:k + 1, :] += sums[half][k]
                dcb_ref[half, 0] += sums[half][3]

    blk, prev, cw, cb = _ffn_specs(tm, c, lp)

    def next_rows(g, i):
        return jnp.minimum((i + 1) * (tm // hb), lp // hb - 1)

    return pl.pallas_call(
        body, name=name, grid=(ng, nblk),
        in_specs=[blk, prev, pl.BlockSpec((2, 1, hb, c), lambda g, i: (0, g, next_rows(g, i), 0)), cw, cb,
                  pl.BlockSpec((1, tm, c), lambda g, i: (g, i, 0)),
                  pl.BlockSpec((1, hb, c), lambda g, i: (g, next_rows(g, i), 0))],
        out_specs=[blk, cw, cb],
        out_shape=[SDS(up4.shape, BF16), SDS(cw4.shape, F32), SDS(cb4.shape, F32)],
        compiler_params=_cp(2))(up4, up4, up4, cw4, cb4, dact, dact)


def _mm_tile(rows):
    return _row_tile(rows, MM_ROWS_MAX)


def _mm_group(a, b, dims, out_dtype, name):
    m, k = a.shape
    ng = b.shape[0]
    n = b.shape[2] if dims == NN else b.shape[1]
    tm = _mm_tile(m)

    def body(a_ref, b_ref, o_ref):
        o_ref[0] = _dot(a_ref[...].astype(BF16), b_ref[0], dims).astype(out_dtype)

    return pl.pallas_call(
        body, name=name, grid=(ng, m // tm),
        in_specs=[pl.BlockSpec((tm, k), lambda g, i: (i, 0)),
                  pl.BlockSpec((1,) + b.shape[1:], lambda g, i: (g, 0, 0))],
        out_specs=pl.BlockSpec((1, tm, n), lambda g, i: (g, i, 0)),
        out_shape=SDS((ng, m, n), out_dtype), compiler_params=_cp(2))(a, b)


def _mm_reduce(a, b, dims, res, name, ride=None):
    ng, m, k = a.shape
    n = b.shape[2] if dims == NN else b.shape[1]
    tm = _mm_tile(m)
    has_res = res is not None

    def body(a_ref, b_ref, *refs):
        o_ref, acc_ref = refs[-2], refs[-1]
        g = pl.program_id(1)
        p = _dot(a_ref[0].astype(BF16), b_ref[0], dims)

        @pl.when(g == 0)
        def _():
            acc_ref[...] = p + refs[0][...] if has_res else p

        @pl.when(g > 0)
        def _():
            acc_ref[...] += p

        @pl.when(g == ng - 1)
        def _():
            o_ref[...] = acc_ref[...]

    row = pl.BlockSpec((tm, n), lambda i, g: (i, 0))
    nblk = m // tm
    ride_in, ride_out, ride_shape, ride_sems = _ride_args(ride)
    out = pl.pallas_call(
        _riding(body, 3 if has_res else 2, 1, ride,
                lambda: (pl.program_id(0) == 0) & (pl.program_id(1) == 0),
                lambda: (pl.program_id(0) == nblk - 1) & (pl.program_id(1) == ng - 1)),
        name=name, grid=(nblk, ng),
        in_specs=[pl.BlockSpec((1, tm, k), lambda i, g: (g, i, 0)),
                  pl.BlockSpec((1,) + b.shape[1:], lambda i, g: (g, 0, 0))] + ([row] if has_res else []) + ride_in,
        out_specs=[row] + ride_out, out_shape=[SDS((m, n), F32)] + ride_shape,
        scratch_shapes=[pltpu.VMEM((tm, n), F32)] + ride_sems,
        compiler_params=_cp(2))(a, b, *([res] if has_res else []), *_ride_arrays(ride))
    return out[0] if ride is None else (out[0], out[1:])


def _mm_tn(a, b, name):
    ga, m, ka = a.shape
    gb, _, n = b.shape
    ng = max(ga, gb)
    tk = _mm_tile(m)
    nk = m // tk

    def body(a_ref, b_ref, o_ref, acc_ref):
        s = pl.program_id(1)
        p = _dot(a_ref[0].astype(BF16), b_ref[0].astype(BF16), TN)

        @pl.when(s == 0)
        def _():
            acc_ref[...] = p

        @pl.when(s > 0)
        def _():
            acc_ref[...] += p

        @pl.when(s == nk - 1)
        def _():
            o_ref[0] = acc_ref[...].astype(BF16)

    return pl.pallas_call(
        body, name=name, grid=(ng, nk),
        in_specs=[pl.BlockSpec((1, tk, ka), (lambda g, s: (g, s, 0)) if ga > 1 else (lambda g, s: (0, s, 0))),
                  pl.BlockSpec((1, tk, n), (lambda g, s: (g, s, 0)) if gb > 1 else (lambda g, s: (0, s, 0)))],
        out_specs=pl.BlockSpec((1, ka, n), lambda g, s: (g, 0, 0)),
        out_shape=SDS((ng, ka, n), BF16),
        scratch_shapes=[pltpu.VMEM((ka, n), F32)], compiler_params=_cp(2))(a, b)


def _pair_tri(kind, sign):
    r = jnp.arange(2 * ATT_BLK)[:, None]
    c = jnp.arange(2 * ATT_BLK)[None, :]
    same = (r < ATT_BLK) == (c < ATT_BLK)
    rel = {"from": r >= c, "before": r < c}[kind]
    return ((same & rel) * sign).astype(BF16)


def _scan_dot(x, tri):
    hi = x.astype(BF16)
    lo = (x - hi.astype(F32)).astype(BF16)
    return _dot(hi, tri) + _dot(lo, tri)


def _split_heads(blk, lane_a):
    zero = jnp.zeros_like(blk)
    return jnp.concatenate([jnp.where(lane_a, blk, zero), jnp.where(lane_a, zero, blk)], axis=0)


def _softplus(z):
    return jnp.maximum(z, 0.0) + jnp.log(1.0 + jnp.exp2(jnp.abs(z) * (-LOG2_E)))


def _visible(qi, j):
    t = qi * ATT_Q + lax.broadcasted_iota(jnp.int32, (ATT_Q, 2 * ATT_BLK), 0)
    s = j * ATT_BLK + (lax.broadcasted_iota(jnp.int32, (ATT_Q, 2 * ATT_BLK), 1) & (ATT_BLK - 1))
    return s < t


def _halves(x):
    return x[:, :ATT_BLK], x[:, ATT_BLK:]


def _rowsum(x):
    return jnp.sum(x, axis=1, keepdims=True)


def _attn_ends(lp):
    last_q = lp // ATT_Q - 1
    return (lambda: (pl.program_id(0) == 0) & (pl.program_id(1) == 0),
            lambda: (pl.program_id(0) == HEAD_PAIRS - 1) & (pl.program_id(1) == last_q))


def _attn_specs(lp):
    bk = ATT_BLK
    qblk = pl.BlockSpec((ATT_Q, bk), lambda p, i: (i, p))
    kblk = pl.BlockSpec((1, lp, bk), lambda p, i: (p // 2, 0, p % 2))
    vblk = pl.BlockSpec((1, lp, bk), lambda p, i: (HEAD_PAIRS // 2 + p // 2, 0, p % 2))
    tri = pl.BlockSpec((2 * bk, 2 * bk), lambda p, i: (0, 0))
    return qblk, kblk, vblk, tri


def _attn_fwd(q, kv, name, ride=None):
    lp, d = q.shape
    bk = ATT_BLK

    def body(q_ref, k_ref, v_ref, tri_ref, o_ref, t_ref):
        qi = pl.program_id(1)
        qs = q_ref[...] * (HEAD_DIM ** -0.5)
        lane_a = lax.broadcasted_iota(jnp.int32, (1, bk), 1) < HEAD_DIM
        tri = tri_ref[...]

        def trip(js, carry, masked):
            oacc, ca, cb = carry
            rows = [pl.ds(pl.multiple_of(j * bk, bk), bk) for j in js]
            zs = [_dot(qs, _split_heads(k_ref[0, r, :], lane_a), NT) for r in rows]
            ms = [_softplus(z) for z in zs]
            if masked:
                ms = [jnp.where(_visible(qi, j), m, 0.0) for j, m in zip(js, ms)]
            ws = [_scan_dot(m, tri) for m in ms]
            for j, r, z, m, w in zip(js, rows, zs, ms, ws):
                exa, exb = _halves(z + w)
                a = jnp.concatenate([jnp.exp(exa - ca), jnp.exp(exb - cb)], axis=1)
                if masked:
                    a = jnp.where(_visible(qi, j), a, 0.0)
                oacc = oacc + _dot(a.astype(BF16), _split_heads(v_ref[0, r, :], lane_a))
                ma, mb = _halves(m)
                ca, cb = ca + _rowsum(ma), cb + _rowsum(mb)
            return oacc, ca, cb

        carry = (jnp.zeros((ATT_Q, bk), F32), jnp.zeros((ATT_Q, 1), F32), jnp.zeros((ATT_Q, 1), F32))
        top = (qi + 1) * ATT_UNROLL - 1
        carry = trip([top - u for u in range(ATT_UNROLL)], carry, True)
        oacc, ca, cb = lax.fori_loop(
            0, qi, lambda g, c: trip([top - (g + 1) * ATT_UNROLL - u for u in range(ATT_UNROLL)], c, False), carry)
        o_ref[...] = oacc.astype(BF16)
        t_ref[...] = jnp.where(lane_a, ca, cb)

    qblk, kblk, vblk, tri = _attn_specs(lp)
    ride_in, ride_out, ride_shape, ride_sems = _ride_args(ride)
    out = pl.pallas_call(
        _riding(body, 4, 2, ride, *_attn_ends(lp)), name=name, grid=(HEAD_PAIRS, lp // ATT_Q),
        in_specs=[qblk, kblk, vblk, tri] + ride_in,
        out_specs=[qblk, qblk] + ride_out, out_shape=[SDS((lp, d), BF16), SDS((lp, d), F32)] + ride_shape,
        scratch_shapes=ride_sems,
        compiler_params=_cp(2))(q, kv, kv, _pair_tri("from", -1), *_ride_arrays(ride))
    return out[0], out[1], out[2:]


def _attn_bwd(q, kv, do, tot, name, ride=None):
    lp, d = q.shape
    bk = ATT_BLK
    scale = HEAD_DIM ** -0.5

    def body(q_ref, k_ref, v_ref, do_ref, t_ref, tri_ref, dq_ref, dk_ref, dv_ref):
        qi = pl.program_id(1)

        @pl.when(qi == 0)
        def _():
            dk_ref[...] = jnp.zeros_like(dk_ref)
            dv_ref[...] = jnp.zeros_like(dv_ref)

        qs = q_ref[...] * scale
        do_blk = do_ref[...]
        lane_a = lax.broadcasted_iota(jnp.int32, (1, bk), 1) < HEAD_DIM
        tot_blk = t_ref[...]
        ta = jnp.max(jnp.where(lane_a, tot_blk, -jnp.inf), axis=1, keepdims=True)
        tb = jnp.max(jnp.where(lane_a, -jnp.inf, tot_blk), axis=1, keepdims=True)
        tri = tri_ref[...]

        def trip(js, carry, masked):
            dq, pa, pb, ea, eb = carry
            rows = [pl.ds(pl.multiple_of(j * bk, bk), bk) for j in js]
            kks = [_split_heads(k_ref[0, r, :], lane_a) for r in rows]
            zs = [_dot(qs, kk, NT) for kk in kks]
            das = [_dot(do_blk, _split_heads(v_ref[0, r, :], lane_a), NT) for r in rows]
            ms = [_softplus(z) for z in zs]
            if masked:
                ms = [jnp.where(_visible(qi, j), m, 0.0) for j, m in zip(js, ms)]
            xs = [_scan_dot(m, tri) for m in ms]
            es, a_bf = [], []
            for j, z, m, x, da in zip(js, zs, ms, xs, das):
                xa, xb = _halves(z + x)
                a = jnp.concatenate([jnp.exp(xa + pa), jnp.exp(xb + pb)], axis=1)
                if masked:
                    a = jnp.where(_visible(qi, j), a, 0.0)
                a_bf.append(a.astype(BF16))
                es.append(a * da)
                ma, mb = _halves(m)
                pa, pb = pa + _rowsum(ma), pb + _rowsum(mb)
            ss = [_dot(e.astype(BF16), tri) for e in es]
            for j, r, kk, z, m, e, s, ab in zip(js, rows, kks, zs, ms, es, ss, a_bf):
                sa, sb = _halves(s)
                e_before = jnp.concatenate([sa + ea, sb + eb], axis=1)
                dz = e - jnp.exp(z - m) * (e + e_before)
                if masked:
                    dz = jnp.where(_visible(qi, j), dz, 0.0)
                dzb = dz.astype(BF16)
                dq = dq + _dot(dzb, kk)
                rk = _dot(dzb, qs, TN)
                rv = _dot(ab, do_blk, TN)
                dk_ref[0, r, :] += jnp.where(lane_a, rk[:bk], rk[bk:])
                dv_ref[0, r, :] += jnp.where(lane_a, rv[:bk], rv[bk:])
                e_a, e_b = _halves(e)
                ea, eb = ea + _rowsum(e_a), eb + _rowsum(e_b)
            return dq, pa, pb, ea, eb

        zcol = jnp.zeros((ATT_Q, 1), F32)
        carry = lax.fori_loop(
            0, qi, lambda g, c: trip([g * ATT_UNROLL + u for u in range(ATT_UNROLL)], c, False),
            (jnp.zeros((ATT_Q, bk), F32), -ta, -tb, zcol, zcol))
        carry = trip([qi * ATT_UNROLL + u for u in range(ATT_UNROLL)], carry, True)
        dq_ref[...] = (carry[0] * scale).astype(BF16)

    qblk, kblk, vblk, tri = _attn_specs(lp)
    ride_in, ride_out, ride_shape, ride_sems = _ride_args(ride)
    out = pl.pallas_call(
        _riding(body, 6, 3, ride, *_attn_ends(lp)), name=name, grid=(HEAD_PAIRS, lp // ATT_Q),
        in_specs=[qblk, kblk, vblk, qblk, qblk, tri] + ride_in,
        out_specs=[qblk, kblk, kblk] + ride_out,
        out_shape=[SDS((lp, d), BF16), SDS((HEAD_PAIRS // 2, lp, 2 * bk), F32),
                   SDS((HEAD_PAIRS // 2, lp, 2 * bk), F32)] + ride_shape,
        scratch_shapes=ride_sems,
        compiler_params=_cp(2))(q, kv, kv, do, tot, _pair_tri("before", 1), *_ride_arrays(ride))
    return out[0], out[1], out[2], out[3:]


def _mesh_pos():
    return lax.axis_index("x"), lax.axis_index("y"), lax.axis_index("c")


def _flip(pos, r):
    x, y, c = pos
    return (1 - x if r & 4 else x, 1 - y if r & 2 else y, 1 - c if r & 1 else c)


def _dev_index(pos):
    return 4 * pos[0] + 2 * pos[1] + pos[2]


class _Ride(NamedTuple):
    kind: str
    arrays: list


def _ride_arrays(ride):
    return [] if ride is None else ride.arrays


def _ride_args(ride):
    if ride is None:
        return [], [], [], []
    n = len(ride.arrays)
    hbm = pl.BlockSpec(memory_space=pl.ANY)
    shapes = [SDS((N_DEV,) + x.shape if ride.kind == "gather" else x.shape, x.dtype) for x in ride.arrays]
    sems = [pltpu.SemaphoreType.DMA((7 * n,)), pltpu.SemaphoreType.DMA((7 * n,)), pltpu.SemaphoreType.DMA((n,))]
    return [hbm] * n, [hbm] * n, shapes, sems


def _riding(body, n_in, n_out, ride, first, last):
    if ride is None:
        return body
    n = len(ride.arrays)
    gather = ride.kind == "gather"

    def wrapped(*refs):
        ins, srcs = refs[:n_in], refs[n_in:n_in + n]
        outs, dsts = refs[n_in + n:n_in + n + n_out], refs[n_in + n + n_out:n_in + 2 * n + n_out]
        scratch, (send_sems, recv_sems, local_sems) = refs[n_in + 2 * n + n_out:-3], refs[-3:]
        me = _mesh_pos()
        mi = _dev_index(me)
        local, sends, lands = [], [], []
        for a in range(n):
            local.append(pltpu.make_async_copy(srcs[a] if gather else srcs[a].at[mi], dsts[a].at[mi], local_sems.at[a]))
            for r in range(1, N_DEV):
                peer = _flip(me, r)
                pi = _dev_index(peer)
                sems = dict(send_sem=send_sems.at[7 * a + r - 1], recv_sem=recv_sems.at[7 * a + r - 1],
                            device_id=peer, device_id_type=pl.DeviceIdType.MESH)
                sends.append(pltpu.make_async_remote_copy(
                    src_ref=srcs[a] if gather else srcs[a].at[pi], dst_ref=dsts[a].at[mi], **sems))
                lands.append(pltpu.make_async_remote_copy(src_ref=dsts[a].at[pi], dst_ref=dsts[a].at[pi], **sems))

        @pl.when(first())
        def _():
            for cp in local + sends:
                cp.start()

        body(*ins, *outs, *scratch)

        @pl.when(last())
        def _():
            for cp in lands:
                cp.wait_recv()
            for cp in sends:
                cp.wait_send()
            for cp in local:
                cp.wait()

    return wrapped


def _all_gather(xs, name):
    n = len(xs)

    def body(*refs):
        x_refs, out_refs = refs[:n], refs[n:2 * n]
        send_sems, recv_sems, local_sems = refs[2 * n:]
        me = _mesh_pos()
        sibling = _flip(me, 1)
        others = [_flip(me, 4), _flip(me, 2), _flip(me, 6)]

        def copy(a, k, block, to, own=False):
            slab = out_refs[a].at[_dev_index(block)]
            return pltpu.make_async_remote_copy(
                src_ref=x_refs[a] if own else slab, dst_ref=slab,
                send_sem=send_sems.at[7 * a + k], recv_sem=recv_sems.at[7 * a + k],
                device_id=to, device_id_type=pl.DeviceIdType.MESH)

        mine = [pltpu.make_async_copy(x_refs[a], out_refs[a].at[_dev_index(me)], local_sems.at[a]) for a in range(n)]
        first = []
        for a in range(n):
            mine[a].start()
            first += [copy(a, 0, me, sibling, own=True)] + [copy(a, 1 + j, me, o, own=True) for j, o in enumerate(others)]
        for cp in first:
            cp.start()
        passed = []
        for a in range(n):
            for j, o in enumerate(others):
                copy(a, 1 + j, o, me).wait_recv()
                passed.append(copy(a, 4 + j, o, sibling))
                passed[-1].start()
        for a in range(n):
            copy(a, 0, sibling, me).wait_recv()
            for j, o in enumerate(others):
                copy(a, 4 + j, _flip(o, 1), me).wait_recv()
        for cp in first + passed:
            cp.wait_send()
        for cp in mine:
            cp.wait()

    hbm = pl.BlockSpec(memory_space=pl.ANY)
    return pl.pallas_call(
        body, name=name, out_shape=[SDS((N_DEV,) + x.shape, x.dtype) for x in xs],
        in_specs=[hbm] * n, out_specs=[hbm] * n,
        scratch_shapes=[pltpu.SemaphoreType.DMA((7 * n,)), pltpu.SemaphoreType.DMA((7 * n,)), pltpu.SemaphoreType.DMA((n,))],
    )(*xs)


def _scatter_to_sibling(ps, name):
    n = len(ps)

    def body(*refs):
        p_refs, out_refs = refs[:n], refs[n:2 * n]
        send_sems, recv_sems = refs[2 * n:]
        me = _mesh_pos()
        sibling = _flip(me, 1)
        copies = []
        for a in range(n):
            for k in range(4):
                copies.append(pltpu.make_async_remote_copy(
                    src_ref=p_refs[a].at[2 * k + sibling[2]], dst_ref=out_refs[a].at[k],
                    send_sem=send_sems.at[4 * a + k], recv_sem=recv_sems.at[4 * a + k],
                    device_id=sibling, device_id_type=pl.DeviceIdType.MESH))
                copies[-1].start()
        for cp in copies:
            cp.wait_recv()
        for cp in copies:
            cp.wait_send()

    hbm = pl.BlockSpec(memory_space=pl.ANY)
    return pl.pallas_call(
        body, name=name, out_shape=[SDS((4,) + p.shape[1:], p.dtype) for p in ps],
        in_specs=[hbm] * n, out_specs=[hbm] * n,
        scratch_shapes=[pltpu.SemaphoreType.DMA((4 * n,)), pltpu.SemaphoreType.DMA((4 * n,))],
    )(*ps)


def _scatter_to_chips(qs, name):
    n = len(qs)

    def body(*refs):
        q_refs, out_refs = refs[:n], refs[n:2 * n]
        send_sems, recv_sems, local_sems = refs[2 * n:]
        me = _mesh_pos()
        my_chip = 2 * me[0] + me[1]
        mine, copies, landed = [], [], []
        for a in range(n):
            mine.append(pltpu.make_async_copy(q_refs[a].at[my_chip], out_refs[a].at[my_chip], local_sems.at[a]))
            mine[-1].start()
            for j, r in enumerate((4, 2, 6)):
                peer = _flip(me, r)
                peer_chip = 2 * peer[0] + peer[1]
                copies.append(pltpu.make_async_remote_copy(
                    src_ref=q_refs[a].at[peer_chip], dst_ref=out_refs[a].at[my_chip],
                    send_sem=send_sems.at[3 * a + j], recv_sem=recv_sems.at[3 * a + j],
                    device_id=peer, device_id_type=pl.DeviceIdType.MESH))
                copies[-1].start()
                slab = out_refs[a].at[peer_chip]
                landed.append(pltpu.make_async_remote_copy(
                    src_ref=slab, dst_ref=slab, send_sem=send_sems.at[3 * a + j], recv_sem=recv_sems.at[3 * a + j],
                    device_id=peer, device_id_type=pl.DeviceIdType.MESH))
        for cp in landed:
            cp.wait_recv()
        for cp in copies:
            cp.wait_send()
        for cp in mine:
            cp.wait()

    hbm = pl.BlockSpec(memory_space=pl.ANY)
    return pl.pallas_call(
        body, name=name, out_shape=[SDS(q.shape, q.dtype) for q in qs],
        in_specs=[hbm] * n, out_specs=[hbm] * n,
        scratch_shapes=[pltpu.SemaphoreType.DMA((3 * n,)), pltpu.SemaphoreType.DMA((3 * n,)), pltpu.SemaphoreType.DMA((n,))],
    )(*qs)


def _pair_sum(p, r, name):
    _, rows, cols = p.shape
    tr = _row_tile(rows, SUM_ROWS_MAX, 16)
    p4 = p.reshape(4, 2, rows, cols)

    def body(p_ref, r_ref, o_ref):
        own = jnp.where(lax.axis_index("c") == 0, p_ref[0, 0].astype(F32), p_ref[0, 1].astype(F32))
        o_ref[0] = (own + r_ref[0].astype(F32)).astype(BF16)

    blk = pl.BlockSpec((1, tr, cols), lambda k, i: (k, i, 0))
    return pl.pallas_call(
        body, name=name, grid=(4, rows // tr),
        in_specs=[pl.BlockSpec((1, 2, tr, cols), lambda k, i: (k, 0, i, 0)), blk],
        out_specs=blk, out_shape=SDS((4, rows, cols), BF16), compiler_params=_cp(2))(p4, r)


def _sum_slabs(a, name):
    n, rows, cols = a.shape
    tr = rows if a.size * a.dtype.itemsize <= SUM_WHOLE_BYTES else _row_tile(rows, SUM_ROWS_MAX, 16)

    def body(a_ref, o_ref):
        acc = a_ref[0].astype(F32)
        for k in range(1, n):
            acc = acc + a_ref[k].astype(F32)
        o_ref[...] = acc

    return pl.pallas_call(
        body, name=name, grid=(rows // tr,),
        in_specs=[pl.BlockSpec((n, tr, cols), lambda i: (0, i, 0))],
        out_specs=pl.BlockSpec((tr, cols), lambda i: (i, 0)),
        out_shape=SDS((rows, cols), F32), compiler_params=_cp(1))(a)


def _reduce_scatter(ps, tag):
    from_sibling = _scatter_to_sibling(ps, f"scatter_sibling_{tag}")
    qs = [_pair_sum(p, r, f"pair_sum_{tag}{a}") for a, (p, r) in enumerate(zip(ps, from_sibling))]
    from_chips = _scatter_to_chips(qs, f"scatter_chips_{tag}")
    return [_sum_slabs(r, f"sum_{tag}{a}") for a, r in enumerate(from_chips)]


def _adamw(w, g, m, v, name):
    rows, cols = w.shape
    tr = _row_tile(rows, 352)

    def body(w_ref, g_ref, m_ref, v_ref, d_ref, mo_ref, vo_ref):
        g_ = g_ref[...]
        m_ = ADAM_B1 * m_ref[...] + (1.0 - ADAM_B1) * g_
        v_ = ADAM_B2 * v_ref[...] + (1.0 - ADAM_B2) * (g_ * g_)
        m_hat = m_ / (1.0 - ADAM_B1 ** ADAM_STEP)
        v_hat = v_ / (1.0 - ADAM_B2 ** ADAM_STEP)
        d_ref[...] = -ADAM_LR * (m_hat / (jnp.sqrt(v_hat) + ADAM_EPS) + ADAM_WD * w_ref[...])
        mo_ref[...] = m_
        vo_ref[...] = v_

    blk = pl.BlockSpec((tr, cols), lambda i: (i, 0))
    return pl.pallas_call(
        body, name=name, grid=(rows // tr,),
        in_specs=[blk] * 4, out_specs=[blk] * 3, out_shape=[SDS((rows, cols), F32)] * 3,
        compiler_params=_cp(1))(w, g, m, v)


def _ffn_fwd(h, gain, w_up, cw4, cb4, w_down4, tag):
    (n2,) = _rms_fwd(h, gain, f"ffn_norm_{tag}")
    up = _mm_group(n2, w_up, NN, F32, f"ffn_up_{tag}")
    up4 = up.reshape((2, 4) + up.shape[1:])
    act = _ffn_act_fwd(up4, cw4, cb4, f"ffn_act_{tag}")
    out = _mm_reduce(act, w_down4, NN, h, f"ffn_down_{tag}")
    return out, (n2, up4, act)


def _ffn_bwd(h, gain, w_up, cw4, cb4, w_down4, saved, dh, tag, ride=None):
    n2, up4, act = saved
    dact = _mm_group(dh, w_down4, NT, F32, f"ffn_dact_{tag}")
    d_w_down = _mm_tn(act, dh[None], f"ffn_dwdown_{tag}")
    dup4, dcw4, dcb4 = _ffn_act_bwd(up4, cw4, cb4, dact, f"ffn_dgate_{tag}")
    dup = dup4.reshape((8,) + dup4.shape[2:])
    dn2 = _mm_reduce(dup, w_up, NT, None, f"ffn_dnorm_{tag}", ride)
    rode = None
    if ride is not None:
        dn2, rode = dn2
    d_w_up = _mm_tn(n2[None], dup, f"ffn_dwup_{tag}")
    dh_in, dgain = _rms_bwd(h, gain, [dn2], dh, f"ffn_dh_{tag}")
    return dh_in, dgain, d_w_up, d_w_down, dcw4, dcb4, rode


def kernel(x, meta_tokens, mix_norm, ffn_norm, pool_w, pool_scale, kv_norm, w_kv, w_q, w_o, ffn_w_up, ffn_conv_w, ffn_conv_b, ffn_w_down, final_norm, loss_target, m_meta_tokens, m_mix_norm, m_ffn_norm, m_pool_w, m_pool_scale, m_kv_norm, m_w_kv, m_w_q, m_w_o, m_ffn_w_up, m_ffn_conv_w, m_ffn_conv_b, m_ffn_w_down, m_final_norm, v_meta_tokens, v_mix_norm, v_ffn_norm, v_pool_w, v_pool_scale, v_kv_norm, v_w_kv, v_w_q, v_w_o, v_ffn_w_up, v_ffn_conv_w, v_ffn_conv_b, v_ffn_w_down, v_final_norm):
    seq, d = x.shape[1], x.shape[2]
    n_tok = N_META + seq
    lp = -(-n_tok // ROW_TILE) * ROW_TILE
    fc = ffn_w_up.shape[2]
    me = _dev_index(_mesh_pos())

    def rows_of(parts):
        rows = [p.size // d for p in parts]
        return [sum(rows[:k]) for k in range(len(parts) + 1)]

    def bf16_rows(parts):
        return jnp.concatenate([p.reshape(-1, d) for p in parts], axis=0).astype(BF16)

    now_parts, later_parts = [pool_w, w_kv, w_q, ffn_w_down[0]], [w_o, ffn_w_down[1]]
    now_off, later_off = rows_of(now_parts), rows_of(later_parts)
    g_now, wup0 = _all_gather([bf16_rows(now_parts), ffn_w_up[0].astype(BF16)], "gather_matrices")
    later = _Ride("gather", [bf16_rows(later_parts), ffn_w_up[1].astype(BF16)])
    pw = g_now[:, now_off[0]:now_off[1]].reshape(N_DEV, 4, POOL_C // N_DEV, POOL_C).transpose(1, 0, 2, 3)
    pw = pw.reshape(4, POOL_C, POOL_C)
    wkv = g_now[:, now_off[1]:now_off[2]].reshape(N_DEV, d, 2 * d // N_DEV)
    wq = g_now[:, now_off[2]:now_off[3]].reshape(1, d, d)
    wdn0 = g_now[:, now_off[3]:now_off[4]].reshape(4, fc, d)

    small_parts = [meta_tokens, pool_scale, ffn_conv_w]
    small_rows = [p.size // 128 for p in small_parts]
    small_pad = -sum(small_rows) % 8
    local_small = jnp.concatenate([p.reshape(-1, 128) for p in small_parts] + [jnp.zeros((small_pad, 128), F32)], axis=0)
    (gs,) = _all_gather([local_small], "gather_vectors")
    r0, r1, r2 = small_rows[0], small_rows[0] + small_rows[1], sum(small_rows)
    meta_full = gs[:, :r0].transpose(1, 0, 2).reshape(N_META, d)
    pscale = gs[:, r0:r1].reshape(1, d)
    cw = gs[:, r1:r2].reshape(N_DEV, 2, 3, fc)
    cw4_l = [cw[:, l].reshape(2, 4, 3, fc) for l in range(2)]
    cb4_l = [ffn_conv_b[l].reshape(2, 4, 1, fc) for l in range(2)]

    h0 = jnp.concatenate([meta_full, x[0], jnp.zeros((lp - n_tok, d), F32)], axis=0)
    h1, diff = _pool_fwd(h0, mix_norm[0:1], pw, pscale, "pool_fwd")
    h2, saved0 = _ffn_fwd(h1, ffn_norm[0:1], wup0, cw4_l[0], cb4_l[0], wdn0, "0")
    gains_b = jnp.stack([kv_norm, mix_norm[1]], axis=0)
    kvn, n3 = _rms_fwd(h2, gains_b, "attn_norms")
    kv = _mm_group(kvn, wkv, NN, BF16, "kv_proj")
    q = _mm_group(n3, wq, NN, BF16, "q_proj")[0]
    o, tot, (g_later, wup1) = _attn_fwd(q, kv, "attn_fwd", later)
    wo = g_later[:, later_off[0]:later_off[1]].reshape(1, d, d)
    wdn1 = g_later[:, later_off[1]:later_off[2]].reshape(4, fc, d)
    h3 = _mm_reduce(o[None], wo, NN, h2, "o_proj")
    h4, saved1 = _ffn_fwd(h3, ffn_norm[1:2], wup1, cw4_l[1], cb4_l[1], wdn1, "1")
    target = jnp.pad(loss_target[0], ((N_META, lp - n_tok), (0, 0)))
    dh4, loss_blk, dg_final = _loss_bwd(h4, final_norm[None], target, seq, "loss")
    loss = lax.psum(loss_blk[0, 0], MESH_AXES)

    dh3, dg_ffn1, d_wup1, d_wdn1, dcw4_1, dcb4_1, _ = _ffn_bwd(
        h3, ffn_norm[1:2], wup1, cw4_l[1], cb4_l[1], wdn1, saved1, dh4, "1")
    d_o = _mm_group(dh3, wo, NT, BF16, "o_proj_dx")[0]
    d_wo = _mm_tn(o[None], dh3[None], "o_proj_dw")
    ride1 = _Ride("scatter", [jnp.concatenate([d_wo.reshape(N_DEV, -1, d), d_wdn1.reshape(N_DEV, -1, d)], axis=1), d_wup1])
    dq, dk, dv, (p_later, p_up1) = _attn_bwd(q, kv, d_o, tot, "attn_bwd", ride1)
    dn3 = _mm_group(dq, wq, NT, F32, "q_proj_dx")[0]
    d_wq = _mm_tn(n3[None], dq[None], "q_proj_dw")
    dkv = jnp.concatenate([dk, dv], axis=0).astype(BF16)
    dkvn = _mm_reduce(dkv, wkv, NT, None, "kv_proj_dx")
    d_wkv = _mm_tn(kvn[None], dkv, "kv_proj_dw")
    dh2, dg_b = _rms_bwd(h2, gains_b, [dkvn, dn3], dh3, "attn_norms_bwd")
    ride2 = _Ride("scatter", [jnp.concatenate([d_wkv.reshape(N_DEV, -1, d), d_wq.reshape(N_DEV, -1, d)], axis=1)])
    dh1, dg_ffn0, d_wup0, d_wdn0, dcw4_0, dcb4_0, (p_proj,) = _ffn_bwd(
        h1, ffn_norm[0:1], wup0, cw4_l[0], cb4_l[0], wdn0, saved0, dh2, "0", ride2)
    dh0, d_pw, d_pscale, dg_mix0 = _pool_bwd(h0, mix_norm[0:1], pw, pscale, diff, dh1, "pool_bwd")
    grad_x = dh0[N_META:n_tok][None]
    d_pw8 = d_pw.reshape(4, N_DEV, POOL_C // N_DEV, POOL_C).transpose(1, 0, 2, 3).reshape(N_DEV, -1, d).astype(BF16)
    s_now, s_up0 = _reduce_scatter([jnp.concatenate([d_pw8, d_wdn0.reshape(N_DEV, -1, d)], axis=1), d_wup0], "matrices")
    s_later, s_up1, s_proj = [_sum_slabs(p, "sum_" + n) for p, n in ((p_later, "later"), (p_up1, "up1"), (p_proj, "proj"))]
    n_pw, n_kv, n_o = pool_w.size // d, w_kv.size // d, w_o.size // d

    rep_parts = [jnp.concatenate([dg_mix0, dg_b[1:2]], axis=0), jnp.concatenate([dg_ffn0, dg_ffn1], axis=0),
                 dg_b[0:1], dg_final, jnp.stack([dcb4_0.reshape(-1), dcb4_1.reshape(-1)], axis=0)]
    rep_shapes = [mix_norm.shape, ffn_norm.shape, kv_norm.shape, final_norm.shape, ffn_conv_b.shape]
    rep_rows = [p.size // 128 for p in rep_parts]
    d_meta8 = dh0[:N_META].reshape(N_META, N_DEV, d // N_DEV).transpose(1, 0, 2).reshape(N_DEV, -1, 128)
    d_cw8 = jnp.stack([dcw4_0.reshape(N_DEV, 3, fc), dcw4_1.reshape(N_DEV, 3, fc)], axis=1).reshape(N_DEV, -1, 128)
    shard_parts = jnp.concatenate([d_meta8, d_pscale.reshape(N_DEV, 1, 128), d_cw8], axis=1)
    n_rep = sum(rep_rows)
    partial_small = jnp.concatenate([p.reshape(-1, 128) for p in rep_parts] + [shard_parts.reshape(-1, 128)], axis=0)
    g_small = _sum_slabs(_all_gather([partial_small], "gather_vector_grads")[0], "sum_vectors")
    g_rep = [g_small[sum(rep_rows[:k]):sum(rep_rows[:k + 1])].reshape(s) for k, s in enumerate(rep_shapes)]
    g_shard = lax.dynamic_index_in_dim(g_small[n_rep:].reshape(N_DEV, -1, 128), me, 0, keepdims=False)
    g_meta = g_shard[:r0].reshape(meta_tokens.shape)
    g_pscale = g_shard[r0:r1].reshape(pool_scale.shape)
    g_cw = g_shard[r1:r2].reshape(ffn_conv_w.shape)

    grads = {
        "meta_tokens": g_meta, "mix_norm": g_rep[0], "ffn_norm": g_rep[1],
        "pool_w": s_now[:n_pw].reshape(pool_w.shape), "pool_scale": g_pscale, "kv_norm": g_rep[2],
        "w_kv": s_proj[:n_kv].reshape(w_kv.shape), "w_q": s_proj[n_kv:].reshape(w_q.shape),
        "w_o": s_later[:n_o].reshape(w_o.shape),
        "ffn_w_up": jnp.stack([s_up0, s_up1], axis=0), "ffn_conv_w": g_cw, "ffn_conv_b": g_rep[4],
        "ffn_w_down": jnp.stack([s_now[n_pw:], s_later[n_o:]], axis=0), "final_norm": g_rep[3],
    }
    names = list(grads)
    weights = dict(zip(names, [meta_tokens, mix_norm, ffn_norm, pool_w, pool_scale, kv_norm, w_kv, w_q, w_o,
                               ffn_w_up, ffn_conv_w, ffn_conv_b, ffn_w_down, final_norm]))
    mom1 = dict(zip(names, [m_meta_tokens, m_mix_norm, m_ffn_norm, m_pool_w, m_pool_scale, m_kv_norm, m_w_kv, m_w_q,
                            m_w_o, m_ffn_w_up, m_ffn_conv_w, m_ffn_conv_b, m_ffn_w_down, m_final_norm]))
    mom2 = dict(zip(names, [v_meta_tokens, v_mix_norm, v_ffn_norm, v_pool_w, v_pool_scale, v_kv_norm, v_w_kv, v_w_q,
                            v_w_o, v_ffn_w_up, v_ffn_conv_w, v_ffn_conv_b, v_ffn_w_down, v_final_norm]))

    delta, new_m, new_v = {}, {}, {}
    matrices = ["pool_w", "w_kv", "w_q", "w_o", "ffn_w_up", "ffn_w_down"]
    for n in matrices:
        shape = weights[n].shape
        flat = (-1, shape[-1])
        dl, nm, nv = _adamw(weights[n].reshape(flat), grads[n].reshape(flat), mom1[n].reshape(flat),
                            mom2[n].reshape(flat), "adamw_" + n)
        delta[n], new_m[n], new_v[n] = dl.reshape(shape), nm.reshape(shape), nv.reshape(shape)
    vectors = [n for n in names if n not in matrices]
    vec_rows = [weights[n].size // 128 for n in vectors]
    vec_pad = -sum(vec_rows) % 8

    def pack(tree):
        return jnp.concatenate([tree[n].reshape(-1, 128) for n in vectors] + [jnp.ones((vec_pad, 128), F32)], axis=0)

    outs = _adamw(pack(weights), pack(grads), pack(mom1), pack(mom2), "adamw_vectors")
    for tree, packed in zip((delta, new_m, new_v), outs):
        for k, n in enumerate(vectors):
            tree[n] = packed[sum(vec_rows[:k]):sum(vec_rows[:k + 1])].reshape(weights[n].shape)

    return (loss, grad_x, *[grads[n] for n in names], *[delta[n] for n in names],
            *[new_m[n] for n in names], *[new_v[n] for n in names])
```

```python
import functools
from typing import NamedTuple

import jax
import jax.numpy as jnp
from jax import lax
from jax.experimental import pallas as pl
from jax.experimental.pallas import tpu as pltpu

F32 = jnp.float32
BF16 = jnp.bfloat16
SDS = jax.ShapeDtypeStruct

N_DEV = 8
N_META = 16
HEAD_DIM = 64
HEAD_PAIRS = 8
RMS_EPS = 1e-6
LOG2_E = 1.4426950408889634
POOL_WINDOWS = (2, 4, 8, 16)
POOL_C = 256
POOL_HALO = 16
CONV_HALO = 8
ROW_TILE = 384
MM_ROWS_MAX = 1408
FFN_ROWS_MAX = 704
SUM_ROWS_MAX = 256
SUM_WHOLE_BYTES = 4 << 20
ATT_BLK = 128
ATT_Q = ROW_TILE
ATT_UNROLL = ATT_Q // ATT_BLK
VMEM_LIMIT = 56 * 1024 * 1024

ADAM_LR = 0.001
ADAM_B1 = 0.9
ADAM_B2 = 0.999
ADAM_EPS = 1e-08
ADAM_WD = 0.01
ADAM_STEP = 10

MESH_AXES = ("x", "y", "c")
NN = (((1,), (0,)), ((), ()))
NT = (((1,), (1,)), ((), ()))
TN = (((0,), (0,)), ((), ()))


def _cp(n_axes):
    return pltpu.CompilerParams(dimension_semantics=("arbitrary",) * n_axes, vmem_limit_bytes=VMEM_LIMIT)


def _dot(a, b, dims=NN):
    return lax.dot_general(a, b, dims, preferred_element_type=F32)


def _rstd(x):
    return lax.rsqrt(jnp.mean(x * x, axis=-1, keepdims=True) + RMS_EPS)


def _row_tile(rows, cap=512, mult=8):
    if rows <= cap:
        return rows
    best = mult
    for t in range(mult, cap + 1, mult):
        if rows % t == 0:
            best = t
    assert rows % best == 0
    return best


def _rms_fwd(h, gains, name):
    lp, d = h.shape
    k = gains.shape[0]
    tm = ROW_TILE

    def body(h_ref, g_ref, *o_refs):
        x = h_ref[...]
        u = x * _rstd(x)
        for j in range(k):
            o_refs[j][...] = (u * g_ref[j:j + 1, :]).astype(BF16)

    row = pl.BlockSpec((tm, d), lambda i: (i, 0))
    return pl.pallas_call(
        body, name=name, grid=(lp // tm,),
        in_specs=[row, pl.BlockSpec((k, d), lambda i: (0, 0))],
        out_specs=[row] * k, out_shape=[SDS((lp, d), BF16)] * k,
        compiler_params=_cp(1))(h, gains)


def _rms_bwd(h, gains, dns, dh_in, name):
    lp, d = h.shape
    k = gains.shape[0]
    tm = ROW_TILE

    def body(h_ref, g_ref, *refs):
        dn_refs, dh_ref, dho_ref, dg_ref = refs[:k], refs[k], refs[k + 1], refs[k + 2]
        i = pl.program_id(0)
        x = h_ref[...]
        r = _rstd(x)
        u = x * r
        du = jnp.zeros_like(x)
        rows = []
        for j in range(k):
            dn = dn_refs[j][...]
            du = du + dn * g_ref[j:j + 1, :]
            rows.append(jnp.sum(dn * u, axis=0, keepdims=True))
        dx = r * (du - u * jnp.mean(du * u, axis=-1, keepdims=True))
        dho_ref[...] = dh_ref[...] + dx

        @pl.when(i == 0)
        def _():
            for j in range(k):
                dg_ref[j:j + 1, :] = rows[j]

        @pl.when(i > 0)
        def _():
            for j in range(k):
                dg_ref[j:j + 1, :] += rows[j]

    row = pl.BlockSpec((tm, d), lambda i: (i, 0))
    vec = pl.BlockSpec((k, d), lambda i: (0, 0))
    return pl.pallas_call(
        body, name=name, grid=(lp // tm,),
        in_specs=[row, vec] + [row] * k + [row],
        out_specs=[row, vec], out_shape=[SDS((lp, d), F32), SDS((k, d), F32)],
        compiler_params=_cp(1))(h, gains, *dns, dh_in)


def _loss_bwd(h, gain, target, n_real, name):
    lp, d = h.shape
    tm = ROW_TILE

    def body(h_ref, g_ref, t_ref, dh_ref, loss_ref, dg_ref):
        i = pl.program_id(0)
        x = h_ref[...]
        g = g_ref[...]
        r = _rstd(x)
        u = x * r
        row = i * tm + lax.broadcasted_iota(jnp.int32, (tm, 1), 0)
        valid = (row >= N_META) & (row < N_META + n_real)
        e = jnp.where(valid, u * g - t_ref[...], 0.0)
        part = 0.5 * jnp.sum(jnp.sum(e * e, axis=-1, keepdims=True), axis=0, keepdims=True) * (1.0 / d)
        dy = e * (1.0 / d)
        du = dy * g
        dh_ref[...] = r * (du - u * jnp.mean(du * u, axis=-1, keepdims=True))
        dgp = jnp.sum(dy * u, axis=0, keepdims=True)

        @pl.when(i == 0)
        def _():
            loss_ref[...] = jnp.broadcast_to(part, (8, 128))
            dg_ref[...] = dgp

        @pl.when(i > 0)
        def _():
            loss_ref[...] += jnp.broadcast_to(part, (8, 128))
            dg_ref[...] += dgp

    row = pl.BlockSpec((tm, d), lambda i: (i, 0))
    vec = pl.BlockSpec((1, d), lambda i: (0, 0))
    return pl.pallas_call(
        body, name=name, grid=(lp // tm,),
        in_specs=[row, vec, row],
        out_specs=[row, pl.BlockSpec((8, 128), lambda i: (0, 0)), vec],
        out_shape=[SDS((lp, d), F32), SDS((8, 128), F32), SDS((1, d), F32)],
        compiler_params=_cp(1))(h, gain, target)


def _pool_fwd(h, gain, w, scale, name):
    lp, d = h.shape
    tm = ROW_TILE
    hb = POOL_HALO

    def body(h_ref, halo_ref, g_ref, w_ref, s_ref, h1_ref, diff_ref):
        i = pl.program_id(0)
        g = g_ref[...]
        x = h_ref[...]
        n = x * _rstd(x) * g
        xh = halo_ref[...]
        nh = jnp.where(i > 0, xh * _rstd(xh) * g, 0.0)
        cur = jnp.concatenate([nh, n], axis=0)
        pos = i * tm + lax.broadcasted_iota(jnp.int32, (tm, 1), 0)
        for gi, win in enumerate(POOL_WINDOWS):
            if gi > 0:
                cur = cur[:, POOL_C:]
            cur = cur + pltpu.roll(cur, win // 2, 0)
            c0 = gi * POOL_C
            count = jnp.minimum(pos + 1, win).astype(F32)
            diff = cur[hb:, :POOL_C] / count - n[:, c0:c0 + POOL_C]
            diff = diff.astype(BF16)
            y = _dot(diff, w_ref[gi])
            h1_ref[:, c0:c0 + POOL_C] = x[:, c0:c0 + POOL_C] + y * s_ref[:, c0:c0 + POOL_C]
            diff_ref[:, c0:c0 + POOL_C] = diff

    row = pl.BlockSpec((tm, d), lambda i: (i, 0))
    halo = pl.BlockSpec((hb, d), lambda i: (jnp.maximum(i * (tm // hb) - 1, 0), 0))
    vec = pl.BlockSpec((1, d), lambda i: (0, 0))
    return pl.pallas_call(
        body, name=name, grid=(lp // tm,),
        in_specs=[row, halo, vec, pl.BlockSpec(w.shape, lambda i: (0, 0, 0)), vec],
        out_specs=[row, row], out_shape=[SDS((lp, d), F32), SDS((lp, d), BF16)],
        compiler_params=_cp(1))(h, h, gain, w, scale)


def _pool_bwd(h, gain, w, scale, diff, dh1, name):
    lp, d = h.shape
    tm = ROW_TILE
    hb = POOL_HALO
    nblk = lp // tm
    ext = tm + hb

    def body(h_ref, g_ref, w_ref, s_ref, diff_ref, dh_ref, dhn_ref, dh0_ref, dw_ref, ds_ref, dg_ref):
        i = pl.program_id(0)
        g = g_ref[...]
        x = h_ref[...]
        r = _rstd(x)
        u = x * r
        dh = dh_ref[...]
        dhn = jnp.where(i < nblk - 1, dhn_ref[...], 0.0)
        dyp = jnp.concatenate([dh, dhn], axis=0) * s_ref[...]
        pos = i * tm + lax.broadcasted_iota(jnp.int32, (ext, 1), 0)
        dn_parts, dw_parts, ds_parts = [], [], []
        for gi, win in enumerate(POOL_WINDOWS):
            c0 = gi * POOL_C
            wg = w_ref[gi]
            dyp_g = dyp[:, c0:c0 + POOL_C].astype(BF16)
            dd = _dot(dyp_g, wg, NT)
            dfg = diff_ref[:, c0:c0 + POOL_C]
            dw_parts.append(_dot(dfg, dyp_g[:tm], TN))
            ds_parts.append(jnp.sum(dh[:, c0:c0 + POOL_C] * _dot(dfg, wg), axis=0, keepdims=True))
            count = jnp.minimum(pos + 1, win).astype(F32)
            cur = dd / count
            sh = 1
            while sh < win:
                cur = cur + pltpu.roll(cur, ext - sh, 0)
                sh *= 2
            dn_parts.append(cur[:tm] - dd[:tm])
        dn = jnp.concatenate(dn_parts, axis=1)
        du = dn * g
        dh0_ref[...] = dh + r * (du - u * jnp.mean(du * u, axis=-1, keepdims=True))
        dgp = jnp.sum(dn * u, axis=0, keepdims=True)
        dsp = jnp.concatenate(ds_parts, axis=1)

        @pl.when(i == 0)
        def _():
            for gi in range(len(POOL_WINDOWS)):
                dw_ref[gi] = dw_parts[gi]
            ds_ref[...] = dsp
            dg_ref[...] = dgp

        @pl.when(i > 0)
        def _():
            for gi in range(len(POOL_WINDOWS)):
                dw_ref[gi] += dw_parts[gi]
            ds_ref[...] += dsp
            dg_ref[...] += dgp

    row = pl.BlockSpec((tm, d), lambda i: (i, 0))
    nxt = pl.BlockSpec((hb, d), lambda i: (jnp.minimum((i + 1) * (tm // hb), lp // hb - 1), 0))
    vec = pl.BlockSpec((1, d), lambda i: (0, 0))
    wsp = pl.BlockSpec(w.shape, lambda i: (0, 0, 0))
    return pl.pallas_call(
        body, name=name, grid=(nblk,),
        in_specs=[row, vec, wsp, vec, row, row, nxt],
        out_specs=[row, wsp, vec, vec],
        out_shape=[SDS((lp, d), F32), SDS(w.shape, F32), SDS((1, d), F32), SDS((1, d), F32)],
        compiler_params=_cp(1))(h, gain, w, scale, diff, dh1, dh1)


def _ffn_specs(tm, c, lp):
    blk = pl.BlockSpec((2, 1, tm, c), lambda g, i: (0, g, i, 0))
    halo = pl.BlockSpec((2, 1, CONV_HALO, c), lambda g, i: (0, g, jnp.maximum(i * (tm // CONV_HALO) - 1, 0), 0))
    cw = pl.BlockSpec((2, 1, 3, c), lambda g, i: (0, g, 0, 0))
    cb = pl.BlockSpec((2, 1, 1, c), lambda g, i: (0, g, 0, 0))
    return blk, halo, cw, cb


def _ffn_up_act(n2, w_up4, cw4, cb4, name, ride=None):
    lp, d = n2.shape
    _, ng, _, c = w_up4.shape
    tm = _row_tile(lp, FFN_ROWS_MAX, 16)
    hb = CONV_HALO

    def body(a_ref, w_ref, cw_ref, cb_ref, up_ref, act_ref, tail_ref):
        @pl.when(pl.program_id(1) == 0)
        def _():
            tail_ref[...] = jnp.zeros_like(tail_ref)

        a = a_ref[...]
        u = []
        for half in range(2):
            x = _dot(a, w_ref[half, 0])
            up_ref[half, 0] = x
            rows = jnp.concatenate([tail_ref[half], x], axis=0)
            u.append(cb_ref[half, 0] + cw_ref[half, 0, 0:1, :] * pltpu.roll(rows, 2, 0)[hb:]
                     + cw_ref[half, 0, 1:2, :] * pltpu.roll(rows, 1, 0)[hb:] + cw_ref[half, 0, 2:3, :] * x)
            tail_ref[half] = x[tm - hb:]
        gate, val = u
        sig = 1.0 / (1.0 + jnp.exp(-gate))
        act_ref[0] = (gate * sig * val).astype(BF16)

    blk, _, cw, cb = _ffn_specs(tm, c, lp)
    (up4, act), rode = _call_with_ride(
        body, ride, name=name, grid=(ng, lp // tm),
        in_specs=[pl.BlockSpec((tm, d), lambda g, i: (i, 0)), pl.BlockSpec((2, 1, d, c), lambda g, i: (0, g, 0, 0)), cw, cb],
        out_specs=[blk, pl.BlockSpec((1, tm, c), lambda g, i: (g, i, 0))],
        out_shape=[SDS((2, ng, lp, c), F32), SDS((ng, lp, c), BF16)],
        scratch_shapes=[pltpu.VMEM((2, hb, c), F32)], args=[n2, w_up4, cw4, cb4])
    return up4, act, rode


def _ffn_act_bwd(up4, cw4, cb4, dh, w_down4, name, ride=None):
    _, ng, lp, c = up4.shape
    d = dh.shape[1]
    tm = ROW_TILE
    hb = CONV_HALO
    nblk = lp // tm
    ext = tm + hb

    def body(up_ref, prev_ref, next_ref, cw_ref, cb_ref, dh_ref, dhn_ref, wd_ref, dup_ref, dcw_ref, dcb_ref):
        i = pl.program_id(1)
        first = i == 0
        last = i == nblk - 1
        dh_rows = jnp.concatenate([dh_ref[...], jnp.where(last, 0.0, dhn_ref[...])], axis=0)
        da = _dot(dh_rows.astype(BF16), wd_ref[0], NT)
        u, taps = [], []
        for half in range(2):
            rows = jnp.concatenate([jnp.where(first, 0.0, prev_ref[half, 0]), up_ref[half, 0],
                                    jnp.where(last, 0.0, next_ref[half, 0])], axis=0)
            x, xm1, xm2 = rows[hb:], pltpu.roll(rows, 1, 0)[hb:], pltpu.roll(rows, 2, 0)[hb:]
            u.append(cb_ref[half, 0] + cw_ref[half, 0, 0:1, :] * xm2 + cw_ref[half, 0, 1:2, :] * xm1
                     + cw_ref[half, 0, 2:3, :] * x)
            taps.append((xm2, xm1, x))
        gate, val = u
        sig = 1.0 / (1.0 + jnp.exp(-gate))
        dus = (da * val * (sig * (1.0 + gate * (1.0 - sig))), da * (gate * sig))
        sums = []
        for half in range(2):
            du = dus[half]
            dup_ref[half, 0] = (cw_ref[half, 0, 2:3, :] * du[:tm] + cw_ref[half, 0, 1:2, :] * pltpu.roll(du, ext - 1, 0)[:tm]
                                + cw_ref[half, 0, 0:1, :] * pltpu.roll(du, ext - 2, 0)[:tm]).astype(BF16)
            sums.append([jnp.sum(du[:tm] * t[:tm], axis=0, keepdims=True) for t in taps[half]]
                        + [jnp.sum(du[:tm], axis=0, keepdims=True)])

        @pl.when(first)
        def _():
            for half in range(2):
                for k in range(3):
                    dcw_ref[half, 0, k:k + 1, :] = sums[half][k]
                dcb_ref[half, 0] = sums[half][3]

        @pl.when(i > 0)
        def _():
            for half in range(2):
                for k in range(3):
                    dcw_ref[half, 0, k:k + 1, :] += sums[half][k]
                dcb_ref[half, 0] += sums[half][3]

    blk, prev, cw, cb = _ffn_specs(tm, c, lp)

    def next_rows(g, i):
        return jnp.minimum((i + 1) * (tm // hb), lp // hb - 1)

    (dup4, dcw4, dcb4), rode = _call_with_ride(
        body, ride, name=name, grid=(ng, nblk),
        in_specs=[blk, prev, pl.BlockSpec((2, 1, hb, c), lambda g, i: (0, g, next_rows(g, i), 0)), cw, cb,
                  pl.BlockSpec((tm, d), lambda g, i: (i, 0)),
                  pl.BlockSpec((hb, d), lambda g, i: (next_rows(g, i), 0)),
                  pl.BlockSpec((1, c, d), lambda g, i: (g, 0, 0))],
        out_specs=[blk, cw, cb],
        out_shape=[SDS(up4.shape, BF16), SDS(cw4.shape, F32), SDS(cb4.shape, F32)],
        args=[up4, up4, up4, cw4, cb4, dh, dh, w_down4])
    return dup4, dcw4, dcb4, rode


def _mm_tile(rows):
    return _row_tile(rows, MM_ROWS_MAX)


def _mm_group(a, b, dims, out_dtype, name):
    m, k = a.shape
    ng = b.shape[0]
    n = b.shape[2] if dims == NN else b.shape[1]
    tm = _mm_tile(m)

    def body(a_ref, b_ref, o_ref):
        o_ref[0] = _dot(a_ref[...].astype(BF16), b_ref[0], dims).astype(out_dtype)

    return pl.pallas_call(
        body, name=name, grid=(ng, m // tm),
        in_specs=[pl.BlockSpec((tm, k), lambda g, i: (i, 0)),
                  pl.BlockSpec((1,) + b.shape[1:], lambda g, i: (g, 0, 0))],
        out_specs=pl.BlockSpec((1, tm, n), lambda g, i: (g, i, 0)),
        out_shape=SDS((ng, m, n), out_dtype), compiler_params=_cp(2))(a, b)


def _mm_reduce(a, b, dims, res, name, ride=None):
    ng, m, k = a.shape
    n = b.shape[2] if dims == NN else b.shape[1]
    tm = _mm_tile(m)
    has_res = res is not None

    def body(a_ref, b_ref, *refs):
        o_ref, acc_ref = refs[-2], refs[-1]
        g = pl.program_id(1)
        p = _dot(a_ref[0].astype(BF16), b_ref[0], dims)

        @pl.when(g == 0)
        def _():
            acc_ref[...] = p + refs[0][...] if has_res else p

        @pl.when(g > 0)
        def _():
            acc_ref[...] += p

        @pl.when(g == ng - 1)
        def _():
            o_ref[...] = acc_ref[...]

    row = pl.BlockSpec((tm, n), lambda i, g: (i, 0))
    (out,), rode = _call_with_ride(
        body, ride, name=name, grid=(m // tm, ng),
        in_specs=[pl.BlockSpec((1, tm, k), lambda i, g: (g, i, 0)),
                  pl.BlockSpec((1,) + b.shape[1:], lambda i, g: (g, 0, 0))] + ([row] if has_res else []),
        out_specs=[row], out_shape=[SDS((m, n), F32)], scratch_shapes=[pltpu.VMEM((tm, n), F32)],
        args=[a, b] + ([res] if has_res else []))
    return out if ride is None else (out, rode)


def _mm_tn(a, b, name, ride=None):
    ga, m, ka = a.shape
    gb, _, n = b.shape
    ng = max(ga, gb)
    tk = _mm_tile(m)
    nk = m // tk

    def body(a_ref, b_ref, o_ref, acc_ref):
        s = pl.program_id(1)
        p = _dot(a_ref[0].astype(BF16), b_ref[0].astype(BF16), TN)

        @pl.when(s == 0)
        def _():
            acc_ref[...] = p

        @pl.when(s > 0)
        def _():
            acc_ref[...] += p

        @pl.when(s == nk - 1)
        def _():
            o_ref[0] = acc_ref[...].astype(BF16)

    (out,), rode = _call_with_ride(
        body, ride, name=name, grid=(ng, nk),
        in_specs=[pl.BlockSpec((1, tk, ka), (lambda g, s: (g, s, 0)) if ga > 1 else (lambda g, s: (0, s, 0))),
                  pl.BlockSpec((1, tk, n), (lambda g, s: (g, s, 0)) if gb > 1 else (lambda g, s: (0, s, 0)))],
        out_specs=[pl.BlockSpec((1, ka, n), lambda g, s: (g, 0, 0))], out_shape=[SDS((ng, ka, n), BF16)],
        scratch_shapes=[pltpu.VMEM((ka, n), F32)], args=[a, b])
    return out if ride is None else (out, rode)


def _pair_tri(kind, sign):
    r = jnp.arange(2 * ATT_BLK)[:, None]
    c = jnp.arange(2 * ATT_BLK)[None, :]
    same = (r < ATT_BLK) == (c < ATT_BLK)
    rel = {"from": r >= c, "before": r < c}[kind]
    return ((same & rel) * sign).astype(BF16)


def _scan_dot(x, tri):
    hi = x.astype(BF16)
    lo = (x - hi.astype(F32)).astype(BF16)
    return _dot(hi, tri) + _dot(lo, tri)


def _split_heads(blk, lane_a):
    zero = jnp.zeros_like(blk)
    return jnp.concatenate([jnp.where(lane_a, blk, zero), jnp.where(lane_a, zero, blk)], axis=0)


def _softplus(z):
    return jnp.maximum(z, 0.0) + jnp.log(1.0 + jnp.exp2(jnp.abs(z) * (-LOG2_E)))


def _visible(qi, j):
    t = qi * ATT_Q + lax.broadcasted_iota(jnp.int32, (ATT_Q, 2 * ATT_BLK), 0)
    s = j * ATT_BLK + (lax.broadcasted_iota(jnp.int32, (ATT_Q, 2 * ATT_BLK), 1) & (ATT_BLK - 1))
    return s < t


def _halves(x):
    return x[:, :ATT_BLK], x[:, ATT_BLK:]


def _rowsum(x):
    return jnp.sum(x, axis=1, keepdims=True)


def _attn_specs(lp):
    bk = ATT_BLK
    qblk = pl.BlockSpec((ATT_Q, bk), lambda p, i: (i, p))
    kblk = pl.BlockSpec((1, lp, bk), lambda p, i: (p // 2, 0, p % 2))
    vblk = pl.BlockSpec((1, lp, bk), lambda p, i: (HEAD_PAIRS // 2 + p // 2, 0, p % 2))
    tri = pl.BlockSpec((2 * bk, 2 * bk), lambda p, i: (0, 0))
    return qblk, kblk, vblk, tri


def _attn_fwd(q, kv, name, ride=None):
    lp, d = q.shape
    bk = ATT_BLK

    def body(q_ref, k_ref, v_ref, tri_ref, o_ref, t_ref):
        qi = pl.program_id(1)
        qs = q_ref[...] * (HEAD_DIM ** -0.5)
        lane_a = lax.broadcasted_iota(jnp.int32, (1, bk), 1) < HEAD_DIM
        tri = tri_ref[...]

        def trip(js, carry, masked):
            oacc, ca, cb = carry
            rows = [pl.ds(pl.multiple_of(j * bk, bk), bk) for j in js]
            zs = [_dot(qs, _split_heads(k_ref[0, r, :], lane_a), NT) for r in rows]
            ms = [_softplus(z) for z in zs]
            if masked:
                ms = [jnp.where(_visible(qi, j), m, 0.0) for j, m in zip(js, ms)]
            ws = [_scan_dot(m, tri) for m in ms]
            for j, r, z, m, w in zip(js, rows, zs, ms, ws):
                exa, exb = _halves(z + w)
                a = jnp.concatenate([jnp.exp(exa - ca), jnp.exp(exb - cb)], axis=1)
                if masked:
                    a = jnp.where(_visible(qi, j), a, 0.0)
                oacc = oacc + _dot(a.astype(BF16), _split_heads(v_ref[0, r, :], lane_a))
                ma, mb = _halves(m)
                ca, cb = ca + _rowsum(ma), cb + _rowsum(mb)
            return oacc, ca, cb

        carry = (jnp.zeros((ATT_Q, bk), F32), jnp.zeros((ATT_Q, 1), F32), jnp.zeros((ATT_Q, 1), F32))
        top = (qi + 1) * ATT_UNROLL - 1
        carry = trip([top - u for u in range(ATT_UNROLL)], carry, True)
        oacc, ca, cb = lax.fori_loop(
            0, qi, lambda g, c: trip([top - (g + 1) * ATT_UNROLL - u for u in range(ATT_UNROLL)], c, False), carry)
        o_ref[...] = oacc.astype(BF16)
        t_ref[...] = jnp.where(lane_a, ca, cb)

    qblk, kblk, vblk, tri = _attn_specs(lp)
    (o, tot), rode = _call_with_ride(
        body, ride, name=name, grid=(HEAD_PAIRS, lp // ATT_Q), in_specs=[qblk, kblk, vblk, tri],
        out_specs=[qblk, qblk], out_shape=[SDS((lp, d), BF16), SDS((lp, d), F32)],
        args=[q, kv, kv, _pair_tri("from", -1)])
    return o, tot, rode


def _attn_bwd(q, kv, do, tot, name, ride=None):
    lp, d = q.shape
    bk = ATT_BLK
    scale = HEAD_DIM ** -0.5

    def body(q_ref, k_ref, v_ref, do_ref, t_ref, tri_ref, dq_ref, dk_ref, dv_ref):
        qi = pl.program_id(1)

        @pl.when(qi == 0)
        def _():
            dk_ref[...] = jnp.zeros_like(dk_ref)
            dv_ref[...] = jnp.zeros_like(dv_ref)

        qs = q_ref[...] * scale
        do_blk = do_ref[...]
        lane_a = lax.broadcasted_iota(jnp.int32, (1, bk), 1) < HEAD_DIM
        tot_blk = t_ref[...]
        ta = jnp.max(jnp.where(lane_a, tot_blk, -jnp.inf), axis=1, keepdims=True)
        tb = jnp.max(jnp.where(lane_a, -jnp.inf, tot_blk), axis=1, keepdims=True)
        tri = tri_ref[...]

        def trip(js, carry, masked):
            dq, pa, pb, ea, eb = carry
            rows = [pl.ds(pl.multiple_of(j * bk, bk), bk) for j in js]
            kks = [_split_heads(k_ref[0, r, :], lane_a) for r in rows]
            zs = [_dot(qs, kk, NT) for kk in kks]
            das = [_dot(do_blk, _split_heads(v_ref[0, r, :], lane_a), NT) for r in rows]
            ms = [_softplus(z) for z in zs]
            if masked:
                ms = [jnp.where(_visible(qi, j), m, 0.0) for j, m in zip(js, ms)]
            xs = [_scan_dot(m, tri) for m in ms]
            es, a_bf = [], []
            for j, z, m, x, da in zip(js, zs, ms, xs, das):
                xa, xb = _halves(z + x)
                a = jnp.concatenate([jnp.exp(xa + pa), jnp.exp(xb + pb)], axis=1)
                if masked:
                    a = jnp.where(_visible(qi, j), a, 0.0)
                a_bf.append(a.astype(BF16))
                es.append(a * da)
                ma, mb = _halves(m)
                pa, pb = pa + _rowsum(ma), pb + _rowsum(mb)
            ss = [_dot(e.astype(BF16), tri) for e in es]
            for j, r, kk, z, m, e, s, ab in zip(js, rows, kks, zs, ms, es, ss, a_bf):
                sa, sb = _halves(s)
                e_before = jnp.concatenate([sa + ea, sb + eb], axis=1)
                dz = e - jnp.exp(z - m) * (e + e_before)
                if masked:
                    dz = jnp.where(_visible(qi, j), dz, 0.0)
                dzb = dz.astype(BF16)
                dq = dq + _dot(dzb, kk)
                rk = _dot(dzb, qs, TN)
                rv = _dot(ab, do_blk, TN)
                dk_ref[0, r, :] += jnp.where(lane_a, rk[:bk], rk[bk:])
                dv_ref[0, r, :] += jnp.where(lane_a, rv[:bk], rv[bk:])
                e_a, e_b = _halves(e)
                ea, eb = ea + _rowsum(e_a), eb + _rowsum(e_b)
            return dq, pa, pb, ea, eb

        zcol = jnp.zeros((ATT_Q, 1), F32)
        carry = lax.fori_loop(
            0, qi, lambda g, c: trip([g * ATT_UNROLL + u for u in range(ATT_UNROLL)], c, False),
            (jnp.zeros((ATT_Q, bk), F32), -ta, -tb, zcol, zcol))
        carry = trip([qi * ATT_UNROLL + u for u in range(ATT_UNROLL)], carry, True)
        dq_ref[...] = (carry[0] * scale).astype(BF16)

    qblk, kblk, vblk, tri = _attn_specs(lp)
    (dq, dk, dv), rode = _call_with_ride(
        body, ride, name=name, grid=(HEAD_PAIRS, lp // ATT_Q), in_specs=[qblk, kblk, vblk, qblk, qblk, tri],
        out_specs=[qblk, kblk, kblk],
        out_shape=[SDS((lp, d), BF16), SDS((HEAD_PAIRS // 2, lp, 2 * bk), F32), SDS((HEAD_PAIRS // 2, lp, 2 * bk), F32)],
        args=[q, kv, kv, do, tot, _pair_tri("before", 1)])
    return dq, dk, dv, rode


def _mesh_pos():
    return lax.axis_index("x"), lax.axis_index("y"), lax.axis_index("c")


def _flip(pos, r):
    x, y, c = pos
    return (1 - x if r & 4 else x, 1 - y if r & 2 else y, 1 - c if r & 1 else c)


def _dev_index(pos):
    return 4 * pos[0] + 2 * pos[1] + pos[2]


class _Ride(NamedTuple):
    kind: str
    arrays: list


def _ride_arrays(ride):
    return [] if ride is None else ride.arrays


def _ride_args(ride):
    if ride is None:
        return [], [], [], []
    n = len(ride.arrays)
    hbm = pl.BlockSpec(memory_space=pl.ANY)
    shapes = [SDS((N_DEV,) + x.shape if ride.kind == "gather" else x.shape, x.dtype) for x in ride.arrays]
    sems = [pltpu.SemaphoreType.DMA((7 * n,)), pltpu.SemaphoreType.DMA((7 * n,)), pltpu.SemaphoreType.DMA((n,))]
    return [hbm] * n, [hbm] * n, shapes, sems


def _riding(body, n_in, n_out, ride, first, last):
    if ride is None:
        return body
    n = len(ride.arrays)
    gather = ride.kind == "gather"

    def wrapped(*refs):
        ins, srcs = refs[:n_in], refs[n_in:n_in + n]
        outs, dsts = refs[n_in + n:n_in + n + n_out], refs[n_in + n + n_out:n_in + 2 * n + n_out]
        scratch, (send_sems, recv_sems, local_sems) = refs[n_in + 2 * n + n_out:-3], refs[-3:]
        me = _mesh_pos()
        mi = _dev_index(me)
        local, sends, lands = [], [], []
        for a in range(n):
            local.append(pltpu.make_async_copy(srcs[a] if gather else srcs[a].at[mi], dsts[a].at[mi], local_sems.at[a]))
            for r in range(1, N_DEV):
                peer = _flip(me, r)
                pi = _dev_index(peer)
                sems = dict(send_sem=send_sems.at[7 * a + r - 1], recv_sem=recv_sems.at[7 * a + r - 1],
                            device_id=peer, device_id_type=pl.DeviceIdType.MESH)
                sends.append(pltpu.make_async_remote_copy(
                    src_ref=srcs[a] if gather else srcs[a].at[pi], dst_ref=dsts[a].at[mi], **sems))
                lands.append(pltpu.make_async_remote_copy(src_ref=dsts[a].at[pi], dst_ref=dsts[a].at[pi], **sems))

        @pl.when(first())
        def _():
            for cp in local + sends:
                cp.start()

        body(*ins, *outs, *scratch)

        @pl.when(last())
        def _():
            for cp in lands:
                cp.wait_recv()
            for cp in sends:
                cp.wait_send()
            for cp in local:
                cp.wait()

    return wrapped


def _call_with_ride(body, ride, *, name, grid, in_specs, out_specs, out_shape, args, scratch_shapes=()):
    ride_in, ride_out, ride_shape, ride_sems = _ride_args(ride)
    axes = range(len(grid))

    def first():
        return functools.reduce(lambda p, k: p & (pl.program_id(k) == 0), axes, True)

    def last():
        return functools.reduce(lambda p, k: p & (pl.program_id(k) == grid[k] - 1), axes, True)

    out = pl.pallas_call(
        _riding(body, len(in_specs), len(out_specs), ride, first, last), name=name, grid=grid,
        in_specs=list(in_specs) + ride_in, out_specs=list(out_specs) + ride_out,
        out_shape=list(out_shape) + ride_shape, scratch_shapes=list(scratch_shapes) + ride_sems,
        compiler_params=_cp(len(grid)))(*args, *_ride_arrays(ride))
    return out[:len(out_specs)], out[len(out_specs):]


def _all_gather(xs, name):
    n = len(xs)

    def body(*refs):
        x_refs, out_refs = refs[:n], refs[n:2 * n]
        send_sems, recv_sems, local_sems = refs[2 * n:]
        me = _mesh_pos()
        sibling = _flip(me, 1)
        others = [_flip(me, 4), _flip(me, 2), _flip(me, 6)]

        def copy(a, k, block, to, own=False):
            slab = out_refs[a].at[_dev_index(block)]
            return pltpu.make_async_remote_copy(
                src_ref=x_refs[a] if own else slab, dst_ref=slab,
                send_sem=send_sems.at[7 * a + k], recv_sem=recv_sems.at[7 * a + k],
                device_id=to, device_id_type=pl.DeviceIdType.MESH)

        mine = [pltpu.make_async_copy(x_refs[a], out_refs[a].at[_dev_index(me)], local_sems.at[a]) for a in range(n)]
        first = []
        for a in range(n):
            mine[a].start()
            first += [copy(a, 0, me, sibling, own=True)] + [copy(a, 1 + j, me, o, own=True) for j, o in enumerate(others)]
        for cp in first:
            cp.start()
        passed = []
        for a in range(n):
            for j, o in enumerate(others):
                copy(a, 1 + j, o, me).wait_recv()
                passed.append(copy(a, 4 + j, o, sibling))
                passed[-1].start()
        for a in range(n):
            copy(a, 0, sibling, me).wait_recv()
            for j, o in enumerate(others):
                copy(a, 4 + j, _flip(o, 1), me).wait_recv()
        for cp in first + passed:
            cp.wait_send()
        for cp in mine:
            cp.wait()

    hbm = pl.BlockSpec(memory_space=pl.ANY)
    return pl.pallas_call(
        body, name=name, out_shape=[SDS((N_DEV,) + x.shape, x.dtype) for x in xs],
        in_specs=[hbm] * n, out_specs=[hbm] * n,
        scratch_shapes=[pltpu.SemaphoreType.DMA((7 * n,)), pltpu.SemaphoreType.DMA((7 * n,)), pltpu.SemaphoreType.DMA((n,))],
    )(*xs)


def _sum_slabs(a, name, ride=None):
    n, rows, cols = a.shape
    tr = rows if a.size * a.dtype.itemsize <= SUM_WHOLE_BYTES else _row_tile(rows, SUM_ROWS_MAX, 16)

    def body(a_ref, o_ref):
        acc = a_ref[0].astype(F32)
        for k in range(1, n):
            acc = acc + a_ref[k].astype(F32)
        o_ref[...] = acc

    (out,), rode = _call_with_ride(
        body, ride, name=name, grid=(rows // tr,),
        in_specs=[pl.BlockSpec((n, tr, cols), lambda i: (0, i, 0))],
        out_specs=[pl.BlockSpec((tr, cols), lambda i: (i, 0))], out_shape=[SDS((rows, cols), F32)], args=[a])
    return out if ride is None else (out, rode)


def _adamw(w, g, m, v, name):
    rows, cols = w.shape
    tr = _row_tile(rows, 352)

    def body(w_ref, g_ref, m_ref, v_ref, d_ref, mo_ref, vo_ref):
        g_ = g_ref[...]
        m_ = ADAM_B1 * m_ref[...] + (1.0 - ADAM_B1) * g_
        v_ = ADAM_B2 * v_ref[...] + (1.0 - ADAM_B2) * (g_ * g_)
        m_hat = m_ / (1.0 - ADAM_B1 ** ADAM_STEP)
        v_hat = v_ / (1.0 - ADAM_B2 ** ADAM_STEP)
        d_ref[...] = -ADAM_LR * (m_hat / (jnp.sqrt(v_hat) + ADAM_EPS) + ADAM_WD * w_ref[...])
        mo_ref[...] = m_
        vo_ref[...] = v_

    blk = pl.BlockSpec((tr, cols), lambda i: (i, 0))
    return pl.pallas_call(
        body, name=name, grid=(rows // tr,),
        in_specs=[blk] * 4, out_specs=[blk] * 3, out_shape=[SDS((rows, cols), F32)] * 3,
        compiler_params=_cp(1))(w, g, m, v)


def _ffn_bwd(h, gain, w_up, cw4, cb4, w_down4, saved, dh, tag, ride_wup=None, scatter_own=False):
    n2, up4, act = saved
    d_w_down = _mm_tn(act, dh[None], f"ffn_dwdown_{tag}")
    ride_gate = _Ride("scatter", [d_w_down.reshape(N_DEV, -1, d_w_down.shape[-1])]) if scatter_own else None
    dup4, dcw4, dcb4, rode = _ffn_act_bwd(up4, cw4, cb4, dh, w_down4, f"ffn_dgate_{tag}", ride_gate)
    if scatter_own:
        (d_w_down,) = rode
    dup = dup4.reshape((8,) + dup4.shape[2:])
    d_w_up, rode_wup = _mm_tn(n2[None], dup, f"ffn_dwup_{tag}", ride_wup), []
    if ride_wup is not None:
        d_w_up, rode_wup = d_w_up
    dn2 = _mm_reduce(dup, w_up, NT, None, f"ffn_dnorm_{tag}", _Ride("scatter", [d_w_up]) if scatter_own else None)
    if scatter_own:
        dn2, (d_w_up,) = dn2
    dh_in, dgain = _rms_bwd(h, gain, [dn2], dh, f"ffn_dh_{tag}")
    return dh_in, dgain, d_w_up, d_w_down, dcw4, dcb4, rode_wup


def kernel(x, meta_tokens, mix_norm, ffn_norm, pool_w, pool_scale, kv_norm, w_kv, w_q, w_o, ffn_w_up, ffn_conv_w, ffn_conv_b, ffn_w_down, final_norm, loss_target, m_meta_tokens, m_mix_norm, m_ffn_norm, m_pool_w, m_pool_scale, m_kv_norm, m_w_kv, m_w_q, m_w_o, m_ffn_w_up, m_ffn_conv_w, m_ffn_conv_b, m_ffn_w_down, m_final_norm, v_meta_tokens, v_mix_norm, v_ffn_norm, v_pool_w, v_pool_scale, v_kv_norm, v_w_kv, v_w_q, v_w_o, v_ffn_w_up, v_ffn_conv_w, v_ffn_conv_b, v_ffn_w_down, v_final_norm):
    seq, d = x.shape[1], x.shape[2]
    n_tok = N_META + seq
    lp = -(-n_tok // ROW_TILE) * ROW_TILE
    fc = ffn_w_up.shape[2]
    me = _dev_index(_mesh_pos())

    def rows_of(parts):
        rows = [p.size // d for p in parts]
        return [sum(rows[:k]) for k in range(len(parts) + 1)]

    def bf16_rows(parts):
        return jnp.concatenate([p.reshape(-1, d) for p in parts], axis=0).astype(BF16)

    g_pw, wup0 = _all_gather([bf16_rows([pool_w]), ffn_w_up[0].astype(BF16)], "gather_matrices")
    pw = g_pw.reshape(N_DEV, 4, POOL_C // N_DEV, POOL_C).transpose(1, 0, 2, 3).reshape(4, POOL_C, POOL_C)
    early_parts, late_parts = [ffn_w_down[0], w_kv], [w_o, ffn_w_down[1]]
    early_off, late_off = rows_of(early_parts), rows_of(late_parts)

    small_parts = [meta_tokens, pool_scale, ffn_conv_w]
    small_rows = [p.size // 128 for p in small_parts]
    small_pad = -sum(small_rows) % 8
    local_small = jnp.concatenate([p.reshape(-1, 128) for p in small_parts] + [jnp.zeros((small_pad, 128), F32)], axis=0)
    (gs,) = _all_gather([local_small], "gather_vectors")
    r0, r1, r2 = small_rows[0], small_rows[0] + small_rows[1], sum(small_rows)
    meta_full = gs[:, :r0].transpose(1, 0, 2).reshape(N_META, d)
    pscale = gs[:, r0:r1].reshape(1, d)
    cw = gs[:, r1:r2].reshape(N_DEV, 2, 3, fc)
    cw4_l = [cw[:, l].reshape(2, 4, 3, fc) for l in range(2)]
    cb4_l = [ffn_conv_b[l].reshape(2, 4, 1, fc) for l in range(2)]

    h0 = jnp.concatenate([meta_full, x[0], jnp.zeros((lp - n_tok, d), F32)], axis=0)
    h1, diff = _pool_fwd(h0, mix_norm[0:1], pw, pscale, "pool_fwd")
    (n2_0,) = _rms_fwd(h1, ffn_norm[0:1], "ffn_norm_0")
    up4_0, act0, (g_early,) = _ffn_up_act(n2_0, wup0.reshape(2, 4, d, fc), cw4_l[0], cb4_l[0], "ffn_up_0",
                                          _Ride("gather", [bf16_rows(early_parts)]))
    wdn0 = g_early[:, early_off[0]:early_off[1]].reshape(4, fc, d)
    wkv = g_early[:, early_off[1]:early_off[2]].reshape(N_DEV, d, 2 * d // N_DEV)
    h2, (wq,) = _mm_reduce(act0, wdn0, NN, h1, "ffn_down_0", _Ride("gather", [w_q[0].astype(BF16)]))
    wq = wq.reshape(1, d, d)
    gains_b = jnp.stack([kv_norm, mix_norm[1]], axis=0)
    kvn, n3 = _rms_fwd(h2, gains_b, "attn_norms")
    kv = _mm_group(kvn, wkv, NN, BF16, "kv_proj")
    q = _mm_group(n3, wq, NN, BF16, "q_proj")[0]
    o, tot, (g_late, wup1) = _attn_fwd(q, kv, "attn_fwd", _Ride("gather", [bf16_rows(late_parts), ffn_w_up[1].astype(BF16)]))
    wo = g_late[:, late_off[0]:late_off[1]].reshape(1, d, d)
    wdn1 = g_late[:, late_off[1]:late_off[2]].reshape(4, fc, d)
    h3 = _mm_reduce(o[None], wo, NN, h2, "o_proj")
    (n2_1,) = _rms_fwd(h3, ffn_norm[1:2], "ffn_norm_1")
    up4_1, act1, _ = _ffn_up_act(n2_1, wup1.reshape(2, 4, d, fc), cw4_l[1], cb4_l[1], "ffn_up_1")
    h4 = _mm_reduce(act1, wdn1, NN, h3, "ffn_down_1")
    target = jnp.pad(loss_target[0], ((N_META, lp - n_tok), (0, 0)))
    dh4, loss_blk, dg_final = _loss_bwd(h4, final_norm[None], target, seq, "loss")
    loss = lax.psum(loss_blk[0, 0], MESH_AXES)

    dh3, dg_ffn1, d_wup1, d_wdn1, dcw4_1, dcb4_1, _ = _ffn_bwd(
        h3, ffn_norm[1:2], wup1, cw4_l[1], cb4_l[1], wdn1, (n2_1, up4_1, act1), dh4, "1")
    d_o = _mm_group(dh3, wo, NT, BF16, "o_proj_dx")[0]
    d_wo = _mm_tn(o[None], dh3[None], "o_proj_dw")
    ride_late = _Ride("scatter", [jnp.concatenate([d_wo.reshape(N_DEV, -1, d), d_wdn1.reshape(N_DEV, -1, d)], axis=1), d_wup1])
    dq, dk, dv, (p_late, p_up1) = _attn_bwd(q, kv, d_o, tot, "attn_bwd", ride_late)
    dn3 = _mm_group(dq, wq, NT, F32, "q_proj_dx")[0]
    d_wq = _mm_tn(n3[None], dq[None], "q_proj_dw")
    dkv = jnp.concatenate([dk, dv], axis=0).astype(BF16)
    dkvn = _mm_reduce(dkv, wkv, NT, None, "kv_proj_dx")
    d_wkv = _mm_tn(kvn[None], dkv, "kv_proj_dw")
    dh2, dg_b = _rms_bwd(h2, gains_b, [dkvn, dn3], dh3, "attn_norms_bwd")
    ride_proj = _Ride("scatter", [jnp.concatenate([d_wkv.reshape(N_DEV, -1, d), d_wq.reshape(N_DEV, -1, d)], axis=1)])

    dh1, dg_ffn0, p_up0, p_dn0, dcw4_0, dcb4_0, (p_proj,) = _ffn_bwd(
        h1, ffn_norm[0:1], wup0, cw4_l[0], cb4_l[0], wdn0, (n2_0, up4_0, act0), dh2, "0", ride_proj, scatter_own=True)
    dh0, d_pw, d_pscale, dg_mix0 = _pool_bwd(h0, mix_norm[0:1], pw, pscale, diff, dh1, "pool_bwd")
    grad_x = dh0[N_META:n_tok][None]
    d_pw8 = d_pw.reshape(4, N_DEV, POOL_C // N_DEV, POOL_C).transpose(1, 0, 2, 3).reshape(N_DEV, -1, d).astype(BF16)
    s_up0, (p_pw,) = _sum_slabs(p_up0, "sum_up0", _Ride("scatter", [d_pw8]))
    s_late, s_up1, s_proj, s_dn0, s_pw = [_sum_slabs(p, "sum_" + n) for p, n in (
        (p_late, "late"), (p_up1, "up1"), (p_proj, "proj"), (p_dn0, "down0"), (p_pw, "pool"))]
    n_kv, n_o = w_kv.size // d, w_o.size // d

    rep_parts = [jnp.concatenate([dg_mix0, dg_b[1:2]], axis=0), jnp.concatenate([dg_ffn0, dg_ffn1], axis=0),
                 dg_b[0:1], dg_final, jnp.stack([dcb4_0.reshape(-1), dcb4_1.reshape(-1)], axis=0)]
    rep_shapes = [mix_norm.shape, ffn_norm.shape, kv_norm.shape, final_norm.shape, ffn_conv_b.shape]
    rep_rows = [p.size // 128 for p in rep_parts]
    d_meta8 = dh0[:N_META].reshape(N_META, N_DEV, d // N_DEV).transpose(1, 0, 2).reshape(N_DEV, -1, 128)
    d_cw8 = jnp.stack([dcw4_0.reshape(N_DEV, 3, fc), dcw4_1.reshape(N_DEV, 3, fc)], axis=1).reshape(N_DEV, -1, 128)
    shard_parts = jnp.concatenate([d_meta8, d_pscale.reshape(N_DEV, 1, 128), d_cw8], axis=1)
    n_rep = sum(rep_rows)
    partial_small = jnp.concatenate([p.reshape(-1, 128) for p in rep_parts] + [shard_parts.reshape(-1, 128)], axis=0)
    g_small = _sum_slabs(_all_gather([partial_small], "gather_vector_grads")[0], "sum_vectors")
    g_rep = [g_small[sum(rep_rows[:k]):sum(rep_rows[:k + 1])].reshape(s) for k, s in enumerate(rep_shapes)]
    g_shard = lax.dynamic_index_in_dim(g_small[n_rep:].reshape(N_DEV, -1, 128), me, 0, keepdims=False)
    g_meta = g_shard[:r0].reshape(meta_tokens.shape)
    g_pscale = g_shard[r0:r1].reshape(pool_scale.shape)
    g_cw = g_shard[r1:r2].reshape(ffn_conv_w.shape)

    grads = {
        "meta_tokens": g_meta, "mix_norm": g_rep[0], "ffn_norm": g_rep[1],
        "pool_w": s_pw.reshape(pool_w.shape), "pool_scale": g_pscale, "kv_norm": g_rep[2],
        "w_kv": s_proj[:n_kv].reshape(w_kv.shape), "w_q": s_proj[n_kv:].reshape(w_q.shape),
        "w_o": s_late[:n_o].reshape(w_o.shape),
        "ffn_w_up": jnp.stack([s_up0, s_up1], axis=0), "ffn_conv_w": g_cw, "ffn_conv_b": g_rep[4],
        "ffn_w_down": jnp.stack([s_dn0, s_late[n_o:]], axis=0), "final_norm": g_rep[3],
    }
    names = list(grads)
    weights = dict(zip(names, [meta_tokens, mix_norm, ffn_norm, pool_w, pool_scale, kv_norm, w_kv, w_q, w_o,
                               ffn_w_up, ffn_conv_w, ffn_conv_b, ffn_w_down, final_norm]))
    mom1 = dict(zip(names, [m_meta_tokens, m_mix_norm, m_ffn_norm, m_pool_w, m_pool_scale, m_kv_norm, m_w_kv, m_w_q,
                            m_w_o, m_ffn_w_up, m_ffn_conv_w, m_ffn_conv_b, m_ffn_w_down, m_final_norm]))
    mom2 = dict(zip(names, [v_meta_tokens, v_mix_norm, v_ffn_norm, v_pool_w, v_pool_scale, v_kv_norm, v_w_kv, v_w_q,
                            v_w_o, v_ffn_w_up, v_ffn_conv_w, v_ffn_conv_b, v_ffn_w_down, v_final_norm]))

    delta, new_m, new_v = {}, {}, {}
    matrices = ["pool_w", "w_kv", "w_q", "w_o", "ffn_w_up", "ffn_w_down"]
    for n in matrices:
        shape = weights[n].shape
        flat = (-1, shape[-1])
        dl, nm, nv = _adamw(weights[n].reshape(flat), grads[n].reshape(flat), mom1[n].reshape(flat),
                            mom2[n].reshape(flat), "adamw_" + n)
        delta[n], new_m[n], new_v[n] = dl.reshape(shape), nm.reshape(shape), nv.reshape(shape)
    vectors = [n for n in names if n not in matrices]
    vec_rows = [weights[n].size // 128 for n in vectors]
    vec_pad = -sum(vec_rows) % 8

    def pack(tree):
        return jnp.concatenate([tree[n].reshape(-1, 128) for n in vectors] + [jnp.ones((vec_pad, 128), F32)], axis=0)

    outs = _adamw(pack(weights), pack(grads), pack(mom1), pack(mom2), "adamw_vectors")
    for tree, packed in zip((delta, new_m, new_v), outs):
        for k, n in enumerate(vectors):
            tree[n] = packed[sum(vec_rows[:k]):sum(vec_rows[:k + 1])].reshape(weights[n].shape)

    return (loss, grad_x, *[grads[n] for n in names], *[delta[n] for n in names],
            *[new_m[n] for n in names], *[new_v[n] for n in names])
```

```python
import functools
from typing import NamedTuple

import jax
import jax.numpy as jnp
from jax import lax
from jax.experimental import pallas as pl
from jax.experimental.pallas import tpu as pltpu

F32 = jnp.float32
BF16 = jnp.bfloat16
SDS = jax.ShapeDtypeStruct

N_DEV = 8
N_META = 16
HEAD_DIM = 64
HEAD_PAIRS = 8
RMS_EPS = 1e-6
LOG2_E = 1.4426950408889634
POOL_WINDOWS = (2, 4, 8, 16)
POOL_C = 256
POOL_HALO = 16
CONV_HALO = 8
ROW_TILE = 384
MM_ROWS_MAX = 1408
FFN_ROWS_MAX = 704
SUM_ROWS_MAX = 256
SUM_WHOLE_BYTES = 4 << 20
ATT_BLK = 128
ATT_Q = ROW_TILE
ATT_UNROLL = ATT_Q // ATT_BLK
UNDERFLOW_AT = 104.0
VMEM_LIMIT = 56 * 1024 * 1024

ADAM_LR = 0.001
ADAM_B1 = 0.9
ADAM_B2 = 0.999
ADAM_EPS = 1e-08
ADAM_WD = 0.01
ADAM_STEP = 10

MESH_AXES = ("x", "y", "c")
NN = (((1,), (0,)), ((), ()))
NT = (((1,), (1,)), ((), ()))
TN = (((0,), (0,)), ((), ()))


def _cp(n_axes):
    return pltpu.CompilerParams(dimension_semantics=("arbitrary",) * n_axes, vmem_limit_bytes=VMEM_LIMIT)


def _dot(a, b, dims=NN):
    return lax.dot_general(a, b, dims, preferred_element_type=F32)


def _rstd(x):
    return lax.rsqrt(jnp.mean(x * x, axis=-1, keepdims=True) + RMS_EPS)


def _row_tile(rows, cap=512, mult=8):
    if rows <= cap:
        return rows
    best = mult
    for t in range(mult, cap + 1, mult):
        if rows % t == 0:
            best = t
    assert rows % best == 0
    return best


def _rms_fwd(h, gains, name):
    lp, d = h.shape
    k = gains.shape[0]
    tm = ROW_TILE

    def body(h_ref, g_ref, *o_refs):
        x = h_ref[...]
        u = x * _rstd(x)
        for j in range(k):
            o_refs[j][...] = (u * g_ref[j:j + 1, :]).astype(BF16)

    row = pl.BlockSpec((tm, d), lambda i: (i, 0))
    return pl.pallas_call(
        body, name=name, grid=(lp // tm,),
        in_specs=[row, pl.BlockSpec((k, d), lambda i: (0, 0))],
        out_specs=[row] * k, out_shape=[SDS((lp, d), BF16)] * k,
        compiler_params=_cp(1))(h, gains)


def _rms_bwd(h, gains, dns, dh_in, name):
    lp, d = h.shape
    k = gains.shape[0]
    tm = ROW_TILE

    def body(h_ref, g_ref, *refs):
        dn_refs, dh_ref, dho_ref, dg_ref = refs[:k], refs[k], refs[k + 1], refs[k + 2]
        i = pl.program_id(0)
        x = h_ref[...]
        r = _rstd(x)
        u = x * r
        du = jnp.zeros_like(x)
        rows = []
        for j in range(k):
            dn = dn_refs[j][...]
            du = du + dn * g_ref[j:j + 1, :]
            rows.append(jnp.sum(dn * u, axis=0, keepdims=True))
        dx = r * (du - u * jnp.mean(du * u, axis=-1, keepdims=True))
        dho_ref[...] = dh_ref[...] + dx

        @pl.when(i == 0)
        def _():
            for j in range(k):
                dg_ref[j:j + 1, :] = rows[j]

        @pl.when(i > 0)
        def _():
            for j in range(k):
                dg_ref[j:j + 1, :] += rows[j]

    row = pl.BlockSpec((tm, d), lambda i: (i, 0))
    vec = pl.BlockSpec((k, d), lambda i: (0, 0))
    return pl.pallas_call(
        body, name=name, grid=(lp // tm,),
        in_specs=[row, vec] + [row] * k + [row],
        out_specs=[row, vec], out_shape=[SDS((lp, d), F32), SDS((k, d), F32)],
        compiler_params=_cp(1))(h, gains, *dns, dh_in)


def _loss_bwd(h, gain, target, n_real, name):
    lp, d = h.shape
    tm = ROW_TILE

    def body(h_ref, g_ref, t_ref, dh_ref, loss_ref, dg_ref):
        i = pl.program_id(0)
        x = h_ref[...]
        g = g_ref[...]
        r = _rstd(x)
        u = x * r
        row = i * tm + lax.broadcasted_iota(jnp.int32, (tm, 1), 0)
        valid = (row >= N_META) & (row < N_META + n_real)
        e = jnp.where(valid, u * g - t_ref[...], 0.0)
        part = 0.5 * jnp.sum(jnp.sum(e * e, axis=-1, keepdims=True), axis=0, keepdims=True) * (1.0 / d)
        dy = e * (1.0 / d)
        du = dy * g
        dh_ref[...] = r * (du - u * jnp.mean(du * u, axis=-1, keepdims=True))
        dgp = jnp.sum(dy * u, axis=0, keepdims=True)

        @pl.when(i == 0)
        def _():
            loss_ref[...] = jnp.broadcast_to(part, (8, 128))
            dg_ref[...] = dgp

        @pl.when(i > 0)
        def _():
            loss_ref[...] += jnp.broadcast_to(part, (8, 128))
            dg_ref[...] += dgp

    row = pl.BlockSpec((tm, d), lambda i: (i, 0))
    vec = pl.BlockSpec((1, d), lambda i: (0, 0))
    return pl.pallas_call(
        body, name=name, grid=(lp // tm,),
        in_specs=[row, vec, row],
        out_specs=[row, pl.BlockSpec((8, 128), lambda i: (0, 0)), vec],
        out_shape=[SDS((lp, d), F32), SDS((8, 128), F32), SDS((1, d), F32)],
        compiler_params=_cp(1))(h, gain, target)


def _pool_fwd(h, gain, w, scale, name):
    lp, d = h.shape
    tm = ROW_TILE
    hb = POOL_HALO

    def body(h_ref, halo_ref, g_ref, w_ref, s_ref, h1_ref, diff_ref):
        i = pl.program_id(0)
        g = g_ref[...]
        x = h_ref[...]
        n = x * _rstd(x) * g
        xh = halo_ref[...]
        nh = jnp.where(i > 0, xh * _rstd(xh) * g, 0.0)
        cur = jnp.concatenate([nh, n], axis=0)
        pos = i * tm + lax.broadcasted_iota(jnp.int32, (tm, 1), 0)
        for gi, win in enumerate(POOL_WINDOWS):
            if gi > 0:
                cur = cur[:, POOL_C:]
            cur = cur + pltpu.roll(cur, win // 2, 0)
            c0 = gi * POOL_C
            count = jnp.minimum(pos + 1, win).astype(F32)
            diff = cur[hb:, :POOL_C] / count - n[:, c0:c0 + POOL_C]
            diff = diff.astype(BF16)
            y = _dot(diff, w_ref[gi])
            h1_ref[:, c0:c0 + POOL_C] = x[:, c0:c0 + POOL_C] + y * s_ref[:, c0:c0 + POOL_C]
            diff_ref[:, c0:c0 + POOL_C] = diff

    row = pl.BlockSpec((tm, d), lambda i: (i, 0))
    halo = pl.BlockSpec((hb, d), lambda i: (jnp.maximum(i * (tm // hb) - 1, 0), 0))
    vec = pl.BlockSpec((1, d), lambda i: (0, 0))
    return pl.pallas_call(
        body, name=name, grid=(lp // tm,),
        in_specs=[row, halo, vec, pl.BlockSpec(w.shape, lambda i: (0, 0, 0)), vec],
        out_specs=[row, row], out_shape=[SDS((lp, d), F32), SDS((lp, d), BF16)],
        compiler_params=_cp(1))(h, h, gain, w, scale)


def _pool_bwd(h, gain, w, scale, diff, dh1, name):
    lp, d = h.shape
    tm = ROW_TILE
    hb = POOL_HALO
    nblk = lp // tm
    ext = tm + hb

    def body(h_ref, g_ref, w_ref, s_ref, diff_ref, dh_ref, dhn_ref, dh0_ref, dw_ref, ds_ref, dg_ref):
        i = pl.program_id(0)
        g = g_ref[...]
        x = h_ref[...]
        r = _rstd(x)
        u = x * r
        dh = dh_ref[...]
        dhn = jnp.where(i < nblk - 1, dhn_ref[...], 0.0)
        dyp = jnp.concatenate([dh, dhn], axis=0) * s_ref[...]
        pos = i * tm + lax.broadcasted_iota(jnp.int32, (ext, 1), 0)
        dn_parts, dw_parts, ds_parts = [], [], []
        for gi, win in enumerate(POOL_WINDOWS):
            c0 = gi * POOL_C
            wg = w_ref[gi]
            dyp_g = dyp[:, c0:c0 + POOL_C].astype(BF16)
            dd = _dot(dyp_g, wg, NT)
            dfg = diff_ref[:, c0:c0 + POOL_C]
            dw_parts.append(_dot(dfg, dyp_g[:tm], TN))
            ds_parts.append(jnp.sum(dh[:, c0:c0 + POOL_C] * _dot(dfg, wg), axis=0, keepdims=True))
            count = jnp.minimum(pos + 1, win).astype(F32)
            cur = dd / count
            sh = 1
            while sh < win:
                cur = cur + pltpu.roll(cur, ext - sh, 0)
                sh *= 2
            dn_parts.append(cur[:tm] - dd[:tm])
        dn = jnp.concatenate(dn_parts, axis=1)
        du = dn * g
        dh0_ref[...] = dh + r * (du - u * jnp.mean(du * u, axis=-1, keepdims=True))
        dgp = jnp.sum(dn * u, axis=0, keepdims=True)
        dsp = jnp.concatenate(ds_parts, axis=1)

        @pl.when(i == 0)
        def _():
            for gi in range(len(POOL_WINDOWS)):
                dw_ref[gi] = dw_parts[gi]
            ds_ref[...] = dsp
            dg_ref[...] = dgp

        @pl.when(i > 0)
        def _():
            for gi in range(len(POOL_WINDOWS)):
                dw_ref[gi] += dw_parts[gi]
            ds_ref[...] += dsp
            dg_ref[...] += dgp

    row = pl.BlockSpec((tm, d), lambda i: (i, 0))
    nxt = pl.BlockSpec((hb, d), lambda i: (jnp.minimum((i + 1) * (tm // hb), lp // hb - 1), 0))
    vec = pl.BlockSpec((1, d), lambda i: (0, 0))
    wsp = pl.BlockSpec(w.shape, lambda i: (0, 0, 0))
    return pl.pallas_call(
        body, name=name, grid=(nblk,),
        in_specs=[row, vec, wsp, vec, row, row, nxt],
        out_specs=[row, wsp, vec, vec],
        out_shape=[SDS((lp, d), F32), SDS(w.shape, F32), SDS((1, d), F32), SDS((1, d), F32)],
        compiler_params=_cp(1))(h, gain, w, scale, diff, dh1, dh1)


def _ffn_specs(tm, c, lp):
    blk = pl.BlockSpec((2, 1, tm, c), lambda g, i: (0, g, i, 0))
    halo = pl.BlockSpec((2, 1, CONV_HALO, c), lambda g, i: (0, g, jnp.maximum(i * (tm // CONV_HALO) - 1, 0), 0))
    cw = pl.BlockSpec((2, 1, 3, c), lambda g, i: (0, g, 0, 0))
    cb = pl.BlockSpec((2, 1, 1, c), lambda g, i: (0, g, 0, 0))
    return blk, halo, cw, cb


def _ffn_up_act(n2, w_up4, cw4, cb4, name, ride=None):
    lp, d = n2.shape
    _, ng, _, c = w_up4.shape
    tm = _row_tile(lp, FFN_ROWS_MAX, 16)
    hb = CONV_HALO

    def body(a_ref, w_ref, cw_ref, cb_ref, up_ref, act_ref, tail_ref):
        @pl.when(pl.program_id(1) == 0)
        def _():
            tail_ref[...] = jnp.zeros_like(tail_ref)

        a = a_ref[...]
        u = []
        for half in range(2):
            x = _dot(a, w_ref[half, 0])
            up_ref[half, 0] = x
            rows = jnp.concatenate([tail_ref[half], x], axis=0)
            u.append(cb_ref[half, 0] + cw_ref[half, 0, 0:1, :] * pltpu.roll(rows, 2, 0)[hb:]
                     + cw_ref[half, 0, 1:2, :] * pltpu.roll(rows, 1, 0)[hb:] + cw_ref[half, 0, 2:3, :] * x)
            tail_ref[half] = x[tm - hb:]
        gate, val = u
        sig = 1.0 / (1.0 + jnp.exp(-gate))
        act_ref[0] = (gate * sig * val).astype(BF16)

    blk, _, cw, cb = _ffn_specs(tm, c, lp)
    (up4, act), rode = _call_with_ride(
        body, ride, name=name, grid=(ng, lp // tm),
        in_specs=[pl.BlockSpec((tm, d), lambda g, i: (i, 0)), pl.BlockSpec((2, 1, d, c), lambda g, i: (0, g, 0, 0)), cw, cb],
        out_specs=[blk, pl.BlockSpec((1, tm, c), lambda g, i: (g, i, 0))],
        out_shape=[SDS((2, ng, lp, c), F32), SDS((ng, lp, c), BF16)],
        scratch_shapes=[pltpu.VMEM((2, hb, c), F32)], args=[n2, w_up4, cw4, cb4])
    return up4, act, rode


def _ffn_act_bwd(up4, cw4, cb4, dh, w_down4, name, ride=None):
    _, ng, lp, c = up4.shape
    d = dh.shape[1]
    tm = ROW_TILE
    hb = CONV_HALO
    nblk = lp // tm
    ext = tm + hb

    def body(up_ref, prev_ref, next_ref, cw_ref, cb_ref, dh_ref, dhn_ref, wd_ref, dup_ref, dcw_ref, dcb_ref):
        i = pl.program_id(1)
        first = i == 0
        last = i == nblk - 1
        dh_rows = jnp.concatenate([dh_ref[...], jnp.where(last, 0.0, dhn_ref[...])], axis=0)
        da = _dot(dh_rows.astype(BF16), wd_ref[0], NT)
        u, taps = [], []
        for half in range(2):
            rows = jnp.concatenate([jnp.where(first, 0.0, prev_ref[half, 0]), up_ref[half, 0],
                                    jnp.where(last, 0.0, next_ref[half, 0])], axis=0)
            x, xm1, xm2 = rows[hb:], pltpu.roll(rows, 1, 0)[hb:], pltpu.roll(rows, 2, 0)[hb:]
            u.append(cb_ref[half, 0] + cw_ref[half, 0, 0:1, :] * xm2 + cw_ref[half, 0, 1:2, :] * xm1
                     + cw_ref[half, 0, 2:3, :] * x)
            taps.append((xm2, xm1, x))
        gate, val = u
        sig = 1.0 / (1.0 + jnp.exp(-gate))
        dus = (da * val * (sig * (1.0 + gate * (1.0 - sig))), da * (gate * sig))
        sums = []
        for half in range(2):
            du = dus[half]
            dup_ref[half, 0] = (cw_ref[half, 0, 2:3, :] * du[:tm] + cw_ref[half, 0, 1:2, :] * pltpu.roll(du, ext - 1, 0)[:tm]
                                + cw_ref[half, 0, 0:1, :] * pltpu.roll(du, ext - 2, 0)[:tm]).astype(BF16)
            sums.append([jnp.sum(du[:tm] * t[:tm], axis=0, keepdims=True) for t in taps[half]]
                        + [jnp.sum(du[:tm], axis=0, keepdims=True)])

        @pl.when(first)
        def _():
            for half in range(2):
                for k in range(3):
                    dcw_ref[half, 0, k:k + 1, :] = sums[half][k]
                dcb_ref[half, 0] = sums[half][3]

        @pl.when(i > 0)
        def _():
            for half in range(2):
                for k in range(3):
                    dcw_ref[half, 0, k:k + 1, :] += sums[half][k]
                dcb_ref[half, 0] += sums[half][3]

    blk, prev, cw, cb = _ffn_specs(tm, c, lp)

    def next_rows(g, i):
        return jnp.minimum((i + 1) * (tm // hb), lp // hb - 1)

    (dup4, dcw4, dcb4), rode = _call_with_ride(
        body, ride, name=name, grid=(ng, nblk),
        in_specs=[blk, prev, pl.BlockSpec((2, 1, hb, c), lambda g, i: (0, g, next_rows(g, i), 0)), cw, cb,
                  pl.BlockSpec((tm, d), lambda g, i: (i, 0)),
                  pl.BlockSpec((hb, d), lambda g, i: (next_rows(g, i), 0)),
                  pl.BlockSpec((1, c, d), lambda g, i: (g, 0, 0))],
        out_specs=[blk, cw, cb],
        out_shape=[SDS(up4.shape, BF16), SDS(cw4.shape, F32), SDS(cb4.shape, F32)],
        args=[up4, up4, up4, cw4, cb4, dh, dh, w_down4])
    return dup4, dcw4, dcb4, rode


def _mm_tile(rows):
    return _row_tile(rows, MM_ROWS_MAX)


def _mm_group(a, b, dims, out_dtype, name):
    m, k = a.shape
    ng = b.shape[0]
    n = b.shape[2] if dims == NN else b.shape[1]
    tm = _mm_tile(m)

    def body(a_ref, b_ref, o_ref):
        o_ref[0] = _dot(a_ref[...].astype(BF16), b_ref[0], dims).astype(out_dtype)

    return pl.pallas_call(
        body, name=name, grid=(ng, m // tm),
        in_specs=[pl.BlockSpec((tm, k), lambda g, i: (i, 0)),
                  pl.BlockSpec((1,) + b.shape[1:], lambda g, i: (g, 0, 0))],
        out_specs=pl.BlockSpec((1, tm, n), lambda g, i: (g, i, 0)),
        out_shape=SDS((ng, m, n), out_dtype), compiler_params=_cp(2))(a, b)


def _mm_reduce(a, b, dims, res, name, ride=None):
    ng, m, k = a.shape
    n = b.shape[2] if dims == NN else b.shape[1]
    tm = _mm_tile(m)
    has_res = res is not None

    def body(a_ref, b_ref, *refs):
        o_ref, acc_ref = refs[-2], refs[-1]
        g = pl.program_id(1)
        p = _dot(a_ref[0].astype(BF16), b_ref[0], dims)

        @pl.when(g == 0)
        def _():
            acc_ref[...] = p + refs[0][...] if has_res else p

        @pl.when(g > 0)
        def _():
            acc_ref[...] += p

        @pl.when(g == ng - 1)
        def _():
            o_ref[...] = acc_ref[...]

    row = pl.BlockSpec((tm, n), lambda i, g: (i, 0))
    (out,), rode = _call_with_ride(
        body, ride, name=name, grid=(m // tm, ng),
        in_specs=[pl.BlockSpec((1, tm, k), lambda i, g: (g, i, 0)),
                  pl.BlockSpec((1,) + b.shape[1:], lambda i, g: (g, 0, 0))] + ([row] if has_res else []),
        out_specs=[row], out_shape=[SDS((m, n), F32)], scratch_shapes=[pltpu.VMEM((tm, n), F32)],
        args=[a, b] + ([res] if has_res else []))
    return out if ride is None else (out, rode)


def _mm_tn(a, b, name, ride=None):
    ga, m, ka = a.shape
    gb, _, n = b.shape
    ng = max(ga, gb)
    tk = _mm_tile(m)
    nk = m // tk

    def body(a_ref, b_ref, o_ref, acc_ref):
        s = pl.program_id(1)
        p = _dot(a_ref[0].astype(BF16), b_ref[0].astype(BF16), TN)

        @pl.when(s == 0)
        def _():
            acc_ref[...] = p

        @pl.when(s > 0)
        def _():
            acc_ref[...] += p

        @pl.when(s == nk - 1)
        def _():
            o_ref[0] = acc_ref[...].astype(BF16)

    (out,), rode = _call_with_ride(
        body, ride, name=name, grid=(ng, nk),
        in_specs=[pl.BlockSpec((1, tk, ka), (lambda g, s: (g, s, 0)) if ga > 1 else (lambda g, s: (0, s, 0))),
                  pl.BlockSpec((1, tk, n), (lambda g, s: (g, s, 0)) if gb > 1 else (lambda g, s: (0, s, 0)))],
        out_specs=[pl.BlockSpec((1, ka, n), lambda g, s: (g, 0, 0))], out_shape=[SDS((ng, ka, n), BF16)],
        scratch_shapes=[pltpu.VMEM((ka, n), F32)], args=[a, b])
    return out if ride is None else (out, rode)


def _pair_tri(kind, sign):
    r = jnp.arange(2 * ATT_BLK)[:, None]
    c = jnp.arange(2 * ATT_BLK)[None, :]
    same = (r < ATT_BLK) == (c < ATT_BLK)
    rel = {"from": r >= c, "before": r < c}[kind]
    return ((same & rel) * sign).astype(BF16)


def _scan_dot(x, tri):
    hi = x.astype(BF16)
    lo = (x - hi.astype(F32)).astype(BF16)
    return _dot(hi, tri) + _dot(lo, tri)


def _split_heads(blk, lane_a):
    zero = jnp.zeros_like(blk)
    return jnp.concatenate([jnp.where(lane_a, blk, zero), jnp.where(lane_a, zero, blk)], axis=0)


def _softplus(z):
    return jnp.maximum(z, 0.0) + jnp.log(1.0 + jnp.exp2(jnp.abs(z) * (-LOG2_E)))


def _visible(qi, j):
    t = qi * ATT_Q + lax.broadcasted_iota(jnp.int32, (ATT_Q, 2 * ATT_BLK), 0)
    s = j * ATT_BLK + (lax.broadcasted_iota(jnp.int32, (ATT_Q, 2 * ATT_BLK), 1) & (ATT_BLK - 1))
    return s < t


def _halves(x):
    return x[:, :ATT_BLK], x[:, ATT_BLK:]


def _rowsum(x):
    return jnp.sum(x, axis=1, keepdims=True)


def _still_visible(ca, cb):
    return jnp.minimum(jnp.min(ca), jnp.min(cb)) < UNDERFLOW_AT


def _attn_specs(lp):
    bk = ATT_BLK
    qblk = pl.BlockSpec((ATT_Q, bk), lambda p, i: (i, p))
    kblk = pl.BlockSpec((1, lp, bk), lambda p, i: (p // 2, 0, p % 2))
    vblk = pl.BlockSpec((1, lp, bk), lambda p, i: (HEAD_PAIRS // 2 + p // 2, 0, p % 2))
    tri = pl.BlockSpec((2 * bk, 2 * bk), lambda p, i: (0, 0))
    return qblk, kblk, vblk, tri


def _attn_fwd(q, kv, name, ride=None):
    lp, d = q.shape
    bk = ATT_BLK

    def body(q_ref, k_ref, v_ref, tri_ref, o_ref):
        qi = pl.program_id(1)
        qs = q_ref[...] * (HEAD_DIM ** -0.5)
        lane_a = lax.broadcasted_iota(jnp.int32, (1, bk), 1) < HEAD_DIM
        tri = tri_ref[...]

        def trip(js, carry, masked):
            oacc, ca, cb = carry
            rows = [pl.ds(pl.multiple_of(j * bk, bk), bk) for j in js]
            zs = [_dot(qs, _split_heads(k_ref[0, r, :], lane_a), NT) for r in rows]
            ms = [_softplus(z) for z in zs]
            if masked:
                ms = [jnp.where(_visible(qi, j), m, 0.0) for j, m in zip(js, ms)]
            ws = [_scan_dot(m, tri) for m in ms]
            for j, r, z, m, w in zip(js, rows, zs, ms, ws):
                exa, exb = _halves(z + w)
                a = jnp.concatenate([jnp.exp(exa - ca), jnp.exp(exb - cb)], axis=1)
                if masked:
                    a = jnp.where(_visible(qi, j), a, 0.0)
                oacc = oacc + _dot(a.astype(BF16), _split_heads(v_ref[0, r, :], lane_a))
                ma, mb = _halves(m)
                ca, cb = ca + _rowsum(ma), cb + _rowsum(mb)
            return oacc, ca, cb

        carry = (jnp.zeros((ATT_Q, bk), F32), jnp.zeros((ATT_Q, 1), F32), jnp.zeros((ATT_Q, 1), F32))
        top = (qi + 1) * ATT_UNROLL - 1
        carry = trip([top - u for u in range(ATT_UNROLL)], carry, True)
        _, oacc, _, _ = lax.while_loop(
            lambda st: (st[0] < qi) & _still_visible(st[2], st[3]),
            lambda st: (st[0] + 1, *trip([top - (st[0] + 1) * ATT_UNROLL - u for u in range(ATT_UNROLL)], st[1:], False)),
            (jnp.int32(0), *carry))
        o_ref[...] = oacc.astype(BF16)

    qblk, kblk, vblk, tri = _attn_specs(lp)
    (o,), rode = _call_with_ride(
        body, ride, name=name, grid=(HEAD_PAIRS, lp // ATT_Q), in_specs=[qblk, kblk, vblk, tri],
        out_specs=[qblk], out_shape=[SDS((lp, d), BF16)], args=[q, kv, kv, _pair_tri("from", -1)])
    return o, rode


def _attn_bwd(q, kv, do, name, ride=None):
    lp, d = q.shape
    bk = ATT_BLK
    scale = HEAD_DIM ** -0.5

    def body(q_ref, k_ref, v_ref, do_ref, tri_ref, dq_ref, dk_ref, dv_ref):
        qi = pl.program_id(1)

        @pl.when(qi == 0)
        def _():
            dk_ref[...] = jnp.zeros_like(dk_ref)
            dv_ref[...] = jnp.zeros_like(dv_ref)

        qs = q_ref[...] * scale
        do_blk = do_ref[...]
        lane_a = lax.broadcasted_iota(jnp.int32, (1, bk), 1) < HEAD_DIM
        tri = tri_ref[...]

        def sums(js, carry, masked):
            ca, cb = carry
            for j in js:
                m = _softplus(_dot(qs, _split_heads(k_ref[0, pl.ds(pl.multiple_of(j * bk, bk), bk), :], lane_a), NT))
                if masked:
                    m = jnp.where(_visible(qi, j), m, 0.0)
                ma, mb = _halves(m)
                ca, cb = ca + _rowsum(ma), cb + _rowsum(mb)
            return ca, cb

        def trip(js, carry, masked):
            dq, pa, pb, ea, eb = carry
            rows = [pl.ds(pl.multiple_of(j * bk, bk), bk) for j in js]
            kks = [_split_heads(k_ref[0, r, :], lane_a) for r in rows]
            zs = [_dot(qs, kk, NT) for kk in kks]
            das = [_dot(do_blk, _split_heads(v_ref[0, r, :], lane_a), NT) for r in rows]
            ms = [_softplus(z) for z in zs]
            if masked:
                ms = [jnp.where(_visible(qi, j), m, 0.0) for j, m in zip(js, ms)]
            xs = [_scan_dot(m, tri) for m in ms]
            es, a_bf = [], []
            for j, z, m, x, da in zip(js, zs, ms, xs, das):
                xa, xb = _halves(z + x)
                a = jnp.concatenate([jnp.exp(xa + pa), jnp.exp(xb + pb)], axis=1)
                if masked:
                    a = jnp.where(_visible(qi, j), a, 0.0)
                a_bf.append(a.astype(BF16))
                es.append(a * da)
                ma, mb = _halves(m)
                pa, pb = pa + _rowsum(ma), pb + _rowsum(mb)
            ss = [_dot(e.astype(BF16), tri) for e in es]
            for j, r, kk, z, m, e, s, ab in zip(js, rows, kks, zs, ms, es, ss, a_bf):
                sa, sb = _halves(s)
                e_before = jnp.concatenate([sa + ea, sb + eb], axis=1)
                dz = e - jnp.exp(z - m) * (e + e_before)
                if masked:
                    dz = jnp.where(_visible(qi, j), dz, 0.0)
                dzb = dz.astype(BF16)
                dq = dq + _dot(dzb, kk)
                rk = _dot(dzb, qs, TN)
                rv = _dot(ab, do_blk, TN)
                dk_ref[0, r, :] += jnp.where(lane_a, rk[:bk], rk[bk:])
                dv_ref[0, r, :] += jnp.where(lane_a, rv[:bk], rv[bk:])
                e_a, e_b = _halves(e)
                ea, eb = ea + _rowsum(e_a), eb + _rowsum(e_b)
            return dq, pa, pb, ea, eb

        zcol = jnp.zeros((ATT_Q, 1), F32)
        diag = [qi * ATT_UNROLL + u for u in range(ATT_UNROLL)]
        n_old, ta, tb = lax.while_loop(
            lambda st: (st[0] < qi) & _still_visible(st[1], st[2]),
            lambda st: (st[0] + 1, *sums([(qi - 1 - st[0]) * ATT_UNROLL + u for u in range(ATT_UNROLL)], st[1:], False)),
            (jnp.int32(0), *sums(diag, (zcol, zcol), True)))
        carry = lax.fori_loop(
            0, n_old, lambda g, c: trip([(qi - n_old + g) * ATT_UNROLL + u for u in range(ATT_UNROLL)], c, False),
            (jnp.zeros((ATT_Q, bk), F32), -ta, -tb, zcol, zcol))
        carry = trip(diag, carry, True)
        dq_ref[...] = (carry[0] * scale).astype(BF16)

    qblk, kblk, vblk, tri = _attn_specs(lp)
    (dq, dk, dv), rode = _call_with_ride(
        body, ride, name=name, grid=(HEAD_PAIRS, lp // ATT_Q), in_specs=[qblk, kblk, vblk, qblk, tri],
        out_specs=[qblk, kblk, kblk],
        out_shape=[SDS((lp, d), BF16), SDS((HEAD_PAIRS // 2, lp, 2 * bk), F32), SDS((HEAD_PAIRS // 2, lp, 2 * bk), F32)],
        args=[q, kv, kv, do, _pair_tri("before", 1)])
    return dq, dk, dv, rode


def _mesh_pos():
    return lax.axis_index("x"), lax.axis_index("y"), lax.axis_index("c")


def _flip(pos, r):
    x, y, c = pos
    return (1 - x if r & 4 else x, 1 - y if r & 2 else y, 1 - c if r & 1 else c)


def _dev_index(pos):
    return 4 * pos[0] + 2 * pos[1] + pos[2]


class _Ride(NamedTuple):
    kind: str
    arrays: list


def _ride_arrays(ride):
    return [] if ride is None else ride.arrays


def _ride_args(ride):
    if ride is None:
        return [], [], [], []
    n = len(ride.arrays)
    hbm = pl.BlockSpec(memory_space=pl.ANY)
    shapes = [SDS((N_DEV,) + x.shape if ride.kind == "gather" else x.shape, x.dtype) for x in ride.arrays]
    sems = [pltpu.SemaphoreType.DMA((7 * n,)), pltpu.SemaphoreType.DMA((7 * n,)), pltpu.SemaphoreType.DMA((n,))]
    return [hbm] * n, [hbm] * n, shapes, sems


def _riding(body, n_in, n_out, ride, first, last):
    if ride is None:
        return body
    n = len(ride.arrays)
    gather = ride.kind == "gather"

    def wrapped(*refs):
        ins, srcs = refs[:n_in], refs[n_in:n_in + n]
        outs, dsts = refs[n_in + n:n_in + n + n_out], refs[n_in + n + n_out:n_in + 2 * n + n_out]
        scratch, (send_sems, recv_sems, local_sems) = refs[n_in + 2 * n + n_out:-3], refs[-3:]
        me = _mesh_pos()
        mi = _dev_index(me)
        local, sends, lands = [], [], []
        for a in range(n):
            local.append(pltpu.make_async_copy(srcs[a] if gather else srcs[a].at[mi], dsts[a].at[mi], local_sems.at[a]))
            for r in range(1, N_DEV):
                peer = _flip(me, r)
                pi = _dev_index(peer)
                sems = dict(send_sem=send_sems.at[7 * a + r - 1], recv_sem=recv_sems.at[7 * a + r - 1],
                            device_id=peer, device_id_type=pl.DeviceIdType.MESH)
                sends.append(pltpu.make_async_remote_copy(
                    src_ref=srcs[a] if gather else srcs[a].at[pi], dst_ref=dsts[a].at[mi], **sems))
                lands.append(pltpu.make_async_remote_copy(src_ref=dsts[a].at[pi], dst_ref=dsts[a].at[pi], **sems))

        @pl.when(first())
        def _():
            for cp in local + sends:
                cp.start()

        body(*ins, *outs, *scratch)

        @pl.when(last())
        def _():
            for cp in lands:
                cp.wait_recv()
            for cp in sends:
                cp.wait_send()
            for cp in local:
                cp.wait()

    return wrapped


def _call_with_ride(body, ride, *, name, grid, in_specs, out_specs, out_shape, args, scratch_shapes=()):
    ride_in, ride_out, ride_shape, ride_sems = _ride_args(ride)
    axes = range(len(grid))

    def first():
        return functools.reduce(lambda p, k: p & (pl.program_id(k) == 0), axes, True)

    def last():
        return functools.reduce(lambda p, k: p & (pl.program_id(k) == grid[k] - 1), axes, True)

    out = pl.pallas_call(
        _riding(body, len(in_specs), len(out_specs), ride, first, last), name=name, grid=grid,
        in_specs=list(in_specs) + ride_in, out_specs=list(out_specs) + ride_out,
        out_shape=list(out_shape) + ride_shape, scratch_shapes=list(scratch_shapes) + ride_sems,
        compiler_params=_cp(len(grid)))(*args, *_ride_arrays(ride))
    return out[:len(out_specs)], out[len(out_specs):]


def _all_gather(xs, name):
    n = len(xs)

    def body(*refs):
        x_refs, out_refs = refs[:n], refs[n:2 * n]
        send_sems, recv_sems, local_sems = refs[2 * n:]
        me = _mesh_pos()
        sibling = _flip(me, 1)
        others = [_flip(me, 4), _flip(me, 2), _flip(me, 6)]

        def copy(a, k, block, to, own=False):
            slab = out_refs[a].at[_dev_index(block)]
            return pltpu.make_async_remote_copy(
                src_ref=x_refs[a] if own else slab, dst_ref=slab,
                send_sem=send_sems.at[7 * a + k], recv_sem=recv_sems.at[7 * a + k],
                device_id=to, device_id_type=pl.DeviceIdType.MESH)

        mine = [pltpu.make_async_copy(x_refs[a], out_refs[a].at[_dev_index(me)], local_sems.at[a]) for a in range(n)]
        first = []
        for a in range(n):
            mine[a].start()
            first += [copy(a, 0, me, sibling, own=True)] + [copy(a, 1 + j, me, o, own=True) for j, o in enumerate(others)]
        for cp in first:
            cp.start()
        passed = []
        for a in range(n):
            for j, o in enumerate(others):
                copy(a, 1 + j, o, me).wait_recv()
                passed.append(copy(a, 4 + j, o, sibling))
                passed[-1].start()
        for a in range(n):
            copy(a, 0, sibling, me).wait_recv()
            for j, o in enumerate(others):
                copy(a, 4 + j, _flip(o, 1), me).wait_recv()
        for cp in first + passed:
            cp.wait_send()
        for cp in mine:
            cp.wait()

    hbm = pl.BlockSpec(memory_space=pl.ANY)
    return pl.pallas_call(
        body, name=name, out_shape=[SDS((N_DEV,) + x.shape, x.dtype) for x in xs],
        in_specs=[hbm] * n, out_specs=[hbm] * n,
        scratch_shapes=[pltpu.SemaphoreType.DMA((7 * n,)), pltpu.SemaphoreType.DMA((7 * n,)), pltpu.SemaphoreType.DMA((n,))],
    )(*xs)


def _sum_slabs(a, name, ride=None):
    n, rows, cols = a.shape
    tr = rows if a.size * a.dtype.itemsize <= SUM_WHOLE_BYTES else _row_tile(rows, SUM_ROWS_MAX, 16)

    def body(a_ref, o_ref):
        acc = a_ref[0].astype(F32)
        for k in range(1, n):
            acc = acc + a_ref[k].astype(F32)
        o_ref[...] = acc

    (out,), rode = _call_with_ride(
        body, ride, name=name, grid=(rows // tr,),
        in_specs=[pl.BlockSpec((n, tr, cols), lambda i: (0, i, 0))],
        out_specs=[pl.BlockSpec((tr, cols), lambda i: (i, 0))], out_shape=[SDS((rows, cols), F32)], args=[a])
    return out if ride is None else (out, rode)


def _adamw(w, g, m, v, name):
    rows, cols = w.shape
    tr = _row_tile(rows, 352)

    def body(w_ref, g_ref, m_ref, v_ref, d_ref, mo_ref, vo_ref):
        g_ = g_ref[...]
        m_ = ADAM_B1 * m_ref[...] + (1.0 - ADAM_B1) * g_
        v_ = ADAM_B2 * v_ref[...] + (1.0 - ADAM_B2) * (g_ * g_)
        m_hat = m_ / (1.0 - ADAM_B1 ** ADAM_STEP)
        v_hat = v_ / (1.0 - ADAM_B2 ** ADAM_STEP)
        d_ref[...] = -ADAM_LR * (m_hat / (jnp.sqrt(v_hat) + ADAM_EPS) + ADAM_WD * w_ref[...])
        mo_ref[...] = m_
        vo_ref[...] = v_

    blk = pl.BlockSpec((tr, cols), lambda i: (i, 0))
    return pl.pallas_call(
        body, name=name, grid=(rows // tr,),
        in_specs=[blk] * 4, out_specs=[blk] * 3, out_shape=[SDS((rows, cols), F32)] * 3,
        compiler_params=_cp(1))(w, g, m, v)


def _ffn_bwd(h, gain, w_up, cw4, cb4, w_down4, saved, dh, tag, ride_wup=None, scatter_own=False):
    n2, up4, act = saved
    d_w_down = _mm_tn(act, dh[None], f"ffn_dwdown_{tag}")
    ride_gate = _Ride("scatter", [d_w_down.reshape(N_DEV, -1, d_w_down.shape[-1])]) if scatter_own else None
    dup4, dcw4, dcb4, rode = _ffn_act_bwd(up4, cw4, cb4, dh, w_down4, f"ffn_dgate_{tag}", ride_gate)
    if scatter_own:
        (d_w_down,) = rode
    dup = dup4.reshape((8,) + dup4.shape[2:])
    d_w_up, rode_wup = _mm_tn(n2[None], dup, f"ffn_dwup_{tag}", ride_wup), []
    if ride_wup is not None:
        d_w_up, rode_wup = d_w_up
    dn2 = _mm_reduce(dup, w_up, NT, None, f"ffn_dnorm_{tag}", _Ride("scatter", [d_w_up]) if scatter_own else None)
    if scatter_own:
        dn2, (d_w_up,) = dn2
    dh_in, dgain = _rms_bwd(h, gain, [dn2], dh, f"ffn_dh_{tag}")
    return dh_in, dgain, d_w_up, d_w_down, dcw4, dcb4, rode_wup


def kernel(x, meta_tokens, mix_norm, ffn_norm, pool_w, pool_scale, kv_norm, w_kv, w_q, w_o, ffn_w_up, ffn_conv_w, ffn_conv_b, ffn_w_down, final_norm, loss_target, m_meta_tokens, m_mix_norm, m_ffn_norm, m_pool_w, m_pool_scale, m_kv_norm, m_w_kv, m_w_q, m_w_o, m_ffn_w_up, m_ffn_conv_w, m_ffn_conv_b, m_ffn_w_down, m_final_norm, v_meta_tokens, v_mix_norm, v_ffn_norm, v_pool_w, v_pool_scale, v_kv_norm, v_w_kv, v_w_q, v_w_o, v_ffn_w_up, v_ffn_conv_w, v_ffn_conv_b, v_ffn_w_down, v_final_norm):
    seq, d = x.shape[1], x.shape[2]
    n_tok = N_META + seq
    lp = -(-n_tok // ROW_TILE) * ROW_TILE
    fc = ffn_w_up.shape[2]
    me = _dev_index(_mesh_pos())

    def rows_of(parts):
        rows = [p.size // d for p in parts]
        return [sum(rows[:k]) for k in range(len(parts) + 1)]

    def bf16_rows(parts):
        return jnp.concatenate([p.reshape(-1, d) for p in parts], axis=0).astype(BF16)

    g_pw, wup0 = _all_gather([bf16_rows([pool_w]), ffn_w_up[0].astype(BF16)], "gather_matrices")
    pw = g_pw.reshape(N_DEV, 4, POOL_C // N_DEV, POOL_C).transpose(1, 0, 2, 3).reshape(4, POOL_C, POOL_C)
    early_parts, late_parts = [ffn_w_down[0], w_kv], [w_o, ffn_w_down[1]]
    early_off, late_off = rows_of(early_parts), rows_of(late_parts)

    small_parts = [meta_tokens, pool_scale, ffn_conv_w]
    small_rows = [p.size // 128 for p in small_parts]
    small_pad = -sum(small_rows) % 8
    local_small = jnp.concatenate([p.reshape(-1, 128) for p in small_parts] + [jnp.zeros((small_pad, 128), F32)], axis=0)
    (gs,) = _all_gather([local_small], "gather_vectors")
    r0, r1, r2 = small_rows[0], small_rows[0] + small_rows[1], sum(small_rows)
    meta_full = gs[:, :r0].transpose(1, 0, 2).reshape(N_META, d)
    pscale = gs[:, r0:r1].reshape(1, d)
    cw = gs[:, r1:r2].reshape(N_DEV, 2, 3, fc)
    cw4_l = [cw[:, l].reshape(2, 4, 3, fc) for l in range(2)]
    cb4_l = [ffn_conv_b[l].reshape(2, 4, 1, fc) for l in range(2)]

    h0 = jnp.concatenate([meta_full, x[0], jnp.zeros((lp - n_tok, d), F32)], axis=0)
    h1, diff = _pool_fwd(h0, mix_norm[0:1], pw, pscale, "pool_fwd")
    (n2_0,) = _rms_fwd(h1, ffn_norm[0:1], "ffn_norm_0")
    up4_0, act0, (g_early,) = _ffn_up_act(n2_0, wup0.reshape(2, 4, d, fc), cw4_l[0], cb4_l[0], "ffn_up_0",
                                          _Ride("gather", [bf16_rows(early_parts)]))
    wdn0 = g_early[:, early_off[0]:early_off[1]].reshape(4, fc, d)
    wkv = g_early[:, early_off[1]:early_off[2]].reshape(N_DEV, d, 2 * d // N_DEV)
    h2, (wq,) = _mm_reduce(act0, wdn0, NN, h1, "ffn_down_0", _Ride("gather", [w_q[0].astype(BF16)]))
    wq = wq.reshape(1, d, d)
    gains_b = jnp.stack([kv_norm, mix_norm[1]], axis=0)
    kvn, n3 = _rms_fwd(h2, gains_b, "attn_norms")
    kv = _mm_group(kvn, wkv, NN, BF16, "kv_proj")
    q = _mm_group(n3, wq, NN, BF16, "q_proj")[0]
    o, (g_late, wup1) = _attn_fwd(q, kv, "attn_fwd", _Ride("gather", [bf16_rows(late_parts), ffn_w_up[1].astype(BF16)]))
    wo = g_late[:, late_off[0]:late_off[1]].reshape(1, d, d)
    wdn1 = g_late[:, late_off[1]:late_off[2]].reshape(4, fc, d)
    h3 = _mm_reduce(o[None], wo, NN, h2, "o_proj")
    (n2_1,) = _rms_fwd(h3, ffn_norm[1:2], "ffn_norm_1")
    up4_1, act1, _ = _ffn_up_act(n2_1, wup1.reshape(2, 4, d, fc), cw4_l[1], cb4_l[1], "ffn_up_1")
    h4 = _mm_reduce(act1, wdn1, NN, h3, "ffn_down_1")
    target = jnp.pad(loss_target[0], ((N_META, lp - n_tok), (0, 0)))
    dh4, loss_blk, dg_final = _loss_bwd(h4, final_norm[None], target, seq, "loss")
    loss = lax.psum(loss_blk[0, 0], MESH_AXES)

    dh3, dg_ffn1, d_wup1, d_wdn1, dcw4_1, dcb4_1, _ = _ffn_bwd(
        h3, ffn_norm[1:2], wup1, cw4_l[1], cb4_l[1], wdn1, (n2_1, up4_1, act1), dh4, "1")
    d_o = _mm_group(dh3, wo, NT, BF16, "o_proj_dx")[0]
    d_wo = _mm_tn(o[None], dh3[None], "o_proj_dw")
    ride_late = _Ride("scatter", [jnp.concatenate([d_wo.reshape(N_DEV, -1, d), d_wdn1.reshape(N_DEV, -1, d)], axis=1), d_wup1])
    dq, dk, dv, (p_late, p_up1) = _attn_bwd(q, kv, d_o, "attn_bwd", ride_late)
    dn3 = _mm_group(dq, wq, NT, F32, "q_proj_dx")[0]
    d_wq = _mm_tn(n3[None], dq[None], "q_proj_dw")
    dkv = jnp.concatenate([dk, dv], axis=0).astype(BF16)
    dkvn = _mm_reduce(dkv, wkv, NT, None, "kv_proj_dx")
    d_wkv = _mm_tn(kvn[None], dkv, "kv_proj_dw")
    dh2, dg_b = _rms_bwd(h2, gains_b, [dkvn, dn3], dh3, "attn_norms_bwd")
    ride_proj = _Ride("scatter", [jnp.concatenate([d_wkv.reshape(N_DEV, -1, d), d_wq.reshape(N_DEV, -1, d)], axis=1)])

    dh1, dg_ffn0, p_up0, p_dn0, dcw4_0, dcb4_0, (p_proj,) = _ffn_bwd(
        h1, ffn_norm[0:1], wup0, cw4_l[0], cb4_l[0], wdn0, (n2_0, up4_0, act0), dh2, "0", ride_proj, scatter_own=True)
    dh0, d_pw, d_pscale, dg_mix0 = _pool_bwd(h0, mix_norm[0:1], pw, pscale, diff, dh1, "pool_bwd")
    grad_x = dh0[N_META:n_tok][None]
    d_pw8 = d_pw.reshape(4, N_DEV, POOL_C // N_DEV, POOL_C).transpose(1, 0, 2, 3).reshape(N_DEV, -1, d).astype(BF16)
    s_up0, (p_pw,) = _sum_slabs(p_up0, "sum_up0", _Ride("scatter", [d_pw8]))
    s_late, s_up1, s_proj, s_dn0, s_pw = [_sum_slabs(p, "sum_" + n) for p, n in (
        (p_late, "late"), (p_up1, "up1"), (p_proj, "proj"), (p_dn0, "down0"), (p_pw, "pool"))]
    n_kv, n_o = w_kv.size // d, w_o.size // d

    rep_parts = [jnp.concatenate([dg_mix0, dg_b[1:2]], axis=0), jnp.concatenate([dg_ffn0, dg_ffn1], axis=0),
                 dg_b[0:1], dg_final, jnp.stack([dcb4_0.reshape(-1), dcb4_1.reshape(-1)], axis=0)]
    rep_shapes = [mix_norm.shape, ffn_norm.shape, kv_norm.shape, final_norm.shape, ffn_conv_b.shape]
    rep_rows = [p.size // 128 for p in rep_parts]
    d_meta8 = dh0[:N_META].reshape(N_META, N_DEV, d // N_DEV).transpose(1, 0, 2).reshape(N_DEV, -1, 128)
    d_cw8 = jnp.stack([dcw4_0.reshape(N_DEV, 3, fc), dcw4_1.reshape(N_DEV, 3, fc)], axis=1).reshape(N_DEV, -1, 128)
    shard_parts = jnp.concatenate([d_meta8, d_pscale.reshape(N_DEV, 1, 128), d_cw8], axis=1)
    n_rep = sum(rep_rows)
    partial_small = jnp.concatenate([p.reshape(-1, 128) for p in rep_parts] + [shard_parts.reshape(-1, 128)], axis=0)
    g_small = _sum_slabs(_all_gather([partial_small], "gather_vector_grads")[0], "sum_vectors")
    g_rep = [g_small[sum(rep_rows[:k]):sum(rep_rows[:k + 1])].reshape(s) for k, s in enumerate(rep_shapes)]
    g_shard = lax.dynamic_index_in_dim(g_small[n_rep:].reshape(N_DEV, -1, 128), me, 0, keepdims=False)
    g_meta = g_shard[:r0].reshape(meta_tokens.shape)
    g_pscale = g_shard[r0:r1].reshape(pool_scale.shape)
    g_cw = g_shard[r1:r2].reshape(ffn_conv_w.shape)

    grads = {
        "meta_tokens": g_meta, "mix_norm": g_rep[0], "ffn_norm": g_rep[1],
        "pool_w": s_pw.reshape(pool_w.shape), "pool_scale": g_pscale, "kv_norm": g_rep[2],
        "w_kv": s_proj[:n_kv].reshape(w_kv.shape), "w_q": s_proj[n_kv:].reshape(w_q.shape),
        "w_o": s_late[:n_o].reshape(w_o.shape),
        "ffn_w_up": jnp.stack([s_up0, s_up1], axis=0), "ffn_conv_w": g_cw, "ffn_conv_b": g_rep[4],
        "ffn_w_down": jnp.stack([s_dn0, s_late[n_o:]], axis=0), "final_norm": g_rep[3],
    }
    names = list(grads)
    weights = dict(zip(names, [meta_tokens, mix_norm, ffn_norm, pool_w, pool_scale, kv_norm, w_kv, w_q, w_o,
                               ffn_w_up, ffn_conv_w, ffn_conv_b, ffn_w_down, final_norm]))
    mom1 = dict(zip(names, [m_meta_tokens, m_mix_norm, m_ffn_norm, m_pool_w, m_pool_scale, m_kv_norm, m_w_kv, m_w_q,
                            m_w_o, m_ffn_w_up, m_ffn_conv_w, m_ffn_conv_b, m_ffn_w_down, m_final_norm]))
    mom2 = dict(zip(names, [v_meta_tokens, v_mix_norm, v_ffn_norm, v_pool_w, v_pool_scale, v_kv_norm, v_w_kv, v_w_q,
                            v_w_o, v_ffn_w_up, v_ffn_conv_w, v_ffn_conv_b, v_ffn_w_down, v_final_norm]))

    delta, new_m, new_v = {}, {}, {}
    matrices = ["pool_w", "w_kv", "w_q", "w_o", "ffn_w_up", "ffn_w_down"]
    for n in matrices:
        shape = weights[n].shape
        flat = (-1, shape[-1])
        dl, nm, nv = _adamw(weights[n].reshape(flat), grads[n].reshape(flat), mom1[n].reshape(flat),
                            mom2[n].reshape(flat), "adamw_" + n)
        delta[n], new_m[n], new_v[n] = dl.reshape(shape), nm.reshape(shape), nv.reshape(shape)
    vectors = [n for n in names if n not in matrices]
    vec_rows = [weights[n].size // 128 for n in vectors]
    vec_pad = -sum(vec_rows) % 8

    def pack(tree):
        return jnp.concatenate([tree[n].reshape(-1, 128) for n in vectors] + [jnp.ones((vec_pad, 128), F32)], axis=0)

    outs = _adamw(pack(weights), pack(grads), pack(mom1), pack(mom2), "adamw_vectors")
    for tree, packed in zip((delta, new_m, new_v), outs):
        for k, n in enumerate(vectors):
            tree[n] = packed[sum(vec_rows[:k]):sum(vec_rows[:k + 1])].reshape(weights[n].shape)

    return (loss, grad_x, *[grads[n] for n in names], *[delta[n] for n in names],
            *[new_m[n] for n in names], *[new_v[n] for n in names])
```

```python
import functools
from typing import NamedTuple

import jax
import jax.numpy as jnp
from jax import lax
from jax.experimental import pallas as pl
from jax.experimental.pallas import tpu as pltpu

F32 = jnp.float32
BF16 = jnp.bfloat16
SDS = jax.ShapeDtypeStruct

N_DEV = 8
N_META = 16
HEAD_DIM = 64
HEAD_PAIRS = 8
RMS_EPS = 1e-6
LOG2_E = 1.4426950408889634
POOL_WINDOWS = (2, 4, 8, 16)
POOL_C = 256
POOL_HALO = 16
CONV_HALO = 8
ROW_TILE = 384
MM_ROWS_MAX = 1408
FFN_ROWS_MAX = 704
SUM_ROWS_MAX = 256
SUM_WHOLE_BYTES = 4 << 20
ATT_BLK = 128
ATT_Q = ROW_TILE
ATT_UNROLL = ATT_Q // ATT_BLK
UNDERFLOW_AT = 104.0
VMEM_LIMIT = 56 * 1024 * 1024

ADAM_LR = 0.001
ADAM_B1 = 0.9
ADAM_B2 = 0.999
ADAM_EPS = 1e-08
ADAM_WD = 0.01
ADAM_STEP = 10

MESH_AXES = ("x", "y", "c")
NN = (((1,), (0,)), ((), ()))
NT = (((1,), (1,)), ((), ()))
TN = (((0,), (0,)), ((), ()))


def _cp(n_axes):
    return pltpu.CompilerParams(dimension_semantics=("arbitrary",) * n_axes, vmem_limit_bytes=VMEM_LIMIT)


def _dot(a, b, dims=NN):
    return lax.dot_general(a, b, dims, preferred_element_type=F32)


def _rstd(x):
    return lax.rsqrt(jnp.mean(x * x, axis=-1, keepdims=True) + RMS_EPS)


def _row_tile(rows, cap=512, mult=8):
    if rows <= cap:
        return rows
    best = mult
    for t in range(mult, cap + 1, mult):
        if rows % t == 0:
            best = t
    assert rows % best == 0
    return best


def _rms_fwd(h, gains, name):
    lp, d = h.shape
    k = gains.shape[0]
    tm = ROW_TILE

    def body(h_ref, g_ref, *o_refs):
        x = h_ref[...]
        u = x * _rstd(x)
        for j in range(k):
            o_refs[j][...] = (u * g_ref[j:j + 1, :]).astype(BF16)

    row = pl.BlockSpec((tm, d), lambda i: (i, 0))
    return pl.pallas_call(
        body, name=name, grid=(lp // tm,),
        in_specs=[row, pl.BlockSpec((k, d), lambda i: (0, 0))],
        out_specs=[row] * k, out_shape=[SDS((lp, d), BF16)] * k,
        compiler_params=_cp(1))(h, gains)


def _rms_bwd(h, gains, dns, dh_in, name):
    lp, d = h.shape
    k = gains.shape[0]
    tm = ROW_TILE

    def body(h_ref, g_ref, *refs):
        dn_refs, dh_ref, dho_ref, dg_ref = refs[:k], refs[k], refs[k + 1], refs[k + 2]
        i = pl.program_id(0)
        x = h_ref[...]
        r = _rstd(x)
        u = x * r
        du = jnp.zeros_like(x)
        rows = []
        for j in range(k):
            dn = dn_refs[j][...]
            du = du + dn * g_ref[j:j + 1, :]
            rows.append(jnp.sum(dn * u, axis=0, keepdims=True))
        dx = r * (du - u * jnp.mean(du * u, axis=-1, keepdims=True))
        dho_ref[...] = dh_ref[...] + dx

        @pl.when(i == 0)
        def _():
            for j in range(k):
                dg_ref[j:j + 1, :] = rows[j]

        @pl.when(i > 0)
        def _():
            for j in range(k):
                dg_ref[j:j + 1, :] += rows[j]

    row = pl.BlockSpec((tm, d), lambda i: (i, 0))
    vec = pl.BlockSpec((k, d), lambda i: (0, 0))
    return pl.pallas_call(
        body, name=name, grid=(lp // tm,),
        in_specs=[row, vec] + [row] * k + [row],
        out_specs=[row, vec], out_shape=[SDS((lp, d), F32), SDS((k, d), F32)],
        compiler_params=_cp(1))(h, gains, *dns, dh_in)


def _loss_bwd(h, gain, target, n_real, name):
    lp, d = h.shape
    tm = ROW_TILE

    def body(h_ref, g_ref, t_ref, dh_ref, loss_ref, dg_ref):
        i = pl.program_id(0)
        x = h_ref[...]
        g = g_ref[...]
        r = _rstd(x)
        u = x * r
        row = i * tm + lax.broadcasted_iota(jnp.int32, (tm, 1), 0)
        valid = (row >= N_META) & (row < N_META + n_real)
        e = jnp.where(valid, u * g - t_ref[...], 0.0)
        part = 0.5 * jnp.sum(jnp.sum(e * e, axis=-1, keepdims=True), axis=0, keepdims=True) * (1.0 / d)
        dy = e * (1.0 / d)
        du = dy * g
        dh_ref[...] = r * (du - u * jnp.mean(du * u, axis=-1, keepdims=True))
        dgp = jnp.sum(dy * u, axis=0, keepdims=True)

        @pl.when(i == 0)
        def _():
            loss_ref[...] = jnp.broadcast_to(part, (8, 128))
            dg_ref[...] = dgp

        @pl.when(i > 0)
        def _():
            loss_ref[...] += jnp.broadcast_to(part, (8, 128))
            dg_ref[...] += dgp

    row = pl.BlockSpec((tm, d), lambda i: (i, 0))
    vec = pl.BlockSpec((1, d), lambda i: (0, 0))
    return pl.pallas_call(
        body, name=name, grid=(lp // tm,),
        in_specs=[row, vec, row],
        out_specs=[row, pl.BlockSpec((8, 128), lambda i: (0, 0)), vec],
        out_shape=[SDS((lp, d), F32), SDS((8, 128), F32), SDS((1, d), F32)],
        compiler_params=_cp(1))(h, gain, target)


def _pool_fwd(h, gain, w, scale, name):
    lp, d = h.shape
    tm = ROW_TILE
    hb = POOL_HALO

    def body(h_ref, halo_ref, g_ref, w_ref, s_ref, h1_ref, diff_ref):
        i = pl.program_id(0)
        g = g_ref[...]
        x = h_ref[...]
        n = x * _rstd(x) * g
        xh = halo_ref[...]
        nh = jnp.where(i > 0, xh * _rstd(xh) * g, 0.0)
        cur = jnp.concatenate([nh, n], axis=0)
        pos = i * tm + lax.broadcasted_iota(jnp.int32, (tm, 1), 0)
        for gi, win in enumerate(POOL_WINDOWS):
            if gi > 0:
                cur = cur[:, POOL_C:]
            cur = cur + pltpu.roll(cur, win // 2, 0)
            c0 = gi * POOL_C
            count = jnp.minimum(pos + 1, win).astype(F32)
            diff = cur[hb:, :POOL_C] / count - n[:, c0:c0 + POOL_C]
            diff = diff.astype(BF16)
            y = _dot(diff, w_ref[gi])
            h1_ref[:, c0:c0 + POOL_C] = x[:, c0:c0 + POOL_C] + y * s_ref[:, c0:c0 + POOL_C]
            diff_ref[:, c0:c0 + POOL_C] = diff

    row = pl.BlockSpec((tm, d), lambda i: (i, 0))
    halo = pl.BlockSpec((hb, d), lambda i: (jnp.maximum(i * (tm // hb) - 1, 0), 0))
    vec = pl.BlockSpec((1, d), lambda i: (0, 0))
    return pl.pallas_call(
        body, name=name, grid=(lp // tm,),
        in_specs=[row, halo, vec, pl.BlockSpec(w.shape, lambda i: (0, 0, 0)), vec],
        out_specs=[row, row], out_shape=[SDS((lp, d), F32), SDS((lp, d), BF16)],
        compiler_params=_cp(1))(h, h, gain, w, scale)


def _pool_bwd(h, gain, w, scale, diff, dh1, name):
    lp, d = h.shape
    tm = ROW_TILE
    hb = POOL_HALO
    nblk = lp // tm
    ext = tm + hb

    def body(h_ref, g_ref, w_ref, s_ref, diff_ref, dh_ref, dhn_ref, dh0_ref, dw_ref, ds_ref, dg_ref):
        i = pl.program_id(0)
        g = g_ref[...]
        x = h_ref[...]
        r = _rstd(x)
        u = x * r
        dh = dh_ref[...]
        dhn = jnp.where(i < nblk - 1, dhn_ref[...], 0.0)
        dyp = jnp.concatenate([dh, dhn], axis=0) * s_ref[...]
        pos = i * tm + lax.broadcasted_iota(jnp.int32, (ext, 1), 0)
        dn_parts, dw_parts, ds_parts = [], [], []
        for gi, win in enumerate(POOL_WINDOWS):
            c0 = gi * POOL_C
            wg = w_ref[gi]
            dyp_g = dyp[:, c0:c0 + POOL_C].astype(BF16)
            dd = _dot(dyp_g, wg, NT)
            dfg = diff_ref[:, c0:c0 + POOL_C]
            dw_parts.append(_dot(dfg, dyp_g[:tm], TN))
            ds_parts.append(jnp.sum(dh[:, c0:c0 + POOL_C] * _dot(dfg, wg), axis=0, keepdims=True))
            count = jnp.minimum(pos + 1, win).astype(F32)
            cur = dd / count
            sh = 1
            while sh < win:
                cur = cur + pltpu.roll(cur, ext - sh, 0)
                sh *= 2
            dn_parts.append(cur[:tm] - dd[:tm])
        dn = jnp.concatenate(dn_parts, axis=1)
        du = dn * g
        dh0_ref[...] = dh + r * (du - u * jnp.mean(du * u, axis=-1, keepdims=True))
        dgp = jnp.sum(dn * u, axis=0, keepdims=True)
        dsp = jnp.concatenate(ds_parts, axis=1)

        @pl.when(i == 0)
        def _():
            for gi in range(len(POOL_WINDOWS)):
                dw_ref[gi] = dw_parts[gi]
            ds_ref[...] = dsp
            dg_ref[...] = dgp

        @pl.when(i > 0)
        def _():
            for gi in range(len(POOL_WINDOWS)):
                dw_ref[gi] += dw_parts[gi]
            ds_ref[...] += dsp
            dg_ref[...] += dgp

    row = pl.BlockSpec((tm, d), lambda i: (i, 0))
    nxt = pl.BlockSpec((hb, d), lambda i: (jnp.minimum((i + 1) * (tm // hb), lp // hb - 1), 0))
    vec = pl.BlockSpec((1, d), lambda i: (0, 0))
    wsp = pl.BlockSpec(w.shape, lambda i: (0, 0, 0))
    return pl.pallas_call(
        body, name=name, grid=(nblk,),
        in_specs=[row, vec, wsp, vec, row, row, nxt],
        out_specs=[row, wsp, vec, vec],
        out_shape=[SDS((lp, d), F32), SDS(w.shape, F32), SDS((1, d), F32), SDS((1, d), F32)],
        compiler_params=_cp(1))(h, gain, w, scale, diff, dh1, dh1)


def _ffn_specs(tm, c, lp):
    blk = pl.BlockSpec((2, 1, tm, c), lambda g, i: (0, g, i, 0))
    halo = pl.BlockSpec((2, 1, CONV_HALO, c), lambda g, i: (0, g, jnp.maximum(i * (tm // CONV_HALO) - 1, 0), 0))
    cw = pl.BlockSpec((2, 1, 3, c), lambda g, i: (0, g, 0, 0))
    cb = pl.BlockSpec((2, 1, 1, c), lambda g, i: (0, g, 0, 0))
    return blk, halo, cw, cb


def _ffn_up_act(n2, w_up4, cw4, cb4, name, ride=None):
    lp, d = n2.shape
    _, ng, _, c = w_up4.shape
    tm = _row_tile(lp, FFN_ROWS_MAX, 16)
    hb = CONV_HALO

    def body(a_ref, w_ref, cw_ref, cb_ref, up_ref, act_ref, tail_ref):
        @pl.when(pl.program_id(1) == 0)
        def _():
            tail_ref[...] = jnp.zeros_like(tail_ref)

        a = a_ref[...]
        u = []
        for half in range(2):
            x = _dot(a, w_ref[half, 0])
            up_ref[half, 0] = x
            rows = jnp.concatenate([tail_ref[half], x], axis=0)
            u.append(cb_ref[half, 0] + cw_ref[half, 0, 0:1, :] * pltpu.roll(rows, 2, 0)[hb:]
                     + cw_ref[half, 0, 1:2, :] * pltpu.roll(rows, 1, 0)[hb:] + cw_ref[half, 0, 2:3, :] * x)
            tail_ref[half] = x[tm - hb:]
        gate, val = u
        sig = 1.0 / (1.0 + jnp.exp(-gate))
        act_ref[0] = (gate * sig * val).astype(BF16)

    blk, _, cw, cb = _ffn_specs(tm, c, lp)
    (up4, act), rode = _call_with_ride(
        body, ride, name=name, grid=(ng, lp // tm),
        in_specs=[pl.BlockSpec((tm, d), lambda g, i: (i, 0)), pl.BlockSpec((2, 1, d, c), lambda g, i: (0, g, 0, 0)), cw, cb],
        out_specs=[blk, pl.BlockSpec((1, tm, c), lambda g, i: (g, i, 0))],
        out_shape=[SDS((2, ng, lp, c), F32), SDS((ng, lp, c), BF16)],
        scratch_shapes=[pltpu.VMEM((2, hb, c), F32)], args=[n2, w_up4, cw4, cb4])
    return up4, act, rode


def _ffn_act_bwd(up4, cw4, cb4, dh, w_down4, name, ride=None):
    _, ng, lp, c = up4.shape
    d = dh.shape[1]
    tm = ROW_TILE
    hb = CONV_HALO
    nblk = lp // tm
    ext = tm + hb

    def body(up_ref, prev_ref, next_ref, cw_ref, cb_ref, dh_ref, dhn_ref, wd_ref, dup_ref, dcw_ref, dcb_ref):
        i = pl.program_id(1)
        first = i == 0
        last = i == nblk - 1
        dh_rows = jnp.concatenate([dh_ref[...], jnp.where(last, 0.0, dhn_ref[...])], axis=0)
        da = _dot(dh_rows.astype(BF16), wd_ref[0], NT)
        u, taps = [], []
        for half in range(2):
            rows = jnp.concatenate([jnp.where(first, 0.0, prev_ref[half, 0]), up_ref[half, 0],
                                    jnp.where(last, 0.0, next_ref[half, 0])], axis=0)
            x, xm1, xm2 = rows[hb:], pltpu.roll(rows, 1, 0)[hb:], pltpu.roll(rows, 2, 0)[hb:]
            u.append(cb_ref[half, 0] + cw_ref[half, 0, 0:1, :] * xm2 + cw_ref[half, 0, 1:2, :] * xm1
                     + cw_ref[half, 0, 2:3, :] * x)
            taps.append((xm2, xm1, x))
        gate, val = u
        sig = 1.0 / (1.0 + jnp.exp(-gate))
        dus = (da * val * (sig * (1.0 + gate * (1.0 - sig))), da * (gate * sig))
        sums = []
        for half in range(2):
            du = dus[half]
            dup_ref[half, 0] = (cw_ref[half, 0, 2:3, :] * du[:tm] + cw_ref[half, 0, 1:2, :] * pltpu.roll(du, ext - 1, 0)[:tm]
                                + cw_ref[half, 0, 0:1, :] * pltpu.roll(du, ext - 2, 0)[:tm]).astype(BF16)
            sums.append([jnp.sum(du[:tm] * t[:tm], axis=0, keepdims=True) for t in taps[half]]
                        + [jnp.sum(du[:tm], axis=0, keepdims=True)])

        @pl.when(first)
        def _():
            for half in range(2):
                for k in range(3):
                    dcw_ref[half, 0, k:k + 1, :] = sums[half][k]
                dcb_ref[half, 0] = sums[half][3]

        @pl.when(i > 0)
        def _():
            for half in range(2):
                for k in range(3):
                    dcw_ref[half, 0, k:k + 1, :] += sums[half][k]
                dcb_ref[half, 0] += sums[half][3]

    blk, prev, cw, cb = _ffn_specs(tm, c, lp)

    def next_rows(g, i):
        return jnp.minimum((i + 1) * (tm // hb), lp // hb - 1)

    (dup4, dcw4, dcb4), rode = _call_with_ride(
        body, ride, name=name, grid=(ng, nblk),
        in_specs=[blk, prev, pl.BlockSpec((2, 1, hb, c), lambda g, i: (0, g, next_rows(g, i), 0)), cw, cb,
                  pl.BlockSpec((tm, d), lambda g, i: (i, 0)),
                  pl.BlockSpec((hb, d), lambda g, i: (next_rows(g, i), 0)),
                  pl.BlockSpec((1, c, d), lambda g, i: (g, 0, 0))],
        out_specs=[blk, cw, cb],
        out_shape=[SDS(up4.shape, BF16), SDS(cw4.shape, F32), SDS(cb4.shape, F32)],
        args=[up4, up4, up4, cw4, cb4, dh, dh, w_down4])
    return dup4, dcw4, dcb4, rode


def _mm_tile(rows):
    return _row_tile(rows, MM_ROWS_MAX)


def _mm_group(a, b, dims, out_dtype, name):
    m, k = a.shape
    ng = b.shape[0]
    n = b.shape[2] if dims == NN else b.shape[1]
    tm = _mm_tile(m)

    def body(a_ref, b_ref, o_ref):
        o_ref[0] = _dot(a_ref[...].astype(BF16), b_ref[0], dims).astype(out_dtype)

    return pl.pallas_call(
        body, name=name, grid=(ng, m // tm),
        in_specs=[pl.BlockSpec((tm, k), lambda g, i: (i, 0)),
                  pl.BlockSpec((1,) + b.shape[1:], lambda g, i: (g, 0, 0))],
        out_specs=pl.BlockSpec((1, tm, n), lambda g, i: (g, i, 0)),
        out_shape=SDS((ng, m, n), out_dtype), compiler_params=_cp(2))(a, b)


def _mm_reduce(a, b, dims, res, name, ride=None):
    ng, m, k = a.shape
    n = b.shape[2] if dims == NN else b.shape[1]
    tm = _mm_tile(m)
    has_res = res is not None

    def body(a_ref, b_ref, *refs):
        o_ref, acc_ref = refs[-2], refs[-1]
        g = pl.program_id(1)
        p = _dot(a_ref[0].astype(BF16), b_ref[0], dims)

        @pl.when(g == 0)
        def _():
            acc_ref[...] = p + refs[0][...] if has_res else p

        @pl.when(g > 0)
        def _():
            acc_ref[...] += p

        @pl.when(g == ng - 1)
        def _():
            o_ref[...] = acc_ref[...]

    row = pl.BlockSpec((tm, n), lambda i, g: (i, 0))
    (out,), rode = _call_with_ride(
        body, ride, name=name, grid=(m // tm, ng),
        in_specs=[pl.BlockSpec((1, tm, k), lambda i, g: (g, i, 0)),
                  pl.BlockSpec((1,) + b.shape[1:], lambda i, g: (g, 0, 0))] + ([row] if has_res else []),
        out_specs=[row], out_shape=[SDS((m, n), F32)], scratch_shapes=[pltpu.VMEM((tm, n), F32)],
        args=[a, b] + ([res] if has_res else []))
    return out if ride is None else (out, rode)


def _mm_tn(a, b, name, ride=None):
    ga, m, ka = a.shape
    gb, _, n = b.shape
    ng = max(ga, gb)
    tk = _mm_tile(m)
    nk = m // tk

    def body(a_ref, b_ref, o_ref, acc_ref):
        s = pl.program_id(1)
        p = _dot(a_ref[0].astype(BF16), b_ref[0].astype(BF16), TN)

        @pl.when(s == 0)
        def _():
            acc_ref[...] = p

        @pl.when(s > 0)
        def _():
            acc_ref[...] += p

        @pl.when(s == nk - 1)
        def _():
            o_ref[0] = acc_ref[...].astype(BF16)

    (out,), rode = _call_with_ride(
        body, ride, name=name, grid=(ng, nk),
        in_specs=[pl.BlockSpec((1, tk, ka), (lambda g, s: (g, s, 0)) if ga > 1 else (lambda g, s: (0, s, 0))),
                  pl.BlockSpec((1, tk, n), (lambda g, s: (g, s, 0)) if gb > 1 else (lambda g, s: (0, s, 0)))],
        out_specs=[pl.BlockSpec((1, ka, n), lambda g, s: (g, 0, 0))], out_shape=[SDS((ng, ka, n), BF16)],
        scratch_shapes=[pltpu.VMEM((ka, n), F32)], args=[a, b])
    return out if ride is None else (out, rode)


def _pair_tri(kind, sign):
    r = jnp.arange(2 * ATT_BLK)[:, None]
    c = jnp.arange(2 * ATT_BLK)[None, :]
    same = (r < ATT_BLK) == (c < ATT_BLK)
    rel = {"from": r >= c, "before": r < c}[kind]
    return ((same & rel) * sign).astype(BF16)


def _scan_dot(x, tri):
    hi = x.astype(BF16)
    lo = (x - hi.astype(F32)).astype(BF16)
    return _dot(hi, tri) + _dot(lo, tri)


def _split_heads(blk, lane_a):
    zero = jnp.zeros_like(blk)
    return jnp.concatenate([jnp.where(lane_a, blk, zero), jnp.where(lane_a, zero, blk)], axis=0)


def _softplus(z):
    return jnp.maximum(z, 0.0) + jnp.log(1.0 + jnp.exp2(jnp.abs(z) * (-LOG2_E)))


def _visible(qi, j):
    t = qi * ATT_Q + lax.broadcasted_iota(jnp.int32, (ATT_Q, 2 * ATT_BLK), 0)
    s = j * ATT_BLK + (lax.broadcasted_iota(jnp.int32, (ATT_Q, 2 * ATT_BLK), 1) & (ATT_BLK - 1))
    return s < t


def _halves(x):
    return x[:, :ATT_BLK], x[:, ATT_BLK:]


def _rowsum(x):
    return jnp.sum(x, axis=1, keepdims=True)


def _still_visible(ca, cb):
    return jnp.minimum(jnp.min(ca), jnp.min(cb)) < UNDERFLOW_AT


def _attn_specs(lp):
    bk = ATT_BLK
    qblk = pl.BlockSpec((ATT_Q, bk), lambda p, i: (i, p))
    kblk = pl.BlockSpec((1, lp, bk), lambda p, i: (p // 2, 0, p % 2))
    vblk = pl.BlockSpec((1, lp, bk), lambda p, i: (HEAD_PAIRS // 2 + p // 2, 0, p % 2))
    tri = pl.BlockSpec((2 * bk, 2 * bk), lambda p, i: (0, 0))
    return qblk, kblk, vblk, tri


def _attn_fwd(q, kv, name, ride=None):
    lp, d = q.shape
    bk = ATT_BLK

    def body(q_ref, k_ref, v_ref, tri_ref, o_ref):
        qi = pl.program_id(1)
        qs = q_ref[...] * (HEAD_DIM ** -0.5)
        lane_a = lax.broadcasted_iota(jnp.int32, (1, bk), 1) < HEAD_DIM
        tri = tri_ref[...]

        def trip(js, carry, masked):
            oacc, ca, cb = carry
            rows = [pl.ds(pl.multiple_of(j * bk, bk), bk) for j in js]
            zs = [_dot(qs, _split_heads(k_ref[0, r, :], lane_a), NT) for r in rows]
            ms = [_softplus(z) for z in zs]
            if masked:
                ms = [jnp.where(_visible(qi, j), m, 0.0) for j, m in zip(js, ms)]
            ws = [_scan_dot(m, tri) for m in ms]
            for j, r, z, m, w in zip(js, rows, zs, ms, ws):
                exa, exb = _halves(z + w)
                a = jnp.concatenate([jnp.exp(exa - ca), jnp.exp(exb - cb)], axis=1)
                if masked:
                    a = jnp.where(_visible(qi, j), a, 0.0)
                oacc = oacc + _dot(a.astype(BF16), _split_heads(v_ref[0, r, :], lane_a))
                ma, mb = _halves(m)
                ca, cb = ca + _rowsum(ma), cb + _rowsum(mb)
            return oacc, ca, cb

        carry = (jnp.zeros((ATT_Q, bk), F32), jnp.zeros((ATT_Q, 1), F32), jnp.zeros((ATT_Q, 1), F32))
        top = (qi + 1) * ATT_UNROLL - 1
        carry = trip([top - u for u in range(ATT_UNROLL)], carry, True)
        _, oacc, _, _ = lax.while_loop(
            lambda st: (st[0] < qi) & _still_visible(st[2], st[3]),
            lambda st: (st[0] + 1, *trip([top - (st[0] + 1) * ATT_UNROLL - u for u in range(ATT_UNROLL)], st[1:], False)),
            (jnp.int32(0), *carry))
        o_ref[...] = oacc.astype(BF16)

    qblk, kblk, vblk, tri = _attn_specs(lp)
    (o,), rode = _call_with_ride(
        body, ride, name=name, grid=(HEAD_PAIRS, lp // ATT_Q), in_specs=[qblk, kblk, vblk, tri],
        out_specs=[qblk], out_shape=[SDS((lp, d), BF16)], args=[q, kv, kv, _pair_tri("from", -1)])
    return o, rode


def _attn_bwd(q, kv, do, name, ride=None):
    lp, d = q.shape
    bk = ATT_BLK
    scale = HEAD_DIM ** -0.5

    def body(q_ref, k_ref, v_ref, do_ref, tri_ref, dq_ref, dk_ref, dv_ref):
        qi = pl.program_id(1)

        @pl.when(qi == 0)
        def _():
            dk_ref[...] = jnp.zeros_like(dk_ref)
            dv_ref[...] = jnp.zeros_like(dv_ref)

        qs = q_ref[...] * scale
        do_blk = do_ref[...]
        lane_a = lax.broadcasted_iota(jnp.int32, (1, bk), 1) < HEAD_DIM
        tri = tri_ref[...]

        def sums(js, carry, masked):
            ca, cb = carry
            for j in js:
                m = _softplus(_dot(qs, _split_heads(k_ref[0, pl.ds(pl.multiple_of(j * bk, bk), bk), :], lane_a), NT))
                if masked:
                    m = jnp.where(_visible(qi, j), m, 0.0)
                ma, mb = _halves(m)
                ca, cb = ca + _rowsum(ma), cb + _rowsum(mb)
            return ca, cb

        def trip(js, carry, masked):
            dq, pa, pb, ea, eb = carry
            rows = [pl.ds(pl.multiple_of(j * bk, bk), bk) for j in js]
            kks = [_split_heads(k_ref[0, r, :], lane_a) for r in rows]
            zs = [_dot(qs, kk, NT) for kk in kks]
            das = [_dot(do_blk, _split_heads(v_ref[0, r, :], lane_a), NT) for r in rows]
            ms = [_softplus(z) for z in zs]
            if masked:
                ms = [jnp.where(_visible(qi, j), m, 0.0) for j, m in zip(js, ms)]
            xs = [_scan_dot(m, tri) for m in ms]
            es, a_bf = [], []
            for j, z, m, x, da in zip(js, zs, ms, xs, das):
                xa, xb = _halves(z + x)
                a = jnp.concatenate([jnp.exp(xa + pa), jnp.exp(xb + pb)], axis=1)
                if masked:
                    a = jnp.where(_visible(qi, j), a, 0.0)
                a_bf.append(a.astype(BF16))
                es.append(a * da)
                ma, mb = _halves(m)
                pa, pb = pa + _rowsum(ma), pb + _rowsum(mb)
            ss = [_dot(e.astype(BF16), tri) for e in es]
            for j, r, kk, z, m, e, s, ab in zip(js, rows, kks, zs, ms, es, ss, a_bf):
                sa, sb = _halves(s)
                e_before = jnp.concatenate([sa + ea, sb + eb], axis=1)
                dz = e - jnp.exp(z - m) * (e + e_before)
                if masked:
                    dz = jnp.where(_visible(qi, j), dz, 0.0)
                dzb = dz.astype(BF16)
                dq = dq + _dot(dzb, kk)
                rk = _dot(dzb, qs, TN)
                rv = _dot(ab, do_blk, TN)
                dk_ref[0, r, :] += jnp.where(lane_a, rk[:bk], rk[bk:])
                dv_ref[0, r, :] += jnp.where(lane_a, rv[:bk], rv[bk:])
                e_a, e_b = _halves(e)
                ea, eb = ea + _rowsum(e_a), eb + _rowsum(e_b)
            return dq, pa, pb, ea, eb

        zcol = jnp.zeros((ATT_Q, 1), F32)
        diag = [qi * ATT_UNROLL + u for u in range(ATT_UNROLL)]
        n_old, ta, tb = lax.while_loop(
            lambda st: (st[0] < qi) & _still_visible(st[1], st[2]),
            lambda st: (st[0] + 1, *sums([(qi - 1 - st[0]) * ATT_UNROLL + u for u in range(ATT_UNROLL)], st[1:], False)),
            (jnp.int32(0), *sums(diag, (zcol, zcol), True)))
        carry = lax.fori_loop(
            0, n_old, lambda g, c: trip([(qi - n_old + g) * ATT_UNROLL + u for u in range(ATT_UNROLL)], c, False),
            (jnp.zeros((ATT_Q, bk), F32), -ta, -tb, zcol, zcol))
        carry = trip(diag, carry, True)
        dq_ref[...] = (carry[0] * scale).astype(BF16)

    qblk, kblk, vblk, tri = _attn_specs(lp)
    (dq, dk, dv), rode = _call_with_ride(
        body, ride, name=name, grid=(HEAD_PAIRS, lp // ATT_Q), in_specs=[qblk, kblk, vblk, qblk, tri],
        out_specs=[qblk, kblk, kblk],
        out_shape=[SDS((lp, d), BF16), SDS((HEAD_PAIRS // 2, lp, 2 * bk), F32), SDS((HEAD_PAIRS // 2, lp, 2 * bk), F32)],
        args=[q, kv, kv, do, _pair_tri("before", 1)])
    return dq, dk, dv, rode


def _mesh_pos():
    return lax.axis_index("x"), lax.axis_index("y"), lax.axis_index("c")


def _flip(pos, r):
    x, y, c = pos
    return (1 - x if r & 4 else x, 1 - y if r & 2 else y, 1 - c if r & 1 else c)


def _dev_index(pos):
    return 4 * pos[0] + 2 * pos[1] + pos[2]


class _Ride(NamedTuple):
    kind: str
    arrays: list


def _ride_arrays(ride):
    return [] if ride is None else ride.arrays


def _ride_args(ride):
    if ride is None:
        return [], [], [], []
    n = len(ride.arrays)
    hbm = pl.BlockSpec(memory_space=pl.ANY)
    shapes = [SDS(x.shape if ride.kind == "scatter" else (N_DEV,) + x.shape, x.dtype) for x in ride.arrays]
    sems = [pltpu.SemaphoreType.DMA((7 * n,)), pltpu.SemaphoreType.DMA((7 * n,)), pltpu.SemaphoreType.DMA((n,))]
    return [hbm] * n, [hbm] * n, shapes, sems


def _riding(body, n_in, n_out, ride, first, middle, last):
    if ride is None:
        return body
    n = len(ride.arrays)

    def wrapped(*refs):
        ins, srcs = refs[:n_in], refs[n_in:n_in + n]
        outs, dsts = refs[n_in + n:n_in + n + n_out], refs[n_in + n + n_out:n_in + 2 * n + n_out]
        scratch, (send_sems, recv_sems, local_sems) = refs[n_in + 2 * n + n_out:-3], refs[-3:]
        me = _mesh_pos()
        mi = _dev_index(me)

        def copy(a, k, src, dst, to):
            return pltpu.make_async_remote_copy(
                src_ref=src, dst_ref=dst, send_sem=send_sems.at[7 * a + k], recv_sem=recv_sems.at[7 * a + k],
                device_id=to, device_id_type=pl.DeviceIdType.MESH)

        local, sends, lands, arrived, passed = [], [], [], [], []
        for a in range(n):
            if ride.kind == "gather_by_chip":
                sibling, others = _flip(me, 1), [_flip(me, 4), _flip(me, 2), _flip(me, 6)]
                local.append(pltpu.make_async_copy(srcs[a], dsts[a].at[mi], local_sems.at[a]))
                sends.append(copy(a, 0, srcs[a], dsts[a].at[mi], sibling))
                lands.append(copy(a, 0, dsts[a].at[_dev_index(sibling)], dsts[a].at[_dev_index(sibling)], me))
                for j, o in enumerate(others):
                    oi, si = _dev_index(o), _dev_index(_flip(o, 1))
                    sends.append(copy(a, 1 + j, srcs[a], dsts[a].at[mi], o))
                    arrived.append(copy(a, 1 + j, dsts[a].at[oi], dsts[a].at[oi], me))
                    passed.append(copy(a, 4 + j, dsts[a].at[oi], dsts[a].at[oi], sibling))
                    lands.append(copy(a, 4 + j, dsts[a].at[si], dsts[a].at[si], me))
                continue
            gather = ride.kind == "gather"
            local.append(pltpu.make_async_copy(srcs[a] if gather else srcs[a].at[mi], dsts[a].at[mi], local_sems.at[a]))
            for r in range(1, N_DEV):
                peer = _flip(me, r)
                pi = _dev_index(peer)
                sends.append(copy(a, r - 1, srcs[a] if gather else srcs[a].at[pi], dsts[a].at[mi], peer))
                lands.append(copy(a, r - 1, dsts[a].at[pi], dsts[a].at[pi], peer))

        @pl.when(first())
        def _():
            for cp in local + sends:
                cp.start()

        if passed:
            @pl.when(middle())
            def _():
                for got, on in zip(arrived, passed):
                    got.wait_recv()
                    on.start()

        body(*ins, *outs, *scratch)

        @pl.when(last())
        def _():
            for cp in lands:
                cp.wait_recv()
            for cp in sends + passed:
                cp.wait_send()
            for cp in local:
                cp.wait()

    return wrapped


def _call_with_ride(body, ride, *, name, grid, in_specs, out_specs, out_shape, args, scratch_shapes=()):
    ride_in, ride_out, ride_shape, ride_sems = _ride_args(ride)
    axes = range(len(grid))
    assert ride is None or ride.kind != "gather_by_chip" or grid[0] >= 4, grid

    def at(step):
        return lambda: functools.reduce(lambda p, k: p & (pl.program_id(k) == step[k]), axes, True)

    ends = [(0,) * len(grid), (3 * grid[0] // 4,) + (0,) * (len(grid) - 1), tuple(g - 1 for g in grid)]
    out = pl.pallas_call(
        _riding(body, len(in_specs), len(out_specs), ride, *map(at, ends)), name=name, grid=grid,
        in_specs=list(in_specs) + ride_in, out_specs=list(out_specs) + ride_out,
        out_shape=list(out_shape) + ride_shape, scratch_shapes=list(scratch_shapes) + ride_sems,
        compiler_params=_cp(len(grid)))(*args, *_ride_arrays(ride))
    return out[:len(out_specs)], out[len(out_specs):]


def _all_gather(xs, name):
    n = len(xs)

    def body(*refs):
        x_refs, out_refs = refs[:n], refs[n:2 * n]
        send_sems, recv_sems, local_sems = refs[2 * n:]
        me = _mesh_pos()
        sibling = _flip(me, 1)
        others = [_flip(me, 4), _flip(me, 2), _flip(me, 6)]

        def copy(a, k, block, to, own=False):
            slab = out_refs[a].at[_dev_index(block)]
            return pltpu.make_async_remote_copy(
                src_ref=x_refs[a] if own else slab, dst_ref=slab,
                send_sem=send_sems.at[7 * a + k], recv_sem=recv_sems.at[7 * a + k],
                device_id=to, device_id_type=pl.DeviceIdType.MESH)

        mine = [pltpu.make_async_copy(x_refs[a], out_refs[a].at[_dev_index(me)], local_sems.at[a]) for a in range(n)]
        first = []
        for a in range(n):
            mine[a].start()
            first += [copy(a, 0, me, sibling, own=True)] + [copy(a, 1 + j, me, o, own=True) for j, o in enumerate(others)]
        for cp in first:
            cp.start()
        passed = []
        for a in range(n):
            for j, o in enumerate(others):
                copy(a, 1 + j, o, me).wait_recv()
                passed.append(copy(a, 4 + j, o, sibling))
                passed[-1].start()
        for a in range(n):
            copy(a, 0, sibling, me).wait_recv()
            for j, o in enumerate(others):
                copy(a, 4 + j, _flip(o, 1), me).wait_recv()
        for cp in first + passed:
            cp.wait_send()
        for cp in mine:
            cp.wait()

    hbm = pl.BlockSpec(memory_space=pl.ANY)
    return pl.pallas_call(
        body, name=name, out_shape=[SDS((N_DEV,) + x.shape, x.dtype) for x in xs],
        in_specs=[hbm] * n, out_specs=[hbm] * n,
        scratch_shapes=[pltpu.SemaphoreType.DMA((7 * n,)), pltpu.SemaphoreType.DMA((7 * n,)), pltpu.SemaphoreType.DMA((n,))],
    )(*xs)


def _sum_slabs(a, name, ride=None):
    n, rows, cols = a.shape
    tr = rows if a.size * a.dtype.itemsize <= SUM_WHOLE_BYTES else _row_tile(rows, SUM_ROWS_MAX, 16)

    def body(a_ref, o_ref):
        acc = a_ref[0].astype(F32)
        for k in range(1, n):
            acc = acc + a_ref[k].astype(F32)
        o_ref[...] = acc

    (out,), rode = _call_with_ride(
        body, ride, name=name, grid=(rows // tr,),
        in_specs=[pl.BlockSpec((n, tr, cols), lambda i: (0, i, 0))],
        out_specs=[pl.BlockSpec((tr, cols), lambda i: (i, 0))], out_shape=[SDS((rows, cols), F32)], args=[a])
    return out if ride is None else (out, rode)


def _adamw(w, g, m, v, name):
    rows, cols = w.shape
    tr = _row_tile(rows, 352)

    def body(w_ref, g_ref, m_ref, v_ref, d_ref, mo_ref, vo_ref):
        g_ = g_ref[...]
        m_ = ADAM_B1 * m_ref[...] + (1.0 - ADAM_B1) * g_
        v_ = ADAM_B2 * v_ref[...] + (1.0 - ADAM_B2) * (g_ * g_)
        m_hat = m_ / (1.0 - ADAM_B1 ** ADAM_STEP)
        v_hat = v_ / (1.0 - ADAM_B2 ** ADAM_STEP)
        d_ref[...] = -ADAM_LR * (m_hat / (jnp.sqrt(v_hat) + ADAM_EPS) + ADAM_WD * w_ref[...])
        mo_ref[...] = m_
        vo_ref[...] = v_

    blk = pl.BlockSpec((tr, cols), lambda i: (i, 0))
    return pl.pallas_call(
        body, name=name, grid=(rows // tr,),
        in_specs=[blk] * 4, out_specs=[blk] * 3, out_shape=[SDS((rows, cols), F32)] * 3,
        compiler_params=_cp(1))(w, g, m, v)


def _ffn_bwd(h, gain, w_up, cw4, cb4, w_down4, saved, dh, tag, ride_wup=None, scatter_own=False):
    n2, up4, act = saved
    d_w_down = _mm_tn(act, dh[None], f"ffn_dwdown_{tag}")
    ride_gate = _Ride("scatter", [d_w_down.reshape(N_DEV, -1, d_w_down.shape[-1])]) if scatter_own else None
    dup4, dcw4, dcb4, rode = _ffn_act_bwd(up4, cw4, cb4, dh, w_down4, f"ffn_dgate_{tag}", ride_gate)
    if scatter_own:
        (d_w_down,) = rode
    dup = dup4.reshape((8,) + dup4.shape[2:])
    d_w_up, rode_wup = _mm_tn(n2[None], dup, f"ffn_dwup_{tag}", ride_wup), []
    if ride_wup is not None:
        d_w_up, rode_wup = d_w_up
    dn2 = _mm_reduce(dup, w_up, NT, None, f"ffn_dnorm_{tag}", _Ride("scatter", [d_w_up]) if scatter_own else None)
    if scatter_own:
        dn2, (d_w_up,) = dn2
    dh_in, dgain = _rms_bwd(h, gain, [dn2], dh, f"ffn_dh_{tag}")
    return dh_in, dgain, d_w_up, d_w_down, dcw4, dcb4, rode_wup


def kernel(x, meta_tokens, mix_norm, ffn_norm, pool_w, pool_scale, kv_norm, w_kv, w_q, w_o, ffn_w_up, ffn_conv_w, ffn_conv_b, ffn_w_down, final_norm, loss_target, m_meta_tokens, m_mix_norm, m_ffn_norm, m_pool_w, m_pool_scale, m_kv_norm, m_w_kv, m_w_q, m_w_o, m_ffn_w_up, m_ffn_conv_w, m_ffn_conv_b, m_ffn_w_down, m_final_norm, v_meta_tokens, v_mix_norm, v_ffn_norm, v_pool_w, v_pool_scale, v_kv_norm, v_w_kv, v_w_q, v_w_o, v_ffn_w_up, v_ffn_conv_w, v_ffn_conv_b, v_ffn_w_down, v_final_norm):
    seq, d = x.shape[1], x.shape[2]
    n_tok = N_META + seq
    lp = -(-n_tok // ROW_TILE) * ROW_TILE
    fc = ffn_w_up.shape[2]
    me = _dev_index(_mesh_pos())

    def rows_of(parts):
        rows = [p.size // d for p in parts]
        return [sum(rows[:k]) for k in range(len(parts) + 1)]

    def bf16_rows(parts):
        return jnp.concatenate([p.reshape(-1, d) for p in parts], axis=0).astype(BF16)

    g_pw, wup0 = _all_gather([bf16_rows([pool_w]), ffn_w_up[0].astype(BF16)], "gather_matrices")
    pw = g_pw.reshape(N_DEV, 4, POOL_C // N_DEV, POOL_C).transpose(1, 0, 2, 3).reshape(4, POOL_C, POOL_C)
    early_parts, late_parts = [ffn_w_down[0], w_kv], [w_o, ffn_w_down[1]]
    early_off, late_off = rows_of(early_parts), rows_of(late_parts)

    small_parts = [meta_tokens, pool_scale, ffn_conv_w]
    small_rows = [p.size // 128 for p in small_parts]
    small_pad = -sum(small_rows) % 8
    local_small = jnp.concatenate([p.reshape(-1, 128) for p in small_parts] + [jnp.zeros((small_pad, 128), F32)], axis=0)
    (gs,) = _all_gather([local_small], "gather_vectors")
    r0, r1, r2 = small_rows[0], small_rows[0] + small_rows[1], sum(small_rows)
    meta_full = gs[:, :r0].transpose(1, 0, 2).reshape(N_META, d)
    pscale = gs[:, r0:r1].reshape(1, d)
    cw = gs[:, r1:r2].reshape(N_DEV, 2, 3, fc)
    cw4_l = [cw[:, l].reshape(2, 4, 3, fc) for l in range(2)]
    cb4_l = [ffn_conv_b[l].reshape(2, 4, 1, fc) for l in range(2)]

    h0 = jnp.concatenate([meta_full, x[0], jnp.zeros((lp - n_tok, d), F32)], axis=0)
    h1, diff = _pool_fwd(h0, mix_norm[0:1], pw, pscale, "pool_fwd")
    (n2_0,) = _rms_fwd(h1, ffn_norm[0:1], "ffn_norm_0")
    up4_0, act0, (g_early,) = _ffn_up_act(n2_0, wup0.reshape(2, 4, d, fc), cw4_l[0], cb4_l[0], "ffn_up_0",
                                          _Ride("gather_by_chip", [bf16_rows(early_parts)]))
    wdn0 = g_early[:, early_off[0]:early_off[1]].reshape(4, fc, d)
    wkv = g_early[:, early_off[1]:early_off[2]].reshape(N_DEV, d, 2 * d // N_DEV)
    h2, (wq,) = _mm_reduce(act0, wdn0, NN, h1, "ffn_down_0", _Ride("gather", [w_q[0].astype(BF16)]))
    wq = wq.reshape(1, d, d)
    gains_b = jnp.stack([kv_norm, mix_norm[1]], axis=0)
    kvn, n3 = _rms_fwd(h2, gains_b, "attn_norms")
    kv = _mm_group(kvn, wkv, NN, BF16, "kv_proj")
    q = _mm_group(n3, wq, NN, BF16, "q_proj")[0]
    o, (g_late, wup1) = _attn_fwd(
        q, kv, "attn_fwd", _Ride("gather_by_chip", [bf16_rows(late_parts), ffn_w_up[1].astype(BF16)]))
    wo = g_late[:, late_off[0]:late_off[1]].reshape(1, d, d)
    wdn1 = g_late[:, late_off[1]:late_off[2]].reshape(4, fc, d)
    h3 = _mm_reduce(o[None], wo, NN, h2, "o_proj")
    (n2_1,) = _rms_fwd(h3, ffn_norm[1:2], "ffn_norm_1")
    up4_1, act1, _ = _ffn_up_act(n2_1, wup1.reshape(2, 4, d, fc), cw4_l[1], cb4_l[1], "ffn_up_1")
    h4 = _mm_reduce(act1, wdn1, NN, h3, "ffn_down_1")
    target = jnp.pad(loss_target[0], ((N_META, lp - n_tok), (0, 0)))
    dh4, loss_blk, dg_final = _loss_bwd(h4, final_norm[None], target, seq, "loss")
    loss = lax.psum(loss_blk[0, 0], MESH_AXES)

    dh3, dg_ffn1, d_wup1, d_wdn1, dcw4_1, dcb4_1, _ = _ffn_bwd(
        h3, ffn_norm[1:2], wup1, cw4_l[1], cb4_l[1], wdn1, (n2_1, up4_1, act1), dh4, "1")
    d_o = _mm_group(dh3, wo, NT, BF16, "o_proj_dx")[0]
    d_wo = _mm_tn(o[None], dh3[None], "o_proj_dw")
    ride_late = _Ride("scatter", [jnp.concatenate([d_wo.reshape(N_DEV, -1, d), d_wdn1.reshape(N_DEV, -1, d)], axis=1), d_wup1])
    dq, dk, dv, (p_late, p_up1) = _attn_bwd(q, kv, d_o, "attn_bwd", ride_late)
    dn3 = _mm_group(dq, wq, NT, F32, "q_proj_dx")[0]
    d_wq = _mm_tn(n3[None], dq[None], "q_proj_dw")
    dkv = jnp.concatenate([dk, dv], axis=0).astype(BF16)
    dkvn = _mm_reduce(dkv, wkv, NT, None, "kv_proj_dx")
    d_wkv = _mm_tn(kvn[None], dkv, "kv_proj_dw")
    dh2, dg_b = _rms_bwd(h2, gains_b, [dkvn, dn3], dh3, "attn_norms_bwd")
    ride_proj = _Ride("scatter", [jnp.concatenate([d_wkv.reshape(N_DEV, -1, d), d_wq.reshape(N_DEV, -1, d)], axis=1)])

    dh1, dg_ffn0, p_up0, p_dn0, dcw4_0, dcb4_0, (p_proj,) = _ffn_bwd(
        h1, ffn_norm[0:1], wup0, cw4_l[0], cb4_l[0], wdn0, (n2_0, up4_0, act0), dh2, "0", ride_proj, scatter_own=True)
    dh0, d_pw, d_pscale, dg_mix0 = _pool_bwd(h0, mix_norm[0:1], pw, pscale, diff, dh1, "pool_bwd")
    grad_x = dh0[N_META:n_tok][None]
    d_pw8 = d_pw.reshape(4, N_DEV, POOL_C // N_DEV, POOL_C).transpose(1, 0, 2, 3).reshape(N_DEV, -1, d).astype(BF16)
    s_up0, (p_pw,) = _sum_slabs(p_up0, "sum_up0", _Ride("scatter", [d_pw8]))
    s_late, s_up1, s_proj, s_dn0, s_pw = [_sum_slabs(p, "sum_" + n) for p, n in (
        (p_late, "late"), (p_up1, "up1"), (p_proj, "proj"), (p_dn0, "down0"), (p_pw, "pool"))]
    n_kv, n_o = w_kv.size // d, w_o.size // d

    rep_parts = [jnp.concatenate([dg_mix0, dg_b[1:2]], axis=0), jnp.concatenate([dg_ffn0, dg_ffn1], axis=0),
                 dg_b[0:1], dg_final, jnp.stack([dcb4_0.reshape(-1), dcb4_1.reshape(-1)], axis=0)]
    rep_shapes = [mix_norm.shape, ffn_norm.shape, kv_norm.shape, final_norm.shape, ffn_conv_b.shape]
    rep_rows = [p.size // 128 for p in rep_parts]
    d_meta8 = dh0[:N_META].reshape(N_META, N_DEV, d // N_DEV).transpose(1, 0, 2).reshape(N_DEV, -1, 128)
    d_cw8 = jnp.stack([dcw4_0.reshape(N_DEV, 3, fc), dcw4_1.reshape(N_DEV, 3, fc)], axis=1).reshape(N_DEV, -1, 128)
    shard_parts = jnp.concatenate([d_meta8, d_pscale.reshape(N_DEV, 1, 128), d_cw8], axis=1)
    n_rep = sum(rep_rows)
    partial_small = jnp.concatenate([p.reshape(-1, 128) for p in rep_parts] + [shard_parts.reshape(-1, 128)], axis=0)
    g_small = _sum_slabs(_all_gather([partial_small], "gather_vector_grads")[0], "sum_vectors")
    g_rep = [g_small[sum(rep_rows[:k]):sum(rep_rows[:k + 1])].reshape(s) for k, s in enumerate(rep_shapes)]
    g_shard = lax.dynamic_index_in_dim(g_small[n_rep:].reshape(N_DEV, -1, 128), me, 0, keepdims=False)
    g_meta = g_shard[:r0].reshape(meta_tokens.shape)
    g_pscale = g_shard[r0:r1].reshape(pool_scale.shape)
    g_cw = g_shard[r1:r2].reshape(ffn_conv_w.shape)

    grads = {
        "meta_tokens": g_meta, "mix_norm": g_rep[0], "ffn_norm": g_rep[1],
        "pool_w": s_pw.reshape(pool_w.shape), "pool_scale": g_pscale, "kv_norm": g_rep[2],
        "w_kv": s_proj[:n_kv].reshape(w_kv.shape), "w_q": s_proj[n_kv:].reshape(w_q.shape),
        "w_o": s_late[:n_o].reshape(w_o.shape),
        "ffn_w_up": jnp.stack([s_up0, s_up1], axis=0), "ffn_conv_w": g_cw, "ffn_conv_b": g_rep[4],
        "ffn_w_down": jnp.stack([s_dn0, s_late[n_o:]], axis=0), "final_norm": g_rep[3],
    }
    names = list(grads)
    weights = dict(zip(names, [meta_tokens, mix_norm, ffn_norm, pool_w, pool_scale, kv_norm, w_kv, w_q, w_o,
                               ffn_w_up, ffn_conv_w, ffn_conv_b, ffn_w_down, final_norm]))
    mom1 = dict(zip(names, [m_meta_tokens, m_mix_norm, m_ffn_norm, m_pool_w, m_pool_scale, m_kv_norm, m_w_kv, m_w_q,
                            m_w_o, m_ffn_w_up, m_ffn_conv_w, m_ffn_conv_b, m_ffn_w_down, m_final_norm]))
    mom2 = dict(zip(names, [v_meta_tokens, v_mix_norm, v_ffn_norm, v_pool_w, v_pool_scale, v_kv_norm, v_w_kv, v_w_q,
                            v_w_o, v_ffn_w_up, v_ffn_conv_w, v_ffn_conv_b, v_ffn_w_down, v_final_norm]))

    delta, new_m, new_v = {}, {}, {}
    for n in names:
        shape = weights[n].shape
        flat = (-1, shape[-1])
        dl, nm, nv = _adamw(weights[n].reshape(flat), grads[n].reshape(flat), mom1[n].reshape(flat),
                            mom2[n].reshape(flat), "adamw_" + n)
        delta[n], new_m[n], new_v[n] = dl.reshape(shape), nm.reshape(shape), nv.reshape(shape)
    return (loss, grad_x, *[grads[n] for n in names], *[delta[n] for n in names],
            *[new_m[n] for n in names], *[new_v[n] for n in names])
```

```python
import functools
from typing import NamedTuple

import jax
import jax.numpy as jnp
from jax import lax
from jax.experimental import pallas as pl
from jax.experimental.pallas import tpu as pltpu

F32 = jnp.float32
BF16 = jnp.bfloat16
SDS = jax.ShapeDtypeStruct

N_DEV = 8
N_META = 16
HEAD_DIM = 64
HEAD_PAIRS = 8
RMS_EPS = 1e-6
LOG2_E = 1.4426950408889634
POOL_WINDOWS = (2, 4, 8, 16)
POOL_C = 256
POOL_HALO = 16
CONV_HALO = 8
ROW_TILE = 384
MM_ROWS_MAX = 1408
FFN_ROWS_MAX = 704
SUM_ROWS_MAX = 256
SUM_WHOLE_BYTES = 4 << 20
ATT_BLK = 128
ATT_Q = ROW_TILE
ATT_UNROLL = ATT_Q // ATT_BLK
UNDERFLOW_AT = 104.0
VMEM_LIMIT = 56 * 1024 * 1024

ADAM_LR = 0.001
ADAM_B1 = 0.9
ADAM_B2 = 0.999
ADAM_EPS = 1e-08
ADAM_WD = 0.01
ADAM_STEP = 10

MESH_AXES = ("x", "y", "c")
NN = (((1,), (0,)), ((), ()))
NT = (((1,), (1,)), ((), ()))
TN = (((0,), (0,)), ((), ()))


def _cp(n_axes):
    return pltpu.CompilerParams(dimension_semantics=("arbitrary",) * n_axes, vmem_limit_bytes=VMEM_LIMIT)


def _dot(a, b, dims=NN):
    return lax.dot_general(a, b, dims, preferred_element_type=F32)


def _rstd(x):
    return lax.rsqrt(jnp.mean(x * x, axis=-1, keepdims=True) + RMS_EPS)


def _row_tile(rows, cap=512, mult=8):
    if rows <= cap:
        return rows
    best = mult
    for t in range(mult, cap + 1, mult):
        if rows % t == 0:
            best = t
    assert rows % best == 0
    return best


def _rms_fwd(h, gains, name):
    lp, d = h.shape
    k = gains.shape[0]
    tm = ROW_TILE

    def body(h_ref, g_ref, *o_refs):
        x = h_ref[...]
        u = x * _rstd(x)
        for j in range(k):
            o_refs[j][...] = (u * g_ref[j:j + 1, :]).astype(BF16)

    row = pl.BlockSpec((tm, d), lambda i: (i, 0))
    return pl.pallas_call(
        body, name=name, grid=(lp // tm,),
        in_specs=[row, pl.BlockSpec((k, d), lambda i: (0, 0))],
        out_specs=[row] * k, out_shape=[SDS((lp, d), BF16)] * k,
        compiler_params=_cp(1))(h, gains)


def _rms_bwd(h, gains, dns, dh_in, name):
    lp, d = h.shape
    k = gains.shape[0]
    tm = ROW_TILE

    def body(h_ref, g_ref, *refs):
        dn_refs, dh_ref, dho_ref, dg_ref = refs[:k], refs[k], refs[k + 1], refs[k + 2]
        i = pl.program_id(0)
        x = h_ref[...]
        r = _rstd(x)
        u = x * r
        du = jnp.zeros_like(x)
        rows = []
        for j in range(k):
            dn = dn_refs[j][...]
            du = du + dn * g_ref[j:j + 1, :]
            rows.append(jnp.sum(dn * u, axis=0, keepdims=True))
        dx = r * (du - u * jnp.mean(du * u, axis=-1, keepdims=True))
        dho_ref[...] = dh_ref[...] + dx

        @pl.when(i == 0)
        def _():
            for j in range(k):
                dg_ref[j:j + 1, :] = rows[j]

        @pl.when(i > 0)
        def _():
            for j in range(k):
                dg_ref[j:j + 1, :] += rows[j]

    row = pl.BlockSpec((tm, d), lambda i: (i, 0))
    vec = pl.BlockSpec((k, d), lambda i: (0, 0))
    return pl.pallas_call(
        body, name=name, grid=(lp // tm,),
        in_specs=[row, vec] + [row] * k + [row],
        out_specs=[row, vec], out_shape=[SDS((lp, d), F32), SDS((k, d), F32)],
        compiler_params=_cp(1))(h, gains, *dns, dh_in)


def _loss_bwd(h, gain, target, n_real, name):
    lp, d = h.shape
    tm = ROW_TILE

    def body(h_ref, g_ref, t_ref, dh_ref, loss_ref, dg_ref):
        i = pl.program_id(0)
        x = h_ref[...]
        g = g_ref[...]
        r = _rstd(x)
        u = x * r
        row = i * tm + lax.broadcasted_iota(jnp.int32, (tm, 1), 0)
        valid = (row >= N_META) & (row < N_META + n_real)
        e = jnp.where(valid, u * g - t_ref[...], 0.0)
        part = 0.5 * jnp.sum(jnp.sum(e * e, axis=-1, keepdims=True), axis=0, keepdims=True) * (1.0 / d)
        dy = e * (1.0 / d)
        du = dy * g
        dh_ref[...] = r * (du - u * jnp.mean(du * u, axis=-1, keepdims=True))
        dgp = jnp.sum(dy * u, axis=0, keepdims=True)

        @pl.when(i == 0)
        def _():
            loss_ref[...] = jnp.broadcast_to(part, (8, 128))
            dg_ref[...] = dgp

        @pl.when(i > 0)
        def _():
            loss_ref[...] += jnp.broadcast_to(part, (8, 128))
            dg_ref[...] += dgp

    row = pl.BlockSpec((tm, d), lambda i: (i, 0))
    vec = pl.BlockSpec((1, d), lambda i: (0, 0))
    return pl.pallas_call(
        body, name=name, grid=(lp // tm,),
        in_specs=[row, vec, row],
        out_specs=[row, pl.BlockSpec((8, 128), lambda i: (0, 0)), vec],
        out_shape=[SDS((lp, d), F32), SDS((8, 128), F32), SDS((1, d), F32)],
        compiler_params=_cp(1))(h, gain, target)


def _pool_fwd(h, gain, w, scale, name):
    lp, d = h.shape
    tm = ROW_TILE
    hb = POOL_HALO

    def body(h_ref, halo_ref, g_ref, w_ref, s_ref, h1_ref, diff_ref):
        i = pl.program_id(0)
        g = g_ref[...]
        x = h_ref[...]
        n = x * _rstd(x) * g
        xh = halo_ref[...]
        nh = jnp.where(i > 0, xh * _rstd(xh) * g, 0.0)
        cur = jnp.concatenate([nh, n], axis=0)
        pos = i * tm + lax.broadcasted_iota(jnp.int32, (tm, 1), 0)
        for gi, win in enumerate(POOL_WINDOWS):
            if gi > 0:
                cur = cur[:, POOL_C:]
            cur = cur + pltpu.roll(cur, win // 2, 0)
            c0 = gi * POOL_C
            count = jnp.minimum(pos + 1, win).astype(F32)
            diff = cur[hb:, :POOL_C] / count - n[:, c0:c0 + POOL_C]
            diff = diff.astype(BF16)
            y = _dot(diff, w_ref[gi])
            h1_ref[:, c0:c0 + POOL_C] = x[:, c0:c0 + POOL_C] + y * s_ref[:, c0:c0 + POOL_C]
            diff_ref[:, c0:c0 + POOL_C] = diff

    row = pl.BlockSpec((tm, d), lambda i: (i, 0))
    halo = pl.BlockSpec((hb, d), lambda i: (jnp.maximum(i * (tm // hb) - 1, 0), 0))
    vec = pl.BlockSpec((1, d), lambda i: (0, 0))
    return pl.pallas_call(
        body, name=name, grid=(lp // tm,),
        in_specs=[row, halo, vec, pl.BlockSpec(w.shape, lambda i: (0, 0, 0)), vec],
        out_specs=[row, row], out_shape=[SDS((lp, d), F32), SDS((lp, d), BF16)],
        compiler_params=_cp(1))(h, h, gain, w, scale)


def _pool_bwd(h, gain, w, scale, diff, dh1, name):
    lp, d = h.shape
    tm = ROW_TILE
    hb = POOL_HALO
    nblk = lp // tm
    ext = tm + hb

    def body(h_ref, g_ref, w_ref, s_ref, diff_ref, dh_ref, dhn_ref, dh0_ref, dw_ref, ds_ref, dg_ref):
        i = pl.program_id(0)
        g = g_ref[...]
        x = h_ref[...]
        r = _rstd(x)
        u = x * r
        dh = dh_ref[...]
        dhn = jnp.where(i < nblk - 1, dhn_ref[...], 0.0)
        dyp = jnp.concatenate([dh, dhn], axis=0) * s_ref[...]
        pos = i * tm + lax.broadcasted_iota(jnp.int32, (ext, 1), 0)
        dn_parts, dw_parts, ds_parts = [], [], []
        for gi, win in enumerate(POOL_WINDOWS):
            c0 = gi * POOL_C
            wg = w_ref[gi]
            dyp_g = dyp[:, c0:c0 + POOL_C].astype(BF16)
            dd = _dot(dyp_g, wg, NT)
            dfg = diff_ref[:, c0:c0 + POOL_C]
            dw_parts.append(_dot(dfg, dyp_g[:tm], TN))
            ds_parts.append(jnp.sum(dh[:, c0:c0 + POOL_C] * _dot(dfg, wg), axis=0, keepdims=True))
            count = jnp.minimum(pos + 1, win).astype(F32)
            cur = dd / count
            sh = 1
            while sh < win:
                cur = cur + pltpu.roll(cur, ext - sh, 0)
                sh *= 2
            dn_parts.append(cur[:tm] - dd[:tm])
        dn = jnp.concatenate(dn_parts, axis=1)
        du = dn * g
        dh0_ref[...] = dh + r * (du - u * jnp.mean(du * u, axis=-1, keepdims=True))
        dgp = jnp.sum(dn * u, axis=0, keepdims=True)
        dsp = jnp.concatenate(ds_parts, axis=1)

        @pl.when(i == 0)
        def _():
            for gi in range(len(POOL_WINDOWS)):
                dw_ref[gi] = dw_parts[gi]
            ds_ref[...] = dsp
            dg_ref[...] = dgp

        @pl.when(i > 0)
        def _():
            for gi in range(len(POOL_WINDOWS)):
                dw_ref[gi] += dw_parts[gi]
            ds_ref[...] += dsp
            dg_ref[...] += dgp

    row = pl.BlockSpec((tm, d), lambda i: (i, 0))
    nxt = pl.BlockSpec((hb, d), lambda i: (jnp.minimum((i + 1) * (tm // hb), lp // hb - 1), 0))
    vec = pl.BlockSpec((1, d), lambda i: (0, 0))
    wsp = pl.BlockSpec(w.shape, lambda i: (0, 0, 0))
    return pl.pallas_call(
        body, name=name, grid=(nblk,),
        in_specs=[row, vec, wsp, vec, row, row, nxt],
        out_specs=[row, wsp, vec, vec],
        out_shape=[SDS((lp, d), F32), SDS(w.shape, F32), SDS((1, d), F32), SDS((1, d), F32)],
        compiler_params=_cp(1))(h, gain, w, scale, diff, dh1, dh1)


def _ffn_specs(tm, c, lp):
    blk = pl.BlockSpec((2, 1, tm, c), lambda g, i: (0, g, i, 0))
    halo = pl.BlockSpec((2, 1, CONV_HALO, c), lambda g, i: (0, g, jnp.maximum(i * (tm // CONV_HALO) - 1, 0), 0))
    cw = pl.BlockSpec((2, 1, 3, c), lambda g, i: (0, g, 0, 0))
    cb = pl.BlockSpec((2, 1, 1, c), lambda g, i: (0, g, 0, 0))
    return blk, halo, cw, cb


def _ffn_up_act(n2, w_up4, cw4, cb4, name, ride=None):
    lp, d = n2.shape
    _, ng, _, c = w_up4.shape
    tm = _row_tile(lp, FFN_ROWS_MAX, 16)
    hb = CONV_HALO

    def body(a_ref, w_ref, cw_ref, cb_ref, up_ref, act_ref, tail_ref):
        @pl.when(pl.program_id(1) == 0)
        def _():
            tail_ref[...] = jnp.zeros_like(tail_ref)

        a = a_ref[...]
        u = []
        for half in range(2):
            x = _dot(a, w_ref[half, 0])
            up_ref[half, 0] = x
            rows = jnp.concatenate([tail_ref[half], x], axis=0)
            u.append(cb_ref[half, 0] + cw_ref[half, 0, 0:1, :] * pltpu.roll(rows, 2, 0)[hb:]
                     + cw_ref[half, 0, 1:2, :] * pltpu.roll(rows, 1, 0)[hb:] + cw_ref[half, 0, 2:3, :] * x)
            tail_ref[half] = x[tm - hb:]
        gate, val = u
        sig = 1.0 / (1.0 + jnp.exp(-gate))
        act_ref[0] = (gate * sig * val).astype(BF16)

    blk, _, cw, cb = _ffn_specs(tm, c, lp)
    (up4, act), rode = _call_with_ride(
        body, ride, name=name, grid=(ng, lp // tm),
        in_specs=[pl.BlockSpec((tm, d), lambda g, i: (i, 0)), pl.BlockSpec((2, 1, d, c), lambda g, i: (0, g, 0, 0)), cw, cb],
        out_specs=[blk, pl.BlockSpec((1, tm, c), lambda g, i: (g, i, 0))],
        out_shape=[SDS((2, ng, lp, c), F32), SDS((ng, lp, c), BF16)],
        scratch_shapes=[pltpu.VMEM((2, hb, c), F32)], args=[n2, w_up4, cw4, cb4])
    return up4, act, rode


def _ffn_act_bwd(up4, cw4, cb4, dh, w_down4, name, ride=None):
    _, ng, lp, c = up4.shape
    d = dh.shape[1]
    tm = ROW_TILE
    hb = CONV_HALO
    nblk = lp // tm
    ext = tm + hb

    def body(up_ref, prev_ref, next_ref, cw_ref, cb_ref, dh_ref, dhn_ref, wd_ref, dup_ref, dcw_ref, dcb_ref):
        i = pl.program_id(1)
        first = i == 0
        last = i == nblk - 1
        dh_rows = jnp.concatenate([dh_ref[...], jnp.where(last, 0.0, dhn_ref[...])], axis=0)
        da = _dot(dh_rows.astype(BF16), wd_ref[0], NT)
        u, taps = [], []
        for half in range(2):
            rows = jnp.concatenate([jnp.where(first, 0.0, prev_ref[half, 0]), up_ref[half, 0],
                                    jnp.where(last, 0.0, next_ref[half, 0])], axis=0)
            x, xm1, xm2 = rows[hb:], pltpu.roll(rows, 1, 0)[hb:], pltpu.roll(rows, 2, 0)[hb:]
            u.append(cb_ref[half, 0] + cw_ref[half, 0, 0:1, :] * xm2 + cw_ref[half, 0, 1:2, :] * xm1
                     + cw_ref[half, 0, 2:3, :] * x)
            taps.append((xm2, xm1, x))
        gate, val = u
        sig = 1.0 / (1.0 + jnp.exp(-gate))
        dus = (da * val * (sig * (1.0 + gate * (1.0 - sig))), da * (gate * sig))
        sums = []
        for half in range(2):
            du = dus[half]
            dup_ref[half, 0] = (cw_ref[half, 0, 2:3, :] * du[:tm] + cw_ref[half, 0, 1:2, :] * pltpu.roll(du, ext - 1, 0)[:tm]
                                + cw_ref[half, 0, 0:1, :] * pltpu.roll(du, ext - 2, 0)[:tm]).astype(BF16)
            sums.append([jnp.sum(du[:tm] * t[:tm], axis=0, keepdims=True) for t in taps[half]]
                        + [jnp.sum(du[:tm], axis=0, keepdims=True)])

        @pl.when(first)
        def _():
            for half in range(2):
                for k in range(3):
                    dcw_ref[half, 0, k:k + 1, :] = sums[half][k]
                dcb_ref[half, 0] = sums[half][3]

        @pl.when(i > 0)
        def _():
            for half in range(2):
                for k in range(3):
                    dcw_ref[half, 0, k:k + 1, :] += sums[half][k]
                dcb_ref[half, 0] += sums[half][3]

    blk, prev, cw, cb = _ffn_specs(tm, c, lp)

    def next_rows(g, i):
        return jnp.minimum((i + 1) * (tm // hb), lp // hb - 1)

    (dup4, dcw4, dcb4), rode = _call_with_ride(
        body, ride, name=name, grid=(ng, nblk),
        in_specs=[blk, prev, pl.BlockSpec((2, 1, hb, c), lambda g, i: (0, g, next_rows(g, i), 0)), cw, cb,
                  pl.BlockSpec((tm, d), lambda g, i: (i, 0)),
                  pl.BlockSpec((hb, d), lambda g, i: (next_rows(g, i), 0)),
                  pl.BlockSpec((1, c, d), lambda g, i: (g, 0, 0))],
        out_specs=[blk, cw, cb],
        out_shape=[SDS(up4.shape, BF16), SDS(cw4.shape, F32), SDS(cb4.shape, F32)],
        args=[up4, up4, up4, cw4, cb4, dh, dh, w_down4])
    return dup4, dcw4, dcb4, rode


def _mm_tile(rows):
    return _row_tile(rows, MM_ROWS_MAX)


def _mm_group(a, b, dims, out_dtype, name):
    m, k = a.shape
    ng = b.shape[0]
    n = b.shape[2] if dims == NN else b.shape[1]
    tm = _mm_tile(m)

    def body(a_ref, b_ref, o_ref):
        o_ref[0] = _dot(a_ref[...].astype(BF16), b_ref[0], dims).astype(out_dtype)

    return pl.pallas_call(
        body, name=name, grid=(ng, m // tm),
        in_specs=[pl.BlockSpec((tm, k), lambda g, i: (i, 0)),
                  pl.BlockSpec((1,) + b.shape[1:], lambda g, i: (g, 0, 0))],
        out_specs=pl.BlockSpec((1, tm, n), lambda g, i: (g, i, 0)),
        out_shape=SDS((ng, m, n), out_dtype), compiler_params=_cp(2))(a, b)


def _mm_reduce(a, b, dims, res, name, ride=None):
    ng, m, k = a.shape
    n = b.shape[2] if dims == NN else b.shape[1]
    tm = _mm_tile(m)
    has_res = res is not None

    def body(a_ref, b_ref, *refs):
        o_ref, acc_ref = refs[-2], refs[-1]
        g = pl.program_id(1)
        p = _dot(a_ref[0].astype(BF16), b_ref[0], dims)

        @pl.when(g == 0)
        def _():
            acc_ref[...] = p + refs[0][...] if has_res else p

        @pl.when(g > 0)
        def _():
            acc_ref[...] += p

        @pl.when(g == ng - 1)
        def _():
            o_ref[...] = acc_ref[...]

    row = pl.BlockSpec((tm, n), lambda i, g: (i, 0))
    (out,), rode = _call_with_ride(
        body, ride, name=name, grid=(m // tm, ng),
        in_specs=[pl.BlockSpec((1, tm, k), lambda i, g: (g, i, 0)),
                  pl.BlockSpec((1,) + b.shape[1:], lambda i, g: (g, 0, 0))] + ([row] if has_res else []),
        out_specs=[row], out_shape=[SDS((m, n), F32)], scratch_shapes=[pltpu.VMEM((tm, n), F32)],
        args=[a, b] + ([res] if has_res else []))
    return out if ride is None else (out, rode)


def _mm_tn(a, b, name, ride=None):
    ga, m, ka = a.shape
    gb, _, n = b.shape
    ng = max(ga, gb)
    tk = _mm_tile(m)
    nk = m // tk

    def body(a_ref, b_ref, o_ref, acc_ref):
        s = pl.program_id(1)
        p = _dot(a_ref[0].astype(BF16), b_ref[0].astype(BF16), TN)

        @pl.when(s == 0)
        def _():
            acc_ref[...] = p

        @pl.when(s > 0)
        def _():
            acc_ref[...] += p

        @pl.when(s == nk - 1)
        def _():
            o_ref[0] = acc_ref[...].astype(BF16)

    (out,), rode = _call_with_ride(
        body, ride, name=name, grid=(ng, nk),
        in_specs=[pl.BlockSpec((1, tk, ka), (lambda g, s: (g, s, 0)) if ga > 1 else (lambda g, s: (0, s, 0))),
                  pl.BlockSpec((1, tk, n), (lambda g, s: (g, s, 0)) if gb > 1 else (lambda g, s: (0, s, 0)))],
        out_specs=[pl.BlockSpec((1, ka, n), lambda g, s: (g, 0, 0))], out_shape=[SDS((ng, ka, n), BF16)],
        scratch_shapes=[pltpu.VMEM((ka, n), F32)], args=[a, b])
    return out if ride is None else (out, rode)


def _pair_tri(kind, sign):
    r = jnp.arange(2 * ATT_BLK)[:, None]
    c = jnp.arange(2 * ATT_BLK)[None, :]
    same = (r < ATT_BLK) == (c < ATT_BLK)
    rel = {"from": r >= c, "before": r < c}[kind]
    return ((same & rel) * sign).astype(BF16)


def _scan_dot(x, tri):
    hi = x.astype(BF16)
    lo = (x - hi.astype(F32)).astype(BF16)
    return _dot(hi, tri) + _dot(lo, tri)


def _split_heads(blk, lane_a):
    zero = jnp.zeros_like(blk)
    return jnp.concatenate([jnp.where(lane_a, blk, zero), jnp.where(lane_a, zero, blk)], axis=0)


def _softplus(z):
    return jnp.maximum(z, 0.0) + jnp.log(1.0 + jnp.exp2(jnp.abs(z) * (-LOG2_E)))


def _visible(qi, j, r0):
    t = qi * ATT_Q + r0 + lax.broadcasted_iota(jnp.int32, (ATT_Q - r0, 2 * ATT_BLK), 0)
    s = j * ATT_BLK + (lax.broadcasted_iota(jnp.int32, (ATT_Q - r0, 2 * ATT_BLK), 1) & (ATT_BLK - 1))
    return s < t


def _add_rows(x, r0, y):
    return x + y if r0 == 0 else jnp.concatenate([x[:r0], x[r0:] + y], axis=0)


def _diag_rows(n):
    return n * ATT_BLK


def _halves(x):
    return x[:, :ATT_BLK], x[:, ATT_BLK:]


def _rowsum(x):
    return jnp.sum(x, axis=1, keepdims=True)


def _still_visible(ca, cb):
    return jnp.minimum(jnp.min(ca), jnp.min(cb)) < UNDERFLOW_AT


def _attn_specs(lp):
    bk = ATT_BLK
    qblk = pl.BlockSpec((ATT_Q, bk), lambda p, i: (i, p))
    kblk = pl.BlockSpec((1, lp, bk), lambda p, i: (p // 2, 0, p % 2))
    vblk = pl.BlockSpec((1, lp, bk), lambda p, i: (HEAD_PAIRS // 2 + p // 2, 0, p % 2))
    tri = pl.BlockSpec((2 * bk, 2 * bk), lambda p, i: (0, 0))
    return qblk, kblk, vblk, tri


def _attn_fwd(q, kv, name, ride=None):
    lp, d = q.shape
    bk = ATT_BLK

    def body(q_ref, k_ref, v_ref, tri_ref, o_ref):
        qi = pl.program_id(1)
        qs = q_ref[...] * (HEAD_DIM ** -0.5)
        lane_a = lax.broadcasted_iota(jnp.int32, (1, bk), 1) < HEAD_DIM
        tri = tri_ref[...]

        def trip(js, carry, masked):
            oacc, ca, cb = carry
            r0s = [_diag_rows(len(js) - 1 - n) if masked else 0 for n in range(len(js))]
            rows = [pl.ds(pl.multiple_of(j * bk, bk), bk) for j in js]
            zs = [_dot(qs[r0:], _split_heads(k_ref[0, r, :], lane_a), NT) for r0, r in zip(r0s, rows)]
            ms = [_softplus(z) for z in zs]
            if masked:
                ms = [jnp.where(_visible(qi, j, r0), m, 0.0) for j, r0, m in zip(js, r0s, ms)]
            ws = [_scan_dot(m, tri) for m in ms]
            for j, r0, r, z, m, w in zip(js, r0s, rows, zs, ms, ws):
                exa, exb = _halves(z + w)
                a = jnp.concatenate([jnp.exp(exa - ca[r0:]), jnp.exp(exb - cb[r0:])], axis=1)
                if masked:
                    a = jnp.where(_visible(qi, j, r0), a, 0.0)
                oacc = _add_rows(oacc, r0, _dot(a.astype(BF16), _split_heads(v_ref[0, r, :], lane_a)))
                ma, mb = _halves(m)
                ca, cb = _add_rows(ca, r0, _rowsum(ma)), _add_rows(cb, r0, _rowsum(mb))
            return oacc, ca, cb

        carry = (jnp.zeros((ATT_Q, bk), F32), jnp.zeros((ATT_Q, 1), F32), jnp.zeros((ATT_Q, 1), F32))
        top = (qi + 1) * ATT_UNROLL - 1
        carry = trip([top - u for u in range(ATT_UNROLL)], carry, True)
        _, oacc, _, _ = lax.while_loop(
            lambda st: (st[0] < qi) & _still_visible(st[2], st[3]),
            lambda st: (st[0] + 1, *trip([top - (st[0] + 1) * ATT_UNROLL - u for u in range(ATT_UNROLL)], st[1:], False)),
            (jnp.int32(0), *carry))
        o_ref[...] = oacc.astype(BF16)

    qblk, kblk, vblk, tri = _attn_specs(lp)
    (o,), rode = _call_with_ride(
        body, ride, name=name, grid=(HEAD_PAIRS, lp // ATT_Q), in_specs=[qblk, kblk, vblk, tri],
        out_specs=[qblk], out_shape=[SDS((lp, d), BF16)], args=[q, kv, kv, _pair_tri("from", -1)])
    return o, rode


def _attn_bwd(q, kv, do, name, ride=None):
    lp, d = q.shape
    bk = ATT_BLK
    scale = HEAD_DIM ** -0.5

    def body(q_ref, k_ref, v_ref, do_ref, tri_ref, dq_ref, dk_ref, dv_ref):
        qi = pl.program_id(1)

        @pl.when(qi == 0)
        def _():
            dk_ref[...] = jnp.zeros_like(dk_ref)
            dv_ref[...] = jnp.zeros_like(dv_ref)

        qs = q_ref[...] * scale
        do_blk = do_ref[...]
        lane_a = lax.broadcasted_iota(jnp.int32, (1, bk), 1) < HEAD_DIM
        tri = tri_ref[...]

        def sums(js, carry, masked):
            ca, cb = carry
            for n, j in enumerate(js):
                r0 = _diag_rows(n) if masked else 0
                m = _softplus(_dot(qs[r0:], _split_heads(k_ref[0, pl.ds(pl.multiple_of(j * bk, bk), bk), :], lane_a), NT))
                if masked:
                    m = jnp.where(_visible(qi, j, r0), m, 0.0)
                ma, mb = _halves(m)
                ca, cb = _add_rows(ca, r0, _rowsum(ma)), _add_rows(cb, r0, _rowsum(mb))
            return ca, cb

        def trip(js, carry, masked):
            dq, pa, pb, ea, eb = carry
            r0s = [_diag_rows(n) if masked else 0 for n in range(len(js))]
            rows = [pl.ds(pl.multiple_of(j * bk, bk), bk) for j in js]
            kks = [_split_heads(k_ref[0, r, :], lane_a) for r in rows]
            zs = [_dot(qs[r0:], kk, NT) for r0, kk in zip(r0s, kks)]
            das = [_dot(do_blk[r0:], _split_heads(v_ref[0, r, :], lane_a), NT) for r0, r in zip(r0s, rows)]
            ms = [_softplus(z) for z in zs]
            if masked:
                ms = [jnp.where(_visible(qi, j, r0), m, 0.0) for j, r0, m in zip(js, r0s, ms)]
            xs = [_scan_dot(m, tri) for m in ms]
            es, a_bf = [], []
            for j, r0, z, m, x, da in zip(js, r0s, zs, ms, xs, das):
                xa, xb = _halves(z + x)
                a = jnp.concatenate([jnp.exp(xa + pa[r0:]), jnp.exp(xb + pb[r0:])], axis=1)
                if masked:
                    a = jnp.where(_visible(qi, j, r0), a, 0.0)
                a_bf.append(a.astype(BF16))
                es.append(a * da)
                ma, mb = _halves(m)
                pa, pb = _add_rows(pa, r0, _rowsum(ma)), _add_rows(pb, r0, _rowsum(mb))
            ss = [_dot(e.astype(BF16), tri) for e in es]
            for j, r0, r, kk, z, m, e, s, ab in zip(js, r0s, rows, kks, zs, ms, es, ss, a_bf):
                sa, sb = _halves(s)
                e_before = jnp.concatenate([sa + ea[r0:], sb + eb[r0:]], axis=1)
                dz = e - jnp.exp(z - m) * (e + e_before)
                if masked:
                    dz = jnp.where(_visible(qi, j, r0), dz, 0.0)
                dzb = dz.astype(BF16)
                dq = _add_rows(dq, r0, _dot(dzb, kk))
                rk = _dot(dzb, qs[r0:], TN)
                rv = _dot(ab, do_blk[r0:], TN)
                dk_ref[0, r, :] += jnp.where(lane_a, rk[:bk], rk[bk:])
                dv_ref[0, r, :] += jnp.where(lane_a, rv[:bk], rv[bk:])
                e_a, e_b = _halves(e)
                ea, eb = _add_rows(ea, r0, _rowsum(e_a)), _add_rows(eb, r0, _rowsum(e_b))
            return dq, pa, pb, ea, eb

        zcol = jnp.zeros((ATT_Q, 1), F32)
        diag = [qi * ATT_UNROLL + u for u in range(ATT_UNROLL)]
        n_old, ta, tb = lax.while_loop(
            lambda st: (st[0] < qi) & _still_visible(st[1], st[2]),
            lambda st: (st[0] + 1, *sums([(qi - 1 - st[0]) * ATT_UNROLL + u for u in range(ATT_UNROLL)], st[1:], False)),
            (jnp.int32(0), *sums(diag, (zcol, zcol), True)))
        carry = lax.fori_loop(
            0, n_old, lambda g, c: trip([(qi - n_old + g) * ATT_UNROLL + u for u in range(ATT_UNROLL)], c, False),
            (jnp.zeros((ATT_Q, bk), F32), -ta, -tb, zcol, zcol))
        carry = trip(diag, carry, True)
        dq_ref[...] = (carry[0] * scale).astype(BF16)

    qblk, kblk, vblk, tri = _attn_specs(lp)
    (dq, dk, dv), rode = _call_with_ride(
        body, ride, name=name, grid=(HEAD_PAIRS, lp // ATT_Q), in_specs=[qblk, kblk, vblk, qblk, tri],
        out_specs=[qblk, kblk, kblk],
        out_shape=[SDS((lp, d), BF16), SDS((HEAD_PAIRS // 2, lp, 2 * bk), F32), SDS((HEAD_PAIRS // 2, lp, 2 * bk), F32)],
        args=[q, kv, kv, do, _pair_tri("before", 1)])
    return dq, dk, dv, rode


def _mesh_pos():
    return lax.axis_index("x"), lax.axis_index("y"), lax.axis_index("c")


def _flip(pos, r):
    x, y, c = pos
    return (1 - x if r & 4 else x, 1 - y if r & 2 else y, 1 - c if r & 1 else c)


def _dev_index(pos):
    return 4 * pos[0] + 2 * pos[1] + pos[2]


class _Ride(NamedTuple):
    kind: str
    arrays: list


def _ride_arrays(ride):
    return [] if ride is None else ride.arrays


def _ride_args(ride):
    if ride is None:
        return [], [], [], []
    n = len(ride.arrays)
    hbm = pl.BlockSpec(memory_space=pl.ANY)
    shapes = [SDS(x.shape if ride.kind == "scatter" else (N_DEV,) + x.shape, x.dtype) for x in ride.arrays]
    sems = [pltpu.SemaphoreType.DMA((7 * n,)), pltpu.SemaphoreType.DMA((7 * n,)), pltpu.SemaphoreType.DMA((n,))]
    return [hbm] * n, [hbm] * n, shapes, sems


def _riding(body, n_in, n_out, ride, first, middle, last):
    if ride is None:
        return body
    n = len(ride.arrays)

    def wrapped(*refs):
        ins, srcs = refs[:n_in], refs[n_in:n_in + n]
        outs, dsts = refs[n_in + n:n_in + n + n_out], refs[n_in + n + n_out:n_in + 2 * n + n_out]
        scratch, (send_sems, recv_sems, local_sems) = refs[n_in + 2 * n + n_out:-3], refs[-3:]
        me = _mesh_pos()
        mi = _dev_index(me)

        def copy(a, k, src, dst, to):
            return pltpu.make_async_remote_copy(
                src_ref=src, dst_ref=dst, send_sem=send_sems.at[7 * a + k], recv_sem=recv_sems.at[7 * a + k],
                device_id=to, device_id_type=pl.DeviceIdType.MESH)

        local, sends, lands, arrived, passed = [], [], [], [], []
        for a in range(n):
            if ride.kind == "gather_by_chip":
                sibling, others = _flip(me, 1), [_flip(me, 4), _flip(me, 2), _flip(me, 6)]
                local.append(pltpu.make_async_copy(srcs[a], dsts[a].at[mi], local_sems.at[a]))
                sends.append(copy(a, 0, srcs[a], dsts[a].at[mi], sibling))
                lands.append(copy(a, 0, dsts[a].at[_dev_index(sibling)], dsts[a].at[_dev_index(sibling)], me))
                for j, o in enumerate(others):
                    oi, si = _dev_index(o), _dev_index(_flip(o, 1))
                    sends.append(copy(a, 1 + j, srcs[a], dsts[a].at[mi], o))
                    arrived.append(copy(a, 1 + j, dsts[a].at[oi], dsts[a].at[oi], me))
                    passed.append(copy(a, 4 + j, dsts[a].at[oi], dsts[a].at[oi], sibling))
                    lands.append(copy(a, 4 + j, dsts[a].at[si], dsts[a].at[si], me))
                continue
            gather = ride.kind == "gather"
            local.append(pltpu.make_async_copy(srcs[a] if gather else srcs[a].at[mi], dsts[a].at[mi], local_sems.at[a]))
            for r in range(1, N_DEV):
                peer = _flip(me, r)
                pi = _dev_index(peer)
                sends.append(copy(a, r - 1, srcs[a] if gather else srcs[a].at[pi], dsts[a].at[mi], peer))
                lands.append(copy(a, r - 1, dsts[a].at[pi], dsts[a].at[pi], peer))

        @pl.when(first())
        def _():
            for cp in local + sends:
                cp.start()

        if passed:
            @pl.when(middle())
            def _():
                for got, on in zip(arrived, passed):
                    got.wait_recv()
                    on.start()

        body(*ins, *outs, *scratch)

        @pl.when(last())
        def _():
            for cp in lands:
                cp.wait_recv()
            for cp in sends + passed:
                cp.wait_send()
            for cp in local:
                cp.wait()

    return wrapped


def _call_with_ride(body, ride, *, name, grid, in_specs, out_specs, out_shape, args, scratch_shapes=()):
    ride_in, ride_out, ride_shape, ride_sems = _ride_args(ride)
    axes = range(len(grid))
    assert ride is None or ride.kind != "gather_by_chip" or grid[0] >= 4, grid

    def at(step):
        return lambda: functools.reduce(lambda p, k: p & (pl.program_id(k) == step[k]), axes, True)

    ends = [(0,) * len(grid), (3 * grid[0] // 4,) + (0,) * (len(grid) - 1), tuple(g - 1 for g in grid)]
    out = pl.pallas_call(
        _riding(body, len(in_specs), len(out_specs), ride, *map(at, ends)), name=name, grid=grid,
        in_specs=list(in_specs) + ride_in, out_specs=list(out_specs) + ride_out,
        out_shape=list(out_shape) + ride_shape, scratch_shapes=list(scratch_shapes) + ride_sems,
        compiler_params=_cp(len(grid)))(*args, *_ride_arrays(ride))
    return out[:len(out_specs)], out[len(out_specs):]


def _all_gather(xs, name):
    n = len(xs)

    def body(*refs):
        x_refs, out_refs = refs[:n], refs[n:2 * n]
        send_sems, recv_sems, local_sems = refs[2 * n:]
        me = _mesh_pos()
        sibling = _flip(me, 1)
        others = [_flip(me, 4), _flip(me, 2), _flip(me, 6)]

        def copy(a, k, block, to, own=False):
            slab = out_refs[a].at[_dev_index(block)]
            return pltpu.make_async_remote_copy(
                src_ref=x_refs[a] if own else slab, dst_ref=slab,
                send_sem=send_sems.at[7 * a + k], recv_sem=recv_sems.at[7 * a + k],
                device_id=to, device_id_type=pl.DeviceIdType.MESH)

        mine = [pltpu.make_async_copy(x_refs[a], out_refs[a].at[_dev_index(me)], local_sems.at[a]) for a in range(n)]
        first = []
        for a in range(n):
            mine[a].start()
            first += [copy(a, 0, me, sibling, own=True)] + [copy(a, 1 + j, me, o, own=True) for j, o in enumerate(others)]
        for cp in first:
            cp.start()
        passed = []
        for a in range(n):
            for j, o in enumerate(others):
                copy(a, 1 + j, o, me).wait_recv()
                passed.append(copy(a, 4 + j, o, sibling))
                passed[-1].start()
        for a in range(n):
            copy(a, 0, sibling, me).wait_recv()
            for j, o in enumerate(others):
                copy(a, 4 + j, _flip(o, 1), me).wait_recv()
        for cp in first + passed:
            cp.wait_send()
        for cp in mine:
            cp.wait()

    hbm = pl.BlockSpec(memory_space=pl.ANY)
    return pl.pallas_call(
        body, name=name, out_shape=[SDS((N_DEV,) + x.shape, x.dtype) for x in xs],
        in_specs=[hbm] * n, out_specs=[hbm] * n,
        scratch_shapes=[pltpu.SemaphoreType.DMA((7 * n,)), pltpu.SemaphoreType.DMA((7 * n,)), pltpu.SemaphoreType.DMA((n,))],
    )(*xs)


def _sum_slabs(a, name, ride=None):
    n, rows, cols = a.shape
    tr = rows if a.size * a.dtype.itemsize <= SUM_WHOLE_BYTES else _row_tile(rows, SUM_ROWS_MAX, 16)

    def body(a_ref, o_ref):
        acc = a_ref[0].astype(F32)
        for k in range(1, n):
            acc = acc + a_ref[k].astype(F32)
        o_ref[...] = acc

    (out,), rode = _call_with_ride(
        body, ride, name=name, grid=(rows // tr,),
        in_specs=[pl.BlockSpec((n, tr, cols), lambda i: (0, i, 0))],
        out_specs=[pl.BlockSpec((tr, cols), lambda i: (i, 0))], out_shape=[SDS((rows, cols), F32)], args=[a])
    return out if ride is None else (out, rode)


def _adamw(w, g, m, v, name):
    rows, cols = w.shape
    tr = _row_tile(rows, 352)

    def body(w_ref, g_ref, m_ref, v_ref, d_ref, mo_ref, vo_ref):
        g_ = g_ref[...]
        m_ = ADAM_B1 * m_ref[...] + (1.0 - ADAM_B1) * g_
        v_ = ADAM_B2 * v_ref[...] + (1.0 - ADAM_B2) * (g_ * g_)
        m_hat = m_ / (1.0 - ADAM_B1 ** ADAM_STEP)
        v_hat = v_ / (1.0 - ADAM_B2 ** ADAM_STEP)
        d_ref[...] = -ADAM_LR * (m_hat / (jnp.sqrt(v_hat) + ADAM_EPS) + ADAM_WD * w_ref[...])
        mo_ref[...] = m_
        vo_ref[...] = v_

    blk = pl.BlockSpec((tr, cols), lambda i: (i, 0))
    return pl.pallas_call(
        body, name=name, grid=(rows // tr,),
        in_specs=[blk] * 4, out_specs=[blk] * 3, out_shape=[SDS((rows, cols), F32)] * 3,
        compiler_params=_cp(1))(w, g, m, v)


def _ffn_bwd(h, gain, w_up, cw4, cb4, w_down4, saved, dh, tag, ride_wup=None, scatter_own=False):
    n2, up4, act = saved
    d_w_down = _mm_tn(act, dh[None], f"ffn_dwdown_{tag}")
    ride_gate = _Ride("scatter", [d_w_down.reshape(N_DEV, -1, d_w_down.shape[-1])]) if scatter_own else None
    dup4, dcw4, dcb4, rode = _ffn_act_bwd(up4, cw4, cb4, dh, w_down4, f"ffn_dgate_{tag}", ride_gate)
    if scatter_own:
        (d_w_down,) = rode
    dup = dup4.reshape((8,) + dup4.shape[2:])
    d_w_up, rode_wup = _mm_tn(n2[None], dup, f"ffn_dwup_{tag}", ride_wup), []
    if ride_wup is not None:
        d_w_up, rode_wup = d_w_up
    dn2 = _mm_reduce(dup, w_up, NT, None, f"ffn_dnorm_{tag}", _Ride("scatter", [d_w_up]) if scatter_own else None)
    if scatter_own:
        dn2, (d_w_up,) = dn2
    dh_in, dgain = _rms_bwd(h, gain, [dn2], dh, f"ffn_dh_{tag}")
    return dh_in, dgain, d_w_up, d_w_down, dcw4, dcb4, rode_wup


def kernel(x, meta_tokens, mix_norm, ffn_norm, pool_w, pool_scale, kv_norm, w_kv, w_q, w_o, ffn_w_up, ffn_conv_w, ffn_conv_b, ffn_w_down, final_norm, loss_target, m_meta_tokens, m_mix_norm, m_ffn_norm, m_pool_w, m_pool_scale, m_kv_norm, m_w_kv, m_w_q, m_w_o, m_ffn_w_up, m_ffn_conv_w, m_ffn_conv_b, m_ffn_w_down, m_final_norm, v_meta_tokens, v_mix_norm, v_ffn_norm, v_pool_w, v_pool_scale, v_kv_norm, v_w_kv, v_w_q, v_w_o, v_ffn_w_up, v_ffn_conv_w, v_ffn_conv_b, v_ffn_w_down, v_final_norm):
    seq, d = x.shape[1], x.shape[2]
    n_tok = N_META + seq
    lp = -(-n_tok // ROW_TILE) * ROW_TILE
    fc = ffn_w_up.shape[2]
    me = _dev_index(_mesh_pos())

    def rows_of(parts):
        rows = [p.size // d for p in parts]
        return [sum(rows[:k]) for k in range(len(parts) + 1)]

    def bf16_rows(parts):
        return jnp.concatenate([p.reshape(-1, d) for p in parts], axis=0).astype(BF16)

    g_pw, wup0 = _all_gather([bf16_rows([pool_w]), ffn_w_up[0].astype(BF16)], "gather_matrices")
    pw = g_pw.reshape(N_DEV, 4, POOL_C // N_DEV, POOL_C).transpose(1, 0, 2, 3).reshape(4, POOL_C, POOL_C)
    early_parts, late_parts = [ffn_w_down[0], w_kv], [w_o, ffn_w_down[1]]
    early_off, late_off = rows_of(early_parts), rows_of(late_parts)

    small_parts = [meta_tokens, pool_scale, ffn_conv_w]
    small_rows = [p.size // 128 for p in small_parts]
    small_pad = -sum(small_rows) % 8
    local_small = jnp.concatenate([p.reshape(-1, 128) for p in small_parts] + [jnp.zeros((small_pad, 128), F32)], axis=0)
    (gs,) = _all_gather([local_small], "gather_vectors")
    r0, r1, r2 = small_rows[0], small_rows[0] + small_rows[1], sum(small_rows)
    meta_full = gs[:, :r0].transpose(1, 0, 2).reshape(N_META, d)
    pscale = gs[:, r0:r1].reshape(1, d)
    cw = gs[:, r1:r2].reshape(N_DEV, 2, 3, fc)
    cw4_l = [cw[:, l].reshape(2, 4, 3, fc) for l in range(2)]
    cb4_l = [ffn_conv_b[l].reshape(2, 4, 1, fc) for l in range(2)]

    h0 = jnp.concatenate([meta_full, x[0], jnp.zeros((lp - n_tok, d), F32)], axis=0)
    h1, diff = _pool_fwd(h0, mix_norm[0:1], pw, pscale, "pool_fwd")
    (n2_0,) = _rms_fwd(h1, ffn_norm[0:1], "ffn_norm_0")
    up4_0, act0, (g_early,) = _ffn_up_act(n2_0, wup0.reshape(2, 4, d, fc), cw4_l[0], cb4_l[0], "ffn_up_0",
                                          _Ride("gather_by_chip", [bf16_rows(early_parts)]))
    wdn0 = g_early[:, early_off[0]:early_off[1]].reshape(4, fc, d)
    wkv = g_early[:, early_off[1]:early_off[2]].reshape(N_DEV, d, 2 * d // N_DEV)
    h2, (wq,) = _mm_reduce(act0, wdn0, NN, h1, "ffn_down_0", _Ride("gather", [w_q[0].astype(BF16)]))
    wq = wq.reshape(1, d, d)
    gains_b = jnp.stack([kv_norm, mix_norm[1]], axis=0)
    kvn, n3 = _rms_fwd(h2, gains_b, "attn_norms")
    kv = _mm_group(kvn, wkv, NN, BF16, "kv_proj")
    q = _mm_group(n3, wq, NN, BF16, "q_proj")[0]
    o, (g_late, wup1) = _attn_fwd(
        q, kv, "attn_fwd", _Ride("gather_by_chip", [bf16_rows(late_parts), ffn_w_up[1].astype(BF16)]))
    wo = g_late[:, late_off[0]:late_off[1]].reshape(1, d, d)
    wdn1 = g_late[:, late_off[1]:late_off[2]].reshape(4, fc, d)
    h3 = _mm_reduce(o[None], wo, NN, h2, "o_proj")
    (n2_1,) = _rms_fwd(h3, ffn_norm[1:2], "ffn_norm_1")
    up4_1, act1, _ = _ffn_up_act(n2_1, wup1.reshape(2, 4, d, fc), cw4_l[1], cb4_l[1], "ffn_up_1")
    h4 = _mm_reduce(act1, wdn1, NN, h3, "ffn_down_1")
    target = jnp.pad(loss_target[0], ((N_META, lp - n_tok), (0, 0)))
    dh4, loss_blk, dg_final = _loss_bwd(h4, final_norm[None], target, seq, "loss")
    loss = lax.psum(loss_blk[0, 0], MESH_AXES)

    dh3, dg_ffn1, d_wup1, d_wdn1, dcw4_1, dcb4_1, _ = _ffn_bwd(
        h3, ffn_norm[1:2], wup1, cw4_l[1], cb4_l[1], wdn1, (n2_1, up4_1, act1), dh4, "1")
    d_o = _mm_group(dh3, wo, NT, BF16, "o_proj_dx")[0]
    d_wo = _mm_tn(o[None], dh3[None], "o_proj_dw")
    ride_late = _Ride("scatter", [jnp.concatenate([d_wo.reshape(N_DEV, -1, d), d_wdn1.reshape(N_DEV, -1, d)], axis=1), d_wup1])
    dq, dk, dv, (p_late, p_up1) = _attn_bwd(q, kv, d_o, "attn_bwd", ride_late)
    dn3 = _mm_group(dq, wq, NT, F32, "q_proj_dx")[0]
    d_wq = _mm_tn(n3[None], dq[None], "q_proj_dw")
    dkv = jnp.concatenate([dk, dv], axis=0).astype(BF16)
    dkvn = _mm_reduce(dkv, wkv, NT, None, "kv_proj_dx")
    d_wkv = _mm_tn(kvn[None], dkv, "kv_proj_dw")
    dh2, dg_b = _rms_bwd(h2, gains_b, [dkvn, dn3], dh3, "attn_norms_bwd")
    ride_proj = _Ride("scatter", [jnp.concatenate([d_wkv.reshape(N_DEV, -1, d), d_wq.reshape(N_DEV, -1, d)], axis=1)])

    dh1, dg_ffn0, p_up0, p_dn0, dcw4_0, dcb4_0, (p_proj,) = _ffn_bwd(
        h1, ffn_norm[0:1], wup0, cw4_l[0], cb4_l[0], wdn0, (n2_0, up4_0, act0), dh2, "0", ride_proj, scatter_own=True)
    dh0, d_pw, d_pscale, dg_mix0 = _pool_bwd(h0, mix_norm[0:1], pw, pscale, diff, dh1, "pool_bwd")
    grad_x = dh0[N_META:n_tok][None]
    d_pw8 = d_pw.reshape(4, N_DEV, POOL_C // N_DEV, POOL_C).transpose(1, 0, 2, 3).reshape(N_DEV, -1, d).astype(BF16)
    s_up0, (p_pw,) = _sum_slabs(p_up0, "sum_up0", _Ride("scatter", [d_pw8]))
    s_late, s_up1, s_proj, s_dn0, s_pw = [_sum_slabs(p, "sum_" + n) for p, n in (
        (p_late, "late"), (p_up1, "up1"), (p_proj, "proj"), (p_dn0, "down0"), (p_pw, "pool"))]
    n_kv, n_o = w_kv.size // d, w_o.size // d

    rep_parts = [jnp.concatenate([dg_mix0, dg_b[1:2]], axis=0), jnp.concatenate([dg_ffn0, dg_ffn1], axis=0),
                 dg_b[0:1], dg_final, jnp.stack([dcb4_0.reshape(-1), dcb4_1.reshape(-1)], axis=0)]
    rep_shapes = [mix_norm.shape, ffn_norm.shape, kv_norm.shape, final_norm.shape, ffn_conv_b.shape]
    rep_rows = [p.size // 128 for p in rep_parts]
    d_meta8 = dh0[:N_META].reshape(N_META, N_DEV, d // N_DEV).transpose(1, 0, 2).reshape(N_DEV, -1, 128)
    d_cw8 = jnp.stack([dcw4_0.reshape(N_DEV, 3, fc), dcw4_1.reshape(N_DEV, 3, fc)], axis=1).reshape(N_DEV, -1, 128)
    shard_parts = jnp.concatenate([d_meta8, d_pscale.reshape(N_DEV, 1, 128), d_cw8], axis=1)
    n_rep = sum(rep_rows)
    partial_small = jnp.concatenate([p.reshape(-1, 128) for p in rep_parts] + [shard_parts.reshape(-1, 128)], axis=0)
    g_small = _sum_slabs(_all_gather([partial_small], "gather_vector_grads")[0], "sum_vectors")
    g_rep = [g_small[sum(rep_rows[:k]):sum(rep_rows[:k + 1])].reshape(s) for k, s in enumerate(rep_shapes)]
    g_shard = lax.dynamic_index_in_dim(g_small[n_rep:].reshape(N_DEV, -1, 128), me, 0, keepdims=False)
    g_meta = g_shard[:r0].reshape(meta_tokens.shape)
    g_pscale = g_shard[r0:r1].reshape(pool_scale.shape)
    g_cw = g_shard[r1:r2].reshape(ffn_conv_w.shape)

    grads = {
        "meta_tokens": g_meta, "mix_norm": g_rep[0], "ffn_norm": g_rep[1],
        "pool_w": s_pw.reshape(pool_w.shape), "pool_scale": g_pscale, "kv_norm": g_rep[2],
        "w_kv": s_proj[:n_kv].reshape(w_kv.shape), "w_q": s_proj[n_kv:].reshape(w_q.shape),
        "w_o": s_late[:n_o].reshape(w_o.shape),
        "ffn_w_up": jnp.stack([s_up0, s_up1], axis=0), "ffn_conv_w": g_cw, "ffn_conv_b": g_rep[4],
        "ffn_w_down": jnp.stack([s_dn0, s_late[n_o:]], axis=0), "final_norm": g_rep[3],
    }
    names = list(grads)
    weights = dict(zip(names, [meta_tokens, mix_norm, ffn_norm, pool_w, pool_scale, kv_norm, w_kv, w_q, w_o,
                               ffn_w_up, ffn_conv_w, ffn_conv_b, ffn_w_down, final_norm]))
    mom1 = dict(zip(names, [m_meta_tokens, m_mix_norm, m_ffn_norm, m_pool_w, m_pool_scale, m_kv_norm, m_w_kv, m_w_q,
                            m_w_o, m_ffn_w_up, m_ffn_conv_w, m_ffn_conv_b, m_ffn_w_down, m_final_norm]))
    mom2 = dict(zip(names, [v_meta_tokens, v_mix_norm, v_ffn_norm, v_pool_w, v_pool_scale, v_kv_norm, v_w_kv, v_w_q,
                            v_w_o, v_ffn_w_up, v_ffn_conv_w, v_ffn_conv_b, v_ffn_w_down, v_final_norm]))

    delta, new_m, new_v = {}, {}, {}
    for n in names:
        shape = weights[n].shape
        flat = (-1, shape[-1])
        dl, nm, nv = _adamw(weights[n].reshape(flat), grads[n].reshape(flat), mom1[n].reshape(flat),
                            mom2[n].reshape(flat), "adamw_" + n)
        delta[n], new_m[n], new_v[n] = dl.reshape(shape), nm.reshape(shape), nv.reshape(shape)
    return (loss, grad_x, *[grads[n] for n in names], *[delta[n] for n in names],
            *[new_m[n] for n in names], *[new_v[n] for n in names])
```

```python
import functools
from typing import NamedTuple

import jax
import jax.numpy as jnp
from jax import lax
from jax.experimental import pallas as pl
from jax.experimental.pallas import tpu as pltpu

F32 = jnp.float32
BF16 = jnp.bfloat16
SDS = jax.ShapeDtypeStruct

N_DEV = 8
N_META = 16
HEAD_DIM = 64
HEAD_PAIRS = 8
RMS_EPS = 1e-6
LOG2_E = 1.4426950408889634
POOL_WINDOWS = (2, 4, 8, 16)
POOL_C = 256
POOL_HALO = 16
CONV_HALO = 8
ROW_TILE = 384
MM_ROWS_MAX = 1408
FFN_ROWS_MAX = 704
SUM_ROWS_MAX = 256
SUM_WHOLE_BYTES = 4 << 20
ATT_BLK = 128
ATT_Q = ROW_TILE
ATT_UNROLL = ATT_Q // ATT_BLK
UNDERFLOW_AT = 104.0
VMEM_LIMIT = 56 * 1024 * 1024

ADAM_LR = 0.001
ADAM_B1 = 0.9
ADAM_B2 = 0.999
ADAM_EPS = 1e-08
ADAM_WD = 0.01
ADAM_STEP = 10

MESH_AXES = ("x", "y", "c")
NN = (((1,), (0,)), ((), ()))
NT = (((1,), (1,)), ((), ()))
TN = (((0,), (0,)), ((), ()))


def _cp(n_axes):
    return pltpu.CompilerParams(dimension_semantics=("arbitrary",) * n_axes, vmem_limit_bytes=VMEM_LIMIT)


def _dot(a, b, dims=NN):
    return lax.dot_general(a, b, dims, preferred_element_type=F32)


def _rstd(x):
    return lax.rsqrt(jnp.mean(x * x, axis=-1, keepdims=True) + RMS_EPS)


def _row_tile(rows, cap=512, mult=8):
    if rows <= cap:
        return rows
    best = mult
    for t in range(mult, cap + 1, mult):
        if rows % t == 0:
            best = t
    assert rows % best == 0
    return best


def _rms_fwd(h, gains, name):
    lp, d = h.shape
    k = gains.shape[0]
    tm = ROW_TILE

    def body(h_ref, g_ref, *o_refs):
        x = h_ref[...]
        u = x * _rstd(x)
        for j in range(k):
            o_refs[j][...] = (u * g_ref[j:j + 1, :]).astype(BF16)

    row = pl.BlockSpec((tm, d), lambda i: (i, 0))
    return pl.pallas_call(
        body, name=name, grid=(lp // tm,),
        in_specs=[row, pl.BlockSpec((k, d), lambda i: (0, 0))],
        out_specs=[row] * k, out_shape=[SDS((lp, d), BF16)] * k,
        compiler_params=_cp(1))(h, gains)


def _rms_bwd(h, gains, dns, dh_in, name):
    lp, d = h.shape
    k = gains.shape[0]
    tm = ROW_TILE

    def body(h_ref, g_ref, *refs):
        dn_refs, dh_ref, dho_ref, dg_ref = refs[:k], refs[k], refs[k + 1], refs[k + 2]
        i = pl.program_id(0)
        x = h_ref[...]
        r = _rstd(x)
        u = x * r
        du = jnp.zeros_like(x)
        rows = []
        for j in range(k):
            dn = dn_refs[j][...]
            du = du + dn * g_ref[j:j + 1, :]
            rows.append(jnp.sum(dn * u, axis=0, keepdims=True))
        dx = r * (du - u * jnp.mean(du * u, axis=-1, keepdims=True))
        dho_ref[...] = dh_ref[...] + dx

        @pl.when(i == 0)
        def _():
            for j in range(k):
                dg_ref[j:j + 1, :] = rows[j]

        @pl.when(i > 0)
        def _():
            for j in range(k):
                dg_ref[j:j + 1, :] += rows[j]

    row = pl.BlockSpec((tm, d), lambda i: (i, 0))
    vec = pl.BlockSpec((k, d), lambda i: (0, 0))
    return pl.pallas_call(
        body, name=name, grid=(lp // tm,),
        in_specs=[row, vec] + [row] * k + [row],
        out_specs=[row, vec], out_shape=[SDS((lp, d), F32), SDS((k, d), F32)],
        compiler_params=_cp(1))(h, gains, *dns, dh_in)


def _loss_bwd(h, gain, target, n_real, name):
    lp, d = h.shape
    tm = ROW_TILE

    def body(h_ref, g_ref, t_ref, dh_ref, loss_ref, dg_ref):
        i = pl.program_id(0)
        x = h_ref[...]
        g = g_ref[...]
        r = _rstd(x)
        u = x * r
        row = i * tm + lax.broadcasted_iota(jnp.int32, (tm, 1), 0)
        valid = (row >= N_META) & (row < N_META + n_real)
        e = jnp.where(valid, u * g - t_ref[...], 0.0)
        part = 0.5 * jnp.sum(jnp.sum(e * e, axis=-1, keepdims=True), axis=0, keepdims=True) * (1.0 / d)
        dy = e * (1.0 / d)
        du = dy * g
        dh_ref[...] = r * (du - u * jnp.mean(du * u, axis=-1, keepdims=True))
        dgp = jnp.sum(dy * u, axis=0, keepdims=True)

        @pl.when(i == 0)
        def _():
            loss_ref[...] = jnp.broadcast_to(part, (8, 128))
            dg_ref[...] = dgp

        @pl.when(i > 0)
        def _():
            loss_ref[...] += jnp.broadcast_to(part, (8, 128))
            dg_ref[...] += dgp

    row = pl.BlockSpec((tm, d), lambda i: (i, 0))
    vec = pl.BlockSpec((1, d), lambda i: (0, 0))
    return pl.pallas_call(
        body, name=name, grid=(lp // tm,),
        in_specs=[row, vec, row],
        out_specs=[row, pl.BlockSpec((8, 128), lambda i: (0, 0)), vec],
        out_shape=[SDS((lp, d), F32), SDS((8, 128), F32), SDS((1, d), F32)],
        compiler_params=_cp(1))(h, gain, target)


def _pool_fwd(h, gain, w, scale, name):
    lp, d = h.shape
    tm = ROW_TILE
    hb = POOL_HALO

    def body(h_ref, halo_ref, g_ref, w_ref, s_ref, h1_ref, diff_ref):
        i = pl.program_id(0)
        g = g_ref[...]
        x = h_ref[...]
        n = x * _rstd(x) * g
        xh = halo_ref[...]
        nh = jnp.where(i > 0, xh * _rstd(xh) * g, 0.0)
        cur = jnp.concatenate([nh, n], axis=0)
        pos = i * tm + lax.broadcasted_iota(jnp.int32, (tm, 1), 0)
        for gi, win in enumerate(POOL_WINDOWS):
            if gi > 0:
                cur = cur[:, POOL_C:]
            cur = cur + pltpu.roll(cur, win // 2, 0)
            c0 = gi * POOL_C
            count = jnp.minimum(pos + 1, win).astype(F32)
            diff = cur[hb:, :POOL_C] / count - n[:, c0:c0 + POOL_C]
            diff = diff.astype(BF16)
            y = _dot(diff, w_ref[gi])
            h1_ref[:, c0:c0 + POOL_C] = x[:, c0:c0 + POOL_C] + y * s_ref[:, c0:c0 + POOL_C]
            diff_ref[:, c0:c0 + POOL_C] = diff

    row = pl.BlockSpec((tm, d), lambda i: (i, 0))
    halo = pl.BlockSpec((hb, d), lambda i: (jnp.maximum(i * (tm // hb) - 1, 0), 0))
    vec = pl.BlockSpec((1, d), lambda i: (0, 0))
    return pl.pallas_call(
        body, name=name, grid=(lp // tm,),
        in_specs=[row, halo, vec, pl.BlockSpec(w.shape, lambda i: (0, 0, 0)), vec],
        out_specs=[row, row], out_shape=[SDS((lp, d), F32), SDS((lp, d), BF16)],
        compiler_params=_cp(1))(h, h, gain, w, scale)


def _pool_bwd(h, gain, w, scale, diff, dh1, name):
    lp, d = h.shape
    tm = ROW_TILE
    hb = POOL_HALO
    nblk = lp // tm
    ext = tm + hb

    def body(h_ref, g_ref, w_ref, s_ref, diff_ref, dh_ref, dhn_ref, dh0_ref, dw_ref, ds_ref, dg_ref):
        i = pl.program_id(0)
        g = g_ref[...]
        x = h_ref[...]
        r = _rstd(x)
        u = x * r
        dh = dh_ref[...]
        dhn = jnp.where(i < nblk - 1, dhn_ref[...], 0.0)
        dyp = jnp.concatenate([dh, dhn], axis=0) * s_ref[...]
        pos = i * tm + lax.broadcasted_iota(jnp.int32, (ext, 1), 0)
        dn_parts, dw_parts, ds_parts = [], [], []
        for gi, win in enumerate(POOL_WINDOWS):
            c0 = gi * POOL_C
            wg = w_ref[gi]
            dyp_g = dyp[:, c0:c0 + POOL_C].astype(BF16)
            dd = _dot(dyp_g, wg, NT)
            dfg = diff_ref[:, c0:c0 + POOL_C]
            dw_parts.append(_dot(dfg, dyp_g[:tm], TN))
            ds_parts.append(jnp.sum(dh[:, c0:c0 + POOL_C] * _dot(dfg, wg), axis=0, keepdims=True))
            count = jnp.minimum(pos + 1, win).astype(F32)
            cur = dd / count
            sh = 1
            while sh < win:
                cur = cur + pltpu.roll(cur, ext - sh, 0)
                sh *= 2
            dn_parts.append(cur[:tm] - dd[:tm])
        dn = jnp.concatenate(dn_parts, axis=1)
        du = dn * g
        dh0_ref[...] = dh + r * (du - u * jnp.mean(du * u, axis=-1, keepdims=True))
        dgp = jnp.sum(dn * u, axis=0, keepdims=True)
        dsp = jnp.concatenate(ds_parts, axis=1)

        @pl.when(i == 0)
        def _():
            for gi in range(len(POOL_WINDOWS)):
                dw_ref[gi] = dw_parts[gi]
            ds_ref[...] = dsp
            dg_ref[...] = dgp

        @pl.when(i > 0)
        def _():
            for gi in range(len(POOL_WINDOWS)):
                dw_ref[gi] += dw_parts[gi]
            ds_ref[...] += dsp
            dg_ref[...] += dgp

    row = pl.BlockSpec((tm, d), lambda i: (i, 0))
    nxt = pl.BlockSpec((hb, d), lambda i: (jnp.minimum((i + 1) * (tm // hb), lp // hb - 1), 0))
    vec = pl.BlockSpec((1, d), lambda i: (0, 0))
    wsp = pl.BlockSpec(w.shape, lambda i: (0, 0, 0))
    return pl.pallas_call(
        body, name=name, grid=(nblk,),
        in_specs=[row, vec, wsp, vec, row, row, nxt],
        out_specs=[row, wsp, vec, vec],
        out_shape=[SDS((lp, d), F32), SDS(w.shape, F32), SDS((1, d), F32), SDS((1, d), F32)],
        compiler_params=_cp(1))(h, gain, w, scale, diff, dh1, dh1)


def _ffn_specs(tm, c, lp):
    blk = pl.BlockSpec((2, 1, tm, c), lambda g, i: (0, g, i, 0))
    halo = pl.BlockSpec((2, 1, CONV_HALO, c), lambda g, i: (0, g, jnp.maximum(i * (tm // CONV_HALO) - 1, 0), 0))
    cw = pl.BlockSpec((2, 1, 3, c), lambda g, i: (0, g, 0, 0))
    cb = pl.BlockSpec((2, 1, 1, c), lambda g, i: (0, g, 0, 0))
    return blk, halo, cw, cb


def _ffn_up_act(n2, w_up4, cw4, cb4, name, ride=None):
    lp, d = n2.shape
    _, ng, _, c = w_up4.shape
    tm = _row_tile(lp, FFN_ROWS_MAX, 16)
    hb = CONV_HALO

    def body(a_ref, w_ref, cw_ref, cb_ref, up_ref, act_ref, tail_ref):
        @pl.when(pl.program_id(1) == 0)
        def _():
            tail_ref[...] = jnp.zeros_like(tail_ref)

        a = a_ref[...]
        u = []
        for half in range(2):
            x = _dot(a, w_ref[half, 0])
            up_ref[half, 0] = x
            rows = jnp.concatenate([tail_ref[half], x], axis=0)
            u.append(cb_ref[half, 0] + cw_ref[half, 0, 0:1, :] * pltpu.roll(rows, 2, 0)[hb:]
                     + cw_ref[half, 0, 1:2, :] * pltpu.roll(rows, 1, 0)[hb:] + cw_ref[half, 0, 2:3, :] * x)
            tail_ref[half] = x[tm - hb:]
        gate, val = u
        sig = 1.0 / (1.0 + jnp.exp(-gate))
        act_ref[0] = (gate * sig * val).astype(BF16)

    blk, _, cw, cb = _ffn_specs(tm, c, lp)
    (up4, act), rode = _call_with_ride(
        body, ride, name=name, grid=(ng, lp // tm),
        in_specs=[pl.BlockSpec((tm, d), lambda g, i: (i, 0)), pl.BlockSpec((2, 1, d, c), lambda g, i: (0, g, 0, 0)), cw, cb],
        out_specs=[blk, pl.BlockSpec((1, tm, c), lambda g, i: (g, i, 0))],
        out_shape=[SDS((2, ng, lp, c), F32), SDS((ng, lp, c), BF16)],
        scratch_shapes=[pltpu.VMEM((2, hb, c), F32)], args=[n2, w_up4, cw4, cb4])
    return up4, act, rode


def _ffn_act_bwd(up4, cw4, cb4, dh, w_down4, name, ride=None):
    _, ng, lp, c = up4.shape
    d = dh.shape[1]
    tm = ROW_TILE
    hb = CONV_HALO
    nblk = lp // tm
    ext = tm + hb

    def body(up_ref, prev_ref, next_ref, cw_ref, cb_ref, dh_ref, dhn_ref, wd_ref, dup_ref, dcw_ref, dcb_ref):
        i = pl.program_id(1)
        first = i == 0
        last = i == nblk - 1
        dh_rows = jnp.concatenate([dh_ref[...], jnp.where(last, 0.0, dhn_ref[...])], axis=0)
        da = _dot(dh_rows.astype(BF16), wd_ref[0], NT)
        u, taps = [], []
        for half in range(2):
            rows = jnp.concatenate([jnp.where(first, 0.0, prev_ref[half, 0]), up_ref[half, 0],
                                    jnp.where(last, 0.0, next_ref[half, 0])], axis=0)
            x, xm1, xm2 = rows[hb:], pltpu.roll(rows, 1, 0)[hb:], pltpu.roll(rows, 2, 0)[hb:]
            u.append(cb_ref[half, 0] + cw_ref[half, 0, 0:1, :] * xm2 + cw_ref[half, 0, 1:2, :] * xm1
                     + cw_ref[half, 0, 2:3, :] * x)
            taps.append((xm2, xm1, x))
        gate, val = u
        sig = 1.0 / (1.0 + jnp.exp(-gate))
        dus = (da * val * (sig * (1.0 + gate * (1.0 - sig))), da * (gate * sig))
        sums = []
        for half in range(2):
            du = dus[half]
            dup_ref[half, 0] = (cw_ref[half, 0, 2:3, :] * du[:tm] + cw_ref[half, 0, 1:2, :] * pltpu.roll(du, ext - 1, 0)[:tm]
                                + cw_ref[half, 0, 0:1, :] * pltpu.roll(du, ext - 2, 0)[:tm]).astype(BF16)
            sums.append([jnp.sum(du[:tm] * t[:tm], axis=0, keepdims=True) for t in taps[half]]
                        + [jnp.sum(du[:tm], axis=0, keepdims=True)])

        @pl.when(first)
        def _():
            for half in range(2):
                for k in range(3):
                    dcw_ref[half, 0, k:k + 1, :] = sums[half][k]
                dcb_ref[half, 0] = sums[half][3]

        @pl.when(i > 0)
        def _():
            for half in range(2):
                for k in range(3):
                    dcw_ref[half, 0, k:k + 1, :] += sums[half][k]
                dcb_ref[half, 0] += sums[half][3]

    blk, prev, cw, cb = _ffn_specs(tm, c, lp)

    def next_rows(g, i):
        return jnp.minimum((i + 1) * (tm // hb), lp // hb - 1)

    (dup4, dcw4, dcb4), rode = _call_with_ride(
        body, ride, name=name, grid=(ng, nblk),
        in_specs=[blk, prev, pl.BlockSpec((2, 1, hb, c), lambda g, i: (0, g, next_rows(g, i), 0)), cw, cb,
                  pl.BlockSpec((tm, d), lambda g, i: (i, 0)),
                  pl.BlockSpec((hb, d), lambda g, i: (next_rows(g, i), 0)),
                  pl.BlockSpec((1, c, d), lambda g, i: (g, 0, 0))],
        out_specs=[blk, cw, cb],
        out_shape=[SDS(up4.shape, BF16), SDS(cw4.shape, F32), SDS(cb4.shape, F32)],
        args=[up4, up4, up4, cw4, cb4, dh, dh, w_down4])
    return dup4, dcw4, dcb4, rode


def _mm_tile(rows):
    return _row_tile(rows, MM_ROWS_MAX)


def _mm_group(a, b, dims, out_dtype, name):
    m, k = a.shape
    ng = b.shape[0]
    n = b.shape[2] if dims == NN else b.shape[1]
    tm = _mm_tile(m)

    def body(a_ref, b_ref, o_ref):
        o_ref[0] = _dot(a_ref[...].astype(BF16), b_ref[0], dims).astype(out_dtype)

    return pl.pallas_call(
        body, name=name, grid=(ng, m // tm),
        in_specs=[pl.BlockSpec((tm, k), lambda g, i: (i, 0)),
                  pl.BlockSpec((1,) + b.shape[1:], lambda g, i: (g, 0, 0))],
        out_specs=pl.BlockSpec((1, tm, n), lambda g, i: (g, i, 0)),
        out_shape=SDS((ng, m, n), out_dtype), compiler_params=_cp(2))(a, b)


def _mm_reduce(a, b, dims, res, name, ride=None):
    ng, m, k = a.shape
    n = b.shape[2] if dims == NN else b.shape[1]
    tm = _mm_tile(m)
    has_res = res is not None

    def body(a_ref, b_ref, *refs):
        o_ref, acc_ref = refs[-2], refs[-1]
        g = pl.program_id(1)
        p = _dot(a_ref[0].astype(BF16), b_ref[0], dims)

        @pl.when(g == 0)
        def _():
            acc_ref[...] = p + refs[0][...] if has_res else p

        @pl.when(g > 0)
        def _():
            acc_ref[...] += p

        @pl.when(g == ng - 1)
        def _():
            o_ref[...] = acc_ref[...]

    row = pl.BlockSpec((tm, n), lambda i, g: (i, 0))
    (out,), rode = _call_with_ride(
        body, ride, name=name, grid=(m // tm, ng),
        in_specs=[pl.BlockSpec((1, tm, k), lambda i, g: (g, i, 0)),
                  pl.BlockSpec((1,) + b.shape[1:], lambda i, g: (g, 0, 0))] + ([row] if has_res else []),
        out_specs=[row], out_shape=[SDS((m, n), F32)], scratch_shapes=[pltpu.VMEM((tm, n), F32)],
        args=[a, b] + ([res] if has_res else []))
    return out if ride is None else (out, rode)


def _mm_tn(a, b, name, ride=None):
    ga, m, ka = a.shape
    gb, _, n = b.shape
    ng = max(ga, gb)
    tk = _mm_tile(m)
    nk = m // tk

    def body(a_ref, b_ref, o_ref, acc_ref):
        s = pl.program_id(1)
        p = _dot(a_ref[0].astype(BF16), b_ref[0].astype(BF16), TN)

        @pl.when(s == 0)
        def _():
            acc_ref[...] = p

        @pl.when(s > 0)
        def _():
            acc_ref[...] += p

        @pl.when(s == nk - 1)
        def _():
            o_ref[0] = acc_ref[...].astype(BF16)

    (out,), rode = _call_with_ride(
        body, ride, name=name, grid=(ng, nk),
        in_specs=[pl.BlockSpec((1, tk, ka), (lambda g, s: (g, s, 0)) if ga > 1 else (lambda g, s: (0, s, 0))),
                  pl.BlockSpec((1, tk, n), (lambda g, s: (g, s, 0)) if gb > 1 else (lambda g, s: (0, s, 0)))],
        out_specs=[pl.BlockSpec((1, ka, n), lambda g, s: (g, 0, 0))], out_shape=[SDS((ng, ka, n), BF16)],
        scratch_shapes=[pltpu.VMEM((ka, n), F32)], args=[a, b])
    return out if ride is None else (out, rode)


def _pair_tri(kind, sign):
    r = jnp.arange(2 * ATT_BLK)[:, None]
    c = jnp.arange(2 * ATT_BLK)[None, :]
    same = (r < ATT_BLK) == (c < ATT_BLK)
    rel = {"from": r >= c, "before": r < c}[kind]
    return ((same & rel) * sign).astype(BF16)


def _scan_dot(x, tri):
    hi = x.astype(BF16)
    lo = (x - hi.astype(F32)).astype(BF16)
    return _dot(hi, tri) + _dot(lo, tri)


def _split_heads(blk, lane_a):
    zero = jnp.zeros_like(blk)
    return jnp.concatenate([jnp.where(lane_a, blk, zero), jnp.where(lane_a, zero, blk)], axis=0)


def _softplus(z):
    return jnp.maximum(z, 0.0) + jnp.log(1.0 + jnp.exp2(jnp.abs(z) * (-LOG2_E)))


def _visible(qi, j, r0):
    t = qi * ATT_Q + r0 + lax.broadcasted_iota(jnp.int32, (ATT_Q - r0, 2 * ATT_BLK), 0)
    s = j * ATT_BLK + (lax.broadcasted_iota(jnp.int32, (ATT_Q - r0, 2 * ATT_BLK), 1) & (ATT_BLK - 1))
    return s < t


def _add_rows(x, r0, y):
    return x + y if r0 == 0 else jnp.concatenate([x[:r0], x[r0:] + y], axis=0)


def _diag_rows(n):
    return n * ATT_BLK


def _halves(x):
    return x[:, :ATT_BLK], x[:, ATT_BLK:]


def _rowsum(x):
    return jnp.sum(x, axis=1, keepdims=True)


def _still_visible(ca, cb):
    return (jnp.minimum(jnp.min(ca), jnp.min(cb)) < UNDERFLOW_AT).astype(jnp.int32)


def _attn_specs(lp):
    bk = ATT_BLK
    qblk = pl.BlockSpec((ATT_Q, bk), lambda p, i: (i, p))
    kblk = pl.BlockSpec((1, lp, bk), lambda p, i: (p // 2, 0, p % 2))
    vblk = pl.BlockSpec((1, lp, bk), lambda p, i: (HEAD_PAIRS // 2 + p // 2, 0, p % 2))
    tri = pl.BlockSpec((2 * bk, 2 * bk), lambda p, i: (0, 0))
    return qblk, kblk, vblk, tri


def _attn_fwd(q, kv, name, ride=None):
    lp, d = q.shape
    bk = ATT_BLK

    def body(q_ref, k_ref, v_ref, tri_ref, o_ref):
        qi = pl.program_id(1)
        qs = q_ref[...] * (HEAD_DIM ** -0.5)
        lane_a = lax.broadcasted_iota(jnp.int32, (1, bk), 1) < HEAD_DIM
        tri = tri_ref[...]

        def trip(js, carry, masked):
            oacc, ca, cb = carry
            r0s = [_diag_rows(len(js) - 1 - n) if masked else 0 for n in range(len(js))]
            rows = [pl.ds(pl.multiple_of(j * bk, bk), bk) for j in js]
            zs = [_dot(qs[r0:], _split_heads(k_ref[0, r, :], lane_a), NT) for r0, r in zip(r0s, rows)]
            ms = [_softplus(z) for z in zs]
            seen = [_visible(qi, j, r0) if masked else None for j, r0 in zip(js, r0s)]
            if masked:
                ms = [jnp.where(v, m, 0.0) for v, m in zip(seen, ms)]
            ws = [_scan_dot(m, tri) for m in ms]
            for v, r0, r, z, m, w in zip(seen, r0s, rows, zs, ms, ws):
                exa, exb = _halves(z + w)
                a = jnp.concatenate([jnp.exp(exa - ca[r0:]), jnp.exp(exb - cb[r0:])], axis=1)
                if masked:
                    a = jnp.where(v, a, 0.0)
                oacc = _add_rows(oacc, r0, _dot(a.astype(BF16), _split_heads(v_ref[0, r, :], lane_a)))
                ma, mb = _halves(m)
                ca, cb = _add_rows(ca, r0, _rowsum(ma)), _add_rows(cb, r0, _rowsum(mb))
            return oacc, ca, cb

        carry = (jnp.zeros((ATT_Q, bk), F32), jnp.zeros((ATT_Q, 1), F32), jnp.zeros((ATT_Q, 1), F32))
        top = (qi + 1) * ATT_UNROLL - 1
        carry = trip([top - u for u in range(ATT_UNROLL)], carry, True)
        def older(st):
            g, _, *c = st
            c = trip([top - (g + 1) * ATT_UNROLL - u for u in range(ATT_UNROLL)], tuple(c), False)
            return (g + 1, _still_visible(c[1], c[2]), *c)

        _, _, oacc, _, _ = lax.while_loop(
            lambda st: (st[0] < qi) & (st[1] > 0), older, (jnp.int32(0), _still_visible(carry[1], carry[2]), *carry))
        o_ref[...] = oacc.astype(BF16)

    qblk, kblk, vblk, tri = _attn_specs(lp)
    (o,), rode = _call_with_ride(
        body, ride, name=name, grid=(HEAD_PAIRS, lp // ATT_Q), in_specs=[qblk, kblk, vblk, tri],
        out_specs=[qblk], out_shape=[SDS((lp, d), BF16)], args=[q, kv, kv, _pair_tri("from", -1)])
    return o, rode


def _attn_bwd(q, kv, do, name, ride=None):
    lp, d = q.shape
    bk = ATT_BLK
    scale = HEAD_DIM ** -0.5

    def body(q_ref, k_ref, v_ref, do_ref, tri_ref, dq_ref, dk_ref, dv_ref):
        qi = pl.program_id(1)

        @pl.when(qi == 0)
        def _():
            dk_ref[...] = jnp.zeros_like(dk_ref)
            dv_ref[...] = jnp.zeros_like(dv_ref)

        qs = q_ref[...] * scale
        do_blk = do_ref[...]
        lane_a = lax.broadcasted_iota(jnp.int32, (1, bk), 1) < HEAD_DIM
        tri = tri_ref[...]

        def sums(js, carry, masked):
            ca, cb = carry
            for n, j in enumerate(js):
                r0 = _diag_rows(n) if masked else 0
                m = _softplus(_dot(qs[r0:], _split_heads(k_ref[0, pl.ds(pl.multiple_of(j * bk, bk), bk), :], lane_a), NT))
                if masked:
                    m = jnp.where(_visible(qi, j, r0), m, 0.0)
                ma, mb = _halves(m)
                ca, cb = _add_rows(ca, r0, _rowsum(ma)), _add_rows(cb, r0, _rowsum(mb))
            return ca, cb

        def trip(js, carry, masked):
            dq, pa, pb, ea, eb = carry
            r0s = [_diag_rows(n) if masked else 0 for n in range(len(js))]
            rows = [pl.ds(pl.multiple_of(j * bk, bk), bk) for j in js]
            kks = [_split_heads(k_ref[0, r, :], lane_a) for r in rows]
            zs = [_dot(qs[r0:], kk, NT) for r0, kk in zip(r0s, kks)]
            das = [_dot(do_blk[r0:], _split_heads(v_ref[0, r, :], lane_a), NT) for r0, r in zip(r0s, rows)]
            ms = [_softplus(z) for z in zs]
            seen = [_visible(qi, j, r0) if masked else None for j, r0 in zip(js, r0s)]
            if masked:
                ms = [jnp.where(v, m, 0.0) for v, m in zip(seen, ms)]
            xs = [_scan_dot(m, tri) for m in ms]
            es, a_bf = [], []
            for v, r0, z, m, x, da in zip(seen, r0s, zs, ms, xs, das):
                xa, xb = _halves(z + x)
                a = jnp.concatenate([jnp.exp(xa + pa[r0:]), jnp.exp(xb + pb[r0:])], axis=1)
                if masked:
                    a = jnp.where(v, a, 0.0)
                a_bf.append(a.astype(BF16))
                es.append(a * da)
                ma, mb = _halves(m)
                pa, pb = _add_rows(pa, r0, _rowsum(ma)), _add_rows(pb, r0, _rowsum(mb))
            ss = [_dot(e.astype(BF16), tri) for e in es]
            for v, r0, r, kk, z, m, e, s, ab in zip(seen, r0s, rows, kks, zs, ms, es, ss, a_bf):
                sa, sb = _halves(s)
                e_before = jnp.concatenate([sa + ea[r0:], sb + eb[r0:]], axis=1)
                dz = e - jnp.exp(z - m) * (e + e_before)
                if masked:
                    dz = jnp.where(v, dz, 0.0)
                dzb = dz.astype(BF16)
                dq = _add_rows(dq, r0, _dot(dzb, kk))
                rk = _dot(dzb, qs[r0:], TN)
                rv = _dot(ab, do_blk[r0:], TN)
                dk_ref[0, r, :] += jnp.where(lane_a, rk[:bk], rk[bk:])
                dv_ref[0, r, :] += jnp.where(lane_a, rv[:bk], rv[bk:])
                e_a, e_b = _halves(e)
                ea, eb = _add_rows(ea, r0, _rowsum(e_a)), _add_rows(eb, r0, _rowsum(e_b))
            return dq, pa, pb, ea, eb

        zcol = jnp.zeros((ATT_Q, 1), F32)
        diag = [qi * ATT_UNROLL + u for u in range(ATT_UNROLL)]
        def older(st):
            g, _, *c = st
            c = sums([(qi - 1 - g) * ATT_UNROLL + u for u in range(ATT_UNROLL)], tuple(c), False)
            return (g + 1, _still_visible(*c), *c)

        seen = sums(diag, (zcol, zcol), True)
        n_old, _, ta, tb = lax.while_loop(
            lambda st: (st[0] < qi) & (st[1] > 0), older, (jnp.int32(0), _still_visible(*seen), *seen))
        carry = lax.fori_loop(
            0, n_old, lambda g, c: trip([(qi - n_old + g) * ATT_UNROLL + u for u in range(ATT_UNROLL)], c, False),
            (jnp.zeros((ATT_Q, bk), F32), -ta, -tb, zcol, zcol))
        carry = trip(diag, carry, True)
        dq_ref[...] = (carry[0] * scale).astype(BF16)

    qblk, kblk, vblk, tri = _attn_specs(lp)
    (dq, dk, dv), rode = _call_with_ride(
        body, ride, name=name, grid=(HEAD_PAIRS, lp // ATT_Q), in_specs=[qblk, kblk, vblk, qblk, tri],
        out_specs=[qblk, kblk, kblk],
        out_shape=[SDS((lp, d), BF16), SDS((HEAD_PAIRS // 2, lp, 2 * bk), F32), SDS((HEAD_PAIRS // 2, lp, 2 * bk), F32)],
        args=[q, kv, kv, do, _pair_tri("before", 1)])
    return dq, dk, dv, rode


def _mesh_pos():
    return lax.axis_index("x"), lax.axis_index("y"), lax.axis_index("c")


def _flip(pos, r):
    x, y, c = pos
    return (1 - x if r & 4 else x, 1 - y if r & 2 else y, 1 - c if r & 1 else c)


def _dev_index(pos):
    return 4 * pos[0] + 2 * pos[1] + pos[2]


class _Ride(NamedTuple):
    kind: str
    arrays: list


def _ride_arrays(ride):
    return [] if ride is None else ride.arrays


def _ride_args(ride):
    if ride is None:
        return [], [], [], []
    n = len(ride.arrays)
    hbm = pl.BlockSpec(memory_space=pl.ANY)
    shapes = [SDS(x.shape if ride.kind == "scatter" else (N_DEV,) + x.shape, x.dtype) for x in ride.arrays]
    sems = [pltpu.SemaphoreType.DMA((7 * n,)), pltpu.SemaphoreType.DMA((7 * n,)), pltpu.SemaphoreType.DMA((n,))]
    return [hbm] * n, [hbm] * n, shapes, sems


def _riding(body, n_in, n_out, ride, first, middle, last):
    if ride is None:
        return body
    n = len(ride.arrays)

    def wrapped(*refs):
        ins, srcs = refs[:n_in], refs[n_in:n_in + n]
        outs, dsts = refs[n_in + n:n_in + n + n_out], refs[n_in + n + n_out:n_in + 2 * n + n_out]
        scratch, (send_sems, recv_sems, local_sems) = refs[n_in + 2 * n + n_out:-3], refs[-3:]
        me = _mesh_pos()
        mi = _dev_index(me)

        def copy(a, k, src, dst, to):
            return pltpu.make_async_remote_copy(
                src_ref=src, dst_ref=dst, send_sem=send_sems.at[7 * a + k], recv_sem=recv_sems.at[7 * a + k],
                device_id=to, device_id_type=pl.DeviceIdType.MESH)

        local, sends, lands, arrived, passed = [], [], [], [], []
        for a in range(n):
            if ride.kind == "gather_by_chip":
                sibling, others = _flip(me, 1), [_flip(me, 4), _flip(me, 2), _flip(me, 6)]
                local.append(pltpu.make_async_copy(srcs[a], dsts[a].at[mi], local_sems.at[a]))
                sends.append(copy(a, 0, srcs[a], dsts[a].at[mi], sibling))
                lands.append(copy(a, 0, dsts[a].at[_dev_index(sibling)], dsts[a].at[_dev_index(sibling)], me))
                for j, o in enumerate(others):
                    oi, si = _dev_index(o), _dev_index(_flip(o, 1))
                    sends.append(copy(a, 1 + j, srcs[a], dsts[a].at[mi], o))
                    arrived.append(copy(a, 1 + j, dsts[a].at[oi], dsts[a].at[oi], me))
                    passed.append(copy(a, 4 + j, dsts[a].at[oi], dsts[a].at[oi], sibling))
                    lands.append(copy(a, 4 + j, dsts[a].at[si], dsts[a].at[si], me))
                continue
            gather = ride.kind == "gather"
            local.append(pltpu.make_async_copy(srcs[a] if gather else srcs[a].at[mi], dsts[a].at[mi], local_sems.at[a]))
            for r in range(1, N_DEV):
                peer = _flip(me, r)
                pi = _dev_index(peer)
                sends.append(copy(a, r - 1, srcs[a] if gather else srcs[a].at[pi], dsts[a].at[mi], peer))
                lands.append(copy(a, r - 1, dsts[a].at[pi], dsts[a].at[pi], peer))

        @pl.when(first())
        def _():
            for cp in local + sends:
                cp.start()

        if passed:
            @pl.when(middle())
            def _():
                for got, on in zip(arrived, passed):
                    got.wait_recv()
                    on.start()

        body(*ins, *outs, *scratch)

        @pl.when(last())
        def _():
            for cp in lands:
                cp.wait_recv()
            for cp in sends + passed:
                cp.wait_send()
            for cp in local:
                cp.wait()

    return wrapped


def _call_with_ride(body, ride, *, name, grid, in_specs, out_specs, out_shape, args, scratch_shapes=()):
    ride_in, ride_out, ride_shape, ride_sems = _ride_args(ride)
    axes = range(len(grid))
    assert ride is None or ride.kind != "gather_by_chip" or grid[0] >= 4, grid

    def at(step):
        return lambda: functools.reduce(lambda p, k: p & (pl.program_id(k) == step[k]), axes, True)

    ends = [(0,) * len(grid), (3 * grid[0] // 4,) + (0,) * (len(grid) - 1), tuple(g - 1 for g in grid)]
    out = pl.pallas_call(
        _riding(body, len(in_specs), len(out_specs), ride, *map(at, ends)), name=name, grid=grid,
        in_specs=list(in_specs) + ride_in, out_specs=list(out_specs) + ride_out,
        out_shape=list(out_shape) + ride_shape, scratch_shapes=list(scratch_shapes) + ride_sems,
        compiler_params=_cp(len(grid)))(*args, *_ride_arrays(ride))
    return out[:len(out_specs)], out[len(out_specs):]


def _all_gather(xs, name):
    n = len(xs)

    def body(*refs):
        x_refs, out_refs = refs[:n], refs[n:2 * n]
        send_sems, recv_sems, local_sems = refs[2 * n:]
        me = _mesh_pos()
        sibling = _flip(me, 1)
        others = [_flip(me, 4), _flip(me, 2), _flip(me, 6)]

        def copy(a, k, block, to, own=False):
            slab = out_refs[a].at[_dev_index(block)]
            return pltpu.make_async_remote_copy(
                src_ref=x_refs[a] if own else slab, dst_ref=slab,
                send_sem=send_sems.at[7 * a + k], recv_sem=recv_sems.at[7 * a + k],
                device_id=to, device_id_type=pl.DeviceIdType.MESH)

        mine = [pltpu.make_async_copy(x_refs[a], out_refs[a].at[_dev_index(me)], local_sems.at[a]) for a in range(n)]
        first = []
        for a in range(n):
            mine[a].start()
            first += [copy(a, 0, me, sibling, own=True)] + [copy(a, 1 + j, me, o, own=True) for j, o in enumerate(others)]
        for cp in first:
            cp.start()
        passed = []
        for a in range(n):
            for j, o in enumerate(others):
                copy(a, 1 + j, o, me).wait_recv()
                passed.append(copy(a, 4 + j, o, sibling))
                passed[-1].start()
        for a in range(n):
            copy(a, 0, sibling, me).wait_recv()
            for j, o in enumerate(others):
                copy(a, 4 + j, _flip(o, 1), me).wait_recv()
        for cp in first + passed:
            cp.wait_send()
        for cp in mine:
            cp.wait()

    hbm = pl.BlockSpec(memory_space=pl.ANY)
    return pl.pallas_call(
        body, name=name, out_shape=[SDS((N_DEV,) + x.shape, x.dtype) for x in xs],
        in_specs=[hbm] * n, out_specs=[hbm] * n,
        scratch_shapes=[pltpu.SemaphoreType.DMA((7 * n,)), pltpu.SemaphoreType.DMA((7 * n,)), pltpu.SemaphoreType.DMA((n,))],
    )(*xs)


def _sum_slabs(a, name, ride=None):
    n, rows, cols = a.shape
    tr = rows if a.size * a.dtype.itemsize <= SUM_WHOLE_BYTES else _row_tile(rows, SUM_ROWS_MAX, 16)

    def body(a_ref, o_ref):
        acc = a_ref[0].astype(F32)
        for k in range(1, n):
            acc = acc + a_ref[k].astype(F32)
        o_ref[...] = acc

    (out,), rode = _call_with_ride(
        body, ride, name=name, grid=(rows // tr,),
        in_specs=[pl.BlockSpec((n, tr, cols), lambda i: (0, i, 0))],
        out_specs=[pl.BlockSpec((tr, cols), lambda i: (i, 0))], out_shape=[SDS((rows, cols), F32)], args=[a])
    return out if ride is None else (out, rode)


def _adamw(w, g, m, v, name):
    rows, cols = w.shape
    tr = _row_tile(rows, 352)

    def body(w_ref, g_ref, m_ref, v_ref, d_ref, mo_ref, vo_ref):
        g_ = g_ref[...]
        m_ = ADAM_B1 * m_ref[...] + (1.0 - ADAM_B1) * g_
        v_ = ADAM_B2 * v_ref[...] + (1.0 - ADAM_B2) * (g_ * g_)
        m_hat = m_ / (1.0 - ADAM_B1 ** ADAM_STEP)
        v_hat = v_ / (1.0 - ADAM_B2 ** ADAM_STEP)
        d_ref[...] = -ADAM_LR * (m_hat / (jnp.sqrt(v_hat) + ADAM_EPS) + ADAM_WD * w_ref[...])
        mo_ref[...] = m_
        vo_ref[...] = v_

    blk = pl.BlockSpec((tr, cols), lambda i: (i, 0))
    return pl.pallas_call(
        body, name=name, grid=(rows // tr,),
        in_specs=[blk] * 4, out_specs=[blk] * 3, out_shape=[SDS((rows, cols), F32)] * 3,
        compiler_params=_cp(1))(w, g, m, v)


def _ffn_bwd(h, gain, w_up, cw4, cb4, w_down4, saved, dh, tag, ride_wup=None, scatter_own=False):
    n2, up4, act = saved
    d_w_down = _mm_tn(act, dh[None], f"ffn_dwdown_{tag}")
    ride_gate = _Ride("scatter", [d_w_down.reshape(N_DEV, -1, d_w_down.shape[-1])]) if scatter_own else None
    dup4, dcw4, dcb4, rode = _ffn_act_bwd(up4, cw4, cb4, dh, w_down4, f"ffn_dgate_{tag}", ride_gate)
    if scatter_own:
        (d_w_down,) = rode
    dup = dup4.reshape((8,) + dup4.shape[2:])
    d_w_up, rode_wup = _mm_tn(n2[None], dup, f"ffn_dwup_{tag}", ride_wup), []
    if ride_wup is not None:
        d_w_up, rode_wup = d_w_up
    dn2 = _mm_reduce(dup, w_up, NT, None, f"ffn_dnorm_{tag}", _Ride("scatter", [d_w_up]) if scatter_own else None)
    if scatter_own:
        dn2, (d_w_up,) = dn2
    dh_in, dgain = _rms_bwd(h, gain, [dn2], dh, f"ffn_dh_{tag}")
    return dh_in, dgain, d_w_up, d_w_down, dcw4, dcb4, rode_wup


def kernel(x, meta_tokens, mix_norm, ffn_norm, pool_w, pool_scale, kv_norm, w_kv, w_q, w_o, ffn_w_up, ffn_conv_w, ffn_conv_b, ffn_w_down, final_norm, loss_target, m_meta_tokens, m_mix_norm, m_ffn_norm, m_pool_w, m_pool_scale, m_kv_norm, m_w_kv, m_w_q, m_w_o, m_ffn_w_up, m_ffn_conv_w, m_ffn_conv_b, m_ffn_w_down, m_final_norm, v_meta_tokens, v_mix_norm, v_ffn_norm, v_pool_w, v_pool_scale, v_kv_norm, v_w_kv, v_w_q, v_w_o, v_ffn_w_up, v_ffn_conv_w, v_ffn_conv_b, v_ffn_w_down, v_final_norm):
    seq, d = x.shape[1], x.shape[2]
    n_tok = N_META + seq
    lp = -(-n_tok // ROW_TILE) * ROW_TILE
    fc = ffn_w_up.shape[2]
    me = _dev_index(_mesh_pos())

    def rows_of(parts):
        rows = [p.size // d for p in parts]
        return [sum(rows[:k]) for k in range(len(parts) + 1)]

    def bf16_rows(parts):
        return jnp.concatenate([p.reshape(-1, d) for p in parts], axis=0).astype(BF16)

    g_pw, wup0 = _all_gather([bf16_rows([pool_w]), ffn_w_up[0].astype(BF16)], "gather_matrices")
    pw = g_pw.reshape(N_DEV, 4, POOL_C // N_DEV, POOL_C).transpose(1, 0, 2, 3).reshape(4, POOL_C, POOL_C)
    early_parts, late_parts = [ffn_w_down[0], w_kv], [w_o, ffn_w_down[1]]
    early_off, late_off = rows_of(early_parts), rows_of(late_parts)

    small_parts = [meta_tokens, pool_scale, ffn_conv_w]
    small_rows = [p.size // 128 for p in small_parts]
    small_pad = -sum(small_rows) % 8
    local_small = jnp.concatenate([p.reshape(-1, 128) for p in small_parts] + [jnp.zeros((small_pad, 128), F32)], axis=0)
    (gs,) = _all_gather([local_small], "gather_vectors")
    r0, r1, r2 = small_rows[0], small_rows[0] + small_rows[1], sum(small_rows)
    meta_full = gs[:, :r0].transpose(1, 0, 2).reshape(N_META, d)
    pscale = gs[:, r0:r1].reshape(1, d)
    cw = gs[:, r1:r2].reshape(N_DEV, 2, 3, fc)
    cw4_l = [cw[:, l].reshape(2, 4, 3, fc) for l in range(2)]
    cb4_l = [ffn_conv_b[l].reshape(2, 4, 1, fc) for l in range(2)]

    h0 = jnp.concatenate([meta_full, x[0], jnp.zeros((lp - n_tok, d), F32)], axis=0)
    h1, diff = _pool_fwd(h0, mix_norm[0:1], pw, pscale, "pool_fwd")
    (n2_0,) = _rms_fwd(h1, ffn_norm[0:1], "ffn_norm_0")
    up4_0, act0, (g_early,) = _ffn_up_act(n2_0, wup0.reshape(2, 4, d, fc), cw4_l[0], cb4_l[0], "ffn_up_0",
                                          _Ride("gather_by_chip", [bf16_rows(early_parts)]))
    wdn0 = g_early[:, early_off[0]:early_off[1]].reshape(4, fc, d)
    wkv = g_early[:, early_off[1]:early_off[2]].reshape(N_DEV, d, 2 * d // N_DEV)
    h2, (wq,) = _mm_reduce(act0, wdn0, NN, h1, "ffn_down_0", _Ride("gather", [w_q[0].astype(BF16)]))
    wq = wq.reshape(1, d, d)
    gains_b = jnp.stack([kv_norm, mix_norm[1]], axis=0)
    kvn, n3 = _rms_fwd(h2, gains_b, "attn_norms")
    kv = _mm_group(kvn, wkv, NN, BF16, "kv_proj")
    q = _mm_group(n3, wq, NN, BF16, "q_proj")[0]
    o, (g_late, wup1) = _attn_fwd(
        q, kv, "attn_fwd", _Ride("gather_by_chip", [bf16_rows(late_parts), ffn_w_up[1].astype(BF16)]))
    wo = g_late[:, late_off[0]:late_off[1]].reshape(1, d, d)
    wdn1 = g_late[:, late_off[1]:late_off[2]].reshape(4, fc, d)
    h3 = _mm_reduce(o[None], wo, NN, h2, "o_proj")
    (n2_1,) = _rms_fwd(h3, ffn_norm[1:2], "ffn_norm_1")
    up4_1, act1, _ = _ffn_up_act(n2_1, wup1.reshape(2, 4, d, fc), cw4_l[1], cb4_l[1], "ffn_up_1")
    h4 = _mm_reduce(act1, wdn1, NN, h3, "ffn_down_1")
    target = jnp.pad(loss_target[0], ((N_META, lp - n_tok), (0, 0)))
    dh4, loss_blk, dg_final = _loss_bwd(h4, final_norm[None], target, seq, "loss")
    loss = lax.psum(loss_blk[0, 0], MESH_AXES)

    dh3, dg_ffn1, d_wup1, d_wdn1, dcw4_1, dcb4_1, _ = _ffn_bwd(
        h3, ffn_norm[1:2], wup1, cw4_l[1], cb4_l[1], wdn1, (n2_1, up4_1, act1), dh4, "1")
    d_o = _mm_group(dh3, wo, NT, BF16, "o_proj_dx")[0]
    d_wo = _mm_tn(o[None], dh3[None], "o_proj_dw")
    ride_late = _Ride("scatter", [jnp.concatenate([d_wo.reshape(N_DEV, -1, d), d_wdn1.reshape(N_DEV, -1, d)], axis=1), d_wup1])
    dq, dk, dv, (p_late, p_up1) = _attn_bwd(q, kv, d_o, "attn_bwd", ride_late)
    dn3 = _mm_group(dq, wq, NT, F32, "q_proj_dx")[0]
    d_wq = _mm_tn(n3[None], dq[None], "q_proj_dw")
    dkv = jnp.concatenate([dk, dv], axis=0).astype(BF16)
    dkvn = _mm_reduce(dkv, wkv, NT, None, "kv_proj_dx")
    d_wkv = _mm_tn(kvn[None], dkv, "kv_proj_dw")
    dh2, dg_b = _rms_bwd(h2, gains_b, [dkvn, dn3], dh3, "attn_norms_bwd")
    ride_proj = _Ride("scatter", [jnp.concatenate([d_wkv.reshape(N_DEV, -1, d), d_wq.reshape(N_DEV, -1, d)], axis=1)])

    dh1, dg_ffn0, p_up0, p_dn0, dcw4_0, dcb4_0, (p_proj,) = _ffn_bwd(
        h1, ffn_norm[0:1], wup0, cw4_l[0], cb4_l[0], wdn0, (n2_0, up4_0, act0), dh2, "0", ride_proj, scatter_own=True)
    dh0, d_pw, d_pscale, dg_mix0 = _pool_bwd(h0, mix_norm[0:1], pw, pscale, diff, dh1, "pool_bwd")
    grad_x = dh0[N_META:n_tok][None]
    d_pw8 = d_pw.reshape(4, N_DEV, POOL_C // N_DEV, POOL_C).transpose(1, 0, 2, 3).reshape(N_DEV, -1, d).astype(BF16)
    s_up0, (p_pw,) = _sum_slabs(p_up0, "sum_up0", _Ride("scatter", [d_pw8]))
    s_late, s_up1, s_proj, s_dn0, s_pw = [_sum_slabs(p, "sum_" + n) for p, n in (
        (p_late, "late"), (p_up1, "up1"), (p_proj, "proj"), (p_dn0, "down0"), (p_pw, "pool"))]
    n_kv, n_o = w_kv.size // d, w_o.size // d

    rep_parts = [jnp.concatenate([dg_mix0, dg_b[1:2]], axis=0), jnp.concatenate([dg_ffn0, dg_ffn1], axis=0),
                 dg_b[0:1], dg_final, jnp.stack([dcb4_0.reshape(-1), dcb4_1.reshape(-1)], axis=0)]
    rep_shapes = [mix_norm.shape, ffn_norm.shape, kv_norm.shape, final_norm.shape, ffn_conv_b.shape]
    rep_rows = [p.size // 128 for p in rep_parts]
    d_meta8 = dh0[:N_META].reshape(N_META, N_DEV, d // N_DEV).transpose(1, 0, 2).reshape(N_DEV, -1, 128)
    d_cw8 = jnp.stack([dcw4_0.reshape(N_DEV, 3, fc), dcw4_1.reshape(N_DEV, 3, fc)], axis=1).reshape(N_DEV, -1, 128)
    shard_parts = jnp.concatenate([d_meta8, d_pscale.reshape(N_DEV, 1, 128), d_cw8], axis=1)
    n_rep = sum(rep_rows)
    partial_small = jnp.concatenate([p.reshape(-1, 128) for p in rep_parts] + [shard_parts.reshape(-1, 128)], axis=0)
    g_small = _sum_slabs(_all_gather([partial_small], "gather_vector_grads")[0], "sum_vectors")
    g_rep = [g_small[sum(rep_rows[:k]):sum(rep_rows[:k + 1])].reshape(s) for k, s in enumerate(rep_shapes)]
    g_shard = lax.dynamic_index_in_dim(g_small[n_rep:].reshape(N_DEV, -1, 128), me, 0, keepdims=False)
    g_meta = g_shard[:r0].reshape(meta_tokens.shape)
    g_pscale = g_shard[r0:r1].reshape(pool_scale.shape)
    g_cw = g_shard[r1:r2].reshape(ffn_conv_w.shape)

    grads = {
        "meta_tokens": g_meta, "mix_norm": g_rep[0], "ffn_norm": g_rep[1],
        "pool_w": s_pw.reshape(pool_w.shape), "pool_scale": g_pscale, "kv_norm": g_rep[2],
        "w_kv": s_proj[:n_kv].reshape(w_kv.shape), "w_q": s_proj[n_kv:].reshape(w_q.shape),
        "w_o": s_late[:n_o].reshape(w_o.shape),
        "ffn_w_up": jnp.stack([s_up0, s_up1], axis=0), "ffn_conv_w": g_cw, "ffn_conv_b": g_rep[4],
        "ffn_w_down": jnp.stack([s_dn0, s_late[n_o:]], axis=0), "final_norm": g_rep[3],
    }
    names = list(grads)
    weights = dict(zip(names, [meta_tokens, mix_norm, ffn_norm, pool_w, pool_scale, kv_norm, w_kv, w_q, w_o,
                               ffn_w_up, ffn_conv_w, ffn_conv_b, ffn_w_down, final_norm]))
    mom1 = dict(zip(names, [m_meta_tokens, m_mix_norm, m_ffn_norm, m_pool_w, m_pool_scale, m_kv_norm, m_w_kv, m_w_q,
                            m_w_o, m_ffn_w_up, m_ffn_conv_w, m_ffn_conv_b, m_ffn_w_down, m_final_norm]))
    mom2 = dict(zip(names, [v_meta_tokens, v_mix_norm, v_ffn_norm, v_pool_w, v_pool_scale, v_kv_norm, v_w_kv, v_w_q,
                            v_w_o, v_ffn_w_up, v_ffn_conv_w, v_ffn_conv_b, v_ffn_w_down, v_final_norm]))

    delta, new_m, new_v = {}, {}, {}
    for n in names:
        shape = weights[n].shape
        flat = (-1, shape[-1])
        dl, nm, nv = _adamw(weights[n].reshape(flat), grads[n].reshape(flat), mom1[n].reshape(flat),
                            mom2[n].reshape(flat), "adamw_" + n)
        delta[n], new_m[n], new_v[n] = dl.reshape(shape), nm.reshape(shape), nv.reshape(shape)
    return (loss, grad_x, *[grads[n] for n in names], *[delta[n] for n in names],
            *[new_m[n] for n in names], *[new_v[n] for n in names])
```

```python
import functools
from typing import NamedTuple

import jax
import jax.numpy as jnp
from jax import lax
from jax.experimental import pallas as pl
from jax.experimental.pallas import tpu as pltpu

F32 = jnp.float32
BF16 = jnp.bfloat16
SDS = jax.ShapeDtypeStruct

N_DEV = 8
N_META = 16
HEAD_DIM = 64
HEAD_PAIRS = 8
RMS_EPS = 1e-6
LOG2_E = 1.4426950408889634
POOL_WINDOWS = (2, 4, 8, 16)
POOL_C = 256
POOL_HALO = 16
CONV_HALO = 8
ROW_TILE = 384
MM_ROWS_MAX = 1408
FFN_ROWS_MAX = 704
SUM_ROWS_MAX = 256
SUM_WHOLE_BYTES = 4 << 20
ATT_BLK = 128
ATT_Q = ROW_TILE
ATT_UNROLL = ATT_Q // ATT_BLK
UNDERFLOW_AT = 104.0
VMEM_LIMIT = 56 * 1024 * 1024

ADAM_LR = 0.001
ADAM_B1 = 0.9
ADAM_B2 = 0.999
ADAM_EPS = 1e-08
ADAM_WD = 0.01
ADAM_STEP = 10

MESH_AXES = ("x", "y", "c")
NN = (((1,), (0,)), ((), ()))
NT = (((1,), (1,)), ((), ()))
TN = (((0,), (0,)), ((), ()))


def _cp(n_axes):
    return pltpu.CompilerParams(dimension_semantics=("arbitrary",) * n_axes, vmem_limit_bytes=VMEM_LIMIT)


def _dot(a, b, dims=NN):
    return lax.dot_general(a, b, dims, preferred_element_type=F32)


def _rstd(x):
    return lax.rsqrt(jnp.mean(x * x, axis=-1, keepdims=True) + RMS_EPS)


def _row_tile(rows, cap=512, mult=8):
    if rows <= cap:
        return rows
    best = mult
    for t in range(mult, cap + 1, mult):
        if rows % t == 0:
            best = t
    assert rows % best == 0
    return best


def _rms_fwd(h, gains, name):
    lp, d = h.shape
    k = gains.shape[0]
    tm = ROW_TILE

    def body(h_ref, g_ref, *o_refs):
        x = h_ref[...]
        u = x * _rstd(x)
        for j in range(k):
            o_refs[j][...] = (u * g_ref[j:j + 1, :]).astype(BF16)

    row = pl.BlockSpec((tm, d), lambda i: (i, 0))
    return pl.pallas_call(
        body, name=name, grid=(lp // tm,),
        in_specs=[row, pl.BlockSpec((k, d), lambda i: (0, 0))],
        out_specs=[row] * k, out_shape=[SDS((lp, d), BF16)] * k,
        compiler_params=_cp(1))(h, gains)


def _rms_bwd(h, gains, dns, dh_in, name):
    lp, d = h.shape
    k = gains.shape[0]
    tm = ROW_TILE

    def body(h_ref, g_ref, *refs):
        dn_refs, dh_ref, dho_ref, dg_ref = refs[:k], refs[k], refs[k + 1], refs[k + 2]
        i = pl.program_id(0)
        x = h_ref[...]
        r = _rstd(x)
        u = x * r
        du = jnp.zeros_like(x)
        rows = []
        for j in range(k):
            dn = dn_refs[j][...]
            du = du + dn * g_ref[j:j + 1, :]
            rows.append(jnp.sum(dn * u, axis=0, keepdims=True))
        dx = r * (du - u * jnp.mean(du * u, axis=-1, keepdims=True))
        dho_ref[...] = dh_ref[...] + dx

        @pl.when(i == 0)
        def _():
            for j in range(k):
                dg_ref[j:j + 1, :] = rows[j]

        @pl.when(i > 0)
        def _():
            for j in range(k):
                dg_ref[j:j + 1, :] += rows[j]

    row = pl.BlockSpec((tm, d), lambda i: (i, 0))
    vec = pl.BlockSpec((k, d), lambda i: (0, 0))
    return pl.pallas_call(
        body, name=name, grid=(lp // tm,),
        in_specs=[row, vec] + [row] * k + [row],
        out_specs=[row, vec], out_shape=[SDS((lp, d), F32), SDS((k, d), F32)],
        compiler_params=_cp(1))(h, gains, *dns, dh_in)


def _loss_bwd(h, gain, target, n_real, name):
    lp, d = h.shape
    tm = ROW_TILE

    def body(h_ref, g_ref, t_ref, dh_ref, loss_ref, dg_ref):
        i = pl.program_id(0)
        x = h_ref[...]
        g = g_ref[...]
        r = _rstd(x)
        u = x * r
        row = i * tm + lax.broadcasted_iota(jnp.int32, (tm, 1), 0)
        valid = (row >= N_META) & (row < N_META + n_real)
        e = jnp.where(valid, u * g - t_ref[...], 0.0)
        part = 0.5 * jnp.sum(jnp.sum(e * e, axis=-1, keepdims=True), axis=0, keepdims=True) * (1.0 / d)
        dy = e * (1.0 / d)
        du = dy * g
        dh_ref[...] = r * (du - u * jnp.mean(du * u, axis=-1, keepdims=True))
        dgp = jnp.sum(dy * u, axis=0, keepdims=True)

        @pl.when(i == 0)
        def _():
            loss_ref[...] = jnp.broadcast_to(part, (8, 128))
            dg_ref[...] = dgp

        @pl.when(i > 0)
        def _():
            loss_ref[...] += jnp.broadcast_to(part, (8, 128))
            dg_ref[...] += dgp

    row = pl.BlockSpec((tm, d), lambda i: (i, 0))
    vec = pl.BlockSpec((1, d), lambda i: (0, 0))
    return pl.pallas_call(
        body, name=name, grid=(lp // tm,),
        in_specs=[row, vec, row],
        out_specs=[row, pl.BlockSpec((8, 128), lambda i: (0, 0)), vec],
        out_shape=[SDS((lp, d), F32), SDS((8, 128), F32), SDS((1, d), F32)],
        compiler_params=_cp(1))(h, gain, target)


def _pool_fwd(h, gain, w, scale, name):
    lp, d = h.shape
    tm = ROW_TILE
    hb = POOL_HALO

    def body(h_ref, halo_ref, g_ref, w_ref, s_ref, h1_ref, diff_ref):
        i = pl.program_id(0)
        g = g_ref[...]
        x = h_ref[...]
        n = x * _rstd(x) * g
        xh = halo_ref[...]
        nh = jnp.where(i > 0, xh * _rstd(xh) * g, 0.0)
        cur = jnp.concatenate([nh, n], axis=0)
        pos = i * tm + lax.broadcasted_iota(jnp.int32, (tm, 1), 0)
        for gi, win in enumerate(POOL_WINDOWS):
            if gi > 0:
                cur = cur[:, POOL_C:]
            cur = cur + pltpu.roll(cur, win // 2, 0)
            c0 = gi * POOL_C
            count = jnp.minimum(pos + 1, win).astype(F32)
            diff = cur[hb:, :POOL_C] / count - n[:, c0:c0 + POOL_C]
            diff = diff.astype(BF16)
            y = _dot(diff, w_ref[gi])
            h1_ref[:, c0:c0 + POOL_C] = x[:, c0:c0 + POOL_C] + y * s_ref[:, c0:c0 + POOL_C]
            diff_ref[:, c0:c0 + POOL_C] = diff

    row = pl.BlockSpec((tm, d), lambda i: (i, 0))
    halo = pl.BlockSpec((hb, d), lambda i: (jnp.maximum(i * (tm // hb) - 1, 0), 0))
    vec = pl.BlockSpec((1, d), lambda i: (0, 0))
    return pl.pallas_call(
        body, name=name, grid=(lp // tm,),
        in_specs=[row, halo, vec, pl.BlockSpec(w.shape, lambda i: (0, 0, 0)), vec],
        out_specs=[row, row], out_shape=[SDS((lp, d), F32), SDS((lp, d), BF16)],
        compiler_params=_cp(1))(h, h, gain, w, scale)


def _pool_bwd(h, gain, w, scale, diff, dh1, name):
    lp, d = h.shape
    tm = ROW_TILE
    hb = POOL_HALO
    nblk = lp // tm
    ext = tm + hb

    def body(h_ref, g_ref, w_ref, s_ref, diff_ref, dh_ref, dhn_ref, dh0_ref, dw_ref, ds_ref, dg_ref):
        i = pl.program_id(0)
        g = g_ref[...]
        x = h_ref[...]
        r = _rstd(x)
        u = x * r
        dh = dh_ref[...]
        dhn = jnp.where(i < nblk - 1, dhn_ref[...], 0.0)
        dyp = jnp.concatenate([dh, dhn], axis=0) * s_ref[...]
        pos = i * tm + lax.broadcasted_iota(jnp.int32, (ext, 1), 0)
        dn_parts, dw_parts, ds_parts = [], [], []
        for gi, win in enumerate(POOL_WINDOWS):
            c0 = gi * POOL_C
            wg = w_ref[gi]
            dyp_g = dyp[:, c0:c0 + POOL_C].astype(BF16)
            dd = _dot(dyp_g, wg, NT)
            dfg = diff_ref[:, c0:c0 + POOL_C]
            dw_parts.append(_dot(dfg, dyp_g[:tm], TN))
            ds_parts.append(jnp.sum(dh[:, c0:c0 + POOL_C] * _dot(dfg, wg), axis=0, keepdims=True))
            count = jnp.minimum(pos + 1, win).astype(F32)
            cur = dd / count
            sh = 1
            while sh < win:
                cur = cur + pltpu.roll(cur, ext - sh, 0)
                sh *= 2
            dn_parts.append(cur[:tm] - dd[:tm])
        dn = jnp.concatenate(dn_parts, axis=1)
        du = dn * g
        dh0_ref[...] = dh + r * (du - u * jnp.mean(du * u, axis=-1, keepdims=True))
        dgp = jnp.sum(dn * u, axis=0, keepdims=True)
        dsp = jnp.concatenate(ds_parts, axis=1)

        @pl.when(i == 0)
        def _():
            for gi in range(len(POOL_WINDOWS)):
                dw_ref[gi] = dw_parts[gi]
            ds_ref[...] = dsp
            dg_ref[...] = dgp

        @pl.when(i > 0)
        def _():
            for gi in range(len(POOL_WINDOWS)):
                dw_ref[gi] += dw_parts[gi]
            ds_ref[...] += dsp
            dg_ref[...] += dgp

    row = pl.BlockSpec((tm, d), lambda i: (i, 0))
    nxt = pl.BlockSpec((hb, d), lambda i: (jnp.minimum((i + 1) * (tm // hb), lp // hb - 1), 0))
    vec = pl.BlockSpec((1, d), lambda i: (0, 0))
    wsp = pl.BlockSpec(w.shape, lambda i: (0, 0, 0))
    return pl.pallas_call(
        body, name=name, grid=(nblk,),
        in_specs=[row, vec, wsp, vec, row, row, nxt],
        out_specs=[row, wsp, vec, vec],
        out_shape=[SDS((lp, d), F32), SDS(w.shape, F32), SDS((1, d), F32), SDS((1, d), F32)],
        compiler_params=_cp(1))(h, gain, w, scale, diff, dh1, dh1)


def _ffn_specs(tm, c, lp):
    blk = pl.BlockSpec((2, 1, tm, c), lambda g, i: (0, g, i, 0))
    halo = pl.BlockSpec((2, 1, CONV_HALO, c), lambda g, i: (0, g, jnp.maximum(i * (tm // CONV_HALO) - 1, 0), 0))
    cw = pl.BlockSpec((2, 1, 3, c), lambda g, i: (0, g, 0, 0))
    cb = pl.BlockSpec((2, 1, 1, c), lambda g, i: (0, g, 0, 0))
    return blk, halo, cw, cb


def _ffn_up_act(n2, w_up4, cw4, cb4, name, ride=None):
    lp, d = n2.shape
    _, ng, _, c = w_up4.shape
    tm = _row_tile(lp, FFN_ROWS_MAX, 16)
    hb = CONV_HALO

    def body(a_ref, w_ref, cw_ref, cb_ref, up_ref, act_ref, tail_ref):
        @pl.when(pl.program_id(1) == 0)
        def _():
            tail_ref[...] = jnp.zeros_like(tail_ref)

        a = a_ref[...]
        u = []
        for half in range(2):
            x = _dot(a, w_ref[half, 0])
            up_ref[half, 0] = x
            rows = jnp.concatenate([tail_ref[half], x], axis=0)
            u.append(cb_ref[half, 0] + cw_ref[half, 0, 0:1, :] * pltpu.roll(rows, 2, 0)[hb:]
                     + cw_ref[half, 0, 1:2, :] * pltpu.roll(rows, 1, 0)[hb:] + cw_ref[half, 0, 2:3, :] * x)
            tail_ref[half] = x[tm - hb:]
        gate, val = u
        sig = 1.0 / (1.0 + jnp.exp(-gate))
        act_ref[0] = (gate * sig * val).astype(BF16)

    blk, _, cw, cb = _ffn_specs(tm, c, lp)
    (up4, act), rode = _call_with_ride(
        body, ride, name=name, grid=(ng, lp // tm),
        in_specs=[pl.BlockSpec((tm, d), lambda g, i: (i, 0)), pl.BlockSpec((2, 1, d, c), lambda g, i: (0, g, 0, 0)), cw, cb],
        out_specs=[blk, pl.BlockSpec((1, tm, c), lambda g, i: (g, i, 0))],
        out_shape=[SDS((2, ng, lp, c), F32), SDS((ng, lp, c), BF16)],
        scratch_shapes=[pltpu.VMEM((2, hb, c), F32)], args=[n2, w_up4, cw4, cb4])
    return up4, act, rode


def _ffn_act_bwd(up4, cw4, cb4, dh, w_down4, name, ride=None):
    _, ng, lp, c = up4.shape
    d = dh.shape[1]
    tm = ROW_TILE
    hb = CONV_HALO
    nblk = lp // tm
    ext = tm + hb

    def body(up_ref, prev_ref, next_ref, cw_ref, cb_ref, dh_ref, dhn_ref, wd_ref, dup_ref, dcw_ref, dcb_ref):
        i = pl.program_id(1)
        first = i == 0
        last = i == nblk - 1
        dh_rows = jnp.concatenate([dh_ref[...], jnp.where(last, 0.0, dhn_ref[...])], axis=0)
        da = _dot(dh_rows.astype(BF16), wd_ref[0], NT)
        u, taps = [], []
        for half in range(2):
            rows = jnp.concatenate([jnp.where(first, 0.0, prev_ref[half, 0]), up_ref[half, 0],
                                    jnp.where(last, 0.0, next_ref[half, 0])], axis=0)
            x, xm1, xm2 = rows[hb:], pltpu.roll(rows, 1, 0)[hb:], pltpu.roll(rows, 2, 0)[hb:]
            u.append(cb_ref[half, 0] + cw_ref[half, 0, 0:1, :] * xm2 + cw_ref[half, 0, 1:2, :] * xm1
                     + cw_ref[half, 0, 2:3, :] * x)
            taps.append((xm2, xm1, x))
        gate, val = u
        sig = 1.0 / (1.0 + jnp.exp(-gate))
        dus = (da * val * (sig * (1.0 + gate * (1.0 - sig))), da * (gate * sig))
        sums = []
        for half in range(2):
            du = dus[half]
            dup_ref[half, 0] = (cw_ref[half, 0, 2:3, :] * du[:tm] + cw_ref[half, 0, 1:2, :] * pltpu.roll(du, ext - 1, 0)[:tm]
                                + cw_ref[half, 0, 0:1, :] * pltpu.roll(du, ext - 2, 0)[:tm]).astype(BF16)
            sums.append([jnp.sum(du[:tm] * t[:tm], axis=0, keepdims=True) for t in taps[half]]
                        + [jnp.sum(du[:tm], axis=0, keepdims=True)])

        @pl.when(first)
        def _():
            for half in range(2):
                for k in range(3):
                    dcw_ref[half, 0, k:k + 1, :] = sums[half][k]
                dcb_ref[half, 0] = sums[half][3]

        @pl.when(i > 0)
        def _():
            for half in range(2):
                for k in range(3):
                    dcw_ref[half, 0, k:k + 1, :] += sums[half][k]
                dcb_ref[half, 0] += sums[half][3]

    blk, prev, cw, cb = _ffn_specs(tm, c, lp)

    def next_rows(g, i):
        return jnp.minimum((i + 1) * (tm // hb), lp // hb - 1)

    (dup4, dcw4, dcb4), rode = _call_with_ride(
        body, ride, name=name, grid=(ng, nblk),
        in_specs=[blk, prev, pl.BlockSpec((2, 1, hb, c), lambda g, i: (0, g, next_rows(g, i), 0)), cw, cb,
                  pl.BlockSpec((tm, d), lambda g, i: (i, 0)),
                  pl.BlockSpec((hb, d), lambda g, i: (next_rows(g, i), 0)),
                  pl.BlockSpec((1, c, d), lambda g, i: (g, 0, 0))],
        out_specs=[blk, cw, cb],
        out_shape=[SDS(up4.shape, BF16), SDS(cw4.shape, F32), SDS(cb4.shape, F32)],
        args=[up4, up4, up4, cw4, cb4, dh, dh, w_down4])
    return dup4, dcw4, dcb4, rode


def _mm_tile(rows):
    return _row_tile(rows, MM_ROWS_MAX)


def _mm_group(a, b, dims, out_dtype, name):
    m, k = a.shape
    ng = b.shape[0]
    n = b.shape[2] if dims == NN else b.shape[1]
    tm = _mm_tile(m)

    def body(a_ref, b_ref, o_ref):
        o_ref[0] = _dot(a_ref[...].astype(BF16), b_ref[0], dims).astype(out_dtype)

    return pl.pallas_call(
        body, name=name, grid=(ng, m // tm),
        in_specs=[pl.BlockSpec((tm, k), lambda g, i: (i, 0)),
                  pl.BlockSpec((1,) + b.shape[1:], lambda g, i: (g, 0, 0))],
        out_specs=pl.BlockSpec((1, tm, n), lambda g, i: (g, i, 0)),
        out_shape=SDS((ng, m, n), out_dtype), compiler_params=_cp(2))(a, b)


def _mm_reduce(a, b, dims, res, name, ride=None):
    ng, m, k = a.shape
    n = b.shape[2] if dims == NN else b.shape[1]
    tm = _mm_tile(m)
    has_res = res is not None

    def body(a_ref, b_ref, *refs):
        o_ref, acc_ref = refs[-2], refs[-1]
        g = pl.program_id(1)
        p = _dot(a_ref[0].astype(BF16), b_ref[0], dims)

        @pl.when(g == 0)
        def _():
            acc_ref[...] = p + refs[0][...] if has_res else p

        @pl.when(g > 0)
        def _():
            acc_ref[...] += p

        @pl.when(g == ng - 1)
        def _():
            o_ref[...] = acc_ref[...]

    row = pl.BlockSpec((tm, n), lambda i, g: (i, 0))
    (out,), rode = _call_with_ride(
        body, ride, name=name, grid=(m // tm, ng),
        in_specs=[pl.BlockSpec((1, tm, k), lambda i, g: (g, i, 0)),
                  pl.BlockSpec((1,) + b.shape[1:], lambda i, g: (g, 0, 0))] + ([row] if has_res else []),
        out_specs=[row], out_shape=[SDS((m, n), F32)], scratch_shapes=[pltpu.VMEM((tm, n), F32)],
        args=[a, b] + ([res] if has_res else []))
    return out if ride is None else (out, rode)


def _mm_tn(a, b, name, ride=None):
    ga, m, ka = a.shape
    gb, _, n = b.shape
    ng = max(ga, gb)
    tk = _mm_tile(m)
    nk = m // tk

    def body(a_ref, b_ref, o_ref, acc_ref):
        s = pl.program_id(1)
        p = _dot(a_ref[0].astype(BF16), b_ref[0].astype(BF16), TN)

        @pl.when(s == 0)
        def _():
            acc_ref[...] = p

        @pl.when(s > 0)
        def _():
            acc_ref[...] += p

        @pl.when(s == nk - 1)
        def _():
            o_ref[0] = acc_ref[...].astype(BF16)

    (out,), rode = _call_with_ride(
        body, ride, name=name, grid=(ng, nk),
        in_specs=[pl.BlockSpec((1, tk, ka), (lambda g, s: (g, s, 0)) if ga > 1 else (lambda g, s: (0, s, 0))),
                  pl.BlockSpec((1, tk, n), (lambda g, s: (g, s, 0)) if gb > 1 else (lambda g, s: (0, s, 0)))],
        out_specs=[pl.BlockSpec((1, ka, n), lambda g, s: (g, 0, 0))], out_shape=[SDS((ng, ka, n), BF16)],
        scratch_shapes=[pltpu.VMEM((ka, n), F32)], args=[a, b])
    return out if ride is None else (out, rode)


def _mm_norm_bwd(a, b, h, gain, dh, name, ride=None):
    ng, m, k = a.shape
    d = b.shape[1]
    tm = _row_tile(m, FFN_ROWS_MAX, 16)
    nblk = m // tm

    def body(a_ref, b_ref, h_ref, g_ref, dh_ref, o_ref, dg_ref, acc_ref):
        i, g = pl.program_id(0), pl.program_id(1)
        p = _dot(a_ref[0], b_ref[0], NT)

        @pl.when(g == 0)
        def _():
            acc_ref[...] = p

        @pl.when(g > 0)
        def _():
            acc_ref[...] += p

        @pl.when(g == ng - 1)
        def _():
            dn = acc_ref[...]
            x = h_ref[...]
            r = _rstd(x)
            u = x * r
            du = dn * g_ref[...]
            o_ref[...] = dh_ref[...] + r * (du - u * jnp.mean(du * u, axis=-1, keepdims=True))
            dgp = jnp.sum(dn * u, axis=0, keepdims=True)

            @pl.when(i == 0)
            def _():
                dg_ref[...] = dgp

            @pl.when(i > 0)
            def _():
                dg_ref[...] += dgp

    row = pl.BlockSpec((tm, d), lambda i, g: (i, 0))
    vec = pl.BlockSpec((1, d), lambda i, g: (0, 0))
    (out, dgain), rode = _call_with_ride(
        body, ride, name=name, grid=(nblk, ng),
        in_specs=[pl.BlockSpec((1, tm, k), lambda i, g: (g, i, 0)), pl.BlockSpec((1, d, k), lambda i, g: (g, 0, 0)),
                  row, vec, row],
        out_specs=[row, vec], out_shape=[SDS((m, d), F32), SDS((1, d), F32)],
        scratch_shapes=[pltpu.VMEM((tm, d), F32)], args=[a, b, h, gain, dh])
    return out, dgain, rode


def _pair_tri(kind, sign):
    r = jnp.arange(2 * ATT_BLK)[:, None]
    c = jnp.arange(2 * ATT_BLK)[None, :]
    same = (r < ATT_BLK) == (c < ATT_BLK)
    rel = {"from": r >= c, "before": r < c}[kind]
    return ((same & rel) * sign).astype(BF16)


def _scan_dot(x, tri):
    hi = x.astype(BF16)
    lo = (x - hi.astype(F32)).astype(BF16)
    return _dot(hi, tri) + _dot(lo, tri)


def _split_heads(blk, lane_a):
    zero = jnp.zeros_like(blk)
    return jnp.concatenate([jnp.where(lane_a, blk, zero), jnp.where(lane_a, zero, blk)], axis=0)


def _softplus(z):
    return jnp.maximum(z, 0.0) + jnp.log(1.0 + jnp.exp2(jnp.abs(z) * (-LOG2_E)))


def _visible(qi, j, r0):
    t = qi * ATT_Q + r0 + lax.broadcasted_iota(jnp.int32, (ATT_Q - r0, 2 * ATT_BLK), 0)
    s = j * ATT_BLK + (lax.broadcasted_iota(jnp.int32, (ATT_Q - r0, 2 * ATT_BLK), 1) & (ATT_BLK - 1))
    return s < t


def _add_rows(x, r0, y):
    return x + y if r0 == 0 else jnp.concatenate([x[:r0], x[r0:] + y], axis=0)


def _diag_rows(n):
    return n * ATT_BLK


def _halves(x):
    return x[:, :ATT_BLK], x[:, ATT_BLK:]


def _rowsum(x):
    return jnp.sum(x, axis=1, keepdims=True)


def _still_visible(ca, cb):
    return (jnp.minimum(jnp.min(ca), jnp.min(cb)) < UNDERFLOW_AT).astype(jnp.int32)


def _attn_specs(lp):
    bk = ATT_BLK
    qblk = pl.BlockSpec((ATT_Q, bk), lambda p, i: (i, p))
    kblk = pl.BlockSpec((1, lp, bk), lambda p, i: (p // 2, 0, p % 2))
    vblk = pl.BlockSpec((1, lp, bk), lambda p, i: (HEAD_PAIRS // 2 + p // 2, 0, p % 2))
    tri = pl.BlockSpec((2 * bk, 2 * bk), lambda p, i: (0, 0))
    return qblk, kblk, vblk, tri


def _attn_fwd(q, kv, name, ride=None):
    lp, d = q.shape
    bk = ATT_BLK

    def body(q_ref, k_ref, v_ref, tri_ref, o_ref):
        qi = pl.program_id(1)
        qs = q_ref[...] * (HEAD_DIM ** -0.5)
        lane_a = lax.broadcasted_iota(jnp.int32, (1, bk), 1) < HEAD_DIM
        tri = tri_ref[...]

        def trip(js, carry, masked):
            oacc, ca, cb = carry
            r0s = [_diag_rows(len(js) - 1 - n) if masked else 0 for n in range(len(js))]
            rows = [pl.ds(pl.multiple_of(j * bk, bk), bk) for j in js]
            zs = [_dot(qs[r0:], _split_heads(k_ref[0, r, :], lane_a), NT) for r0, r in zip(r0s, rows)]
            ms = [_softplus(z) for z in zs]
            seen = [_visible(qi, j, r0) if masked else None for j, r0 in zip(js, r0s)]
            if masked:
                ms = [jnp.where(v, m, 0.0) for v, m in zip(seen, ms)]
            ws = [_scan_dot(m, tri) for m in ms]
            for v, r0, r, z, m, w in zip(seen, r0s, rows, zs, ms, ws):
                exa, exb = _halves(z + w)
                a = jnp.concatenate([jnp.exp(exa - ca[r0:]), jnp.exp(exb - cb[r0:])], axis=1)
                if masked:
                    a = jnp.where(v, a, 0.0)
                oacc = _add_rows(oacc, r0, _dot(a.astype(BF16), _split_heads(v_ref[0, r, :], lane_a)))
                ma, mb = _halves(m)
                ca, cb = _add_rows(ca, r0, _rowsum(ma)), _add_rows(cb, r0, _rowsum(mb))
            return oacc, ca, cb

        carry = (jnp.zeros((ATT_Q, bk), F32), jnp.zeros((ATT_Q, 1), F32), jnp.zeros((ATT_Q, 1), F32))
        top = (qi + 1) * ATT_UNROLL - 1
        carry = trip([top - u for u in range(ATT_UNROLL)], carry, True)
        def older(st):
            g, _, *c = st
            c = trip([top - (g + 1) * ATT_UNROLL - u for u in range(ATT_UNROLL)], tuple(c), False)
            return (g + 1, _still_visible(c[1], c[2]), *c)

        _, _, oacc, _, _ = lax.while_loop(
            lambda st: (st[0] < qi) & (st[1] > 0), older, (jnp.int32(0), _still_visible(carry[1], carry[2]), *carry))
        o_ref[...] = oacc.astype(BF16)

    qblk, kblk, vblk, tri = _attn_specs(lp)
    (o,), rode = _call_with_ride(
        body, ride, name=name, grid=(HEAD_PAIRS, lp // ATT_Q), in_specs=[qblk, kblk, vblk, tri],
        out_specs=[qblk], out_shape=[SDS((lp, d), BF16)], args=[q, kv, kv, _pair_tri("from", -1)])
    return o, rode


def _attn_bwd(q, kv, do, name, ride=None):
    lp, d = q.shape
    bk = ATT_BLK
    scale = HEAD_DIM ** -0.5

    def body(q_ref, k_ref, v_ref, do_ref, tri_ref, dq_ref, dk_ref, dv_ref):
        qi = pl.program_id(1)

        @pl.when(qi == 0)
        def _():
            dk_ref[...] = jnp.zeros_like(dk_ref)
            dv_ref[...] = jnp.zeros_like(dv_ref)

        qs = q_ref[...] * scale
        do_blk = do_ref[...]
        lane_a = lax.broadcasted_iota(jnp.int32, (1, bk), 1) < HEAD_DIM
        tri = tri_ref[...]

        def sums(js, carry, masked):
            ca, cb = carry
            for n, j in enumerate(js):
                r0 = _diag_rows(n) if masked else 0
                m = _softplus(_dot(qs[r0:], _split_heads(k_ref[0, pl.ds(pl.multiple_of(j * bk, bk), bk), :], lane_a), NT))
                if masked:
                    m = jnp.where(_visible(qi, j, r0), m, 0.0)
                ma, mb = _halves(m)
                ca, cb = _add_rows(ca, r0, _rowsum(ma)), _add_rows(cb, r0, _rowsum(mb))
            return ca, cb

        def trip(js, carry, masked):
            dq, pa, pb, ea, eb = carry
            r0s = [_diag_rows(n) if masked else 0 for n in range(len(js))]
            rows = [pl.ds(pl.multiple_of(j * bk, bk), bk) for j in js]
            kks = [_split_heads(k_ref[0, r, :], lane_a) for r in rows]
            zs = [_dot(qs[r0:], kk, NT) for r0, kk in zip(r0s, kks)]
            das = [_dot(do_blk[r0:], _split_heads(v_ref[0, r, :], lane_a), NT) for r0, r in zip(r0s, rows)]
            ms = [_softplus(z) for z in zs]
            seen = [_visible(qi, j, r0) if masked else None for j, r0 in zip(js, r0s)]
            if masked:
                ms = [jnp.where(v, m, 0.0) for v, m in zip(seen, ms)]
            xs = [_scan_dot(m, tri) for m in ms]
            es, a_bf = [], []
            for v, r0, z, m, x, da in zip(seen, r0s, zs, ms, xs, das):
                xa, xb = _halves(z + x)
                a = jnp.concatenate([jnp.exp(xa + pa[r0:]), jnp.exp(xb + pb[r0:])], axis=1)
                if masked:
                    a = jnp.where(v, a, 0.0)
                a_bf.append(a.astype(BF16))
                es.append(a * da)
                ma, mb = _halves(m)
                pa, pb = _add_rows(pa, r0, _rowsum(ma)), _add_rows(pb, r0, _rowsum(mb))
            ss = [_dot(e.astype(BF16), tri) for e in es]
            for v, r0, r, kk, z, m, e, s, ab in zip(seen, r0s, rows, kks, zs, ms, es, ss, a_bf):
                sa, sb = _halves(s)
                e_before = jnp.concatenate([sa + ea[r0:], sb + eb[r0:]], axis=1)
                dz = e - jnp.exp(z - m) * (e + e_before)
                if masked:
                    dz = jnp.where(v, dz, 0.0)
                dzb = dz.astype(BF16)
                dq = _add_rows(dq, r0, _dot(dzb, kk))
                rk = _dot(dzb, qs[r0:], TN)
                rv = _dot(ab, do_blk[r0:], TN)
                dk_ref[0, r, :] += jnp.where(lane_a, rk[:bk], rk[bk:])
                dv_ref[0, r, :] += jnp.where(lane_a, rv[:bk], rv[bk:])
                e_a, e_b = _halves(e)
                ea, eb = _add_rows(ea, r0, _rowsum(e_a)), _add_rows(eb, r0, _rowsum(e_b))
            return dq, pa, pb, ea, eb

        zcol = jnp.zeros((ATT_Q, 1), F32)
        diag = [qi * ATT_UNROLL + u for u in range(ATT_UNROLL)]
        def older(st):
            g, _, *c = st
            c = sums([(qi - 1 - g) * ATT_UNROLL + u for u in range(ATT_UNROLL)], tuple(c), False)
            return (g + 1, _still_visible(*c), *c)

        seen = sums(diag, (zcol, zcol), True)
        n_old, _, ta, tb = lax.while_loop(
            lambda st: (st[0] < qi) & (st[1] > 0), older, (jnp.int32(0), _still_visible(*seen), *seen))
        carry = lax.fori_loop(
            0, n_old, lambda g, c: trip([(qi - n_old + g) * ATT_UNROLL + u for u in range(ATT_UNROLL)], c, False),
            (jnp.zeros((ATT_Q, bk), F32), -ta, -tb, zcol, zcol))
        carry = trip(diag, carry, True)
        dq_ref[...] = (carry[0] * scale).astype(BF16)

    qblk, kblk, vblk, tri = _attn_specs(lp)
    (dq, dk, dv), rode = _call_with_ride(
        body, ride, name=name, grid=(HEAD_PAIRS, lp // ATT_Q), in_specs=[qblk, kblk, vblk, qblk, tri],
        out_specs=[qblk, kblk, kblk],
        out_shape=[SDS((lp, d), BF16), SDS((HEAD_PAIRS // 2, lp, 2 * bk), F32), SDS((HEAD_PAIRS // 2, lp, 2 * bk), F32)],
        args=[q, kv, kv, do, _pair_tri("before", 1)])
    return dq, dk, dv, rode


def _mesh_pos():
    return lax.axis_index("x"), lax.axis_index("y"), lax.axis_index("c")


def _flip(pos, r):
    x, y, c = pos
    return (1 - x if r & 4 else x, 1 - y if r & 2 else y, 1 - c if r & 1 else c)


def _dev_index(pos):
    return 4 * pos[0] + 2 * pos[1] + pos[2]


class _Ride(NamedTuple):
    kind: str
    arrays: list


def _ride_arrays(ride):
    return [] if ride is None else ride.arrays


def _ride_args(ride):
    if ride is None:
        return [], [], [], []
    n = len(ride.arrays)
    hbm = pl.BlockSpec(memory_space=pl.ANY)
    shapes = [SDS(x.shape if ride.kind == "scatter" else (N_DEV,) + x.shape, x.dtype) for x in ride.arrays]
    sems = [pltpu.SemaphoreType.DMA((7 * n,)), pltpu.SemaphoreType.DMA((7 * n,)), pltpu.SemaphoreType.DMA((n,))]
    return [hbm] * n, [hbm] * n, shapes, sems


def _riding(body, n_in, n_out, ride, first, middle, last):
    if ride is None:
        return body
    n = len(ride.arrays)

    def wrapped(*refs):
        ins, srcs = refs[:n_in], refs[n_in:n_in + n]
        outs, dsts = refs[n_in + n:n_in + n + n_out], refs[n_in + n + n_out:n_in + 2 * n + n_out]
        scratch, (send_sems, recv_sems, local_sems) = refs[n_in + 2 * n + n_out:-3], refs[-3:]
        me = _mesh_pos()
        mi = _dev_index(me)

        def copy(a, k, src, dst, to):
            return pltpu.make_async_remote_copy(
                src_ref=src, dst_ref=dst, send_sem=send_sems.at[7 * a + k], recv_sem=recv_sems.at[7 * a + k],
                device_id=to, device_id_type=pl.DeviceIdType.MESH)

        local, sends, lands, arrived, passed = [], [], [], [], []
        for a in range(n):
            if ride.kind == "gather_by_chip":
                sibling, others = _flip(me, 1), [_flip(me, 4), _flip(me, 2), _flip(me, 6)]
                local.append(pltpu.make_async_copy(srcs[a], dsts[a].at[mi], local_sems.at[a]))
                sends.append(copy(a, 0, srcs[a], dsts[a].at[mi], sibling))
                lands.append(copy(a, 0, dsts[a].at[_dev_index(sibling)], dsts[a].at[_dev_index(sibling)], me))
                for j, o in enumerate(others):
                    oi, si = _dev_index(o), _dev_index(_flip(o, 1))
                    sends.append(copy(a, 1 + j, srcs[a], dsts[a].at[mi], o))
                    arrived.append(copy(a, 1 + j, dsts[a].at[oi], dsts[a].at[oi], me))
                    passed.append(copy(a, 4 + j, dsts[a].at[oi], dsts[a].at[oi], sibling))
                    lands.append(copy(a, 4 + j, dsts[a].at[si], dsts[a].at[si], me))
                continue
            gather = ride.kind == "gather"
            local.append(pltpu.make_async_copy(srcs[a] if gather else srcs[a].at[mi], dsts[a].at[mi], local_sems.at[a]))
            for r in range(1, N_DEV):
                peer = _flip(me, r)
                pi = _dev_index(peer)
                sends.append(copy(a, r - 1, srcs[a] if gather else srcs[a].at[pi], dsts[a].at[mi], peer))
                lands.append(copy(a, r - 1, dsts[a].at[pi], dsts[a].at[pi], peer))

        @pl.when(first())
        def _():
            for cp in local + sends:
                cp.start()

        if passed:
            @pl.when(middle())
            def _():
                for got, on in zip(arrived, passed):
                    got.wait_recv()
                    on.start()

        body(*ins, *outs, *scratch)

        @pl.when(last())
        def _():
            for cp in lands:
                cp.wait_recv()
            for cp in sends + passed:
                cp.wait_send()
            for cp in local:
                cp.wait()

    return wrapped


def _call_with_ride(body, ride, *, name, grid, in_specs, out_specs, out_shape, args, scratch_shapes=()):
    ride_in, ride_out, ride_shape, ride_sems = _ride_args(ride)
    axes = range(len(grid))
    assert ride is None or ride.kind != "gather_by_chip" or grid[0] >= 4, grid

    def at(step):
        return lambda: functools.reduce(lambda p, k: p & (pl.program_id(k) == step[k]), axes, True)

    ends = [(0,) * len(grid), (3 * grid[0] // 4,) + (0,) * (len(grid) - 1), tuple(g - 1 for g in grid)]
    out = pl.pallas_call(
        _riding(body, len(in_specs), len(out_specs), ride, *map(at, ends)), name=name, grid=grid,
        in_specs=list(in_specs) + ride_in, out_specs=list(out_specs) + ride_out,
        out_shape=list(out_shape) + ride_shape, scratch_shapes=list(scratch_shapes) + ride_sems,
        compiler_params=_cp(len(grid)))(*args, *_ride_arrays(ride))
    return out[:len(out_specs)], out[len(out_specs):]


def _all_gather(xs, name):
    n = len(xs)

    def body(*refs):
        x_refs, out_refs = refs[:n], refs[n:2 * n]
        send_sems, recv_sems, local_sems = refs[2 * n:]
        me = _mesh_pos()
        sibling = _flip(me, 1)
        others = [_flip(me, 4), _flip(me, 2), _flip(me, 6)]

        def copy(a, k, block, to, own=False):
            slab = out_refs[a].at[_dev_index(block)]
            return pltpu.make_async_remote_copy(
                src_ref=x_refs[a] if own else slab, dst_ref=slab,
                send_sem=send_sems.at[7 * a + k], recv_sem=recv_sems.at[7 * a + k],
                device_id=to, device_id_type=pl.DeviceIdType.MESH)

        mine = [pltpu.make_async_copy(x_refs[a], out_refs[a].at[_dev_index(me)], local_sems.at[a]) for a in range(n)]
        first = []
        for a in range(n):
            mine[a].start()
            first += [copy(a, 0, me, sibling, own=True)] + [copy(a, 1 + j, me, o, own=True) for j, o in enumerate(others)]
        for cp in first:
            cp.start()
        passed = []
        for a in range(n):
            for j, o in enumerate(others):
                copy(a, 1 + j, o, me).wait_recv()
                passed.append(copy(a, 4 + j, o, sibling))
                passed[-1].start()
        for a in range(n):
            copy(a, 0, sibling, me).wait_recv()
            for j, o in enumerate(others):
                copy(a, 4 + j, _flip(o, 1), me).wait_recv()
        for cp in first + passed:
            cp.wait_send()
        for cp in mine:
            cp.wait()

    hbm = pl.BlockSpec(memory_space=pl.ANY)
    return pl.pallas_call(
        body, name=name, out_shape=[SDS((N_DEV,) + x.shape, x.dtype) for x in xs],
        in_specs=[hbm] * n, out_specs=[hbm] * n,
        scratch_shapes=[pltpu.SemaphoreType.DMA((7 * n,)), pltpu.SemaphoreType.DMA((7 * n,)), pltpu.SemaphoreType.DMA((n,))],
    )(*xs)


def _sum_slabs(a, name, ride=None):
    n, rows, cols = a.shape
    tr = rows if a.size * a.dtype.itemsize <= SUM_WHOLE_BYTES else _row_tile(rows, SUM_ROWS_MAX, 16)

    def body(a_ref, o_ref):
        acc = a_ref[0].astype(F32)
        for k in range(1, n):
            acc = acc + a_ref[k].astype(F32)
        o_ref[...] = acc

    (out,), rode = _call_with_ride(
        body, ride, name=name, grid=(rows // tr,),
        in_specs=[pl.BlockSpec((n, tr, cols), lambda i: (0, i, 0))],
        out_specs=[pl.BlockSpec((tr, cols), lambda i: (i, 0))], out_shape=[SDS((rows, cols), F32)], args=[a])
    return out if ride is None else (out, rode)


def _adamw(w, g, m, v, name):
    rows, cols = w.shape
    tr = _row_tile(rows, 352)

    def body(w_ref, g_ref, m_ref, v_ref, d_ref, mo_ref, vo_ref):
        g_ = g_ref[...]
        m_ = ADAM_B1 * m_ref[...] + (1.0 - ADAM_B1) * g_
        v_ = ADAM_B2 * v_ref[...] + (1.0 - ADAM_B2) * (g_ * g_)
        m_hat = m_ / (1.0 - ADAM_B1 ** ADAM_STEP)
        v_hat = v_ / (1.0 - ADAM_B2 ** ADAM_STEP)
        d_ref[...] = -ADAM_LR * (m_hat / (jnp.sqrt(v_hat) + ADAM_EPS) + ADAM_WD * w_ref[...])
        mo_ref[...] = m_
        vo_ref[...] = v_

    blk = pl.BlockSpec((tr, cols), lambda i: (i, 0))
    return pl.pallas_call(
        body, name=name, grid=(rows // tr,),
        in_specs=[blk] * 4, out_specs=[blk] * 3, out_shape=[SDS((rows, cols), F32)] * 3,
        compiler_params=_cp(1))(w, g, m, v)


def _ffn_bwd(h, gain, w_up, cw4, cb4, w_down4, saved, dh, tag, ride_wup=None, scatter_own=False):
    n2, up4, act = saved
    d_w_down = _mm_tn(act, dh[None], f"ffn_dwdown_{tag}")
    ride_gate = _Ride("scatter", [d_w_down.reshape(N_DEV, -1, d_w_down.shape[-1])]) if scatter_own else None
    dup4, dcw4, dcb4, rode = _ffn_act_bwd(up4, cw4, cb4, dh, w_down4, f"ffn_dgate_{tag}", ride_gate)
    if scatter_own:
        (d_w_down,) = rode
    dup = dup4.reshape((8,) + dup4.shape[2:])
    d_w_up, rode_wup = _mm_tn(n2[None], dup, f"ffn_dwup_{tag}", ride_wup), []
    if ride_wup is not None:
        d_w_up, rode_wup = d_w_up
    dh_in, dgain, rode = _mm_norm_bwd(dup, w_up, h, gain, dh, f"ffn_dnorm_{tag}",
                                      _Ride("scatter", [d_w_up]) if scatter_own else None)
    if scatter_own:
        (d_w_up,) = rode
    return dh_in, dgain, d_w_up, d_w_down, dcw4, dcb4, rode_wup


def kernel(x, meta_tokens, mix_norm, ffn_norm, pool_w, pool_scale, kv_norm, w_kv, w_q, w_o, ffn_w_up, ffn_conv_w, ffn_conv_b, ffn_w_down, final_norm, loss_target, m_meta_tokens, m_mix_norm, m_ffn_norm, m_pool_w, m_pool_scale, m_kv_norm, m_w_kv, m_w_q, m_w_o, m_ffn_w_up, m_ffn_conv_w, m_ffn_conv_b, m_ffn_w_down, m_final_norm, v_meta_tokens, v_mix_norm, v_ffn_norm, v_pool_w, v_pool_scale, v_kv_norm, v_w_kv, v_w_q, v_w_o, v_ffn_w_up, v_ffn_conv_w, v_ffn_conv_b, v_ffn_w_down, v_final_norm):
    seq, d = x.shape[1], x.shape[2]
    n_tok = N_META + seq
    lp = -(-n_tok // ROW_TILE) * ROW_TILE
    fc = ffn_w_up.shape[2]
    me = _dev_index(_mesh_pos())

    def rows_of(parts):
        rows = [p.size // d for p in parts]
        return [sum(rows[:k]) for k in range(len(parts) + 1)]

    def bf16_rows(parts):
        return jnp.concatenate([p.reshape(-1, d) for p in parts], axis=0).astype(BF16)

    small_parts = [meta_tokens, pool_scale, ffn_conv_w]
    small_rows = [p.size // 128 for p in small_parts]
    small_pad = -sum(small_rows) % 8
    local_small = jnp.concatenate([p.reshape(-1, 128) for p in small_parts] + [jnp.zeros((small_pad, 128), F32)], axis=0)
    g_pw, wup0, gs = _all_gather([bf16_rows([pool_w]), ffn_w_up[0].astype(BF16), local_small], "gather_first")
    pw = g_pw.reshape(N_DEV, 4, POOL_C // N_DEV, POOL_C).transpose(1, 0, 2, 3).reshape(4, POOL_C, POOL_C)
    early_parts, late_parts = [ffn_w_down[0], w_kv], [w_o, ffn_w_down[1]]
    early_off, late_off = rows_of(early_parts), rows_of(late_parts)
    r0, r1, r2 = small_rows[0], small_rows[0] + small_rows[1], sum(small_rows)
    meta_full = gs[:, :r0].transpose(1, 0, 2).reshape(N_META, d)
    pscale = gs[:, r0:r1].reshape(1, d)
    cw = gs[:, r1:r2].reshape(N_DEV, 2, 3, fc)
    cw4_l = [cw[:, l].reshape(2, 4, 3, fc) for l in range(2)]
    cb4_l = [ffn_conv_b[l].reshape(2, 4, 1, fc) for l in range(2)]

    h0 = jnp.concatenate([meta_full, x[0], jnp.zeros((lp - n_tok, d), F32)], axis=0)
    h1, diff = _pool_fwd(h0, mix_norm[0:1], pw, pscale, "pool_fwd")
    (n2_0,) = _rms_fwd(h1, ffn_norm[0:1], "ffn_norm_0")
    up4_0, act0, (g_early,) = _ffn_up_act(n2_0, wup0.reshape(2, 4, d, fc), cw4_l[0], cb4_l[0], "ffn_up_0",
                                          _Ride("gather_by_chip", [bf16_rows(early_parts)]))
    wdn0 = g_early[:, early_off[0]:early_off[1]].reshape(4, fc, d)
    wkv = g_early[:, early_off[1]:early_off[2]].reshape(N_DEV, d, 2 * d // N_DEV)
    h2, (wq,) = _mm_reduce(act0, wdn0, NN, h1, "ffn_down_0", _Ride("gather", [w_q[0].astype(BF16)]))
    wq = wq.reshape(1, d, d)
    gains_b = jnp.stack([kv_norm, mix_norm[1]], axis=0)
    kvn, n3 = _rms_fwd(h2, gains_b, "attn_norms")
    kv = _mm_group(kvn, wkv, NN, BF16, "kv_proj")
    q = _mm_group(n3, wq, NN, BF16, "q_proj")[0]
    o, (g_late, wup1) = _attn_fwd(
        q, kv, "attn_fwd", _Ride("gather_by_chip", [bf16_rows(late_parts), ffn_w_up[1].astype(BF16)]))
    wo = g_late[:, late_off[0]:late_off[1]].reshape(1, d, d)
    wdn1 = g_late[:, late_off[1]:late_off[2]].reshape(4, fc, d)
    h3 = _mm_reduce(o[None], wo, NN, h2, "o_proj")
    (n2_1,) = _rms_fwd(h3, ffn_norm[1:2], "ffn_norm_1")
    up4_1, act1, _ = _ffn_up_act(n2_1, wup1.reshape(2, 4, d, fc), cw4_l[1], cb4_l[1], "ffn_up_1")
    h4 = _mm_reduce(act1, wdn1, NN, h3, "ffn_down_1")
    target = jnp.pad(loss_target[0], ((N_META, lp - n_tok), (0, 0)))
    dh4, loss_blk, dg_final = _loss_bwd(h4, final_norm[None], target, seq, "loss")
    loss = lax.psum(loss_blk[0, 0], MESH_AXES)

    dh3, dg_ffn1, d_wup1, d_wdn1, dcw4_1, dcb4_1, _ = _ffn_bwd(
        h3, ffn_norm[1:2], wup1, cw4_l[1], cb4_l[1], wdn1, (n2_1, up4_1, act1), dh4, "1")
    d_o = _mm_group(dh3, wo, NT, BF16, "o_proj_dx")[0]
    d_wo = _mm_tn(o[None], dh3[None], "o_proj_dw")
    ride_late = _Ride("scatter", [jnp.concatenate([d_wo.reshape(N_DEV, -1, d), d_wdn1.reshape(N_DEV, -1, d)], axis=1), d_wup1])
    dq, dk, dv, (p_late, p_up1) = _attn_bwd(q, kv, d_o, "attn_bwd", ride_late)
    dn3 = _mm_group(dq, wq, NT, F32, "q_proj_dx")[0]
    d_wq = _mm_tn(n3[None], dq[None], "q_proj_dw")
    dkv = jnp.concatenate([dk, dv], axis=0).astype(BF16)
    dkvn = _mm_reduce(dkv, wkv, NT, None, "kv_proj_dx")
    d_wkv = _mm_tn(kvn[None], dkv, "kv_proj_dw")
    dh2, dg_b = _rms_bwd(h2, gains_b, [dkvn, dn3], dh3, "attn_norms_bwd")
    ride_proj = _Ride("scatter", [jnp.concatenate([d_wkv.reshape(N_DEV, -1, d), d_wq.reshape(N_DEV, -1, d)], axis=1)])

    dh1, dg_ffn0, p_up0, p_dn0, dcw4_0, dcb4_0, (p_proj,) = _ffn_bwd(
        h1, ffn_norm[0:1], wup0, cw4_l[0], cb4_l[0], wdn0, (n2_0, up4_0, act0), dh2, "0", ride_proj, scatter_own=True)
    dh0, d_pw, d_pscale, dg_mix0 = _pool_bwd(h0, mix_norm[0:1], pw, pscale, diff, dh1, "pool_bwd")
    grad_x = dh0[N_META:n_tok][None]
    d_pw8 = d_pw.reshape(4, N_DEV, POOL_C // N_DEV, POOL_C).transpose(1, 0, 2, 3).reshape(N_DEV, -1, d).astype(BF16)
    s_up0, (p_pw,) = _sum_slabs(p_up0, "sum_up0", _Ride("scatter", [d_pw8]))
    s_late, s_up1, s_proj, s_dn0, s_pw = [_sum_slabs(p, "sum_" + n) for p, n in (
        (p_late, "late"), (p_up1, "up1"), (p_proj, "proj"), (p_dn0, "down0"), (p_pw, "pool"))]
    n_kv, n_o = w_kv.size // d, w_o.size // d

    rep_parts = [jnp.concatenate([dg_mix0, dg_b[1:2]], axis=0), jnp.concatenate([dg_ffn0, dg_ffn1], axis=0),
                 dg_b[0:1], dg_final, jnp.stack([dcb4_0.reshape(-1), dcb4_1.reshape(-1)], axis=0)]
    rep_shapes = [mix_norm.shape, ffn_norm.shape, kv_norm.shape, final_norm.shape, ffn_conv_b.shape]
    rep_rows = [p.size // 128 for p in rep_parts]
    d_meta8 = dh0[:N_META].reshape(N_META, N_DEV, d // N_DEV).transpose(1, 0, 2).reshape(N_DEV, -1, 128)
    d_cw8 = jnp.stack([dcw4_0.reshape(N_DEV, 3, fc), dcw4_1.reshape(N_DEV, 3, fc)], axis=1).reshape(N_DEV, -1, 128)
    shard_parts = jnp.concatenate([d_meta8, d_pscale.reshape(N_DEV, 1, 128), d_cw8], axis=1)
    n_rep = sum(rep_rows)
    partial_small = jnp.concatenate([p.reshape(-1, 128) for p in rep_parts] + [shard_parts.reshape(-1, 128)], axis=0)
    g_small = _sum_slabs(_all_gather([partial_small], "gather_vector_grads")[0], "sum_vectors")
    g_rep = [g_small[sum(rep_rows[:k]):sum(rep_rows[:k + 1])].reshape(s) for k, s in enumerate(rep_shapes)]
    g_shard = lax.dynamic_index_in_dim(g_small[n_rep:].reshape(N_DEV, -1, 128), me, 0, keepdims=False)
    g_meta = g_shard[:r0].reshape(meta_tokens.shape)
    g_pscale = g_shard[r0:r1].reshape(pool_scale.shape)
    g_cw = g_shard[r1:r2].reshape(ffn_conv_w.shape)

    grads = {
        "meta_tokens": g_meta, "mix_norm": g_rep[0], "ffn_norm": g_rep[1],
        "pool_w": s_pw.reshape(pool_w.shape), "pool_scale": g_pscale, "kv_norm": g_rep[2],
        "w_kv": s_proj[:n_kv].reshape(w_kv.shape), "w_q": s_proj[n_kv:].reshape(w_q.shape),
        "w_o": s_late[:n_o].reshape(w_o.shape),
        "ffn_w_up": jnp.stack([s_up0, s_up1], axis=0), "ffn_conv_w": g_cw, "ffn_conv_b": g_rep[4],
        "ffn_w_down": jnp.stack([s_dn0, s_late[n_o:]], axis=0), "final_norm": g_rep[3],
    }
    names = list(grads)
    weights = dict(zip(names, [meta_tokens, mix_norm, ffn_norm, pool_w, pool_scale, kv_norm, w_kv, w_q, w_o,
                               ffn_w_up, ffn_conv_w, ffn_conv_b, ffn_w_down, final_norm]))
    mom1 = dict(zip(names, [m_meta_tokens, m_mix_norm, m_ffn_norm, m_pool_w, m_pool_scale, m_kv_norm, m_w_kv, m_w_q,
                            m_w_o, m_ffn_w_up, m_ffn_conv_w, m_ffn_conv_b, m_ffn_w_down, m_final_norm]))
    mom2 = dict(zip(names, [v_meta_tokens, v_mix_norm, v_ffn_norm, v_pool_w, v_pool_scale, v_kv_norm, v_w_kv, v_w_q,
                            v_w_o, v_ffn_w_up, v_ffn_conv_w, v_ffn_conv_b, v_ffn_w_down, v_final_norm]))

    delta, new_m, new_v = {}, {}, {}
    for n in names:
        shape = weights[n].shape
        flat = (-1, shape[-1])
        dl, nm, nv = _adamw(weights[n].reshape(flat), grads[n].reshape(flat), mom1[n].reshape(flat),
                            mom2[n].reshape(flat), "adamw_" + n)
        delta[n], new_m[n], new_v[n] = dl.reshape(shape), nm.reshape(shape), nv.reshape(shape)
    return (loss, grad_x, *[grads[n] for n in names], *[delta[n] for n in names],
            *[new_m[n] for n in names], *[new_v[n] for n in names])
```

```python
import functools
from typing import NamedTuple

import jax
import jax.numpy as jnp
from jax import lax
from jax.experimental import pallas as pl
from jax.experimental.pallas import tpu as pltpu

F32 = jnp.float32
BF16 = jnp.bfloat16
SDS = jax.ShapeDtypeStruct

N_DEV = 8
N_META = 16
HEAD_DIM = 64
HEAD_PAIRS = 8
RMS_EPS = 1e-6
LOG2_E = 1.4426950408889634
POOL_WINDOWS = (2, 4, 8, 16)
POOL_C = 256
POOL_HALO = 16
CONV_HALO = 8
ROW_TILE = 384
MM_ROWS_MAX = 1408
FFN_ROWS_MAX = 704
SUM_ROWS_MAX = 256
SUM_WHOLE_BYTES = 4 << 20
ATT_BLK = 128
ATT_Q = ROW_TILE
ATT_UNROLL = ATT_Q // ATT_BLK
UNDERFLOW_AT = 104.0
VMEM_LIMIT = 56 * 1024 * 1024

ADAM_LR = 0.001
ADAM_B1 = 0.9
ADAM_B2 = 0.999
ADAM_EPS = 1e-08
ADAM_WD = 0.01
ADAM_STEP = 10

MESH_AXES = ("x", "y", "c")
NN = (((1,), (0,)), ((), ()))
NT = (((1,), (1,)), ((), ()))
TN = (((0,), (0,)), ((), ()))


def _cp(n_axes):
    return pltpu.CompilerParams(dimension_semantics=("arbitrary",) * n_axes, vmem_limit_bytes=VMEM_LIMIT)


def _dot(a, b, dims=NN):
    return lax.dot_general(a, b, dims, preferred_element_type=F32)


def _rstd(x):
    return lax.rsqrt(jnp.mean(x * x, axis=-1, keepdims=True) + RMS_EPS)


def _row_tile(rows, cap=512, mult=8):
    if rows <= cap:
        return rows
    best = mult
    for t in range(mult, cap + 1, mult):
        if rows % t == 0:
            best = t
    assert rows % best == 0
    return best


def _rms_fwd(h, gains, name):
    lp, d = h.shape
    k = gains.shape[0]
    tm = ROW_TILE

    def body(h_ref, g_ref, *o_refs):
        x = h_ref[...]
        u = x * _rstd(x)
        for j in range(k):
            o_refs[j][...] = (u * g_ref[j:j + 1, :]).astype(BF16)

    row = pl.BlockSpec((tm, d), lambda i: (i, 0))
    return pl.pallas_call(
        body, name=name, grid=(lp // tm,),
        in_specs=[row, pl.BlockSpec((k, d), lambda i: (0, 0))],
        out_specs=[row] * k, out_shape=[SDS((lp, d), BF16)] * k,
        compiler_params=_cp(1))(h, gains)


def _rms_bwd(h, gains, dns, dh_in, name):
    lp, d = h.shape
    k = gains.shape[0]
    tm = ROW_TILE

    def body(h_ref, g_ref, *refs):
        dn_refs, dh_ref, dho_ref, dg_ref = refs[:k], refs[k], refs[k + 1], refs[k + 2]
        i = pl.program_id(0)
        x = h_ref[...]
        r = _rstd(x)
        u = x * r
        du = jnp.zeros_like(x)
        rows = []
        for j in range(k):
            dn = dn_refs[j][...]
            du = du + dn * g_ref[j:j + 1, :]
            rows.append(jnp.sum(dn * u, axis=0, keepdims=True))
        dx = r * (du - u * jnp.mean(du * u, axis=-1, keepdims=True))
        dho_ref[...] = dh_ref[...] + dx

        @pl.when(i == 0)
        def _():
            for j in range(k):
                dg_ref[j:j + 1, :] = rows[j]

        @pl.when(i > 0)
        def _():
            for j in range(k):
                dg_ref[j:j + 1, :] += rows[j]

    row = pl.BlockSpec((tm, d), lambda i: (i, 0))
    vec = pl.BlockSpec((k, d), lambda i: (0, 0))
    return pl.pallas_call(
        body, name=name, grid=(lp // tm,),
        in_specs=[row, vec] + [row] * k + [row],
        out_specs=[row, vec], out_shape=[SDS((lp, d), F32), SDS((k, d), F32)],
        compiler_params=_cp(1))(h, gains, *dns, dh_in)


def _loss_bwd(h, gain, target, n_real, name):
    lp, d = h.shape
    tm = ROW_TILE

    def body(h_ref, g_ref, t_ref, dh_ref, loss_ref, dg_ref):
        i = pl.program_id(0)
        x = h_ref[...]
        g = g_ref[...]
        r = _rstd(x)
        u = x * r
        row = i * tm + lax.broadcasted_iota(jnp.int32, (tm, 1), 0)
        valid = (row >= N_META) & (row < N_META + n_real)
        e = jnp.where(valid, u * g - t_ref[...], 0.0)
        part = 0.5 * jnp.sum(jnp.sum(e * e, axis=-1, keepdims=True), axis=0, keepdims=True) * (1.0 / d)
        dy = e * (1.0 / d)
        du = dy * g
        dh_ref[...] = r * (du - u * jnp.mean(du * u, axis=-1, keepdims=True))
        dgp = jnp.sum(dy * u, axis=0, keepdims=True)

        @pl.when(i == 0)
        def _():
            loss_ref[...] = jnp.broadcast_to(part, (8, 128))
            dg_ref[...] = dgp

        @pl.when(i > 0)
        def _():
            loss_ref[...] += jnp.broadcast_to(part, (8, 128))
            dg_ref[...] += dgp

    row = pl.BlockSpec((tm, d), lambda i: (i, 0))
    vec = pl.BlockSpec((1, d), lambda i: (0, 0))
    return pl.pallas_call(
        body, name=name, grid=(lp // tm,),
        in_specs=[row, vec, row],
        out_specs=[row, pl.BlockSpec((8, 128), lambda i: (0, 0)), vec],
        out_shape=[SDS((lp, d), F32), SDS((8, 128), F32), SDS((1, d), F32)],
        compiler_params=_cp(1))(h, gain, target)


def _pool_fwd(h, gain, w, scale, name):
    lp, d = h.shape
    tm = ROW_TILE
    hb = POOL_HALO

    def body(h_ref, halo_ref, g_ref, w_ref, s_ref, h1_ref, diff_ref):
        i = pl.program_id(0)
        g = g_ref[...]
        x = h_ref[...]
        n = x * _rstd(x) * g
        xh = halo_ref[...]
        nh = jnp.where(i > 0, xh * _rstd(xh) * g, 0.0)
        cur = jnp.concatenate([nh, n], axis=0)
        pos = i * tm + lax.broadcasted_iota(jnp.int32, (tm, 1), 0)
        for gi, win in enumerate(POOL_WINDOWS):
            if gi > 0:
                cur = cur[:, POOL_C:]
            cur = cur + pltpu.roll(cur, win // 2, 0)
            c0 = gi * POOL_C
            count = jnp.minimum(pos + 1, win).astype(F32)
            diff = cur[hb:, :POOL_C] / count - n[:, c0:c0 + POOL_C]
            diff = diff.astype(BF16)
            y = _dot(diff, w_ref[gi])
            h1_ref[:, c0:c0 + POOL_C] = x[:, c0:c0 + POOL_C] + y * s_ref[:, c0:c0 + POOL_C]
            diff_ref[:, c0:c0 + POOL_C] = diff

    row = pl.BlockSpec((tm, d), lambda i: (i, 0))
    halo = pl.BlockSpec((hb, d), lambda i: (jnp.maximum(i * (tm // hb) - 1, 0), 0))
    vec = pl.BlockSpec((1, d), lambda i: (0, 0))
    return pl.pallas_call(
        body, name=name, grid=(lp // tm,),
        in_specs=[row, halo, vec, pl.BlockSpec(w.shape, lambda i: (0, 0, 0)), vec],
        out_specs=[row, row], out_shape=[SDS((lp, d), F32), SDS((lp, d), BF16)],
        compiler_params=_cp(1))(h, h, gain, w, scale)


def _pool_bwd(h, gain, w, scale, diff, dh1, name):
    lp, d = h.shape
    tm = ROW_TILE
    hb = POOL_HALO
    nblk = lp // tm
    ext = tm + hb

    def body(h_ref, g_ref, w_ref, s_ref, diff_ref, dh_ref, dhn_ref, dh0_ref, dw_ref, ds_ref, dg_ref):
        i = pl.program_id(0)
        g = g_ref[...]
        x = h_ref[...]
        r = _rstd(x)
        u = x * r
        dh = dh_ref[...]
        dhn = jnp.where(i < nblk - 1, dhn_ref[...], 0.0)
        dyp = jnp.concatenate([dh, dhn], axis=0) * s_ref[...]
        pos = i * tm + lax.broadcasted_iota(jnp.int32, (ext, 1), 0)
        dn_parts, dw_parts, ds_parts = [], [], []
        for gi, win in enumerate(POOL_WINDOWS):
            c0 = gi * POOL_C
            wg = w_ref[gi]
            dyp_g = dyp[:, c0:c0 + POOL_C].astype(BF16)
            dd = _dot(dyp_g, wg, NT)
            dfg = diff_ref[:, c0:c0 + POOL_C]
            dw_parts.append(_dot(dfg, dyp_g[:tm], TN))
            ds_parts.append(jnp.sum(dh[:, c0:c0 + POOL_C] * _dot(dfg, wg), axis=0, keepdims=True))
            count = jnp.minimum(pos + 1, win).astype(F32)
            cur = dd / count
            sh = 1
            while sh < win:
                cur = cur + pltpu.roll(cur, ext - sh, 0)
                sh *= 2
            dn_parts.append(cur[:tm] - dd[:tm])
        dn = jnp.concatenate(dn_parts, axis=1)
        du = dn * g
        dh0_ref[...] = dh + r * (du - u * jnp.mean(du * u, axis=-1, keepdims=True))
        dgp = jnp.sum(dn * u, axis=0, keepdims=True)
        dsp = jnp.concatenate(ds_parts, axis=1)

        @pl.when(i == 0)
        def _():
            for gi in range(len(POOL_WINDOWS)):
                dw_ref[gi] = dw_parts[gi]
            ds_ref[...] = dsp
            dg_ref[...] = dgp

        @pl.when(i > 0)
        def _():
            for gi in range(len(POOL_WINDOWS)):
                dw_ref[gi] += dw_parts[gi]
            ds_ref[...] += dsp
            dg_ref[...] += dgp

    row = pl.BlockSpec((tm, d), lambda i: (i, 0))
    nxt = pl.BlockSpec((hb, d), lambda i: (jnp.minimum((i + 1) * (tm // hb), lp // hb - 1), 0))
    vec = pl.BlockSpec((1, d), lambda i: (0, 0))
    wsp = pl.BlockSpec(w.shape, lambda i: (0, 0, 0))
    return pl.pallas_call(
        body, name=name, grid=(nblk,),
        in_specs=[row, vec, wsp, vec, row, row, nxt],
        out_specs=[row, wsp, vec, vec],
        out_shape=[SDS((lp, d), F32), SDS(w.shape, F32), SDS((1, d), F32), SDS((1, d), F32)],
        compiler_params=_cp(1))(h, gain, w, scale, diff, dh1, dh1)


def _ffn_specs(tm, c, lp):
    blk = pl.BlockSpec((2, 1, tm, c), lambda g, i: (0, g, i, 0))
    halo = pl.BlockSpec((2, 1, CONV_HALO, c), lambda g, i: (0, g, jnp.maximum(i * (tm // CONV_HALO) - 1, 0), 0))
    cw = pl.BlockSpec((2, 1, 3, c), lambda g, i: (0, g, 0, 0))
    cb = pl.BlockSpec((2, 1, 1, c), lambda g, i: (0, g, 0, 0))
    return blk, halo, cw, cb


def _ffn_up_act(n2, w_up4, cw4, cb4, name, ride=None):
    lp, d = n2.shape
    _, ng, _, c = w_up4.shape
    tm = _row_tile(lp, FFN_ROWS_MAX, 16)
    hb = CONV_HALO

    def body(a_ref, w_ref, cw_ref, cb_ref, up_ref, act_ref, tail_ref):
        @pl.when(pl.program_id(1) == 0)
        def _():
            tail_ref[...] = jnp.zeros_like(tail_ref)

        a = a_ref[...]
        u = []
        for half in range(2):
            x = _dot(a, w_ref[half, 0])
            up_ref[half, 0] = x
            rows = jnp.concatenate([tail_ref[half], x], axis=0)
            u.append(cb_ref[half, 0] + cw_ref[half, 0, 0:1, :] * pltpu.roll(rows, 2, 0)[hb:]
                     + cw_ref[half, 0, 1:2, :] * pltpu.roll(rows, 1, 0)[hb:] + cw_ref[half, 0, 2:3, :] * x)
            tail_ref[half] = x[tm - hb:]
        gate, val = u
        sig = 1.0 / (1.0 + jnp.exp(-gate))
        act_ref[0] = (gate * sig * val).astype(BF16)

    blk, _, cw, cb = _ffn_specs(tm, c, lp)
    (up4, act), rode = _call_with_ride(
        body, ride, name=name, grid=(ng, lp // tm),
        in_specs=[pl.BlockSpec((tm, d), lambda g, i: (i, 0)), pl.BlockSpec((2, 1, d, c), lambda g, i: (0, g, 0, 0)), cw, cb],
        out_specs=[blk, pl.BlockSpec((1, tm, c), lambda g, i: (g, i, 0))],
        out_shape=[SDS((2, ng, lp, c), F32), SDS((ng, lp, c), BF16)],
        scratch_shapes=[pltpu.VMEM((2, hb, c), F32)], args=[n2, w_up4, cw4, cb4])
    return up4, act, rode


def _ffn_act_bwd(up4, cw4, cb4, dh, w_down4, name, ride=None):
    _, ng, lp, c = up4.shape
    d = dh.shape[1]
    tm = ROW_TILE
    hb = CONV_HALO
    nblk = lp // tm
    ext = tm + hb

    def body(up_ref, prev_ref, next_ref, cw_ref, cb_ref, dh_ref, dhn_ref, wd_ref, dup_ref, dcw_ref, dcb_ref):
        i = pl.program_id(1)
        first = i == 0
        last = i == nblk - 1
        dh_rows = jnp.concatenate([dh_ref[...], jnp.where(last, 0.0, dhn_ref[...])], axis=0)
        da = _dot(dh_rows.astype(BF16), wd_ref[0], NT)
        u, taps = [], []
        for half in range(2):
            rows = jnp.concatenate([jnp.where(first, 0.0, prev_ref[half, 0]), up_ref[half, 0],
                                    jnp.where(last, 0.0, next_ref[half, 0])], axis=0)
            x, xm1, xm2 = rows[hb:], pltpu.roll(rows, 1, 0)[hb:], pltpu.roll(rows, 2, 0)[hb:]
            u.append(cb_ref[half, 0] + cw_ref[half, 0, 0:1, :] * xm2 + cw_ref[half, 0, 1:2, :] * xm1
                     + cw_ref[half, 0, 2:3, :] * x)
            taps.append((xm2, xm1, x))
        gate, val = u
        sig = 1.0 / (1.0 + jnp.exp(-gate))
        dus = (da * val * (sig * (1.0 + gate * (1.0 - sig))), da * (gate * sig))
        sums = []
        for half in range(2):
            du = dus[half]
            dup_ref[half, 0] = (cw_ref[half, 0, 2:3, :] * du[:tm] + cw_ref[half, 0, 1:2, :] * pltpu.roll(du, ext - 1, 0)[:tm]
                                + cw_ref[half, 0, 0:1, :] * pltpu.roll(du, ext - 2, 0)[:tm]).astype(BF16)
            sums.append([jnp.sum(du[:tm] * t[:tm], axis=0, keepdims=True) for t in taps[half]]
                        + [jnp.sum(du[:tm], axis=0, keepdims=True)])

        @pl.when(first)
        def _():
            for half in range(2):
                for k in range(3):
                    dcw_ref[half, 0, k:k + 1, :] = sums[half][k]
                dcb_ref[half, 0] = sums[half][3]

        @pl.when(i > 0)
        def _():
            for half in range(2):
                for k in range(3):
                    dcw_ref[half, 0, k:k + 1, :] += sums[half][k]
                dcb_ref[half, 0] += sums[half][3]

    blk, prev, cw, cb = _ffn_specs(tm, c, lp)

    def next_rows(g, i):
        return jnp.minimum((i + 1) * (tm // hb), lp // hb - 1)

    (dup4, dcw4, dcb4), rode = _call_with_ride(
        body, ride, name=name, grid=(ng, nblk),
        in_specs=[blk, prev, pl.BlockSpec((2, 1, hb, c), lambda g, i: (0, g, next_rows(g, i), 0)), cw, cb,
                  pl.BlockSpec((tm, d), lambda g, i: (i, 0)),
                  pl.BlockSpec((hb, d), lambda g, i: (next_rows(g, i), 0)),
                  pl.BlockSpec((1, c, d), lambda g, i: (g, 0, 0))],
        out_specs=[blk, cw, cb],
        out_shape=[SDS(up4.shape, BF16), SDS(cw4.shape, F32), SDS(cb4.shape, F32)],
        args=[up4, up4, up4, cw4, cb4, dh, dh, w_down4])
    return dup4, dcw4, dcb4, rode


def _mm_tile(rows):
    return _row_tile(rows, MM_ROWS_MAX)


def _mm_group(a, b, dims, out_dtype, name):
    m, k = a.shape
    ng = b.shape[0]
    n = b.shape[2] if dims == NN else b.shape[1]
    tm = _mm_tile(m)

    def body(a_ref, b_ref, o_ref):
        o_ref[0] = _dot(a_ref[...].astype(BF16), b_ref[0], dims).astype(out_dtype)

    return pl.pallas_call(
        body, name=name, grid=(ng, m // tm),
        in_specs=[pl.BlockSpec((tm, k), lambda g, i: (i, 0)),
                  pl.BlockSpec((1,) + b.shape[1:], lambda g, i: (g, 0, 0))],
        out_specs=pl.BlockSpec((1, tm, n), lambda g, i: (g, i, 0)),
        out_shape=SDS((ng, m, n), out_dtype), compiler_params=_cp(2))(a, b)


def _mm_reduce(a, b, dims, res, name, ride=None, gains=None):
    ng, m, k = a.shape
    n = b.shape[2] if dims == NN else b.shape[1]
    tm = _mm_tile(m)
    has_res = res is not None
    n_norm = 0 if gains is None else gains.shape[0]

    def body(a_ref, b_ref, *refs):
        extra, (o_ref, *n_refs, acc_ref) = refs[:has_res + (n_norm > 0)], refs[has_res + (n_norm > 0):]
        g = pl.program_id(1)
        p = _dot(a_ref[0].astype(BF16), b_ref[0], dims)

        @pl.when(g == 0)
        def _():
            acc_ref[...] = p + extra[0][...] if has_res else p

        @pl.when(g > 0)
        def _():
            acc_ref[...] += p

        @pl.when(g == ng - 1)
        def _():
            x = acc_ref[...]
            o_ref[...] = x
            if n_norm:
                u = x * _rstd(x)
                for j in range(n_norm):
                    n_refs[j][...] = (u * extra[-1][j:j + 1, :]).astype(BF16)

    row = pl.BlockSpec((tm, n), lambda i, g: (i, 0))
    (out, *norms), rode = _call_with_ride(
        body, ride, name=name, grid=(m // tm, ng),
        in_specs=[pl.BlockSpec((1, tm, k), lambda i, g: (g, i, 0)),
                  pl.BlockSpec((1,) + b.shape[1:], lambda i, g: (g, 0, 0))] + ([row] if has_res else [])
        + ([pl.BlockSpec((n_norm, n), lambda i, g: (0, 0))] if n_norm else []),
        out_specs=[row] * (1 + n_norm), out_shape=[SDS((m, n), F32)] + [SDS((m, n), BF16)] * n_norm,
        scratch_shapes=[pltpu.VMEM((tm, n), F32)],
        args=[a, b] + ([res] if has_res else []) + ([gains] if n_norm else []))
    results = (out,) + ((norms,) if n_norm else ()) + ((rode,) if ride is not None else ())
    return results[0] if len(results) == 1 else results


def _mm_tn(a, b, name, ride=None):
    ga, m, ka = a.shape
    gb, _, n = b.shape
    ng = max(ga, gb)
    tk = _mm_tile(m)
    nk = m // tk

    def body(a_ref, b_ref, o_ref, acc_ref):
        s = pl.program_id(1)
        p = _dot(a_ref[0].astype(BF16), b_ref[0].astype(BF16), TN)

        @pl.when(s == 0)
        def _():
            acc_ref[...] = p

        @pl.when(s > 0)
        def _():
            acc_ref[...] += p

        @pl.when(s == nk - 1)
        def _():
            o_ref[0] = acc_ref[...].astype(BF16)

    (out,), rode = _call_with_ride(
        body, ride, name=name, grid=(ng, nk),
        in_specs=[pl.BlockSpec((1, tk, ka), (lambda g, s: (g, s, 0)) if ga > 1 else (lambda g, s: (0, s, 0))),
                  pl.BlockSpec((1, tk, n), (lambda g, s: (g, s, 0)) if gb > 1 else (lambda g, s: (0, s, 0)))],
        out_specs=[pl.BlockSpec((1, ka, n), lambda g, s: (g, 0, 0))], out_shape=[SDS((ng, ka, n), BF16)],
        scratch_shapes=[pltpu.VMEM((ka, n), F32)], args=[a, b])
    return out if ride is None else (out, rode)


def _mm_norm_bwd(a, b, h, gain, dh, name, ride=None):
    ng, m, k = a.shape
    d = b.shape[1]
    tm = _row_tile(m, FFN_ROWS_MAX, 16)
    nblk = m // tm

    def body(a_ref, b_ref, h_ref, g_ref, dh_ref, o_ref, dg_ref, acc_ref):
        i, g = pl.program_id(0), pl.program_id(1)
        p = _dot(a_ref[0], b_ref[0], NT)

        @pl.when(g == 0)
        def _():
            acc_ref[...] = p

        @pl.when(g > 0)
        def _():
            acc_ref[...] += p

        @pl.when(g == ng - 1)
        def _():
            dn = acc_ref[...]
            x = h_ref[...]
            r = _rstd(x)
            u = x * r
            du = dn * g_ref[...]
            o_ref[...] = dh_ref[...] + r * (du - u * jnp.mean(du * u, axis=-1, keepdims=True))
            dgp = jnp.sum(dn * u, axis=0, keepdims=True)

            @pl.when(i == 0)
            def _():
                dg_ref[...] = dgp

            @pl.when(i > 0)
            def _():
                dg_ref[...] += dgp

    row = pl.BlockSpec((tm, d), lambda i, g: (i, 0))
    vec = pl.BlockSpec((1, d), lambda i, g: (0, 0))
    (out, dgain), rode = _call_with_ride(
        body, ride, name=name, grid=(nblk, ng),
        in_specs=[pl.BlockSpec((1, tm, k), lambda i, g: (g, i, 0)), pl.BlockSpec((1, d, k), lambda i, g: (g, 0, 0)),
                  row, vec, row],
        out_specs=[row, vec], out_shape=[SDS((m, d), F32), SDS((1, d), F32)],
        scratch_shapes=[pltpu.VMEM((tm, d), F32)], args=[a, b, h, gain, dh])
    return out, dgain, rode


def _pair_tri(kind, sign):
    r = jnp.arange(2 * ATT_BLK)[:, None]
    c = jnp.arange(2 * ATT_BLK)[None, :]
    same = (r < ATT_BLK) == (c < ATT_BLK)
    rel = {"from": r >= c, "before": r < c}[kind]
    return ((same & rel) * sign).astype(BF16)


def _scan_dot(x, tri):
    hi = x.astype(BF16)
    lo = (x - hi.astype(F32)).astype(BF16)
    return _dot(hi, tri) + _dot(lo, tri)


def _split_heads(blk, lane_a):
    zero = jnp.zeros_like(blk)
    return jnp.concatenate([jnp.where(lane_a, blk, zero), jnp.where(lane_a, zero, blk)], axis=0)


def _softplus(z):
    return jnp.maximum(z, 0.0) + jnp.log(1.0 + jnp.exp2(jnp.abs(z) * (-LOG2_E)))


def _visible(qi, j, r0):
    t = qi * ATT_Q + r0 + lax.broadcasted_iota(jnp.int32, (ATT_Q - r0, 2 * ATT_BLK), 0)
    s = j * ATT_BLK + (lax.broadcasted_iota(jnp.int32, (ATT_Q - r0, 2 * ATT_BLK), 1) & (ATT_BLK - 1))
    return s < t


def _add_rows(x, r0, y):
    return x + y if r0 == 0 else jnp.concatenate([x[:r0], x[r0:] + y], axis=0)


def _diag_rows(n):
    return n * ATT_BLK


def _halves(x):
    return x[:, :ATT_BLK], x[:, ATT_BLK:]


def _rowsum(x):
    return jnp.sum(x, axis=1, keepdims=True)


def _still_visible(ca, cb):
    return (jnp.minimum(jnp.min(ca), jnp.min(cb)) < UNDERFLOW_AT).astype(jnp.int32)


def _attn_specs(lp):
    bk = ATT_BLK
    qblk = pl.BlockSpec((ATT_Q, bk), lambda p, i: (i, p))
    kblk = pl.BlockSpec((1, lp, bk), lambda p, i: (p // 2, 0, p % 2))
    vblk = pl.BlockSpec((1, lp, bk), lambda p, i: (HEAD_PAIRS // 2 + p // 2, 0, p % 2))
    tri = pl.BlockSpec((2 * bk, 2 * bk), lambda p, i: (0, 0))
    return qblk, kblk, vblk, tri


def _attn_fwd(q, kv, name, ride=None):
    lp, d = q.shape
    bk = ATT_BLK

    def body(q_ref, k_ref, v_ref, tri_ref, o_ref):
        qi = pl.program_id(1)
        qs = q_ref[...] * (HEAD_DIM ** -0.5)
        lane_a = lax.broadcasted_iota(jnp.int32, (1, bk), 1) < HEAD_DIM
        tri = tri_ref[...]

        def trip(js, carry, masked):
            oacc, ca, cb = carry
            r0s = [_diag_rows(len(js) - 1 - n) if masked else 0 for n in range(len(js))]
            rows = [pl.ds(pl.multiple_of(j * bk, bk), bk) for j in js]
            zs = [_dot(qs[r0:], _split_heads(k_ref[0, r, :], lane_a), NT) for r0, r in zip(r0s, rows)]
            ms = [_softplus(z) for z in zs]
            seen = [_visible(qi, j, r0) if masked else None for j, r0 in zip(js, r0s)]
            if masked:
                ms = [jnp.where(v, m, 0.0) for v, m in zip(seen, ms)]
            ws = [_scan_dot(m, tri) for m in ms]
            for v, r0, r, z, m, w in zip(seen, r0s, rows, zs, ms, ws):
                exa, exb = _halves(z + w)
                a = jnp.concatenate([jnp.exp(exa - ca[r0:]), jnp.exp(exb - cb[r0:])], axis=1)
                if masked:
                    a = jnp.where(v, a, 0.0)
                oacc = _add_rows(oacc, r0, _dot(a.astype(BF16), _split_heads(v_ref[0, r, :], lane_a)))
                ma, mb = _halves(m)
                ca, cb = _add_rows(ca, r0, _rowsum(ma)), _add_rows(cb, r0, _rowsum(mb))
            return oacc, ca, cb

        carry = (jnp.zeros((ATT_Q, bk), F32), jnp.zeros((ATT_Q, 1), F32), jnp.zeros((ATT_Q, 1), F32))
        top = (qi + 1) * ATT_UNROLL - 1
        carry = trip([top - u for u in range(ATT_UNROLL)], carry, True)
        def older(st):
            g, _, *c = st
            c = trip([top - (g + 1) * ATT_UNROLL - u for u in range(ATT_UNROLL)], tuple(c), False)
            return (g + 1, _still_visible(c[1], c[2]), *c)

        _, _, oacc, _, _ = lax.while_loop(
            lambda st: (st[0] < qi) & (st[1] > 0), older, (jnp.int32(0), _still_visible(carry[1], carry[2]), *carry))
        o_ref[...] = oacc.astype(BF16)

    qblk, kblk, vblk, tri = _attn_specs(lp)
    (o,), rode = _call_with_ride(
        body, ride, name=name, grid=(HEAD_PAIRS, lp // ATT_Q), in_specs=[qblk, kblk, vblk, tri],
        out_specs=[qblk], out_shape=[SDS((lp, d), BF16)], args=[q, kv, kv, _pair_tri("from", -1)])
    return o, rode


def _attn_bwd(q, kv, do, name, ride=None):
    lp, d = q.shape
    bk = ATT_BLK
    scale = HEAD_DIM ** -0.5

    def body(q_ref, k_ref, v_ref, do_ref, tri_ref, dq_ref, dk_ref, dv_ref):
        qi = pl.program_id(1)

        @pl.when(qi == 0)
        def _():
            dk_ref[...] = jnp.zeros_like(dk_ref)
            dv_ref[...] = jnp.zeros_like(dv_ref)

        qs = q_ref[...] * scale
        do_blk = do_ref[...]
        lane_a = lax.broadcasted_iota(jnp.int32, (1, bk), 1) < HEAD_DIM
        tri = tri_ref[...]

        def sums(js, carry, masked):
            ca, cb = carry
            for n, j in enumerate(js):
                r0 = _diag_rows(n) if masked else 0
                m = _softplus(_dot(qs[r0:], _split_heads(k_ref[0, pl.ds(pl.multiple_of(j * bk, bk), bk), :], lane_a), NT))
                if masked:
                    m = jnp.where(_visible(qi, j, r0), m, 0.0)
                ma, mb = _halves(m)
                ca, cb = _add_rows(ca, r0, _rowsum(ma)), _add_rows(cb, r0, _rowsum(mb))
            return ca, cb

        def trip(js, carry, masked):
            dq, pa, pb, ea, eb = carry
            r0s = [_diag_rows(n) if masked else 0 for n in range(len(js))]
            rows = [pl.ds(pl.multiple_of(j * bk, bk), bk) for j in js]
            kks = [_split_heads(k_ref[0, r, :], lane_a) for r in rows]
            zs = [_dot(qs[r0:], kk, NT) for r0, kk in zip(r0s, kks)]
            das = [_dot(do_blk[r0:], _split_heads(v_ref[0, r, :], lane_a), NT) for r0, r in zip(r0s, rows)]
            ms = [_softplus(z) for z in zs]
            seen = [_visible(qi, j, r0) if masked else None for j, r0 in zip(js, r0s)]
            if masked:
                ms = [jnp.where(v, m, 0.0) for v, m in zip(seen, ms)]
            xs = [_scan_dot(m, tri) for m in ms]
            es, a_bf = [], []
            for v, r0, z, m, x, da in zip(seen, r0s, zs, ms, xs, das):
                xa, xb = _halves(z + x)
                a = jnp.concatenate([jnp.exp(xa + pa[r0:]), jnp.exp(xb + pb[r0:])], axis=1)
                if masked:
                    a = jnp.where(v, a, 0.0)
                a_bf.append(a.astype(BF16))
                es.append(a * da)
                ma, mb = _halves(m)
                pa, pb = _add_rows(pa, r0, _rowsum(ma)), _add_rows(pb, r0, _rowsum(mb))
            ss = [_dot(e.astype(BF16), tri) for e in es]
            for v, r0, r, kk, z, m, e, s, ab in zip(seen, r0s, rows, kks, zs, ms, es, ss, a_bf):
                sa, sb = _halves(s)
                e_before = jnp.concatenate([sa + ea[r0:], sb + eb[r0:]], axis=1)
                dz = e - jnp.exp(z - m) * (e + e_before)
                if masked:
                    dz = jnp.where(v, dz, 0.0)
                dzb = dz.astype(BF16)
                dq = _add_rows(dq, r0, _dot(dzb, kk))
                rk = _dot(dzb, qs[r0:], TN)
                rv = _dot(ab, do_blk[r0:], TN)
                dk_ref[0, r, :] += jnp.where(lane_a, rk[:bk], rk[bk:])
                dv_ref[0, r, :] += jnp.where(lane_a, rv[:bk], rv[bk:])
                e_a, e_b = _halves(e)
                ea, eb = _add_rows(ea, r0, _rowsum(e_a)), _add_rows(eb, r0, _rowsum(e_b))
            return dq, pa, pb, ea, eb

        zcol = jnp.zeros((ATT_Q, 1), F32)
        diag = [qi * ATT_UNROLL + u for u in range(ATT_UNROLL)]
        def older(st):
            g, _, *c = st
            c = sums([(qi - 1 - g) * ATT_UNROLL + u for u in range(ATT_UNROLL)], tuple(c), False)
            return (g + 1, _still_visible(*c), *c)

        seen = sums(diag, (zcol, zcol), True)
        n_old, _, ta, tb = lax.while_loop(
            lambda st: (st[0] < qi) & (st[1] > 0), older, (jnp.int32(0), _still_visible(*seen), *seen))
        carry = lax.fori_loop(
            0, n_old, lambda g, c: trip([(qi - n_old + g) * ATT_UNROLL + u for u in range(ATT_UNROLL)], c, False),
            (jnp.zeros((ATT_Q, bk), F32), -ta, -tb, zcol, zcol))
        carry = trip(diag, carry, True)
        dq_ref[...] = (carry[0] * scale).astype(BF16)

    qblk, kblk, vblk, tri = _attn_specs(lp)
    (dq, dk, dv), rode = _call_with_ride(
        body, ride, name=name, grid=(HEAD_PAIRS, lp // ATT_Q), in_specs=[qblk, kblk, vblk, qblk, tri],
        out_specs=[qblk, kblk, kblk],
        out_shape=[SDS((lp, d), BF16), SDS((HEAD_PAIRS // 2, lp, 2 * bk), F32), SDS((HEAD_PAIRS // 2, lp, 2 * bk), F32)],
        args=[q, kv, kv, do, _pair_tri("before", 1)])
    return dq, dk, dv, rode


def _mesh_pos():
    return lax.axis_index("x"), lax.axis_index("y"), lax.axis_index("c")


def _flip(pos, r):
    x, y, c = pos
    return (1 - x if r & 4 else x, 1 - y if r & 2 else y, 1 - c if r & 1 else c)


def _dev_index(pos):
    return 4 * pos[0] + 2 * pos[1] + pos[2]


class _Ride(NamedTuple):
    kind: str
    arrays: list


def _ride_arrays(ride):
    return [] if ride is None else ride.arrays


def _ride_args(ride):
    if ride is None:
        return [], [], [], []
    n = len(ride.arrays)
    hbm = pl.BlockSpec(memory_space=pl.ANY)
    shapes = [SDS(x.shape if ride.kind == "scatter" else (N_DEV,) + x.shape, x.dtype) for x in ride.arrays]
    sems = [pltpu.SemaphoreType.DMA((7 * n,)), pltpu.SemaphoreType.DMA((7 * n,)), pltpu.SemaphoreType.DMA((n,))]
    return [hbm] * n, [hbm] * n, shapes, sems


def _riding(body, n_in, n_out, ride, first, middle, last):
    if ride is None:
        return body
    n = len(ride.arrays)

    def wrapped(*refs):
        ins, srcs = refs[:n_in], refs[n_in:n_in + n]
        outs, dsts = refs[n_in + n:n_in + n + n_out], refs[n_in + n + n_out:n_in + 2 * n + n_out]
        scratch, (send_sems, recv_sems, local_sems) = refs[n_in + 2 * n + n_out:-3], refs[-3:]
        me = _mesh_pos()
        mi = _dev_index(me)

        def copy(a, k, src, dst, to):
            return pltpu.make_async_remote_copy(
                src_ref=src, dst_ref=dst, send_sem=send_sems.at[7 * a + k], recv_sem=recv_sems.at[7 * a + k],
                device_id=to, device_id_type=pl.DeviceIdType.MESH)

        local, sends, lands, arrived, passed = [], [], [], [], []
        for a in range(n):
            if ride.kind == "gather_by_chip":
                sibling, others = _flip(me, 1), [_flip(me, 4), _flip(me, 2), _flip(me, 6)]
                local.append(pltpu.make_async_copy(srcs[a], dsts[a].at[mi], local_sems.at[a]))
                sends.append(copy(a, 0, srcs[a], dsts[a].at[mi], sibling))
                lands.append(copy(a, 0, dsts[a].at[_dev_index(sibling)], dsts[a].at[_dev_index(sibling)], me))
                for j, o in enumerate(others):
                    oi, si = _dev_index(o), _dev_index(_flip(o, 1))
                    sends.append(copy(a, 1 + j, srcs[a], dsts[a].at[mi], o))
                    arrived.append(copy(a, 1 + j, dsts[a].at[oi], dsts[a].at[oi], me))
                    passed.append(copy(a, 4 + j, dsts[a].at[oi], dsts[a].at[oi], sibling))
                    lands.append(copy(a, 4 + j, dsts[a].at[si], dsts[a].at[si], me))
                continue
            gather = ride.kind == "gather"
            local.append(pltpu.make_async_copy(srcs[a] if gather else srcs[a].at[mi], dsts[a].at[mi], local_sems.at[a]))
            for r in range(1, N_DEV):
                peer = _flip(me, r)
                pi = _dev_index(peer)
                sends.append(copy(a, r - 1, srcs[a] if gather else srcs[a].at[pi], dsts[a].at[mi], peer))
                lands.append(copy(a, r - 1, dsts[a].at[pi], dsts[a].at[pi], peer))

        @pl.when(first())
        def _():
            for cp in local + sends:
                cp.start()

        if passed:
            @pl.when(middle())
            def _():
                for got, on in zip(arrived, passed):
                    got.wait_recv()
                    on.start()

        body(*ins, *outs, *scratch)

        @pl.when(last())
        def _():
            for cp in lands:
                cp.wait_recv()
            for cp in sends + passed:
                cp.wait_send()
            for cp in local:
                cp.wait()

    return wrapped


def _call_with_ride(body, ride, *, name, grid, in_specs, out_specs, out_shape, args, scratch_shapes=()):
    ride_in, ride_out, ride_shape, ride_sems = _ride_args(ride)
    axes = range(len(grid))
    assert ride is None or ride.kind != "gather_by_chip" or grid[0] >= 4, grid

    def at(step):
        return lambda: functools.reduce(lambda p, k: p & (pl.program_id(k) == step[k]), axes, True)

    ends = [(0,) * len(grid), (3 * grid[0] // 4,) + (0,) * (len(grid) - 1), tuple(g - 1 for g in grid)]
    out = pl.pallas_call(
        _riding(body, len(in_specs), len(out_specs), ride, *map(at, ends)), name=name, grid=grid,
        in_specs=list(in_specs) + ride_in, out_specs=list(out_specs) + ride_out,
        out_shape=list(out_shape) + ride_shape, scratch_shapes=list(scratch_shapes) + ride_sems,
        compiler_params=_cp(len(grid)))(*args, *_ride_arrays(ride))
    return out[:len(out_specs)], out[len(out_specs):]


def _all_gather(xs, name):
    n = len(xs)

    def body(*refs):
        x_refs, out_refs = refs[:n], refs[n:2 * n]
        send_sems, recv_sems, local_sems = refs[2 * n:]
        me = _mesh_pos()
        sibling = _flip(me, 1)
        others = [_flip(me, 4), _flip(me, 2), _flip(me, 6)]

        def copy(a, k, block, to, own=False):
            slab = out_refs[a].at[_dev_index(block)]
            return pltpu.make_async_remote_copy(
                src_ref=x_refs[a] if own else slab, dst_ref=slab,
                send_sem=send_sems.at[7 * a + k], recv_sem=recv_sems.at[7 * a + k],
                device_id=to, device_id_type=pl.DeviceIdType.MESH)

        mine = [pltpu.make_async_copy(x_refs[a], out_refs[a].at[_dev_index(me)], local_sems.at[a]) for a in range(n)]
        first = []
        for a in range(n):
            mine[a].start()
            first += [copy(a, 0, me, sibling, own=True)] + [copy(a, 1 + j, me, o, own=True) for j, o in enumerate(others)]
        for cp in first:
            cp.start()
        passed = []
        for a in range(n):
            for j, o in enumerate(others):
                copy(a, 1 + j, o, me).wait_recv()
                passed.append(copy(a, 4 + j, o, sibling))
                passed[-1].start()
        for a in range(n):
            copy(a, 0, sibling, me).wait_recv()
            for j, o in enumerate(others):
                copy(a, 4 + j, _flip(o, 1), me).wait_recv()
        for cp in first + passed:
            cp.wait_send()
        for cp in mine:
            cp.wait()

    hbm = pl.BlockSpec(memory_space=pl.ANY)
    return pl.pallas_call(
        body, name=name, out_shape=[SDS((N_DEV,) + x.shape, x.dtype) for x in xs],
        in_specs=[hbm] * n, out_specs=[hbm] * n,
        scratch_shapes=[pltpu.SemaphoreType.DMA((7 * n,)), pltpu.SemaphoreType.DMA((7 * n,)), pltpu.SemaphoreType.DMA((n,))],
    )(*xs)


def _sum_slabs(a, name, ride=None):
    n, rows, cols = a.shape
    tr = rows if a.size * a.dtype.itemsize <= SUM_WHOLE_BYTES else _row_tile(rows, SUM_ROWS_MAX, 16)

    def body(a_ref, o_ref):
        acc = a_ref[0].astype(F32)
        for k in range(1, n):
            acc = acc + a_ref[k].astype(F32)
        o_ref[...] = acc

    (out,), rode = _call_with_ride(
        body, ride, name=name, grid=(rows // tr,),
        in_specs=[pl.BlockSpec((n, tr, cols), lambda i: (0, i, 0))],
        out_specs=[pl.BlockSpec((tr, cols), lambda i: (i, 0))], out_shape=[SDS((rows, cols), F32)], args=[a])
    return out if ride is None else (out, rode)


def _adamw(w, g, m, v, name):
    rows, cols = w.shape
    tr = _row_tile(rows, 352)

    def body(w_ref, g_ref, m_ref, v_ref, d_ref, mo_ref, vo_ref):
        g_ = g_ref[...]
        m_ = ADAM_B1 * m_ref[...] + (1.0 - ADAM_B1) * g_
        v_ = ADAM_B2 * v_ref[...] + (1.0 - ADAM_B2) * (g_ * g_)
        m_hat = m_ / (1.0 - ADAM_B1 ** ADAM_STEP)
        v_hat = v_ / (1.0 - ADAM_B2 ** ADAM_STEP)
        d_ref[...] = -ADAM_LR * (m_hat / (jnp.sqrt(v_hat) + ADAM_EPS) + ADAM_WD * w_ref[...])
        mo_ref[...] = m_
        vo_ref[...] = v_

    blk = pl.BlockSpec((tr, cols), lambda i: (i, 0))
    return pl.pallas_call(
        body, name=name, grid=(rows // tr,),
        in_specs=[blk] * 4, out_specs=[blk] * 3, out_shape=[SDS((rows, cols), F32)] * 3,
        compiler_params=_cp(1))(w, g, m, v)


def _ffn_bwd(h, gain, w_up, cw4, cb4, w_down4, saved, dh, tag, ride_wup=None, scatter_own=False):
    n2, up4, act = saved
    d_w_down = _mm_tn(act, dh[None], f"ffn_dwdown_{tag}")
    ride_gate = _Ride("scatter", [d_w_down.reshape(N_DEV, -1, d_w_down.shape[-1])]) if scatter_own else None
    dup4, dcw4, dcb4, rode = _ffn_act_bwd(up4, cw4, cb4, dh, w_down4, f"ffn_dgate_{tag}", ride_gate)
    if scatter_own:
        (d_w_down,) = rode
    dup = dup4.reshape((8,) + dup4.shape[2:])
    d_w_up, rode_wup = _mm_tn(n2[None], dup, f"ffn_dwup_{tag}", ride_wup), []
    if ride_wup is not None:
        d_w_up, rode_wup = d_w_up
    dh_in, dgain, rode = _mm_norm_bwd(dup, w_up, h, gain, dh, f"ffn_dnorm_{tag}",
                                      _Ride("scatter", [d_w_up]) if scatter_own else None)
    if scatter_own:
        (d_w_up,) = rode
    return dh_in, dgain, d_w_up, d_w_down, dcw4, dcb4, rode_wup


def kernel(x, meta_tokens, mix_norm, ffn_norm, pool_w, pool_scale, kv_norm, w_kv, w_q, w_o, ffn_w_up, ffn_conv_w, ffn_conv_b, ffn_w_down, final_norm, loss_target, m_meta_tokens, m_mix_norm, m_ffn_norm, m_pool_w, m_pool_scale, m_kv_norm, m_w_kv, m_w_q, m_w_o, m_ffn_w_up, m_ffn_conv_w, m_ffn_conv_b, m_ffn_w_down, m_final_norm, v_meta_tokens, v_mix_norm, v_ffn_norm, v_pool_w, v_pool_scale, v_kv_norm, v_w_kv, v_w_q, v_w_o, v_ffn_w_up, v_ffn_conv_w, v_ffn_conv_b, v_ffn_w_down, v_final_norm):
    seq, d = x.shape[1], x.shape[2]
    n_tok = N_META + seq
    lp = -(-n_tok // ROW_TILE) * ROW_TILE
    fc = ffn_w_up.shape[2]
    me = _dev_index(_mesh_pos())

    def rows_of(parts):
        rows = [p.size // d for p in parts]
        return [sum(rows[:k]) for k in range(len(parts) + 1)]

    def bf16_rows(parts):
        return jnp.concatenate([p.reshape(-1, d) for p in parts], axis=0).astype(BF16)

    small_parts = [meta_tokens, pool_scale, ffn_conv_w]
    small_rows = [p.size // 128 for p in small_parts]
    small_pad = -sum(small_rows) % 8
    local_small = jnp.concatenate([p.reshape(-1, 128) for p in small_parts] + [jnp.zeros((small_pad, 128), F32)], axis=0)
    g_pw, wup0, gs = _all_gather([bf16_rows([pool_w]), ffn_w_up[0].astype(BF16), local_small], "gather_first")
    pw = g_pw.reshape(N_DEV, 4, POOL_C // N_DEV, POOL_C).transpose(1, 0, 2, 3).reshape(4, POOL_C, POOL_C)
    early_parts, late_parts = [ffn_w_down[0], w_kv], [w_o, ffn_w_down[1]]
    early_off, late_off = rows_of(early_parts), rows_of(late_parts)
    r0, r1, r2 = small_rows[0], small_rows[0] + small_rows[1], sum(small_rows)
    meta_full = gs[:, :r0].transpose(1, 0, 2).reshape(N_META, d)
    pscale = gs[:, r0:r1].reshape(1, d)
    cw = gs[:, r1:r2].reshape(N_DEV, 2, 3, fc)
    cw4_l = [cw[:, l].reshape(2, 4, 3, fc) for l in range(2)]
    cb4_l = [ffn_conv_b[l].reshape(2, 4, 1, fc) for l in range(2)]

    h0 = jnp.concatenate([meta_full, x[0], jnp.zeros((lp - n_tok, d), F32)], axis=0)
    h1, diff = _pool_fwd(h0, mix_norm[0:1], pw, pscale, "pool_fwd")
    (n2_0,) = _rms_fwd(h1, ffn_norm[0:1], "ffn_norm_0")
    up4_0, act0, (g_early,) = _ffn_up_act(n2_0, wup0.reshape(2, 4, d, fc), cw4_l[0], cb4_l[0], "ffn_up_0",
                                          _Ride("gather_by_chip", [bf16_rows(early_parts)]))
    wdn0 = g_early[:, early_off[0]:early_off[1]].reshape(4, fc, d)
    wkv = g_early[:, early_off[1]:early_off[2]].reshape(N_DEV, d, 2 * d // N_DEV)
    gains_b = jnp.stack([kv_norm, mix_norm[1]], axis=0)
    h2, (kvn, n3), (wq,) = _mm_reduce(act0, wdn0, NN, h1, "ffn_down_0", _Ride("gather", [w_q[0].astype(BF16)]), gains_b)
    wq = wq.reshape(1, d, d)
    kv = _mm_group(kvn, wkv, NN, BF16, "kv_proj")
    q = _mm_group(n3, wq, NN, BF16, "q_proj")[0]
    o, (g_late, wup1) = _attn_fwd(
        q, kv, "attn_fwd", _Ride("gather_by_chip", [bf16_rows(late_parts), ffn_w_up[1].astype(BF16)]))
    wo = g_late[:, late_off[0]:late_off[1]].reshape(1, d, d)
    wdn1 = g_late[:, late_off[1]:late_off[2]].reshape(4, fc, d)
    h3, (n2_1,) = _mm_reduce(o[None], wo, NN, h2, "o_proj", gains=ffn_norm[1:2])
    up4_1, act1, _ = _ffn_up_act(n2_1, wup1.reshape(2, 4, d, fc), cw4_l[1], cb4_l[1], "ffn_up_1")
    h4 = _mm_reduce(act1, wdn1, NN, h3, "ffn_down_1")
    target = jnp.pad(loss_target[0], ((N_META, lp - n_tok), (0, 0)))
    dh4, loss_blk, dg_final = _loss_bwd(h4, final_norm[None], target, seq, "loss")
    loss = lax.psum(loss_blk[0, 0], MESH_AXES)

    dh3, dg_ffn1, d_wup1, d_wdn1, dcw4_1, dcb4_1, _ = _ffn_bwd(
        h3, ffn_norm[1:2], wup1, cw4_l[1], cb4_l[1], wdn1, (n2_1, up4_1, act1), dh4, "1")
    d_o = _mm_group(dh3, wo, NT, BF16, "o_proj_dx")[0]
    d_wo = _mm_tn(o[None], dh3[None], "o_proj_dw")
    ride_late = _Ride("scatter", [jnp.concatenate([d_wo.reshape(N_DEV, -1, d), d_wdn1.reshape(N_DEV, -1, d)], axis=1), d_wup1])
    dq, dk, dv, (p_late, p_up1) = _attn_bwd(q, kv, d_o, "attn_bwd", ride_late)
    dn3 = _mm_group(dq, wq, NT, F32, "q_proj_dx")[0]
    d_wq = _mm_tn(n3[None], dq[None], "q_proj_dw")
    dkv = jnp.concatenate([dk, dv], axis=0).astype(BF16)
    dkvn = _mm_reduce(dkv, wkv, NT, None, "kv_proj_dx")
    d_wkv = _mm_tn(kvn[None], dkv, "kv_proj_dw")
    dh2, dg_b = _rms_bwd(h2, gains_b, [dkvn, dn3], dh3, "attn_norms_bwd")
    ride_proj = _Ride("scatter", [jnp.concatenate([d_wkv.reshape(N_DEV, -1, d), d_wq.reshape(N_DEV, -1, d)], axis=1)])

    dh1, dg_ffn0, p_up0, p_dn0, dcw4_0, dcb4_0, (p_proj,) = _ffn_bwd(
        h1, ffn_norm[0:1], wup0, cw4_l[0], cb4_l[0], wdn0, (n2_0, up4_0, act0), dh2, "0", ride_proj, scatter_own=True)
    dh0, d_pw, d_pscale, dg_mix0 = _pool_bwd(h0, mix_norm[0:1], pw, pscale, diff, dh1, "pool_bwd")
    grad_x = dh0[N_META:n_tok][None]
    d_pw8 = d_pw.reshape(4, N_DEV, POOL_C // N_DEV, POOL_C).transpose(1, 0, 2, 3).reshape(N_DEV, -1, d).astype(BF16)
    s_up0, (p_pw,) = _sum_slabs(p_up0, "sum_up0", _Ride("scatter", [d_pw8]))
    s_late, s_up1, s_proj, s_dn0, s_pw = [_sum_slabs(p, "sum_" + n) for p, n in (
        (p_late, "late"), (p_up1, "up1"), (p_proj, "proj"), (p_dn0, "down0"), (p_pw, "pool"))]
    n_kv, n_o = w_kv.size // d, w_o.size // d

    rep_parts = [jnp.concatenate([dg_mix0, dg_b[1:2]], axis=0), jnp.concatenate([dg_ffn0, dg_ffn1], axis=0),
                 dg_b[0:1], dg_final, jnp.stack([dcb4_0.reshape(-1), dcb4_1.reshape(-1)], axis=0)]
    rep_shapes = [mix_norm.shape, ffn_norm.shape, kv_norm.shape, final_norm.shape, ffn_conv_b.shape]
    rep_rows = [p.size // 128 for p in rep_parts]
    d_meta8 = dh0[:N_META].reshape(N_META, N_DEV, d // N_DEV).transpose(1, 0, 2).reshape(N_DEV, -1, 128)
    d_cw8 = jnp.stack([dcw4_0.reshape(N_DEV, 3, fc), dcw4_1.reshape(N_DEV, 3, fc)], axis=1).reshape(N_DEV, -1, 128)
    shard_parts = jnp.concatenate([d_meta8, d_pscale.reshape(N_DEV, 1, 128), d_cw8], axis=1)
    n_rep = sum(rep_rows)
    partial_small = jnp.concatenate([p.reshape(-1, 128) for p in rep_parts] + [shard_parts.reshape(-1, 128)], axis=0)
    g_small = _sum_slabs(_all_gather([partial_small], "gather_vector_grads")[0], "sum_vectors")
    g_rep = [g_small[sum(rep_rows[:k]):sum(rep_rows[:k + 1])].reshape(s) for k, s in enumerate(rep_shapes)]
    g_shard = lax.dynamic_index_in_dim(g_small[n_rep:].reshape(N_DEV, -1, 128), me, 0, keepdims=False)
    g_meta = g_shard[:r0].reshape(meta_tokens.shape)
    g_pscale = g_shard[r0:r1].reshape(pool_scale.shape)
    g_cw = g_shard[r1:r2].reshape(ffn_conv_w.shape)

    grads = {
        "meta_tokens": g_meta, "mix_norm": g_rep[0], "ffn_norm": g_rep[1],
        "pool_w": s_pw.reshape(pool_w.shape), "pool_scale": g_pscale, "kv_norm": g_rep[2],
        "w_kv": s_proj[:n_kv].reshape(w_kv.shape), "w_q": s_proj[n_kv:].reshape(w_q.shape),
        "w_o": s_late[:n_o].reshape(w_o.shape),
        "ffn_w_up": jnp.stack([s_up0, s_up1], axis=0), "ffn_conv_w": g_cw, "ffn_conv_b": g_rep[4],
        "ffn_w_down": jnp.stack([s_dn0, s_late[n_o:]], axis=0), "final_norm": g_rep[3],
    }
    names = list(grads)
    weights = dict(zip(names, [meta_tokens, mix_norm, ffn_norm, pool_w, pool_scale, kv_norm, w_kv, w_q, w_o,
                               ffn_w_up, ffn_conv_w, ffn_conv_b, ffn_w_down, final_norm]))
    mom1 = dict(zip(names, [m_meta_tokens, m_mix_norm, m_ffn_norm, m_pool_w, m_pool_scale, m_kv_norm, m_w_kv, m_w_q,
                            m_w_o, m_ffn_w_up, m_ffn_conv_w, m_ffn_conv_b, m_ffn_w_down, m_final_norm]))
    mom2 = dict(zip(names, [v_meta_tokens, v_mix_norm, v_ffn_norm, v_pool_w, v_pool_scale, v_kv_norm, v_w_kv, v_w_q,
                            v_w_o, v_ffn_w_up, v_ffn_conv_w, v_ffn_conv_b, v_ffn_w_down, v_final_norm]))

    delta, new_m, new_v = {}, {}, {}
    for n in names:
        shape = weights[n].shape
        flat = (-1, shape[-1])
        dl, nm, nv = _adamw(weights[n].reshape(flat), grads[n].reshape(flat), mom1[n].reshape(flat),
                            mom2[n].reshape(flat), "adamw_" + n)
        delta[n], new_m[n], new_v[n] = dl.reshape(shape), nm.reshape(shape), nv.reshape(shape)
    return (loss, grad_x, *[grads[n] for n in names], *[delta[n] for n in names],
            *[new_m[n] for n in names], *[new_v[n] for n in names])
```

```python
import functools
from typing import NamedTuple

import jax
import jax.numpy as jnp
from jax import lax
from jax.experimental import pallas as pl
from jax.experimental.pallas import tpu as pltpu

F32 = jnp.float32
BF16 = jnp.bfloat16
SDS = jax.ShapeDtypeStruct

N_DEV = 8
N_META = 16
HEAD_DIM = 64
HEAD_PAIRS = 8
RMS_EPS = 1e-6
LOG2_E = 1.4426950408889634
POOL_WINDOWS = (2, 4, 8, 16)
POOL_C = 256
POOL_HALO = 16
CONV_HALO = 8
ROW_TILE = 384
MM_ROWS_MAX = 1408
FFN_ROWS_MAX = 704
SUM_ROWS_MAX = 256
SUM_WHOLE_BYTES = 4 << 20
ATT_BLK = 128
ATT_Q = ROW_TILE
ATT_UNROLL = ATT_Q // ATT_BLK
UNDERFLOW_AT = 104.0
VMEM_LIMIT = 56 * 1024 * 1024

ADAM_LR = 0.001
ADAM_B1 = 0.9
ADAM_B2 = 0.999
ADAM_EPS = 1e-08
ADAM_WD = 0.01
ADAM_STEP = 10

MESH_AXES = ("x", "y", "c")
NN = (((1,), (0,)), ((), ()))
NT = (((1,), (1,)), ((), ()))
TN = (((0,), (0,)), ((), ()))


def _cp(n_axes):
    return pltpu.CompilerParams(dimension_semantics=("arbitrary",) * n_axes, vmem_limit_bytes=VMEM_LIMIT)


def _dot(a, b, dims=NN):
    return lax.dot_general(a, b, dims, preferred_element_type=F32)


def _rstd(x):
    return lax.rsqrt(jnp.mean(x * x, axis=-1, keepdims=True) + RMS_EPS)


def _row_tile(rows, cap=512, mult=8):
    if rows <= cap:
        return rows
    best = mult
    for t in range(mult, cap + 1, mult):
        if rows % t == 0:
            best = t
    assert rows % best == 0
    return best


def _rms_fwd(h, gains, name):
    lp, d = h.shape
    k = gains.shape[0]
    tm = ROW_TILE

    def body(h_ref, g_ref, *o_refs):
        x = h_ref[...]
        u = x * _rstd(x)
        for j in range(k):
            o_refs[j][...] = (u * g_ref[j:j + 1, :]).astype(BF16)

    row = pl.BlockSpec((tm, d), lambda i: (i, 0))
    return pl.pallas_call(
        body, name=name, grid=(lp // tm,),
        in_specs=[row, pl.BlockSpec((k, d), lambda i: (0, 0))],
        out_specs=[row] * k, out_shape=[SDS((lp, d), BF16)] * k,
        compiler_params=_cp(1))(h, gains)


def _loss_bwd(h, gain, target, n_real, name):
    lp, d = h.shape
    tm = ROW_TILE

    def body(h_ref, g_ref, t_ref, dh_ref, loss_ref, dg_ref):
        i = pl.program_id(0)
        x = h_ref[...]
        g = g_ref[...]
        r = _rstd(x)
        u = x * r
        row = i * tm + lax.broadcasted_iota(jnp.int32, (tm, 1), 0)
        valid = (row >= N_META) & (row < N_META + n_real)
        e = jnp.where(valid, u * g - t_ref[...], 0.0)
        part = 0.5 * jnp.sum(jnp.sum(e * e, axis=-1, keepdims=True), axis=0, keepdims=True) * (1.0 / d)
        dy = e * (1.0 / d)
        du = dy * g
        dh_ref[...] = r * (du - u * jnp.mean(du * u, axis=-1, keepdims=True))
        dgp = jnp.sum(dy * u, axis=0, keepdims=True)

        @pl.when(i == 0)
        def _():
            loss_ref[...] = jnp.broadcast_to(part, (8, 128))
            dg_ref[...] = dgp

        @pl.when(i > 0)
        def _():
            loss_ref[...] += jnp.broadcast_to(part, (8, 128))
            dg_ref[...] += dgp

    row = pl.BlockSpec((tm, d), lambda i: (i, 0))
    vec = pl.BlockSpec((1, d), lambda i: (0, 0))
    return pl.pallas_call(
        body, name=name, grid=(lp // tm,),
        in_specs=[row, vec, row],
        out_specs=[row, pl.BlockSpec((8, 128), lambda i: (0, 0)), vec],
        out_shape=[SDS((lp, d), F32), SDS((8, 128), F32), SDS((1, d), F32)],
        compiler_params=_cp(1))(h, gain, target)


def _pool_fwd(h, gain, w, scale, name):
    lp, d = h.shape
    tm = ROW_TILE
    hb = POOL_HALO

    def body(h_ref, halo_ref, g_ref, w_ref, s_ref, h1_ref, diff_ref):
        i = pl.program_id(0)
        g = g_ref[...]
        x = h_ref[...]
        n = x * _rstd(x) * g
        xh = halo_ref[...]
        nh = jnp.where(i > 0, xh * _rstd(xh) * g, 0.0)
        cur = jnp.concatenate([nh, n], axis=0)
        pos = i * tm + lax.broadcasted_iota(jnp.int32, (tm, 1), 0)
        for gi, win in enumerate(POOL_WINDOWS):
            if gi > 0:
                cur = cur[:, POOL_C:]
            cur = cur + pltpu.roll(cur, win // 2, 0)
            c0 = gi * POOL_C
            count = jnp.minimum(pos + 1, win).astype(F32)
            diff = cur[hb:, :POOL_C] / count - n[:, c0:c0 + POOL_C]
            diff = diff.astype(BF16)
            y = _dot(diff, w_ref[gi])
            h1_ref[:, c0:c0 + POOL_C] = x[:, c0:c0 + POOL_C] + y * s_ref[:, c0:c0 + POOL_C]
            diff_ref[:, c0:c0 + POOL_C] = diff

    row = pl.BlockSpec((tm, d), lambda i: (i, 0))
    halo = pl.BlockSpec((hb, d), lambda i: (jnp.maximum(i * (tm // hb) - 1, 0), 0))
    vec = pl.BlockSpec((1, d), lambda i: (0, 0))
    return pl.pallas_call(
        body, name=name, grid=(lp // tm,),
        in_specs=[row, halo, vec, pl.BlockSpec(w.shape, lambda i: (0, 0, 0)), vec],
        out_specs=[row, row], out_shape=[SDS((lp, d), F32), SDS((lp, d), BF16)],
        compiler_params=_cp(1))(h, h, gain, w, scale)


def _pool_bwd(h, gain, w, scale, diff, dh1, name):
    lp, d = h.shape
    tm = ROW_TILE
    hb = POOL_HALO
    nblk = lp // tm
    ext = tm + hb

    def body(h_ref, g_ref, w_ref, s_ref, diff_ref, dh_ref, dhn_ref, dh0_ref, dw_ref, ds_ref, dg_ref):
        i = pl.program_id(0)
        g = g_ref[...]
        x = h_ref[...]
        r = _rstd(x)
        u = x * r
        dh = dh_ref[...]
        dhn = jnp.where(i < nblk - 1, dhn_ref[...], 0.0)
        dyp = jnp.concatenate([dh, dhn], axis=0) * s_ref[...]
        pos = i * tm + lax.broadcasted_iota(jnp.int32, (ext, 1), 0)
        dn_parts, dw_parts, ds_parts = [], [], []
        for gi, win in enumerate(POOL_WINDOWS):
            c0 = gi * POOL_C
            wg = w_ref[gi]
            dyp_g = dyp[:, c0:c0 + POOL_C].astype(BF16)
            dd = _dot(dyp_g, wg, NT)
            dfg = diff_ref[:, c0:c0 + POOL_C]
            dw_parts.append(_dot(dfg, dyp_g[:tm], TN))
            ds_parts.append(jnp.sum(dh[:, c0:c0 + POOL_C] * _dot(dfg, wg), axis=0, keepdims=True))
            count = jnp.minimum(pos + 1, win).astype(F32)
            cur = dd / count
            sh = 1
            while sh < win:
                cur = cur + pltpu.roll(cur, ext - sh, 0)
                sh *= 2
            dn_parts.append(cur[:tm] - dd[:tm])
        dn = jnp.concatenate(dn_parts, axis=1)
        du = dn * g
        dh0_ref[...] = dh + r * (du - u * jnp.mean(du * u, axis=-1, keepdims=True))
        dgp = jnp.sum(dn * u, axis=0, keepdims=True)
        dsp = jnp.concatenate(ds_parts, axis=1)

        @pl.when(i == 0)
        def _():
            for gi in range(len(POOL_WINDOWS)):
                dw_ref[gi] = dw_parts[gi]
            ds_ref[...] = dsp
            dg_ref[...] = dgp

        @pl.when(i > 0)
        def _():
            for gi in range(len(POOL_WINDOWS)):
                dw_ref[gi] += dw_parts[gi]
            ds_ref[...] += dsp
            dg_ref[...] += dgp

    row = pl.BlockSpec((tm, d), lambda i: (i, 0))
    nxt = pl.BlockSpec((hb, d), lambda i: (jnp.minimum((i + 1) * (tm // hb), lp // hb - 1), 0))
    vec = pl.BlockSpec((1, d), lambda i: (0, 0))
    wsp = pl.BlockSpec(w.shape, lambda i: (0, 0, 0))
    return pl.pallas_call(
        body, name=name, grid=(nblk,),
        in_specs=[row, vec, wsp, vec, row, row, nxt],
        out_specs=[row, wsp, vec, vec],
        out_shape=[SDS((lp, d), F32), SDS(w.shape, F32), SDS((1, d), F32), SDS((1, d), F32)],
        compiler_params=_cp(1))(h, gain, w, scale, diff, dh1, dh1)


def _ffn_specs(tm, c, lp):
    blk = pl.BlockSpec((2, 1, tm, c), lambda g, i: (0, g, i, 0))
    halo = pl.BlockSpec((2, 1, CONV_HALO, c), lambda g, i: (0, g, jnp.maximum(i * (tm // CONV_HALO) - 1, 0), 0))
    cw = pl.BlockSpec((2, 1, 3, c), lambda g, i: (0, g, 0, 0))
    cb = pl.BlockSpec((2, 1, 1, c), lambda g, i: (0, g, 0, 0))
    return blk, halo, cw, cb


def _ffn_up_act(n2, w_up4, cw4, cb4, name, ride=None):
    lp, d = n2.shape
    _, ng, _, c = w_up4.shape
    tm = _row_tile(lp, FFN_ROWS_MAX, 16)
    hb = CONV_HALO

    def body(a_ref, w_ref, cw_ref, cb_ref, up_ref, act_ref, tail_ref):
        @pl.when(pl.program_id(1) == 0)
        def _():
            tail_ref[...] = jnp.zeros_like(tail_ref)

        a = a_ref[...]
        u = []
        for half in range(2):
            x = _dot(a, w_ref[half, 0])
            up_ref[half, 0] = x
            rows = jnp.concatenate([tail_ref[half], x], axis=0)
            u.append(cb_ref[half, 0] + cw_ref[half, 0, 0:1, :] * pltpu.roll(rows, 2, 0)[hb:]
                     + cw_ref[half, 0, 1:2, :] * pltpu.roll(rows, 1, 0)[hb:] + cw_ref[half, 0, 2:3, :] * x)
            tail_ref[half] = x[tm - hb:]
        gate, val = u
        sig = 1.0 / (1.0 + jnp.exp(-gate))
        act_ref[0] = (gate * sig * val).astype(BF16)

    blk, _, cw, cb = _ffn_specs(tm, c, lp)
    (up4, act), rode = _call_with_ride(
        body, ride, name=name, grid=(ng, lp // tm),
        in_specs=[pl.BlockSpec((tm, d), lambda g, i: (i, 0)), pl.BlockSpec((2, 1, d, c), lambda g, i: (0, g, 0, 0)), cw, cb],
        out_specs=[blk, pl.BlockSpec((1, tm, c), lambda g, i: (g, i, 0))],
        out_shape=[SDS((2, ng, lp, c), F32), SDS((ng, lp, c), BF16)],
        scratch_shapes=[pltpu.VMEM((2, hb, c), F32)], args=[n2, w_up4, cw4, cb4])
    return up4, act, rode


def _ffn_act_bwd(up4, cw4, cb4, dh, w_down4, name, ride=None):
    _, ng, lp, c = up4.shape
    d = dh.shape[1]
    tm = ROW_TILE
    hb = CONV_HALO
    nblk = lp // tm
    ext = tm + hb

    def body(up_ref, prev_ref, next_ref, cw_ref, cb_ref, dh_ref, dhn_ref, wd_ref, dup_ref, dcw_ref, dcb_ref):
        i = pl.program_id(1)
        first = i == 0
        last = i == nblk - 1
        dh_rows = jnp.concatenate([dh_ref[...], jnp.where(last, 0.0, dhn_ref[...])], axis=0)
        da = _dot(dh_rows.astype(BF16), wd_ref[0], NT)
        u, taps = [], []
        for half in range(2):
            rows = jnp.concatenate([jnp.where(first, 0.0, prev_ref[half, 0]), up_ref[half, 0],
                                    jnp.where(last, 0.0, next_ref[half, 0])], axis=0)
            x, xm1, xm2 = rows[hb:], pltpu.roll(rows, 1, 0)[hb:], pltpu.roll(rows, 2, 0)[hb:]
            u.append(cb_ref[half, 0] + cw_ref[half, 0, 0:1, :] * xm2 + cw_ref[half, 0, 1:2, :] * xm1
                     + cw_ref[half, 0, 2:3, :] * x)
            taps.append((xm2, xm1, x))
        gate, val = u
        sig = 1.0 / (1.0 + jnp.exp(-gate))
        dus = (da * val * (sig * (1.0 + gate * (1.0 - sig))), da * (gate * sig))
        sums = []
        for half in range(2):
            du = dus[half]
            dup_ref[half, 0] = (cw_ref[half, 0, 2:3, :] * du[:tm] + cw_ref[half, 0, 1:2, :] * pltpu.roll(du, ext - 1, 0)[:tm]
                                + cw_ref[half, 0, 0:1, :] * pltpu.roll(du, ext - 2, 0)[:tm]).astype(BF16)
            sums.append([jnp.sum(du[:tm] * t[:tm], axis=0, keepdims=True) for t in taps[half]]
                        + [jnp.sum(du[:tm], axis=0, keepdims=True)])

        @pl.when(first)
        def _():
            for half in range(2):
                for k in range(3):
                    dcw_ref[half, 0, k:k + 1, :] = sums[half][k]
                dcb_ref[half, 0] = sums[half][3]

        @pl.when(i > 0)
        def _():
            for half in range(2):
                for k in range(3):
                    dcw_ref[half, 0, k:k + 1, :] += sums[half][k]
                dcb_ref[half, 0] += sums[half][3]

    blk, prev, cw, cb = _ffn_specs(tm, c, lp)

    def next_rows(g, i):
        return jnp.minimum((i + 1) * (tm // hb), lp // hb - 1)

    (dup4, dcw4, dcb4), rode = _call_with_ride(
        body, ride, name=name, grid=(ng, nblk),
        in_specs=[blk, prev, pl.BlockSpec((2, 1, hb, c), lambda g, i: (0, g, next_rows(g, i), 0)), cw, cb,
                  pl.BlockSpec((tm, d), lambda g, i: (i, 0)),
                  pl.BlockSpec((hb, d), lambda g, i: (next_rows(g, i), 0)),
                  pl.BlockSpec((1, c, d), lambda g, i: (g, 0, 0))],
        out_specs=[blk, cw, cb],
        out_shape=[SDS(up4.shape, BF16), SDS(cw4.shape, F32), SDS(cb4.shape, F32)],
        args=[up4, up4, up4, cw4, cb4, dh, dh, w_down4])
    return dup4, dcw4, dcb4, rode


def _mm_tile(rows):
    return _row_tile(rows, MM_ROWS_MAX)


def _mm_group(a, b, dims, out_dtype, name):
    m, k = a.shape
    ng = b.shape[0]
    n = b.shape[2] if dims == NN else b.shape[1]
    tm = _mm_tile(m)

    def body(a_ref, b_ref, o_ref):
        o_ref[0] = _dot(a_ref[...].astype(BF16), b_ref[0], dims).astype(out_dtype)

    return pl.pallas_call(
        body, name=name, grid=(ng, m // tm),
        in_specs=[pl.BlockSpec((tm, k), lambda g, i: (i, 0)),
                  pl.BlockSpec((1,) + b.shape[1:], lambda g, i: (g, 0, 0))],
        out_specs=pl.BlockSpec((1, tm, n), lambda g, i: (g, i, 0)),
        out_shape=SDS((ng, m, n), out_dtype), compiler_params=_cp(2))(a, b)


def _mm_reduce(a, b, dims, res, name, ride=None, gains=None):
    ng, m, k = a.shape
    n = b.shape[2] if dims == NN else b.shape[1]
    tm = _mm_tile(m)
    has_res = res is not None
    n_norm = 0 if gains is None else gains.shape[0]

    def body(a_ref, b_ref, *refs):
        extra, (o_ref, *n_refs, acc_ref) = refs[:has_res + (n_norm > 0)], refs[has_res + (n_norm > 0):]
        g = pl.program_id(1)
        p = _dot(a_ref[0].astype(BF16), b_ref[0], dims)

        @pl.when(g == 0)
        def _():
            acc_ref[...] = p + extra[0][...] if has_res else p

        @pl.when(g > 0)
        def _():
            acc_ref[...] += p

        @pl.when(g == ng - 1)
        def _():
            x = acc_ref[...]
            o_ref[...] = x
            if n_norm:
                u = x * _rstd(x)
                for j in range(n_norm):
                    n_refs[j][...] = (u * extra[-1][j:j + 1, :]).astype(BF16)

    row = pl.BlockSpec((tm, n), lambda i, g: (i, 0))
    (out, *norms), rode = _call_with_ride(
        body, ride, name=name, grid=(m // tm, ng),
        in_specs=[pl.BlockSpec((1, tm, k), lambda i, g: (g, i, 0)),
                  pl.BlockSpec((1,) + b.shape[1:], lambda i, g: (g, 0, 0))] + ([row] if has_res else [])
        + ([pl.BlockSpec((n_norm, n), lambda i, g: (0, 0))] if n_norm else []),
        out_specs=[row] * (1 + n_norm), out_shape=[SDS((m, n), F32)] + [SDS((m, n), BF16)] * n_norm,
        scratch_shapes=[pltpu.VMEM((tm, n), F32)],
        args=[a, b] + ([res] if has_res else []) + ([gains] if n_norm else []))
    results = (out,) + ((norms,) if n_norm else ()) + ((rode,) if ride is not None else ())
    return results[0] if len(results) == 1 else results


def _mm_tn(a, b, name, ride=None):
    ga, m, ka = a.shape
    gb, _, n = b.shape
    ng = max(ga, gb)
    tk = _mm_tile(m)
    nk = m // tk

    def body(a_ref, b_ref, o_ref, acc_ref):
        s = pl.program_id(1)
        p = _dot(a_ref[0].astype(BF16), b_ref[0].astype(BF16), TN)

        @pl.when(s == 0)
        def _():
            acc_ref[...] = p

        @pl.when(s > 0)
        def _():
            acc_ref[...] += p

        @pl.when(s == nk - 1)
        def _():
            o_ref[0] = acc_ref[...].astype(BF16)

    (out,), rode = _call_with_ride(
        body, ride, name=name, grid=(ng, nk),
        in_specs=[pl.BlockSpec((1, tk, ka), (lambda g, s: (g, s, 0)) if ga > 1 else (lambda g, s: (0, s, 0))),
                  pl.BlockSpec((1, tk, n), (lambda g, s: (g, s, 0)) if gb > 1 else (lambda g, s: (0, s, 0)))],
        out_specs=[pl.BlockSpec((1, ka, n), lambda g, s: (g, 0, 0))], out_shape=[SDS((ng, ka, n), BF16)],
        scratch_shapes=[pltpu.VMEM((ka, n), F32)], args=[a, b])
    return out if ride is None else (out, rode)


def _mm_norm_bwd(a, b, h, gains, dh, name, ride=None, more_dns=()):
    ng, m, k = a.shape
    d = b.shape[1]
    n_more = len(more_dns)
    tm = _row_tile(m, FFN_ROWS_MAX, 16)
    nblk = m // tm

    def body(a_ref, b_ref, h_ref, g_ref, dh_ref, *refs):
        more_refs, (o_ref, dg_ref, acc_ref) = refs[:n_more], refs[n_more:]
        i, g = pl.program_id(0), pl.program_id(1)
        p = _dot(a_ref[0], b_ref[0], NT)

        @pl.when(g == 0)
        def _():
            acc_ref[...] = p

        @pl.when(g > 0)
        def _():
            acc_ref[...] += p

        @pl.when(g == ng - 1)
        def _():
            x = h_ref[...]
            r = _rstd(x)
            u = x * r
            du = jnp.zeros_like(x)
            rows = []
            for j, dn in enumerate([acc_ref[...]] + [ref[...] for ref in more_refs]):
                du = du + dn * g_ref[j:j + 1, :]
                rows.append(jnp.sum(dn * u, axis=0, keepdims=True))
            o_ref[...] = dh_ref[...] + r * (du - u * jnp.mean(du * u, axis=-1, keepdims=True))

            @pl.when(i == 0)
            def _():
                for j in range(1 + n_more):
                    dg_ref[j:j + 1, :] = rows[j]

            @pl.when(i > 0)
            def _():
                for j in range(1 + n_more):
                    dg_ref[j:j + 1, :] += rows[j]

    row = pl.BlockSpec((tm, d), lambda i, g: (i, 0))
    vec = pl.BlockSpec((1 + n_more, d), lambda i, g: (0, 0))
    (out, dgains), rode = _call_with_ride(
        body, ride, name=name, grid=(nblk, ng),
        in_specs=[pl.BlockSpec((1, tm, k), lambda i, g: (g, i, 0)), pl.BlockSpec((1, d, k), lambda i, g: (g, 0, 0)),
                  row, vec, row] + [row] * n_more,
        out_specs=[row, vec], out_shape=[SDS((m, d), F32), SDS((1 + n_more, d), F32)],
        scratch_shapes=[pltpu.VMEM((tm, d), F32)], args=[a, b, h, gains, dh, *more_dns])
    return out, dgains, rode


def _pair_tri(kind, sign):
    r = jnp.arange(2 * ATT_BLK)[:, None]
    c = jnp.arange(2 * ATT_BLK)[None, :]
    same = (r < ATT_BLK) == (c < ATT_BLK)
    rel = {"from": r >= c, "before": r < c}[kind]
    return ((same & rel) * sign).astype(BF16)


def _scan_dot(x, tri):
    hi = x.astype(BF16)
    lo = (x - hi.astype(F32)).astype(BF16)
    return _dot(hi, tri) + _dot(lo, tri)


def _split_heads(blk, lane_a):
    zero = jnp.zeros_like(blk)
    return jnp.concatenate([jnp.where(lane_a, blk, zero), jnp.where(lane_a, zero, blk)], axis=0)


def _softplus(z):
    return jnp.maximum(z, 0.0) + jnp.log(1.0 + jnp.exp2(jnp.abs(z) * (-LOG2_E)))


def _visible(qi, j, r0):
    t = qi * ATT_Q + r0 + lax.broadcasted_iota(jnp.int32, (ATT_Q - r0, 2 * ATT_BLK), 0)
    s = j * ATT_BLK + (lax.broadcasted_iota(jnp.int32, (ATT_Q - r0, 2 * ATT_BLK), 1) & (ATT_BLK - 1))
    return s < t


def _add_rows(x, r0, y):
    return x + y if r0 == 0 else jnp.concatenate([x[:r0], x[r0:] + y], axis=0)


def _diag_rows(n):
    return n * ATT_BLK


def _halves(x):
    return x[:, :ATT_BLK], x[:, ATT_BLK:]


def _rowsum(x):
    return jnp.sum(x, axis=1, keepdims=True)


def _still_visible(ca, cb):
    return (jnp.minimum(jnp.min(ca), jnp.min(cb)) < UNDERFLOW_AT).astype(jnp.int32)


def _attn_specs(lp):
    bk = ATT_BLK
    qblk = pl.BlockSpec((ATT_Q, bk), lambda p, i: (i, p))
    kblk = pl.BlockSpec((1, lp, bk), lambda p, i: (p // 2, 0, p % 2))
    vblk = pl.BlockSpec((1, lp, bk), lambda p, i: (HEAD_PAIRS // 2 + p // 2, 0, p % 2))
    tri = pl.BlockSpec((2 * bk, 2 * bk), lambda p, i: (0, 0))
    return qblk, kblk, vblk, tri


def _attn_fwd(q, kv, name, ride=None):
    lp, d = q.shape
    bk = ATT_BLK

    def body(q_ref, k_ref, v_ref, tri_ref, o_ref):
        qi = pl.program_id(1)
        qs = q_ref[...] * (HEAD_DIM ** -0.5)
        lane_a = lax.broadcasted_iota(jnp.int32, (1, bk), 1) < HEAD_DIM
        tri = tri_ref[...]

        def trip(js, carry, masked):
            oacc, ca, cb = carry
            r0s = [_diag_rows(len(js) - 1 - n) if masked else 0 for n in range(len(js))]
            rows = [pl.ds(pl.multiple_of(j * bk, bk), bk) for j in js]
            zs = [_dot(qs[r0:], _split_heads(k_ref[0, r, :], lane_a), NT) for r0, r in zip(r0s, rows)]
            ms = [_softplus(z) for z in zs]
            seen = [_visible(qi, j, r0) if masked else None for j, r0 in zip(js, r0s)]
            if masked:
                ms = [jnp.where(v, m, 0.0) for v, m in zip(seen, ms)]
            ws = [_scan_dot(m, tri) for m in ms]
            for v, r0, r, z, m, w in zip(seen, r0s, rows, zs, ms, ws):
                exa, exb = _halves(z + w)
                a = jnp.concatenate([jnp.exp(exa - ca[r0:]), jnp.exp(exb - cb[r0:])], axis=1)
                if masked:
                    a = jnp.where(v, a, 0.0)
                oacc = _add_rows(oacc, r0, _dot(a.astype(BF16), _split_heads(v_ref[0, r, :], lane_a)))
                ma, mb = _halves(m)
                ca, cb = _add_rows(ca, r0, _rowsum(ma)), _add_rows(cb, r0, _rowsum(mb))
            return oacc, ca, cb

        carry = (jnp.zeros((ATT_Q, bk), F32), jnp.zeros((ATT_Q, 1), F32), jnp.zeros((ATT_Q, 1), F32))
        top = (qi + 1) * ATT_UNROLL - 1
        carry = trip([top - u for u in range(ATT_UNROLL)], carry, True)
        def older(st):
            g, _, *c = st
            c = trip([top - (g + 1) * ATT_UNROLL - u for u in range(ATT_UNROLL)], tuple(c), False)
            return (g + 1, _still_visible(c[1], c[2]), *c)

        _, _, oacc, _, _ = lax.while_loop(
            lambda st: (st[0] < qi) & (st[1] > 0), older, (jnp.int32(0), _still_visible(carry[1], carry[2]), *carry))
        o_ref[...] = oacc.astype(BF16)

    qblk, kblk, vblk, tri = _attn_specs(lp)
    (o,), rode = _call_with_ride(
        body, ride, name=name, grid=(HEAD_PAIRS, lp // ATT_Q), in_specs=[qblk, kblk, vblk, tri],
        out_specs=[qblk], out_shape=[SDS((lp, d), BF16)], args=[q, kv, kv, _pair_tri("from", -1)])
    return o, rode


def _attn_bwd(q, kv, do, name, ride=None):
    lp, d = q.shape
    bk = ATT_BLK
    scale = HEAD_DIM ** -0.5

    def body(q_ref, k_ref, v_ref, do_ref, tri_ref, dq_ref, dk_ref, dv_ref):
        qi = pl.program_id(1)

        @pl.when(qi == 0)
        def _():
            dk_ref[...] = jnp.zeros_like(dk_ref)
            dv_ref[...] = jnp.zeros_like(dv_ref)

        qs = q_ref[...] * scale
        do_blk = do_ref[...]
        lane_a = lax.broadcasted_iota(jnp.int32, (1, bk), 1) < HEAD_DIM
        tri = tri_ref[...]

        def sums(js, carry, masked):
            ca, cb = carry
            for n, j in enumerate(js):
                r0 = _diag_rows(n) if masked else 0
                m = _softplus(_dot(qs[r0:], _split_heads(k_ref[0, pl.ds(pl.multiple_of(j * bk, bk), bk), :], lane_a), NT))
                if masked:
                    m = jnp.where(_visible(qi, j, r0), m, 0.0)
                ma, mb = _halves(m)
                ca, cb = _add_rows(ca, r0, _rowsum(ma)), _add_rows(cb, r0, _rowsum(mb))
            return ca, cb

        def trip(js, carry, masked):
            dq, pa, pb, ea, eb = carry
            r0s = [_diag_rows(n) if masked else 0 for n in range(len(js))]
            rows = [pl.ds(pl.multiple_of(j * bk, bk), bk) for j in js]
            kks = [_split_heads(k_ref[0, r, :], lane_a) for r in rows]
            zs = [_dot(qs[r0:], kk, NT) for r0, kk in zip(r0s, kks)]
            das = [_dot(do_blk[r0:], _split_heads(v_ref[0, r, :], lane_a), NT) for r0, r in zip(r0s, rows)]
            ms = [_softplus(z) for z in zs]
            seen = [_visible(qi, j, r0) if masked else None for j, r0 in zip(js, r0s)]
            if masked:
                ms = [jnp.where(v, m, 0.0) for v, m in zip(seen, ms)]
            xs = [_scan_dot(m, tri) for m in ms]
            es, a_bf = [], []
            for v, r0, z, m, x, da in zip(seen, r0s, zs, ms, xs, das):
                xa, xb = _halves(z + x)
                a = jnp.concatenate([jnp.exp(xa + pa[r0:]), jnp.exp(xb + pb[r0:])], axis=1)
                if masked:
                    a = jnp.where(v, a, 0.0)
                a_bf.append(a.astype(BF16))
                es.append(a * da)
                ma, mb = _halves(m)
                pa, pb = _add_rows(pa, r0, _rowsum(ma)), _add_rows(pb, r0, _rowsum(mb))
            ss = [_dot(e.astype(BF16), tri) for e in es]
            for v, r0, r, kk, z, m, e, s, ab in zip(seen, r0s, rows, kks, zs, ms, es, ss, a_bf):
                sa, sb = _halves(s)
                e_before = jnp.concatenate([sa + ea[r0:], sb + eb[r0:]], axis=1)
                dz = e - jnp.exp(z - m) * (e + e_before)
                if masked:
                    dz = jnp.where(v, dz, 0.0)
                dzb = dz.astype(BF16)
                dq = _add_rows(dq, r0, _dot(dzb, kk))
                rk = _dot(dzb, qs[r0:], TN)
                rv = _dot(ab, do_blk[r0:], TN)
                dk_ref[0, r, :] += jnp.where(lane_a, rk[:bk], rk[bk:])
                dv_ref[0, r, :] += jnp.where(lane_a, rv[:bk], rv[bk:])
                e_a, e_b = _halves(e)
                ea, eb = _add_rows(ea, r0, _rowsum(e_a)), _add_rows(eb, r0, _rowsum(e_b))
            return dq, pa, pb, ea, eb

        zcol = jnp.zeros((ATT_Q, 1), F32)
        diag = [qi * ATT_UNROLL + u for u in range(ATT_UNROLL)]
        def older(st):
            g, _, *c = st
            c = sums([(qi - 1 - g) * ATT_UNROLL + u for u in range(ATT_UNROLL)], tuple(c), False)
            return (g + 1, _still_visible(*c), *c)

        seen = sums(diag, (zcol, zcol), True)
        n_old, _, ta, tb = lax.while_loop(
            lambda st: (st[0] < qi) & (st[1] > 0), older, (jnp.int32(0), _still_visible(*seen), *seen))
        carry = lax.fori_loop(
            0, n_old, lambda g, c: trip([(qi - n_old + g) * ATT_UNROLL + u for u in range(ATT_UNROLL)], c, False),
            (jnp.zeros((ATT_Q, bk), F32), -ta, -tb, zcol, zcol))
        carry = trip(diag, carry, True)
        dq_ref[...] = (carry[0] * scale).astype(BF16)

    qblk, kblk, vblk, tri = _attn_specs(lp)
    (dq, dk, dv), rode = _call_with_ride(
        body, ride, name=name, grid=(HEAD_PAIRS, lp // ATT_Q), in_specs=[qblk, kblk, vblk, qblk, tri],
        out_specs=[qblk, kblk, kblk],
        out_shape=[SDS((lp, d), BF16), SDS((HEAD_PAIRS // 2, lp, 2 * bk), F32), SDS((HEAD_PAIRS // 2, lp, 2 * bk), F32)],
        args=[q, kv, kv, do, _pair_tri("before", 1)])
    return dq, dk, dv, rode


def _mesh_pos():
    return lax.axis_index("x"), lax.axis_index("y"), lax.axis_index("c")


def _flip(pos, r):
    x, y, c = pos
    return (1 - x if r & 4 else x, 1 - y if r & 2 else y, 1 - c if r & 1 else c)


def _dev_index(pos):
    return 4 * pos[0] + 2 * pos[1] + pos[2]


class _Ride(NamedTuple):
    kind: str
    arrays: list


def _ride_arrays(ride):
    return [] if ride is None else ride.arrays


def _ride_args(ride):
    if ride is None:
        return [], [], [], []
    n = len(ride.arrays)
    hbm = pl.BlockSpec(memory_space=pl.ANY)
    shapes = [SDS(x.shape if ride.kind == "scatter" else (N_DEV,) + x.shape, x.dtype) for x in ride.arrays]
    sems = [pltpu.SemaphoreType.DMA((7 * n,)), pltpu.SemaphoreType.DMA((7 * n,)), pltpu.SemaphoreType.DMA((n,))]
    return [hbm] * n, [hbm] * n, shapes, sems


def _riding(body, n_in, n_out, ride, first, middle, last):
    if ride is None:
        return body
    n = len(ride.arrays)

    def wrapped(*refs):
        ins, srcs = refs[:n_in], refs[n_in:n_in + n]
        outs, dsts = refs[n_in + n:n_in + n + n_out], refs[n_in + n + n_out:n_in + 2 * n + n_out]
        scratch, (send_sems, recv_sems, local_sems) = refs[n_in + 2 * n + n_out:-3], refs[-3:]
        me = _mesh_pos()
        mi = _dev_index(me)

        def copy(a, k, src, dst, to):
            return pltpu.make_async_remote_copy(
                src_ref=src, dst_ref=dst, send_sem=send_sems.at[7 * a + k], recv_sem=recv_sems.at[7 * a + k],
                device_id=to, device_id_type=pl.DeviceIdType.MESH)

        local, sends, lands, arrived, passed = [], [], [], [], []
        for a in range(n):
            if ride.kind == "gather_by_chip":
                sibling, others = _flip(me, 1), [_flip(me, 4), _flip(me, 2), _flip(me, 6)]
                local.append(pltpu.make_async_copy(srcs[a], dsts[a].at[mi], local_sems.at[a]))
                sends.append(copy(a, 0, srcs[a], dsts[a].at[mi], sibling))
                lands.append(copy(a, 0, dsts[a].at[_dev_index(sibling)], dsts[a].at[_dev_index(sibling)], me))
                for j, o in enumerate(others):
                    oi, si = _dev_index(o), _dev_index(_flip(o, 1))
                    sends.append(copy(a, 1 + j, srcs[a], dsts[a].at[mi], o))
                    arrived.append(copy(a, 1 + j, dsts[a].at[oi], dsts[a].at[oi], me))
                    passed.append(copy(a, 4 + j, dsts[a].at[oi], dsts[a].at[oi], sibling))
                    lands.append(copy(a, 4 + j, dsts[a].at[si], dsts[a].at[si], me))
                continue
            gather = ride.kind == "gather"
            local.append(pltpu.make_async_copy(srcs[a] if gather else srcs[a].at[mi], dsts[a].at[mi], local_sems.at[a]))
            for r in range(1, N_DEV):
                peer = _flip(me, r)
                pi = _dev_index(peer)
                sends.append(copy(a, r - 1, srcs[a] if gather else srcs[a].at[pi], dsts[a].at[mi], peer))
                lands.append(copy(a, r - 1, dsts[a].at[pi], dsts[a].at[pi], peer))

        @pl.when(first())
        def _():
            for cp in local + sends:
                cp.start()

        if passed:
            @pl.when(middle())
            def _():
                for got, on in zip(arrived, passed):
                    got.wait_recv()
                    on.start()

        body(*ins, *outs, *scratch)

        @pl.when(last())
        def _():
            for cp in lands:
                cp.wait_recv()
            for cp in sends + passed:
                cp.wait_send()
            for cp in local:
                cp.wait()

    return wrapped


def _call_with_ride(body, ride, *, name, grid, in_specs, out_specs, out_shape, args, scratch_shapes=()):
    ride_in, ride_out, ride_shape, ride_sems = _ride_args(ride)
    axes = range(len(grid))
    assert ride is None or ride.kind != "gather_by_chip" or grid[0] >= 4, grid

    def at(step):
        return lambda: functools.reduce(lambda p, k: p & (pl.program_id(k) == step[k]), axes, True)

    ends = [(0,) * len(grid), (3 * grid[0] // 4,) + (0,) * (len(grid) - 1), tuple(g - 1 for g in grid)]
    out = pl.pallas_call(
        _riding(body, len(in_specs), len(out_specs), ride, *map(at, ends)), name=name, grid=grid,
        in_specs=list(in_specs) + ride_in, out_specs=list(out_specs) + ride_out,
        out_shape=list(out_shape) + ride_shape, scratch_shapes=list(scratch_shapes) + ride_sems,
        compiler_params=_cp(len(grid)))(*args, *_ride_arrays(ride))
    return out[:len(out_specs)], out[len(out_specs):]


def _all_gather(xs, name):
    n = len(xs)

    def body(*refs):
        x_refs, out_refs = refs[:n], refs[n:2 * n]
        send_sems, recv_sems, local_sems = refs[2 * n:]
        me = _mesh_pos()
        sibling = _flip(me, 1)
        others = [_flip(me, 4), _flip(me, 2), _flip(me, 6)]

        def copy(a, k, block, to, own=False):
            slab = out_refs[a].at[_dev_index(block)]
            return pltpu.make_async_remote_copy(
                src_ref=x_refs[a] if own else slab, dst_ref=slab,
                send_sem=send_sems.at[7 * a + k], recv_sem=recv_sems.at[7 * a + k],
                device_id=to, device_id_type=pl.DeviceIdType.MESH)

        mine = [pltpu.make_async_copy(x_refs[a], out_refs[a].at[_dev_index(me)], local_sems.at[a]) for a in range(n)]
        first = []
        for a in range(n):
            mine[a].start()
            first += [copy(a, 0, me, sibling, own=True)] + [copy(a, 1 + j, me, o, own=True) for j, o in enumerate(others)]
        for cp in first:
            cp.start()
        passed = []
        for a in range(n):
            for j, o in enumerate(others):
                copy(a, 1 + j, o, me).wait_recv()
                passed.append(copy(a, 4 + j, o, sibling))
                passed[-1].start()
        for a in range(n):
            copy(a, 0, sibling, me).wait_recv()
            for j, o in enumerate(others):
                copy(a, 4 + j, _flip(o, 1), me).wait_recv()
        for cp in first + passed:
            cp.wait_send()
        for cp in mine:
            cp.wait()

    hbm = pl.BlockSpec(memory_space=pl.ANY)
    return pl.pallas_call(
        body, name=name, out_shape=[SDS((N_DEV,) + x.shape, x.dtype) for x in xs],
        in_specs=[hbm] * n, out_specs=[hbm] * n,
        scratch_shapes=[pltpu.SemaphoreType.DMA((7 * n,)), pltpu.SemaphoreType.DMA((7 * n,)), pltpu.SemaphoreType.DMA((n,))],
    )(*xs)


def _sum_slabs(a, name, ride=None):
    n, rows, cols = a.shape
    tr = rows if a.size * a.dtype.itemsize <= SUM_WHOLE_BYTES else _row_tile(rows, SUM_ROWS_MAX, 16)

    def body(a_ref, o_ref):
        acc = a_ref[0].astype(F32)
        for k in range(1, n):
            acc = acc + a_ref[k].astype(F32)
        o_ref[...] = acc

    (out,), rode = _call_with_ride(
        body, ride, name=name, grid=(rows // tr,),
        in_specs=[pl.BlockSpec((n, tr, cols), lambda i: (0, i, 0))],
        out_specs=[pl.BlockSpec((tr, cols), lambda i: (i, 0))], out_shape=[SDS((rows, cols), F32)], args=[a])
    return out if ride is None else (out, rode)


def _adamw(w, g, m, v, name):
    rows, cols = w.shape
    tr = _row_tile(rows, 352)

    def body(w_ref, g_ref, m_ref, v_ref, d_ref, mo_ref, vo_ref):
        g_ = g_ref[...]
        m_ = ADAM_B1 * m_ref[...] + (1.0 - ADAM_B1) * g_
        v_ = ADAM_B2 * v_ref[...] + (1.0 - ADAM_B2) * (g_ * g_)
        m_hat = m_ / (1.0 - ADAM_B1 ** ADAM_STEP)
        v_hat = v_ / (1.0 - ADAM_B2 ** ADAM_STEP)
        d_ref[...] = -ADAM_LR * (m_hat / (jnp.sqrt(v_hat) + ADAM_EPS) + ADAM_WD * w_ref[...])
        mo_ref[...] = m_
        vo_ref[...] = v_

    blk = pl.BlockSpec((tr, cols), lambda i: (i, 0))
    return pl.pallas_call(
        body, name=name, grid=(rows // tr,),
        in_specs=[blk] * 4, out_specs=[blk] * 3, out_shape=[SDS((rows, cols), F32)] * 3,
        compiler_params=_cp(1))(w, g, m, v)


def _ffn_bwd(h, gain, w_up, cw4, cb4, w_down4, saved, dh, tag, ride_wup=None, scatter_own=False):
    n2, up4, act = saved
    d_w_down = _mm_tn(act, dh[None], f"ffn_dwdown_{tag}")
    ride_gate = _Ride("scatter", [d_w_down.reshape(N_DEV, -1, d_w_down.shape[-1])]) if scatter_own else None
    dup4, dcw4, dcb4, rode = _ffn_act_bwd(up4, cw4, cb4, dh, w_down4, f"ffn_dgate_{tag}", ride_gate)
    if scatter_own:
        (d_w_down,) = rode
    dup = dup4.reshape((8,) + dup4.shape[2:])
    d_w_up, rode_wup = _mm_tn(n2[None], dup, f"ffn_dwup_{tag}", ride_wup), []
    if ride_wup is not None:
        d_w_up, rode_wup = d_w_up
    dh_in, dgain, rode = _mm_norm_bwd(dup, w_up, h, gain, dh, f"ffn_dnorm_{tag}",
                                      _Ride("scatter", [d_w_up]) if scatter_own else None)
    if scatter_own:
        (d_w_up,) = rode
    return dh_in, dgain, d_w_up, d_w_down, dcw4, dcb4, rode_wup


def kernel(x, meta_tokens, mix_norm, ffn_norm, pool_w, pool_scale, kv_norm, w_kv, w_q, w_o, ffn_w_up, ffn_conv_w, ffn_conv_b, ffn_w_down, final_norm, loss_target, m_meta_tokens, m_mix_norm, m_ffn_norm, m_pool_w, m_pool_scale, m_kv_norm, m_w_kv, m_w_q, m_w_o, m_ffn_w_up, m_ffn_conv_w, m_ffn_conv_b, m_ffn_w_down, m_final_norm, v_meta_tokens, v_mix_norm, v_ffn_norm, v_pool_w, v_pool_scale, v_kv_norm, v_w_kv, v_w_q, v_w_o, v_ffn_w_up, v_ffn_conv_w, v_ffn_conv_b, v_ffn_w_down, v_final_norm):
    seq, d = x.shape[1], x.shape[2]
    n_tok = N_META + seq
    lp = -(-n_tok // ROW_TILE) * ROW_TILE
    fc = ffn_w_up.shape[2]
    me = _dev_index(_mesh_pos())

    def rows_of(parts):
        rows = [p.size // d for p in parts]
        return [sum(rows[:k]) for k in range(len(parts) + 1)]

    def bf16_rows(parts):
        return jnp.concatenate([p.reshape(-1, d) for p in parts], axis=0).astype(BF16)

    small_parts = [meta_tokens, pool_scale, ffn_conv_w]
    small_rows = [p.size // 128 for p in small_parts]
    small_pad = -sum(small_rows) % 8
    local_small = jnp.concatenate([p.reshape(-1, 128) for p in small_parts] + [jnp.zeros((small_pad, 128), F32)], axis=0)
    g_pw, wup0, gs = _all_gather([bf16_rows([pool_w]), ffn_w_up[0].astype(BF16), local_small], "gather_first")
    pw = g_pw.reshape(N_DEV, 4, POOL_C // N_DEV, POOL_C).transpose(1, 0, 2, 3).reshape(4, POOL_C, POOL_C)
    early_parts, late_parts = [ffn_w_down[0], w_kv], [w_o, ffn_w_down[1]]
    early_off, late_off = rows_of(early_parts), rows_of(late_parts)
    r0, r1, r2 = small_rows[0], small_rows[0] + small_rows[1], sum(small_rows)
    meta_full = gs[:, :r0].transpose(1, 0, 2).reshape(N_META, d)
    pscale = gs[:, r0:r1].reshape(1, d)
    cw = gs[:, r1:r2].reshape(N_DEV, 2, 3, fc)
    cw4_l = [cw[:, l].reshape(2, 4, 3, fc) for l in range(2)]
    cb4_l = [ffn_conv_b[l].reshape(2, 4, 1, fc) for l in range(2)]

    h0 = jnp.concatenate([meta_full, x[0], jnp.zeros((lp - n_tok, d), F32)], axis=0)
    h1, diff = _pool_fwd(h0, mix_norm[0:1], pw, pscale, "pool_fwd")
    (n2_0,) = _rms_fwd(h1, ffn_norm[0:1], "ffn_norm_0")
    up4_0, act0, (g_early,) = _ffn_up_act(n2_0, wup0.reshape(2, 4, d, fc), cw4_l[0], cb4_l[0], "ffn_up_0",
                                          _Ride("gather_by_chip", [bf16_rows(early_parts)]))
    wdn0 = g_early[:, early_off[0]:early_off[1]].reshape(4, fc, d)
    wkv = g_early[:, early_off[1]:early_off[2]].reshape(N_DEV, d, 2 * d // N_DEV)
    gains_b = jnp.stack([kv_norm, mix_norm[1]], axis=0)
    h2, (kvn, n3), (wq,) = _mm_reduce(act0, wdn0, NN, h1, "ffn_down_0", _Ride("gather", [w_q[0].astype(BF16)]), gains_b)
    wq = wq.reshape(1, d, d)
    kv = _mm_group(kvn, wkv, NN, BF16, "kv_proj")
    q = _mm_group(n3, wq, NN, BF16, "q_proj")[0]
    o, (g_late, wup1) = _attn_fwd(
        q, kv, "attn_fwd", _Ride("gather_by_chip", [bf16_rows(late_parts), ffn_w_up[1].astype(BF16)]))
    wo = g_late[:, late_off[0]:late_off[1]].reshape(1, d, d)
    wdn1 = g_late[:, late_off[1]:late_off[2]].reshape(4, fc, d)
    h3, (n2_1,) = _mm_reduce(o[None], wo, NN, h2, "o_proj", gains=ffn_norm[1:2])
    up4_1, act1, _ = _ffn_up_act(n2_1, wup1.reshape(2, 4, d, fc), cw4_l[1], cb4_l[1], "ffn_up_1")
    h4 = _mm_reduce(act1, wdn1, NN, h3, "ffn_down_1")
    target = jnp.pad(loss_target[0], ((N_META, lp - n_tok), (0, 0)))
    dh4, loss_blk, dg_final = _loss_bwd(h4, final_norm[None], target, seq, "loss")
    loss = lax.psum(loss_blk[0, 0], MESH_AXES)

    dh3, dg_ffn1, d_wup1, d_wdn1, dcw4_1, dcb4_1, _ = _ffn_bwd(
        h3, ffn_norm[1:2], wup1, cw4_l[1], cb4_l[1], wdn1, (n2_1, up4_1, act1), dh4, "1")
    d_o = _mm_group(dh3, wo, NT, BF16, "o_proj_dx")[0]
    d_wo = _mm_tn(o[None], dh3[None], "o_proj_dw")
    ride_late = _Ride("scatter", [jnp.concatenate([d_wo.reshape(N_DEV, -1, d), d_wdn1.reshape(N_DEV, -1, d)], axis=1), d_wup1])
    dq, dk, dv, (p_late, p_up1) = _attn_bwd(q, kv, d_o, "attn_bwd", ride_late)
    dn3 = _mm_group(dq, wq, NT, F32, "q_proj_dx")[0]
    d_wq = _mm_tn(n3[None], dq[None], "q_proj_dw")
    dkv = jnp.concatenate([dk, dv], axis=0).astype(BF16)
    d_wkv = _mm_tn(kvn[None], dkv, "kv_proj_dw")
    dh2, dg_b, _ = _mm_norm_bwd(dkv, wkv, h2, gains_b, dh3, "kv_proj_dx", more_dns=[dn3])
    ride_proj = _Ride("scatter", [jnp.concatenate([d_wkv.reshape(N_DEV, -1, d), d_wq.reshape(N_DEV, -1, d)], axis=1)])

    dh1, dg_ffn0, p_up0, p_dn0, dcw4_0, dcb4_0, (p_proj,) = _ffn_bwd(
        h1, ffn_norm[0:1], wup0, cw4_l[0], cb4_l[0], wdn0, (n2_0, up4_0, act0), dh2, "0", ride_proj, scatter_own=True)
    dh0, d_pw, d_pscale, dg_mix0 = _pool_bwd(h0, mix_norm[0:1], pw, pscale, diff, dh1, "pool_bwd")
    grad_x = dh0[N_META:n_tok][None]
    d_pw8 = d_pw.reshape(4, N_DEV, POOL_C // N_DEV, POOL_C).transpose(1, 0, 2, 3).reshape(N_DEV, -1, d).astype(BF16)
    s_up0, (p_pw,) = _sum_slabs(p_up0, "sum_up0", _Ride("scatter", [d_pw8]))
    s_late, s_up1, s_proj, s_dn0, s_pw = [_sum_slabs(p, "sum_" + n) for p, n in (
        (p_late, "late"), (p_up1, "up1"), (p_proj, "proj"), (p_dn0, "down0"), (p_pw, "pool"))]
    n_kv, n_o = w_kv.size // d, w_o.size // d

    rep_parts = [jnp.concatenate([dg_mix0, dg_b[1:2]], axis=0), jnp.concatenate([dg_ffn0, dg_ffn1], axis=0),
                 dg_b[0:1], dg_final, jnp.stack([dcb4_0.reshape(-1), dcb4_1.reshape(-1)], axis=0)]
    rep_shapes = [mix_norm.shape, ffn_norm.shape, kv_norm.shape, final_norm.shape, ffn_conv_b.shape]
    rep_rows = [p.size // 128 for p in rep_parts]
    d_meta8 = dh0[:N_META].reshape(N_META, N_DEV, d // N_DEV).transpose(1, 0, 2).reshape(N_DEV, -1, 128)
    d_cw8 = jnp.stack([dcw4_0.reshape(N_DEV, 3, fc), dcw4_1.reshape(N_DEV, 3, fc)], axis=1).reshape(N_DEV, -1, 128)
    shard_parts = jnp.concatenate([d_meta8, d_pscale.reshape(N_DEV, 1, 128), d_cw8], axis=1)
    n_rep = sum(rep_rows)
    partial_small = jnp.concatenate([p.reshape(-1, 128) for p in rep_parts] + [shard_parts.reshape(-1, 128)], axis=0)
    g_small = _sum_slabs(_all_gather([partial_small], "gather_vector_grads")[0], "sum_vectors")
    g_rep = [g_small[sum(rep_rows[:k]):sum(rep_rows[:k + 1])].reshape(s) for k, s in enumerate(rep_shapes)]
    g_shard = lax.dynamic_index_in_dim(g_small[n_rep:].reshape(N_DEV, -1, 128), me, 0, keepdims=False)
    g_meta = g_shard[:r0].reshape(meta_tokens.shape)
    g_pscale = g_shard[r0:r1].reshape(pool_scale.shape)
    g_cw = g_shard[r1:r2].reshape(ffn_conv_w.shape)

    grads = {
        "meta_tokens": g_meta, "mix_norm": g_rep[0], "ffn_norm": g_rep[1],
        "pool_w": s_pw.reshape(pool_w.shape), "pool_scale": g_pscale, "kv_norm": g_rep[2],
        "w_kv": s_proj[:n_kv].reshape(w_kv.shape), "w_q": s_proj[n_kv:].reshape(w_q.shape),
        "w_o": s_late[:n_o].reshape(w_o.shape),
        "ffn_w_up": jnp.stack([s_up0, s_up1], axis=0), "ffn_conv_w": g_cw, "ffn_conv_b": g_rep[4],
        "ffn_w_down": jnp.stack([s_dn0, s_late[n_o:]], axis=0), "final_norm": g_rep[3],
    }
    names = list(grads)
    weights = dict(zip(names, [meta_tokens, mix_norm, ffn_norm, pool_w, pool_scale, kv_norm, w_kv, w_q, w_o,
                               ffn_w_up, ffn_conv_w, ffn_conv_b, ffn_w_down, final_norm]))
    mom1 = dict(zip(names, [m_meta_tokens, m_mix_norm, m_ffn_norm, m_pool_w, m_pool_scale, m_kv_norm, m_w_kv, m_w_q,
                            m_w_o, m_ffn_w_up, m_ffn_conv_w, m_ffn_conv_b, m_ffn_w_down, m_final_norm]))
    mom2 = dict(zip(names, [v_meta_tokens, v_mix_norm, v_ffn_norm, v_pool_w, v_pool_scale, v_kv_norm, v_w_kv, v_w_q,
                            v_w_o, v_ffn_w_up, v_ffn_conv_w, v_ffn_conv_b, v_ffn_w_down, v_final_norm]))

    delta, new_m, new_v = {}, {}, {}
    for n in names:
        shape = weights[n].shape
        flat = (-1, shape[-1])
        dl, nm, nv = _adamw(weights[n].reshape(flat), grads[n].reshape(flat), mom1[n].reshape(flat),
                            mom2[n].reshape(flat), "adamw_" + n)
        delta[n], new_m[n], new_v[n] = dl.reshape(shape), nm.reshape(shape), nv.reshape(shape)
    return (loss, grad_x, *[grads[n] for n in names], *[delta[n] for n in names],
            *[new_m[n] for n in names], *[new_v[n] for n in names])
```

```python
import functools
from typing import NamedTuple

import jax
import jax.numpy as jnp
from jax import lax
from jax.experimental import pallas as pl
from jax.experimental.pallas import tpu as pltpu

F32 = jnp.float32
BF16 = jnp.bfloat16
SDS = jax.ShapeDtypeStruct

N_DEV = 8
N_META = 16
HEAD_DIM = 64
HEAD_PAIRS = 8
RMS_EPS = 1e-6
LOG2_E = 1.4426950408889634
POOL_WINDOWS = (2, 4, 8, 16)
POOL_C = 256
POOL_HALO = 16
CONV_HALO = 8
ROW_TILE = 384
MM_ROWS_MAX = 1408
FFN_ROWS_MAX = 704
SUM_ROWS_MAX = 256
SUM_WHOLE_BYTES = 4 << 20
ATT_BLK = 128
ATT_Q = ROW_TILE
ATT_UNROLL = ATT_Q // ATT_BLK
ATT_OLD = 2
UNDERFLOW_AT = 104.0
VMEM_LIMIT = 56 * 1024 * 1024

ADAM_LR = 0.001
ADAM_B1 = 0.9
ADAM_B2 = 0.999
ADAM_EPS = 1e-08
ADAM_WD = 0.01
ADAM_STEP = 10

MESH_AXES = ("x", "y", "c")
NN = (((1,), (0,)), ((), ()))
NT = (((1,), (1,)), ((), ()))
TN = (((0,), (0,)), ((), ()))


def _cp(n_axes):
    return pltpu.CompilerParams(dimension_semantics=("arbitrary",) * n_axes, vmem_limit_bytes=VMEM_LIMIT)


def _dot(a, b, dims=NN):
    return lax.dot_general(a, b, dims, preferred_element_type=F32)


def _rstd(x):
    return lax.rsqrt(jnp.mean(x * x, axis=-1, keepdims=True) + RMS_EPS)


def _row_tile(rows, cap=512, mult=8):
    if rows <= cap:
        return rows
    best = mult
    for t in range(mult, cap + 1, mult):
        if rows % t == 0:
            best = t
    assert rows % best == 0
    return best


def _rms_fwd(h, gains, name):
    lp, d = h.shape
    k = gains.shape[0]
    tm = ROW_TILE

    def body(h_ref, g_ref, *o_refs):
        x = h_ref[...]
        u = x * _rstd(x)
        for j in range(k):
            o_refs[j][...] = (u * g_ref[j:j + 1, :]).astype(BF16)

    row = pl.BlockSpec((tm, d), lambda i: (i, 0))
    return pl.pallas_call(
        body, name=name, grid=(lp // tm,),
        in_specs=[row, pl.BlockSpec((k, d), lambda i: (0, 0))],
        out_specs=[row] * k, out_shape=[SDS((lp, d), BF16)] * k,
        compiler_params=_cp(1))(h, gains)


def _rms_bwd(h, gains, dns, dh_in, name):
    lp, d = h.shape
    k = gains.shape[0]
    tm = ROW_TILE

    def body(h_ref, g_ref, *refs):
        dn_refs, dh_ref, dho_ref, dg_ref = refs[:k], refs[k], refs[k + 1], refs[k + 2]
        i = pl.program_id(0)
        x = h_ref[...]
        r = _rstd(x)
        u = x * r
        du = jnp.zeros_like(x)
        rows = []
        for j in range(k):
            dn = dn_refs[j][...]
            du = du + dn * g_ref[j:j + 1, :]
            rows.append(jnp.sum(dn * u, axis=0, keepdims=True))
        dx = r * (du - u * jnp.mean(du * u, axis=-1, keepdims=True))
        dho_ref[...] = dh_ref[...] + dx

        @pl.when(i == 0)
        def _():
            for j in range(k):
                dg_ref[j:j + 1, :] = rows[j]

        @pl.when(i > 0)
        def _():
            for j in range(k):
                dg_ref[j:j + 1, :] += rows[j]

    row = pl.BlockSpec((tm, d), lambda i: (i, 0))
    vec = pl.BlockSpec((k, d), lambda i: (0, 0))
    return pl.pallas_call(
        body, name=name, grid=(lp // tm,),
        in_specs=[row, vec] + [row] * k + [row],
        out_specs=[row, vec], out_shape=[SDS((lp, d), F32), SDS((k, d), F32)],
        compiler_params=_cp(1))(h, gains, *dns, dh_in)


def _loss_bwd(h, gain, target, n_real, name):
    lp, d = h.shape
    tm = ROW_TILE

    def body(h_ref, g_ref, t_ref, dh_ref, loss_ref, dg_ref):
        i = pl.program_id(0)
        x = h_ref[...]
        g = g_ref[...]
        r = _rstd(x)
        u = x * r
        row = i * tm + lax.broadcasted_iota(jnp.int32, (tm, 1), 0)
        valid = (row >= N_META) & (row < N_META + n_real)
        e = jnp.where(valid, u * g - t_ref[...], 0.0)
        part = 0.5 * jnp.sum(jnp.sum(e * e, axis=-1, keepdims=True), axis=0, keepdims=True) * (1.0 / d)
        dy = e * (1.0 / d)
        du = dy * g
        dh_ref[...] = r * (du - u * jnp.mean(du * u, axis=-1, keepdims=True))
        dgp = jnp.sum(dy * u, axis=0, keepdims=True)

        @pl.when(i == 0)
        def _():
            loss_ref[...] = jnp.broadcast_to(part, (8, 128))
            dg_ref[...] = dgp

        @pl.when(i > 0)
        def _():
            loss_ref[...] += jnp.broadcast_to(part, (8, 128))
            dg_ref[...] += dgp

    row = pl.BlockSpec((tm, d), lambda i: (i, 0))
    vec = pl.BlockSpec((1, d), lambda i: (0, 0))
    return pl.pallas_call(
        body, name=name, grid=(lp // tm,),
        in_specs=[row, vec, row],
        out_specs=[row, pl.BlockSpec((8, 128), lambda i: (0, 0)), vec],
        out_shape=[SDS((lp, d), F32), SDS((8, 128), F32), SDS((1, d), F32)],
        compiler_params=_cp(1))(h, gain, target)


def _pool_fwd(h, gain, w, scale, name):
    lp, d = h.shape
    tm = ROW_TILE
    hb = POOL_HALO

    def body(h_ref, halo_ref, g_ref, w_ref, s_ref, h1_ref, diff_ref):
        i = pl.program_id(0)
        g = g_ref[...]
        x = h_ref[...]
        n = x * _rstd(x) * g
        xh = halo_ref[...]
        nh = jnp.where(i > 0, xh * _rstd(xh) * g, 0.0)
        cur = jnp.concatenate([nh, n], axis=0)
        pos = i * tm + lax.broadcasted_iota(jnp.int32, (tm, 1), 0)
        for gi, win in enumerate(POOL_WINDOWS):
            if gi > 0:
                cur = cur[:, POOL_C:]
            cur = cur + pltpu.roll(cur, win // 2, 0)
            c0 = gi * POOL_C
            count = jnp.minimum(pos + 1, win).astype(F32)
            diff = cur[hb:, :POOL_C] / count - n[:, c0:c0 + POOL_C]
            diff = diff.astype(BF16)
            y = _dot(diff, w_ref[gi])
            h1_ref[:, c0:c0 + POOL_C] = x[:, c0:c0 + POOL_C] + y * s_ref[:, c0:c0 + POOL_C]
            diff_ref[:, c0:c0 + POOL_C] = diff

    row = pl.BlockSpec((tm, d), lambda i: (i, 0))
    halo = pl.BlockSpec((hb, d), lambda i: (jnp.maximum(i * (tm // hb) - 1, 0), 0))
    vec = pl.BlockSpec((1, d), lambda i: (0, 0))
    return pl.pallas_call(
        body, name=name, grid=(lp // tm,),
        in_specs=[row, halo, vec, pl.BlockSpec(w.shape, lambda i: (0, 0, 0)), vec],
        out_specs=[row, row], out_shape=[SDS((lp, d), F32), SDS((lp, d), BF16)],
        compiler_params=_cp(1))(h, h, gain, w, scale)


def _pool_bwd(h, gain, w, scale, diff, dh1, name):
    lp, d = h.shape
    tm = ROW_TILE
    hb = POOL_HALO
    nblk = lp // tm
    ext = tm + hb

    def body(h_ref, g_ref, w_ref, s_ref, diff_ref, dh_ref, dhn_ref, dh0_ref, dw_ref, ds_ref, dg_ref):
        i = pl.program_id(0)
        g = g_ref[...]
        x = h_ref[...]
        r = _rstd(x)
        u = x * r
        dh = dh_ref[...]
        dhn = jnp.where(i < nblk - 1, dhn_ref[...], 0.0)
        dyp = jnp.concatenate([dh, dhn], axis=0) * s_ref[...]
        pos = i * tm + lax.broadcasted_iota(jnp.int32, (ext, 1), 0)
        dn_parts, dw_parts, ds_parts = [], [], []
        for gi, win in enumerate(POOL_WINDOWS):
            c0 = gi * POOL_C
            wg = w_ref[gi]
            dyp_g = dyp[:, c0:c0 + POOL_C].astype(BF16)
            dd = _dot(dyp_g, wg, NT)
            dfg = diff_ref[:, c0:c0 + POOL_C]
            dw_parts.append(_dot(dfg, dyp_g[:tm], TN))
            ds_parts.append(jnp.sum(dh[:, c0:c0 + POOL_C] * _dot(dfg, wg), axis=0, keepdims=True))
            count = jnp.minimum(pos + 1, win).astype(F32)
            cur = dd / count
            sh = 1
            while sh < win:
                cur = cur + pltpu.roll(cur, ext - sh, 0)
                sh *= 2
            dn_parts.append(cur[:tm] - dd[:tm])
        dn = jnp.concatenate(dn_parts, axis=1)
        du = dn * g
        dh0_ref[...] = dh + r * (du - u * jnp.mean(du * u, axis=-1, keepdims=True))
        dgp = jnp.sum(dn * u, axis=0, keepdims=True)
        dsp = jnp.concatenate(ds_parts, axis=1)

        @pl.when(i == 0)
        def _():
            for gi in range(len(POOL_WINDOWS)):
                dw_ref[gi] = dw_parts[gi]
            ds_ref[...] = dsp
            dg_ref[...] = dgp

        @pl.when(i > 0)
        def _():
            for gi in range(len(POOL_WINDOWS)):
                dw_ref[gi] += dw_parts[gi]
            ds_ref[...] += dsp
            dg_ref[...] += dgp

    row = pl.BlockSpec((tm, d), lambda i: (i, 0))
    nxt = pl.BlockSpec((hb, d), lambda i: (jnp.minimum((i + 1) * (tm // hb), lp // hb - 1), 0))
    vec = pl.BlockSpec((1, d), lambda i: (0, 0))
    wsp = pl.BlockSpec(w.shape, lambda i: (0, 0, 0))
    return pl.pallas_call(
        body, name=name, grid=(nblk,),
        in_specs=[row, vec, wsp, vec, row, row, nxt],
        out_specs=[row, wsp, vec, vec],
        out_shape=[SDS((lp, d), F32), SDS(w.shape, F32), SDS((1, d), F32), SDS((1, d), F32)],
        compiler_params=_cp(1))(h, gain, w, scale, diff, dh1, dh1)


def _ffn_specs(tm, c, lp):
    blk = pl.BlockSpec((2, 1, tm, c), lambda g, i: (0, g, i, 0))
    halo = pl.BlockSpec((2, 1, CONV_HALO, c), lambda g, i: (0, g, jnp.maximum(i * (tm // CONV_HALO) - 1, 0), 0))
    cw = pl.BlockSpec((2, 1, 3, c), lambda g, i: (0, g, 0, 0))
    cb = pl.BlockSpec((2, 1, 1, c), lambda g, i: (0, g, 0, 0))
    return blk, halo, cw, cb


def _ffn_up_act(n2, w_up4, cw4, cb4, name, ride=None):
    lp, d = n2.shape
    _, ng, _, c = w_up4.shape
    tm = _row_tile(lp, FFN_ROWS_MAX, 16)
    hb = CONV_HALO

    def body(a_ref, w_ref, cw_ref, cb_ref, up_ref, act_ref, tail_ref):
        @pl.when(pl.program_id(1) == 0)
        def _():
            tail_ref[...] = jnp.zeros_like(tail_ref)

        a = a_ref[...]
        u = []
        for half in range(2):
            x = _dot(a, w_ref[half, 0])
            up_ref[half, 0] = x
            rows = jnp.concatenate([tail_ref[half], x], axis=0)
            u.append(cb_ref[half, 0] + cw_ref[half, 0, 0:1, :] * pltpu.roll(rows, 2, 0)[hb:]
                     + cw_ref[half, 0, 1:2, :] * pltpu.roll(rows, 1, 0)[hb:] + cw_ref[half, 0, 2:3, :] * x)
            tail_ref[half] = x[tm - hb:]
        gate, val = u
        sig = 1.0 / (1.0 + jnp.exp(-gate))
        act_ref[0] = (gate * sig * val).astype(BF16)

    blk, _, cw, cb = _ffn_specs(tm, c, lp)
    (up4, act), rode = _call_with_ride(
        body, ride, name=name, grid=(ng, lp // tm),
        in_specs=[pl.BlockSpec((tm, d), lambda g, i: (i, 0)), pl.BlockSpec((2, 1, d, c), lambda g, i: (0, g, 0, 0)), cw, cb],
        out_specs=[blk, pl.BlockSpec((1, tm, c), lambda g, i: (g, i, 0))],
        out_shape=[SDS((2, ng, lp, c), F32), SDS((ng, lp, c), BF16)],
        scratch_shapes=[pltpu.VMEM((2, hb, c), F32)], args=[n2, w_up4, cw4, cb4])
    return up4, act, rode


def _ffn_act_bwd(up4, cw4, cb4, dh, w_down4, name, ride=None):
    _, ng, lp, c = up4.shape
    d = dh.shape[1]
    tm = ROW_TILE
    hb = CONV_HALO
    nblk = lp // tm
    ext = tm + hb

    def body(up_ref, prev_ref, next_ref, cw_ref, cb_ref, dh_ref, dhn_ref, wd_ref, dup_ref, dcw_ref, dcb_ref):
        i = pl.program_id(1)
        first = i == 0
        last = i == nblk - 1
        dh_rows = jnp.concatenate([dh_ref[...], jnp.where(last, 0.0, dhn_ref[...])], axis=0)
        da = _dot(dh_rows.astype(BF16), wd_ref[0], NT)
        u, taps = [], []
        for half in range(2):
            rows = jnp.concatenate([jnp.where(first, 0.0, prev_ref[half, 0]), up_ref[half, 0],
                                    jnp.where(last, 0.0, next_ref[half, 0])], axis=0)
            x, xm1, xm2 = rows[hb:], pltpu.roll(rows, 1, 0)[hb:], pltpu.roll(rows, 2, 0)[hb:]
            u.append(cb_ref[half, 0] + cw_ref[half, 0, 0:1, :] * xm2 + cw_ref[half, 0, 1:2, :] * xm1
                     + cw_ref[half, 0, 2:3, :] * x)
            taps.append((xm2, xm1, x))
        gate, val = u
        sig = 1.0 / (1.0 + jnp.exp(-gate))
        dus = (da * val * (sig * (1.0 + gate * (1.0 - sig))), da * (gate * sig))
        sums = []
        for half in range(2):
            du = dus[half]
            dup_ref[half, 0] = (cw_ref[half, 0, 2:3, :] * du[:tm] + cw_ref[half, 0, 1:2, :] * pltpu.roll(du, ext - 1, 0)[:tm]
                                + cw_ref[half, 0, 0:1, :] * pltpu.roll(du, ext - 2, 0)[:tm]).astype(BF16)
            sums.append([jnp.sum(du[:tm] * t[:tm], axis=0, keepdims=True) for t in taps[half]]
                        + [jnp.sum(du[:tm], axis=0, keepdims=True)])

        @pl.when(first)
        def _():
            for half in range(2):
                for k in range(3):
                    dcw_ref[half, 0, k:k + 1, :] = sums[half][k]
                dcb_ref[half, 0] = sums[half][3]

        @pl.when(i > 0)
        def _():
            for half in range(2):
                for k in range(3):
                    dcw_ref[half, 0, k:k + 1, :] += sums[half][k]
                dcb_ref[half, 0] += sums[half][3]

    blk, prev, cw, cb = _ffn_specs(tm, c, lp)

    def next_rows(g, i):
        return jnp.minimum((i + 1) * (tm // hb), lp // hb - 1)

    (dup4, dcw4, dcb4), rode = _call_with_ride(
        body, ride, name=name, grid=(ng, nblk),
        in_specs=[blk, prev, pl.BlockSpec((2, 1, hb, c), lambda g, i: (0, g, next_rows(g, i), 0)), cw, cb,
                  pl.BlockSpec((tm, d), lambda g, i: (i, 0)),
                  pl.BlockSpec((hb, d), lambda g, i: (next_rows(g, i), 0)),
                  pl.BlockSpec((1, c, d), lambda g, i: (g, 0, 0))],
        out_specs=[blk, cw, cb],
        out_shape=[SDS(up4.shape, BF16), SDS(cw4.shape, F32), SDS(cb4.shape, F32)],
        args=[up4, up4, up4, cw4, cb4, dh, dh, w_down4])
    return dup4, dcw4, dcb4, rode


def _mm_tile(rows):
    return _row_tile(rows, MM_ROWS_MAX)


def _mm_group(a, b, dims, out_dtype, name):
    m, k = a.shape
    ng = b.shape[0]
    n = b.shape[2] if dims == NN else b.shape[1]
    tm = _mm_tile(m)

    def body(a_ref, b_ref, o_ref):
        o_ref[0] = _dot(a_ref[...].astype(BF16), b_ref[0], dims).astype(out_dtype)

    return pl.pallas_call(
        body, name=name, grid=(ng, m // tm),
        in_specs=[pl.BlockSpec((tm, k), lambda g, i: (i, 0)),
                  pl.BlockSpec((1,) + b.shape[1:], lambda g, i: (g, 0, 0))],
        out_specs=pl.BlockSpec((1, tm, n), lambda g, i: (g, i, 0)),
        out_shape=SDS((ng, m, n), out_dtype), compiler_params=_cp(2))(a, b)


def _mm_reduce(a, b, dims, res, name, ride=None, gains=None):
    ng, m, k = a.shape
    n = b.shape[2] if dims == NN else b.shape[1]
    tm = _mm_tile(m)
    has_res = res is not None
    n_norm = 0 if gains is None else gains.shape[0]

    def body(a_ref, b_ref, *refs):
        extra, (o_ref, *n_refs, acc_ref) = refs[:has_res + (n_norm > 0)], refs[has_res + (n_norm > 0):]
        g = pl.program_id(1)
        p = _dot(a_ref[0].astype(BF16), b_ref[0], dims)

        @pl.when(g == 0)
        def _():
            acc_ref[...] = p + extra[0][...] if has_res else p

        @pl.when(g > 0)
        def _():
            acc_ref[...] += p

        @pl.when(g == ng - 1)
        def _():
            x = acc_ref[...]
            o_ref[...] = x
            if n_norm:
                u = x * _rstd(x)
                for j in range(n_norm):
                    n_refs[j][...] = (u * extra[-1][j:j + 1, :]).astype(BF16)

    row = pl.BlockSpec((tm, n), lambda i, g: (i, 0))
    (out, *norms), rode = _call_with_ride(
        body, ride, name=name, grid=(m // tm, ng),
        in_specs=[pl.BlockSpec((1, tm, k), lambda i, g: (g, i, 0)),
                  pl.BlockSpec((1,) + b.shape[1:], lambda i, g: (g, 0, 0))] + ([row] if has_res else [])
        + ([pl.BlockSpec((n_norm, n), lambda i, g: (0, 0))] if n_norm else []),
        out_specs=[row] * (1 + n_norm), out_shape=[SDS((m, n), F32)] + [SDS((m, n), BF16)] * n_norm,
        scratch_shapes=[pltpu.VMEM((tm, n), F32)],
        args=[a, b] + ([res] if has_res else []) + ([gains] if n_norm else []))
    results = (out,) + ((norms,) if n_norm else ()) + ((rode,) if ride is not None else ())
    return results[0] if len(results) == 1 else results


def _mm_tn(a, b, name, ride=None):
    ga, m, ka = a.shape
    gb, _, n = b.shape
    ng = max(ga, gb)
    tk = _mm_tile(m)
    nk = m // tk

    def body(a_ref, b_ref, o_ref, acc_ref):
        s = pl.program_id(1)
        p = _dot(a_ref[0].astype(BF16), b_ref[0].astype(BF16), TN)

        @pl.when(s == 0)
        def _():
            acc_ref[...] = p

        @pl.when(s > 0)
        def _():
            acc_ref[...] += p

        @pl.when(s == nk - 1)
        def _():
            o_ref[0] = acc_ref[...].astype(BF16)

    (out,), rode = _call_with_ride(
        body, ride, name=name, grid=(ng, nk),
        in_specs=[pl.BlockSpec((1, tk, ka), (lambda g, s: (g, s, 0)) if ga > 1 else (lambda g, s: (0, s, 0))),
                  pl.BlockSpec((1, tk, n), (lambda g, s: (g, s, 0)) if gb > 1 else (lambda g, s: (0, s, 0)))],
        out_specs=[pl.BlockSpec((1, ka, n), lambda g, s: (g, 0, 0))], out_shape=[SDS((ng, ka, n), BF16)],
        scratch_shapes=[pltpu.VMEM((ka, n), F32)], args=[a, b])
    return out if ride is None else (out, rode)


def _mm_norm_bwd(a, b, h, gain, dh, name, ride=None):
    ng, m, k = a.shape
    d = b.shape[1]
    tm = _row_tile(m, FFN_ROWS_MAX, 16)
    nblk = m // tm

    def body(a_ref, b_ref, h_ref, g_ref, dh_ref, o_ref, dg_ref, acc_ref):
        i, g = pl.program_id(0), pl.program_id(1)
        p = _dot(a_ref[0], b_ref[0], NT)

        @pl.when(g == 0)
        def _():
            acc_ref[...] = p

        @pl.when(g > 0)
        def _():
            acc_ref[...] += p

        @pl.when(g == ng - 1)
        def _():
            dn = acc_ref[...]
            x = h_ref[...]
            r = _rstd(x)
            u = x * r
            du = dn * g_ref[...]
            o_ref[...] = dh_ref[...] + r * (du - u * jnp.mean(du * u, axis=-1, keepdims=True))
            dgp = jnp.sum(dn * u, axis=0, keepdims=True)

            @pl.when(i == 0)
            def _():
                dg_ref[...] = dgp

            @pl.when(i > 0)
            def _():
                dg_ref[...] += dgp

    row = pl.BlockSpec((tm, d), lambda i, g: (i, 0))
    vec = pl.BlockSpec((1, d), lambda i, g: (0, 0))
    (out, dgain), rode = _call_with_ride(
        body, ride, name=name, grid=(nblk, ng),
        in_specs=[pl.BlockSpec((1, tm, k), lambda i, g: (g, i, 0)), pl.BlockSpec((1, d, k), lambda i, g: (g, 0, 0)),
                  row, vec, row],
        out_specs=[row, vec], out_shape=[SDS((m, d), F32), SDS((1, d), F32)],
        scratch_shapes=[pltpu.VMEM((tm, d), F32)], args=[a, b, h, gain, dh])
    return out, dgain, rode


def _pair_tri(kind, sign):
    r = jnp.arange(2 * ATT_BLK)[:, None]
    c = jnp.arange(2 * ATT_BLK)[None, :]
    same = (r < ATT_BLK) == (c < ATT_BLK)
    rel = {"from": r >= c, "before": r < c}[kind]
    return ((same & rel) * sign).astype(BF16)


def _scan_dot(x, tri):
    hi = x.astype(BF16)
    lo = (x - hi.astype(F32)).astype(BF16)
    return _dot(hi, tri) + _dot(lo, tri)


def _split_heads(blk, lane_a):
    zero = jnp.zeros_like(blk)
    return jnp.concatenate([jnp.where(lane_a, blk, zero), jnp.where(lane_a, zero, blk)], axis=0)


def _softplus(z):
    return jnp.maximum(z, 0.0) + jnp.log(1.0 + jnp.exp2(jnp.abs(z) * (-LOG2_E)))


def _visible(qi, j, r0):
    t = qi * ATT_Q + r0 + lax.broadcasted_iota(jnp.int32, (ATT_Q - r0, 2 * ATT_BLK), 0)
    s = j * ATT_BLK + (lax.broadcasted_iota(jnp.int32, (ATT_Q - r0, 2 * ATT_BLK), 1) & (ATT_BLK - 1))
    return s < t


def _add_rows(x, r0, y):
    return x + y if r0 == 0 else jnp.concatenate([x[:r0], x[r0:] + y], axis=0)


def _diag_rows(n):
    return n * ATT_BLK


def _halves(x):
    return x[:, :ATT_BLK], x[:, ATT_BLK:]


def _rowsum(x):
    return jnp.sum(x, axis=1, keepdims=True)


def _still_visible(ca, cb):
    return (jnp.minimum(jnp.min(ca), jnp.min(cb)) < UNDERFLOW_AT).astype(jnp.int32)


def _attn_specs(lp):
    bk = ATT_BLK
    qblk = pl.BlockSpec((ATT_Q, bk), lambda p, i: (i, p))
    kblk = pl.BlockSpec((1, lp, bk), lambda p, i: (p // 2, 0, p % 2))
    vblk = pl.BlockSpec((1, lp, bk), lambda p, i: (HEAD_PAIRS // 2 + p // 2, 0, p % 2))
    tri = pl.BlockSpec((2 * bk, 2 * bk), lambda p, i: (0, 0))
    return qblk, kblk, vblk, tri


def _attn_fwd(q, kv, name, ride=None):
    lp, d = q.shape
    bk = ATT_BLK

    def body(q_ref, k_ref, v_ref, tri_ref, o_ref):
        qi = pl.program_id(1)
        qs = q_ref[...] * (HEAD_DIM ** -0.5)
        lane_a = lax.broadcasted_iota(jnp.int32, (1, bk), 1) < HEAD_DIM
        tri = tri_ref[...]

        def trip(js, carry, masked, live=None):
            oacc, ca, cb = carry
            r0s = [_diag_rows(len(js) - 1 - n) if masked else 0 for n in range(len(js))]
            rows = [pl.ds(pl.multiple_of(j * bk, bk), bk) for j in js]
            zs = [_dot(qs[r0:], _split_heads(k_ref[0, r, :], lane_a), NT) for r0, r in zip(r0s, rows)]
            ms = [_softplus(z) for z in zs]
            seen = [_visible(qi, j, r0) if masked else None for j, r0 in zip(js, r0s)] if live is None else live
            if masked or live is not None:
                ms = [jnp.where(v, m, 0.0) for v, m in zip(seen, ms)]
            ws = [_scan_dot(m, tri) for m in ms]
            for v, r0, r, z, m, w in zip(seen, r0s, rows, zs, ms, ws):
                exa, exb = _halves(z + w)
                a = jnp.concatenate([jnp.exp(exa - ca[r0:]), jnp.exp(exb - cb[r0:])], axis=1)
                if masked or live is not None:
                    a = jnp.where(v, a, 0.0)
                oacc = _add_rows(oacc, r0, _dot(a.astype(BF16), _split_heads(v_ref[0, r, :], lane_a)))
                ma, mb = _halves(m)
                ca, cb = _add_rows(ca, r0, _rowsum(ma)), _add_rows(cb, r0, _rowsum(mb))
            return oacc, ca, cb

        carry = (jnp.zeros((ATT_Q, bk), F32), jnp.zeros((ATT_Q, 1), F32), jnp.zeros((ATT_Q, 1), F32))
        top = (qi + 1) * ATT_UNROLL - 1
        carry = trip([top - u for u in range(ATT_UNROLL)], carry, True)
        def older(st):
            g, _, *c = st
            js = [top - ATT_UNROLL - g * ATT_OLD - u for u in range(ATT_OLD)]
            c = trip([jnp.maximum(j, 0) for j in js], tuple(c), False, [j >= 0 for j in js])
            return (g + 1, _still_visible(c[1], c[2]), *c)

        _, _, oacc, _, _ = lax.while_loop(
            lambda st: (top - ATT_UNROLL - st[0] * ATT_OLD >= 0) & (st[1] > 0), older,
            (jnp.int32(0), _still_visible(carry[1], carry[2]), *carry))
        o_ref[...] = oacc.astype(BF16)

    qblk, kblk, vblk, tri = _attn_specs(lp)
    (o,), rode = _call_with_ride(
        body, ride, name=name, grid=(HEAD_PAIRS, lp // ATT_Q), in_specs=[qblk, kblk, vblk, tri],
        out_specs=[qblk], out_shape=[SDS((lp, d), BF16)], args=[q, kv, kv, _pair_tri("from", -1)])
    return o, rode


def _attn_bwd(q, kv, do, name, ride=None):
    lp, d = q.shape
    bk = ATT_BLK
    scale = HEAD_DIM ** -0.5

    def body(q_ref, k_ref, v_ref, do_ref, tri_ref, dq_ref, dk_ref, dv_ref):
        qi = pl.program_id(1)

        @pl.when(qi == 0)
        def _():
            dk_ref[...] = jnp.zeros_like(dk_ref)
            dv_ref[...] = jnp.zeros_like(dv_ref)

        qs = q_ref[...] * scale
        do_blk = do_ref[...]
        lane_a = lax.broadcasted_iota(jnp.int32, (1, bk), 1) < HEAD_DIM
        tri = tri_ref[...]

        def sums(js, carry, masked, live=None):
            ca, cb = carry
            for n, j in enumerate(js):
                r0 = _diag_rows(n) if masked else 0
                m = _softplus(_dot(qs[r0:], _split_heads(k_ref[0, pl.ds(pl.multiple_of(j * bk, bk), bk), :], lane_a), NT))
                if masked or live is not None:
                    m = jnp.where(_visible(qi, j, r0) if live is None else live[n], m, 0.0)
                ma, mb = _halves(m)
                ca, cb = _add_rows(ca, r0, _rowsum(ma)), _add_rows(cb, r0, _rowsum(mb))
            return ca, cb

        def trip(js, carry, masked, live=None):
            dq, pa, pb, ea, eb = carry
            r0s = [_diag_rows(n) if masked else 0 for n in range(len(js))]
            rows = [pl.ds(pl.multiple_of(j * bk, bk), bk) for j in js]
            kks = [_split_heads(k_ref[0, r, :], lane_a) for r in rows]
            zs = [_dot(qs[r0:], kk, NT) for r0, kk in zip(r0s, kks)]
            das = [_dot(do_blk[r0:], _split_heads(v_ref[0, r, :], lane_a), NT) for r0, r in zip(r0s, rows)]
            ms = [_softplus(z) for z in zs]
            seen = [_visible(qi, j, r0) if masked else None for j, r0 in zip(js, r0s)] if live is None else live
            if masked or live is not None:
                ms = [jnp.where(v, m, 0.0) for v, m in zip(seen, ms)]
            xs = [_scan_dot(m, tri) for m in ms]
            es, a_bf = [], []
            for v, r0, z, m, x, da in zip(seen, r0s, zs, ms, xs, das):
                xa, xb = _halves(z + x)
                a = jnp.concatenate([jnp.exp(xa + pa[r0:]), jnp.exp(xb + pb[r0:])], axis=1)
                if masked or live is not None:
                    a = jnp.where(v, a, 0.0)
                a_bf.append(a.astype(BF16))
                es.append(a * da)
                ma, mb = _halves(m)
                pa, pb = _add_rows(pa, r0, _rowsum(ma)), _add_rows(pb, r0, _rowsum(mb))
            ss = [_dot(e.astype(BF16), tri) for e in es]
            for v, r0, r, kk, z, m, e, s, ab in zip(seen, r0s, rows, kks, zs, ms, es, ss, a_bf):
                sa, sb = _halves(s)
                e_before = jnp.concatenate([sa + ea[r0:], sb + eb[r0:]], axis=1)
                dz = e - jnp.exp(z - m) * (e + e_before)
                if masked or live is not None:
                    dz = jnp.where(v, dz, 0.0)
                dzb = dz.astype(BF16)
                dq = _add_rows(dq, r0, _dot(dzb, kk))
                rk = _dot(dzb, qs[r0:], TN)
                rv = _dot(ab, do_blk[r0:], TN)
                dk_ref[0, r, :] += jnp.where(lane_a, rk[:bk], rk[bk:])
                dv_ref[0, r, :] += jnp.where(lane_a, rv[:bk], rv[bk:])
                e_a, e_b = _halves(e)
                ea, eb = _add_rows(ea, r0, _rowsum(e_a)), _add_rows(eb, r0, _rowsum(e_b))
            return dq, pa, pb, ea, eb

        zcol = jnp.zeros((ATT_Q, 1), F32)
        diag = [qi * ATT_UNROLL + u for u in range(ATT_UNROLL)]
        newest_old = qi * ATT_UNROLL - 1

        def old_blocks(g):
            js = [newest_old - g * ATT_OLD - (ATT_OLD - 1 - u) for u in range(ATT_OLD)]
            return [jnp.maximum(j, 0) for j in js], [j >= 0 for j in js]

        def older(st):
            g, _, *c = st
            js, live = old_blocks(g)
            c = sums(js, tuple(c), False, live)
            return (g + 1, _still_visible(*c), *c)

        seen = sums(diag, (zcol, zcol), True)
        n_old, _, ta, tb = lax.while_loop(
            lambda st: (newest_old - st[0] * ATT_OLD >= 0) & (st[1] > 0), older,
            (jnp.int32(0), _still_visible(*seen), *seen))
        carry = lax.fori_loop(
            0, n_old, lambda g, c: trip(old_blocks(n_old - 1 - g)[0], c, False, old_blocks(n_old - 1 - g)[1]),
            (jnp.zeros((ATT_Q, bk), F32), -ta, -tb, zcol, zcol))
        carry = trip(diag, carry, True)
        dq_ref[...] = (carry[0] * scale).astype(BF16)

    qblk, kblk, vblk, tri = _attn_specs(lp)
    (dq, dk, dv), rode = _call_with_ride(
        body, ride, name=name, grid=(HEAD_PAIRS, lp // ATT_Q), in_specs=[qblk, kblk, vblk, qblk, tri],
        out_specs=[qblk, kblk, kblk],
        out_shape=[SDS((lp, d), BF16), SDS((HEAD_PAIRS // 2, lp, 2 * bk), F32), SDS((HEAD_PAIRS // 2, lp, 2 * bk), F32)],
        args=[q, kv, kv, do, _pair_tri("before", 1)])
    return dq, dk, dv, rode


def _mesh_pos():
    return lax.axis_index("x"), lax.axis_index("y"), lax.axis_index("c")


def _flip(pos, r):
    x, y, c = pos
    return (1 - x if r & 4 else x, 1 - y if r & 2 else y, 1 - c if r & 1 else c)


def _dev_index(pos):
    return 4 * pos[0] + 2 * pos[1] + pos[2]


class _Ride(NamedTuple):
    kind: str
    arrays: list


def _ride_arrays(ride):
    return [] if ride is None else ride.arrays


def _ride_args(ride):
    if ride is None:
        return [], [], [], []
    n = len(ride.arrays)
    hbm = pl.BlockSpec(memory_space=pl.ANY)
    shapes = [SDS(x.shape if ride.kind == "scatter" else (N_DEV,) + x.shape, x.dtype) for x in ride.arrays]
    sems = [pltpu.SemaphoreType.DMA((7 * n,)), pltpu.SemaphoreType.DMA((7 * n,)), pltpu.SemaphoreType.DMA((n,))]
    return [hbm] * n, [hbm] * n, shapes, sems


def _riding(body, n_in, n_out, ride, first, middle, last):
    if ride is None:
        return body
    n = len(ride.arrays)

    def wrapped(*refs):
        ins, srcs = refs[:n_in], refs[n_in:n_in + n]
        outs, dsts = refs[n_in + n:n_in + n + n_out], refs[n_in + n + n_out:n_in + 2 * n + n_out]
        scratch, (send_sems, recv_sems, local_sems) = refs[n_in + 2 * n + n_out:-3], refs[-3:]
        me = _mesh_pos()
        mi = _dev_index(me)

        def copy(a, k, src, dst, to):
            return pltpu.make_async_remote_copy(
                src_ref=src, dst_ref=dst, send_sem=send_sems.at[7 * a + k], recv_sem=recv_sems.at[7 * a + k],
                device_id=to, device_id_type=pl.DeviceIdType.MESH)

        local, sends, lands, arrived, passed = [], [], [], [], []
        for a in range(n):
            if ride.kind == "gather_by_chip":
                sibling, others = _flip(me, 1), [_flip(me, 4), _flip(me, 2), _flip(me, 6)]
                local.append(pltpu.make_async_copy(srcs[a], dsts[a].at[mi], local_sems.at[a]))
                sends.append(copy(a, 0, srcs[a], dsts[a].at[mi], sibling))
                lands.append(copy(a, 0, dsts[a].at[_dev_index(sibling)], dsts[a].at[_dev_index(sibling)], me))
                for j, o in enumerate(others):
                    oi, si = _dev_index(o), _dev_index(_flip(o, 1))
                    sends.append(copy(a, 1 + j, srcs[a], dsts[a].at[mi], o))
                    arrived.append(copy(a, 1 + j, dsts[a].at[oi], dsts[a].at[oi], me))
                    passed.append(copy(a, 4 + j, dsts[a].at[oi], dsts[a].at[oi], sibling))
                    lands.append(copy(a, 4 + j, dsts[a].at[si], dsts[a].at[si], me))
                continue
            gather = ride.kind == "gather"
            local.append(pltpu.make_async_copy(srcs[a] if gather else srcs[a].at[mi], dsts[a].at[mi], local_sems.at[a]))
            for r in range(1, N_DEV):
                peer = _flip(me, r)
                pi = _dev_index(peer)
                sends.append(copy(a, r - 1, srcs[a] if gather else srcs[a].at[pi], dsts[a].at[mi], peer))
                lands.append(copy(a, r - 1, dsts[a].at[pi], dsts[a].at[pi], peer))

        @pl.when(first())
        def _():
            for cp in local + sends:
                cp.start()

        if passed:
            @pl.when(middle())
            def _():
                for got, on in zip(arrived, passed):
                    got.wait_recv()
                    on.start()

        body(*ins, *outs, *scratch)

        @pl.when(last())
        def _():
            for cp in lands:
                cp.wait_recv()
            for cp in sends + passed:
                cp.wait_send()
            for cp in local:
                cp.wait()

    return wrapped


def _call_with_ride(body, ride, *, name, grid, in_specs, out_specs, out_shape, args, scratch_shapes=()):
    ride_in, ride_out, ride_shape, ride_sems = _ride_args(ride)
    axes = range(len(grid))
    assert ride is None or ride.kind != "gather_by_chip" or grid[0] >= 4, grid

    def at(step):
        return lambda: functools.reduce(lambda p, k: p & (pl.program_id(k) == step[k]), axes, True)

    ends = [(0,) * len(grid), (3 * grid[0] // 4,) + (0,) * (len(grid) - 1), tuple(g - 1 for g in grid)]
    out = pl.pallas_call(
        _riding(body, len(in_specs), len(out_specs), ride, *map(at, ends)), name=name, grid=grid,
        in_specs=list(in_specs) + ride_in, out_specs=list(out_specs) + ride_out,
        out_shape=list(out_shape) + ride_shape, scratch_shapes=list(scratch_shapes) + ride_sems,
        compiler_params=_cp(len(grid)))(*args, *_ride_arrays(ride))
    return out[:len(out_specs)], out[len(out_specs):]


def _all_gather(xs, name):
    n = len(xs)

    def body(*refs):
        x_refs, out_refs = refs[:n], refs[n:2 * n]
        send_sems, recv_sems, local_sems = refs[2 * n:]
        me = _mesh_pos()
        sibling = _flip(me, 1)
        others = [_flip(me, 4), _flip(me, 2), _flip(me, 6)]

        def copy(a, k, block, to, own=False):
            slab = out_refs[a].at[_dev_index(block)]
            return pltpu.make_async_remote_copy(
                src_ref=x_refs[a] if own else slab, dst_ref=slab,
                send_sem=send_sems.at[7 * a + k], recv_sem=recv_sems.at[7 * a + k],
                device_id=to, device_id_type=pl.DeviceIdType.MESH)

        mine = [pltpu.make_async_copy(x_refs[a], out_refs[a].at[_dev_index(me)], local_sems.at[a]) for a in range(n)]
        first = []
        for a in range(n):
            mine[a].start()
            first += [copy(a, 0, me, sibling, own=True)] + [copy(a, 1 + j, me, o, own=True) for j, o in enumerate(others)]
        for cp in first:
            cp.start()
        passed = []
        for a in range(n):
            for j, o in enumerate(others):
                copy(a, 1 + j, o, me).wait_recv()
                passed.append(copy(a, 4 + j, o, sibling))
                passed[-1].start()
        for a in range(n):
            copy(a, 0, sibling, me).wait_recv()
            for j, o in enumerate(others):
                copy(a, 4 + j, _flip(o, 1), me).wait_recv()
        for cp in first + passed:
            cp.wait_send()
        for cp in mine:
            cp.wait()

    hbm = pl.BlockSpec(memory_space=pl.ANY)
    return pl.pallas_call(
        body, name=name, out_shape=[SDS((N_DEV,) + x.shape, x.dtype) for x in xs],
        in_specs=[hbm] * n, out_specs=[hbm] * n,
        scratch_shapes=[pltpu.SemaphoreType.DMA((7 * n,)), pltpu.SemaphoreType.DMA((7 * n,)), pltpu.SemaphoreType.DMA((n,))],
    )(*xs)


def _sum_slabs(a, name, ride=None):
    n, rows, cols = a.shape
    tr = rows if a.size * a.dtype.itemsize <= SUM_WHOLE_BYTES else _row_tile(rows, SUM_ROWS_MAX, 16)

    def body(a_ref, o_ref):
        acc = a_ref[0].astype(F32)
        for k in range(1, n):
            acc = acc + a_ref[k].astype(F32)
        o_ref[...] = acc

    (out,), rode = _call_with_ride(
        body, ride, name=name, grid=(rows // tr,),
        in_specs=[pl.BlockSpec((n, tr, cols), lambda i: (0, i, 0))],
        out_specs=[pl.BlockSpec((tr, cols), lambda i: (i, 0))], out_shape=[SDS((rows, cols), F32)], args=[a])
    return out if ride is None else (out, rode)


def _adamw(w, g, m, v, name):
    rows, cols = w.shape
    tr = _row_tile(rows, 352)

    def body(w_ref, g_ref, m_ref, v_ref, d_ref, mo_ref, vo_ref):
        g_ = g_ref[...]
        m_ = ADAM_B1 * m_ref[...] + (1.0 - ADAM_B1) * g_
        v_ = ADAM_B2 * v_ref[...] + (1.0 - ADAM_B2) * (g_ * g_)
        m_hat = m_ / (1.0 - ADAM_B1 ** ADAM_STEP)
        v_hat = v_ / (1.0 - ADAM_B2 ** ADAM_STEP)
        d_ref[...] = -ADAM_LR * (m_hat / (jnp.sqrt(v_hat) + ADAM_EPS) + ADAM_WD * w_ref[...])
        mo_ref[...] = m_
        vo_ref[...] = v_

    blk = pl.BlockSpec((tr, cols), lambda i: (i, 0))
    return pl.pallas_call(
        body, name=name, grid=(rows // tr,),
        in_specs=[blk] * 4, out_specs=[blk] * 3, out_shape=[SDS((rows, cols), F32)] * 3,
        compiler_params=_cp(1))(w, g, m, v)


def _ffn_bwd(h, gain, w_up, cw4, cb4, w_down4, saved, dh, tag, ride_wup=None, scatter_own=False):
    n2, up4, act = saved
    d_w_down = _mm_tn(act, dh[None], f"ffn_dwdown_{tag}")
    ride_gate = _Ride("scatter", [d_w_down.reshape(N_DEV, -1, d_w_down.shape[-1])]) if scatter_own else None
    dup4, dcw4, dcb4, rode = _ffn_act_bwd(up4, cw4, cb4, dh, w_down4, f"ffn_dgate_{tag}", ride_gate)
    if scatter_own:
        (d_w_down,) = rode
    dup = dup4.reshape((8,) + dup4.shape[2:])
    d_w_up, rode_wup = _mm_tn(n2[None], dup, f"ffn_dwup_{tag}", ride_wup), []
    if ride_wup is not None:
        d_w_up, rode_wup = d_w_up
    dh_in, dgain, rode = _mm_norm_bwd(dup, w_up, h, gain, dh, f"ffn_dnorm_{tag}",
                                      _Ride("scatter", [d_w_up]) if scatter_own else None)
    if scatter_own:
        (d_w_up,) = rode
    return dh_in, dgain, d_w_up, d_w_down, dcw4, dcb4, rode_wup


def kernel(x, meta_tokens, mix_norm, ffn_norm, pool_w, pool_scale, kv_norm, w_kv, w_q, w_o, ffn_w_up, ffn_conv_w, ffn_conv_b, ffn_w_down, final_norm, loss_target, m_meta_tokens, m_mix_norm, m_ffn_norm, m_pool_w, m_pool_scale, m_kv_norm, m_w_kv, m_w_q, m_w_o, m_ffn_w_up, m_ffn_conv_w, m_ffn_conv_b, m_ffn_w_down, m_final_norm, v_meta_tokens, v_mix_norm, v_ffn_norm, v_pool_w, v_pool_scale, v_kv_norm, v_w_kv, v_w_q, v_w_o, v_ffn_w_up, v_ffn_conv_w, v_ffn_conv_b, v_ffn_w_down, v_final_norm):
    seq, d = x.shape[1], x.shape[2]
    n_tok = N_META + seq
    lp = -(-n_tok // ROW_TILE) * ROW_TILE
    fc = ffn_w_up.shape[2]
    me = _dev_index(_mesh_pos())

    def rows_of(parts):
        rows = [p.size // d for p in parts]
        return [sum(rows[:k]) for k in range(len(parts) + 1)]

    def bf16_rows(parts):
        return jnp.concatenate([p.reshape(-1, d) for p in parts], axis=0).astype(BF16)

    small_parts = [meta_tokens, pool_scale, ffn_conv_w]
    small_rows = [p.size // 128 for p in small_parts]
    small_pad = -sum(small_rows) % 8
    local_small = jnp.concatenate([p.reshape(-1, 128) for p in small_parts] + [jnp.zeros((small_pad, 128), F32)], axis=0)
    g_pw, wup0, gs = _all_gather([bf16_rows([pool_w]), ffn_w_up[0].astype(BF16), local_small], "gather_first")
    pw = g_pw.reshape(N_DEV, 4, POOL_C // N_DEV, POOL_C).transpose(1, 0, 2, 3).reshape(4, POOL_C, POOL_C)
    early_parts, late_parts = [ffn_w_down[0], w_kv], [w_o, ffn_w_down[1]]
    early_off, late_off = rows_of(early_parts), rows_of(late_parts)
    r0, r1, r2 = small_rows[0], small_rows[0] + small_rows[1], sum(small_rows)
    meta_full = gs[:, :r0].transpose(1, 0, 2).reshape(N_META, d)
    pscale = gs[:, r0:r1].reshape(1, d)
    cw = gs[:, r1:r2].reshape(N_DEV, 2, 3, fc)
    cw4_l = [cw[:, l].reshape(2, 4, 3, fc) for l in range(2)]
    cb4_l = [ffn_conv_b[l].reshape(2, 4, 1, fc) for l in range(2)]

    h0 = jnp.concatenate([meta_full, x[0], jnp.zeros((lp - n_tok, d), F32)], axis=0)
    h1, diff = _pool_fwd(h0, mix_norm[0:1], pw, pscale, "pool_fwd")
    (n2_0,) = _rms_fwd(h1, ffn_norm[0:1], "ffn_norm_0")
    up4_0, act0, (g_early,) = _ffn_up_act(n2_0, wup0.reshape(2, 4, d, fc), cw4_l[0], cb4_l[0], "ffn_up_0",
                                          _Ride("gather_by_chip", [bf16_rows(early_parts)]))
    wdn0 = g_early[:, early_off[0]:early_off[1]].reshape(4, fc, d)
    wkv = g_early[:, early_off[1]:early_off[2]].reshape(N_DEV, d, 2 * d // N_DEV)
    gains_b = jnp.stack([kv_norm, mix_norm[1]], axis=0)
    h2, (kvn, n3), (wq,) = _mm_reduce(act0, wdn0, NN, h1, "ffn_down_0", _Ride("gather", [w_q[0].astype(BF16)]), gains_b)
    wq = wq.reshape(1, d, d)
    kv = _mm_group(kvn, wkv, NN, BF16, "kv_proj")
    q = _mm_group(n3, wq, NN, BF16, "q_proj")[0]
    o, (g_late, wup1) = _attn_fwd(
        q, kv, "attn_fwd", _Ride("gather_by_chip", [bf16_rows(late_parts), ffn_w_up[1].astype(BF16)]))
    wo = g_late[:, late_off[0]:late_off[1]].reshape(1, d, d)
    wdn1 = g_late[:, late_off[1]:late_off[2]].reshape(4, fc, d)
    h3, (n2_1,) = _mm_reduce(o[None], wo, NN, h2, "o_proj", gains=ffn_norm[1:2])
    up4_1, act1, _ = _ffn_up_act(n2_1, wup1.reshape(2, 4, d, fc), cw4_l[1], cb4_l[1], "ffn_up_1")
    h4 = _mm_reduce(act1, wdn1, NN, h3, "ffn_down_1")
    target = jnp.pad(loss_target[0], ((N_META, lp - n_tok), (0, 0)))
    dh4, loss_blk, dg_final = _loss_bwd(h4, final_norm[None], target, seq, "loss")
    loss = lax.psum(loss_blk[0, 0], MESH_AXES)

    dh3, dg_ffn1, d_wup1, d_wdn1, dcw4_1, dcb4_1, _ = _ffn_bwd(
        h3, ffn_norm[1:2], wup1, cw4_l[1], cb4_l[1], wdn1, (n2_1, up4_1, act1), dh4, "1")
    d_o = _mm_group(dh3, wo, NT, BF16, "o_proj_dx")[0]
    d_wo = _mm_tn(o[None], dh3[None], "o_proj_dw")
    ride_late = _Ride("scatter", [jnp.concatenate([d_wo.reshape(N_DEV, -1, d), d_wdn1.reshape(N_DEV, -1, d)], axis=1), d_wup1])
    dq, dk, dv, (p_late, p_up1) = _attn_bwd(q, kv, d_o, "attn_bwd", ride_late)
    dn3 = _mm_group(dq, wq, NT, F32, "q_proj_dx")[0]
    d_wq = _mm_tn(n3[None], dq[None], "q_proj_dw")
    dkv = jnp.concatenate([dk, dv], axis=0).astype(BF16)
    dkvn = _mm_reduce(dkv, wkv, NT, None, "kv_proj_dx")
    d_wkv = _mm_tn(kvn[None], dkv, "kv_proj_dw")
    dh2, dg_b = _rms_bwd(h2, gains_b, [dkvn, dn3], dh3, "attn_norms_bwd")
    ride_proj = _Ride("scatter", [jnp.concatenate([d_wkv.reshape(N_DEV, -1, d), d_wq.reshape(N_DEV, -1, d)], axis=1)])

    dh1, dg_ffn0, p_up0, p_dn0, dcw4_0, dcb4_0, (p_proj,) = _ffn_bwd(
        h1, ffn_norm[0:1], wup0, cw4_l[0], cb4_l[0], wdn0, (n2_0, up4_0, act0), dh2, "0", ride_proj, scatter_own=True)
    dh0, d_pw, d_pscale, dg_mix0 = _pool_bwd(h0, mix_norm[0:1], pw, pscale, diff, dh1, "pool_bwd")
    grad_x = dh0[N_META:n_tok][None]
    d_pw8 = d_pw.reshape(4, N_DEV, POOL_C // N_DEV, POOL_C).transpose(1, 0, 2, 3).reshape(N_DEV, -1, d).astype(BF16)
    s_up0, (p_pw,) = _sum_slabs(p_up0, "sum_up0", _Ride("scatter", [d_pw8]))
    s_late, s_up1, s_proj, s_dn0, s_pw = [_sum_slabs(p, "sum_" + n) for p, n in (
        (p_late, "late"), (p_up1, "up1"), (p_proj, "proj"), (p_dn0, "down0"), (p_pw, "pool"))]
    n_kv, n_o = w_kv.size // d, w_o.size // d

    rep_parts = [jnp.concatenate([dg_mix0, dg_b[1:2]], axis=0), jnp.concatenate([dg_ffn0, dg_ffn1], axis=0),
                 dg_b[0:1], dg_final, jnp.stack([dcb4_0.reshape(-1), dcb4_1.reshape(-1)], axis=0)]
    rep_shapes = [mix_norm.shape, ffn_norm.shape, kv_norm.shape, final_norm.shape, ffn_conv_b.shape]
    rep_rows = [p.size // 128 for p in rep_parts]
    d_meta8 = dh0[:N_META].reshape(N_META, N_DEV, d // N_DEV).transpose(1, 0, 2).reshape(N_DEV, -1, 128)
    d_cw8 = jnp.stack([dcw4_0.reshape(N_DEV, 3, fc), dcw4_1.reshape(N_DEV, 3, fc)], axis=1).reshape(N_DEV, -1, 128)
    shard_parts = jnp.concatenate([d_meta8, d_pscale.reshape(N_DEV, 1, 128), d_cw8], axis=1)
    n_rep = sum(rep_rows)
    partial_small = jnp.concatenate([p.reshape(-1, 128) for p in rep_parts] + [shard_parts.reshape(-1, 128)], axis=0)
    g_small = _sum_slabs(_all_gather([partial_small], "gather_vector_grads")[0], "sum_vectors")
    g_rep = [g_small[sum(rep_rows[:k]):sum(rep_rows[:k + 1])].reshape(s) for k, s in enumerate(rep_shapes)]
    g_shard = lax.dynamic_index_in_dim(g_small[n_rep:].reshape(N_DEV, -1, 128), me, 0, keepdims=False)
    g_meta = g_shard[:r0].reshape(meta_tokens.shape)
    g_pscale = g_shard[r0:r1].reshape(pool_scale.shape)
    g_cw = g_shard[r1:r2].reshape(ffn_conv_w.shape)

    grads = {
        "meta_tokens": g_meta, "mix_norm": g_rep[0], "ffn_norm": g_rep[1],
        "pool_w": s_pw.reshape(pool_w.shape), "pool_scale": g_pscale, "kv_norm": g_rep[2],
        "w_kv": s_proj[:n_kv].reshape(w_kv.shape), "w_q": s_proj[n_kv:].reshape(w_q.shape),
        "w_o": s_late[:n_o].reshape(w_o.shape),
        "ffn_w_up": jnp.stack([s_up0, s_up1], axis=0), "ffn_conv_w": g_cw, "ffn_conv_b": g_rep[4],
        "ffn_w_down": jnp.stack([s_dn0, s_late[n_o:]], axis=0), "final_norm": g_rep[3],
    }
    names = list(grads)
    weights = dict(zip(names, [meta_tokens, mix_norm, ffn_norm, pool_w, pool_scale, kv_norm, w_kv, w_q, w_o,
                               ffn_w_up, ffn_conv_w, ffn_conv_b, ffn_w_down, final_norm]))
    mom1 = dict(zip(names, [m_meta_tokens, m_mix_norm, m_ffn_norm, m_pool_w, m_pool_scale, m_kv_norm, m_w_kv, m_w_q,
                            m_w_o, m_ffn_w_up, m_ffn_conv_w, m_ffn_conv_b, m_ffn_w_down, m_final_norm]))
    mom2 = dict(zip(names, [v_meta_tokens, v_mix_norm, v_ffn_norm, v_pool_w, v_pool_scale, v_kv_norm, v_w_kv, v_w_q,
                            v_w_o, v_ffn_w_up, v_ffn_conv_w, v_ffn_conv_b, v_ffn_w_down, v_final_norm]))

    delta, new_m, new_v = {}, {}, {}
    for n in names:
        shape = weights[n].shape
        flat = (-1, shape[-1])
        dl, nm, nv = _adamw(weights[n].reshape(flat), grads[n].reshape(flat), mom1[n].reshape(flat),
                            mom2[n].reshape(flat), "adamw_" + n)
        delta[n], new_m[n], new_v[n] = dl.reshape(shape), nm.reshape(shape), nv.reshape(shape)
    return (loss, grad_x, *[grads[n] for n in names], *[delta[n] for n in names],
            *[new_m[n] for n in names], *[new_v[n] for n in names])
```

```python
import functools
from typing import NamedTuple

import jax
import jax.numpy as jnp
from jax import lax
from jax.experimental import pallas as pl
from jax.experimental.pallas import tpu as pltpu

F32 = jnp.float32
BF16 = jnp.bfloat16
SDS = jax.ShapeDtypeStruct

N_DEV = 8
N_META = 16
HEAD_DIM = 64
HEAD_PAIRS = 8
RMS_EPS = 1e-6
LOG2_E = 1.4426950408889634
POOL_WINDOWS = (2, 4, 8, 16)
POOL_C = 256
POOL_HALO = 16
CONV_HALO = 8
ROW_TILE = 384
MM_ROWS_MAX = 1408
FFN_ROWS_MAX = 704
SUM_ROWS_MAX = 256
SUM_WHOLE_BYTES = 4 << 20
ATT_BLK = 128
ATT_Q = ROW_TILE
ATT_UNROLL = ATT_Q // ATT_BLK
ATT_OLD = 2
UNDERFLOW_AT = 104.0
VMEM_LIMIT = 56 * 1024 * 1024

ADAM_LR = 0.001
ADAM_B1 = 0.9
ADAM_B2 = 0.999
ADAM_EPS = 1e-08
ADAM_WD = 0.01
ADAM_STEP = 10

MESH_AXES = ("x", "y", "c")
NN = (((1,), (0,)), ((), ()))
NT = (((1,), (1,)), ((), ()))
TN = (((0,), (0,)), ((), ()))


def _cp(n_axes):
    return pltpu.CompilerParams(dimension_semantics=("arbitrary",) * n_axes, vmem_limit_bytes=VMEM_LIMIT)


def _dot(a, b, dims=NN):
    return lax.dot_general(a, b, dims, preferred_element_type=F32)


def _rstd(x):
    return lax.rsqrt(jnp.mean(x * x, axis=-1, keepdims=True) + RMS_EPS)


def _row_tile(rows, cap=512, mult=8):
    if rows <= cap:
        return rows
    best = mult
    for t in range(mult, cap + 1, mult):
        if rows % t == 0:
            best = t
    assert rows % best == 0
    return best


def _rms_fwd(h, gains, name):
    lp, d = h.shape
    k = gains.shape[0]
    tm = ROW_TILE

    def body(h_ref, g_ref, *o_refs):
        x = h_ref[...]
        u = x * _rstd(x)
        for j in range(k):
            o_refs[j][...] = (u * g_ref[j:j + 1, :]).astype(BF16)

    row = pl.BlockSpec((tm, d), lambda i: (i, 0))
    return pl.pallas_call(
        body, name=name, grid=(lp // tm,),
        in_specs=[row, pl.BlockSpec((k, d), lambda i: (0, 0))],
        out_specs=[row] * k, out_shape=[SDS((lp, d), BF16)] * k,
        compiler_params=_cp(1))(h, gains)


def _rms_bwd(h, gains, dns, dh_in, name):
    lp, d = h.shape
    k = gains.shape[0]
    tm = ROW_TILE

    def body(h_ref, g_ref, *refs):
        dn_refs, dh_ref, dho_ref, dg_ref = refs[:k], refs[k], refs[k + 1], refs[k + 2]
        i = pl.program_id(0)
        x = h_ref[...]
        r = _rstd(x)
        u = x * r
        du = jnp.zeros_like(x)
        rows = []
        for j in range(k):
            dn = dn_refs[j][...]
            du = du + dn * g_ref[j:j + 1, :]
            rows.append(jnp.sum(dn * u, axis=0, keepdims=True))
        dx = r * (du - u * jnp.mean(du * u, axis=-1, keepdims=True))
        dho_ref[...] = dh_ref[...] + dx

        @pl.when(i == 0)
        def _():
            for j in range(k):
                dg_ref[j:j + 1, :] = rows[j]

        @pl.when(i > 0)
        def _():
            for j in range(k):
                dg_ref[j:j + 1, :] += rows[j]

    row = pl.BlockSpec((tm, d), lambda i: (i, 0))
    vec = pl.BlockSpec((k, d), lambda i: (0, 0))
    return pl.pallas_call(
        body, name=name, grid=(lp // tm,),
        in_specs=[row, vec] + [row] * k + [row],
        out_specs=[row, vec], out_shape=[SDS((lp, d), F32), SDS((k, d), F32)],
        compiler_params=_cp(1))(h, gains, *dns, dh_in)


def _loss_bwd(h, gain, target, n_real, name):
    lp, d = h.shape
    tm = ROW_TILE

    def body(h_ref, g_ref, t_ref, dh_ref, loss_ref, dg_ref):
        i = pl.program_id(0)
        x = h_ref[...]
        g = g_ref[...]
        r = _rstd(x)
        u = x * r
        row = i * tm + lax.broadcasted_iota(jnp.int32, (tm, 1), 0)
        valid = (row >= N_META) & (row < N_META + n_real)
        e = jnp.where(valid, u * g - t_ref[...], 0.0)
        part = 0.5 * jnp.sum(jnp.sum(e * e, axis=-1, keepdims=True), axis=0, keepdims=True) * (1.0 / d)
        dy = e * (1.0 / d)
        du = dy * g
        dh_ref[...] = r * (du - u * jnp.mean(du * u, axis=-1, keepdims=True))
        dgp = jnp.sum(dy * u, axis=0, keepdims=True)

        @pl.when(i == 0)
        def _():
            loss_ref[...] = jnp.broadcast_to(part, (8, 128))
            dg_ref[...] = dgp

        @pl.when(i > 0)
        def _():
            loss_ref[...] += jnp.broadcast_to(part, (8, 128))
            dg_ref[...] += dgp

    row = pl.BlockSpec((tm, d), lambda i: (i, 0))
    vec = pl.BlockSpec((1, d), lambda i: (0, 0))
    return pl.pallas_call(
        body, name=name, grid=(lp // tm,),
        in_specs=[row, vec, row],
        out_specs=[row, pl.BlockSpec((8, 128), lambda i: (0, 0)), vec],
        out_shape=[SDS((lp, d), F32), SDS((8, 128), F32), SDS((1, d), F32)],
        compiler_params=_cp(1))(h, gain, target)


def _pool_fwd(h, gain, w, scale, name, ride=None):
    lp, d = h.shape
    tm = ROW_TILE
    hb = POOL_HALO

    def body(h_ref, halo_ref, g_ref, w_ref, s_ref, h1_ref, diff_ref):
        i = pl.program_id(0)
        g = g_ref[...]
        x = h_ref[...]
        n = x * _rstd(x) * g
        xh = halo_ref[...]
        nh = jnp.where(i > 0, xh * _rstd(xh) * g, 0.0)
        cur = jnp.concatenate([nh, n], axis=0)
        pos = i * tm + lax.broadcasted_iota(jnp.int32, (tm, 1), 0)
        for gi, win in enumerate(POOL_WINDOWS):
            if gi > 0:
                cur = cur[:, POOL_C:]
            cur = cur + pltpu.roll(cur, win // 2, 0)
            c0 = gi * POOL_C
            count = jnp.minimum(pos + 1, win).astype(F32)
            diff = cur[hb:, :POOL_C] / count - n[:, c0:c0 + POOL_C]
            diff = diff.astype(BF16)
            y = _dot(diff, w_ref[gi])
            h1_ref[:, c0:c0 + POOL_C] = x[:, c0:c0 + POOL_C] + y * s_ref[:, c0:c0 + POOL_C]
            diff_ref[:, c0:c0 + POOL_C] = diff

    row = pl.BlockSpec((tm, d), lambda i: (i, 0))
    halo = pl.BlockSpec((hb, d), lambda i: (jnp.maximum(i * (tm // hb) - 1, 0), 0))
    vec = pl.BlockSpec((1, d), lambda i: (0, 0))
    (h1, diff), rode = _call_with_ride(
        body, ride, name=name, grid=(lp // tm,),
        in_specs=[row, halo, vec, pl.BlockSpec(w.shape, lambda i: (0, 0, 0)), vec],
        out_specs=[row, row], out_shape=[SDS((lp, d), F32), SDS((lp, d), BF16)], args=[h, h, gain, w, scale])
    return h1, diff, rode


def _pool_bwd(h, gain, w, scale, diff, dh1, name):
    lp, d = h.shape
    tm = ROW_TILE
    hb = POOL_HALO
    nblk = lp // tm
    ext = tm + hb

    def body(h_ref, g_ref, w_ref, s_ref, diff_ref, dh_ref, dhn_ref, dh0_ref, dw_ref, ds_ref, dg_ref):
        i = pl.program_id(0)
        g = g_ref[...]
        x = h_ref[...]
        r = _rstd(x)
        u = x * r
        dh = dh_ref[...]
        dhn = jnp.where(i < nblk - 1, dhn_ref[...], 0.0)
        dyp = jnp.concatenate([dh, dhn], axis=0) * s_ref[...]
        pos = i * tm + lax.broadcasted_iota(jnp.int32, (ext, 1), 0)
        dn_parts, dw_parts, ds_parts = [], [], []
        for gi, win in enumerate(POOL_WINDOWS):
            c0 = gi * POOL_C
            wg = w_ref[gi]
            dyp_g = dyp[:, c0:c0 + POOL_C].astype(BF16)
            dd = _dot(dyp_g, wg, NT)
            dfg = diff_ref[:, c0:c0 + POOL_C]
            dw_parts.append(_dot(dfg, dyp_g[:tm], TN))
            ds_parts.append(jnp.sum(dh[:, c0:c0 + POOL_C] * _dot(dfg, wg), axis=0, keepdims=True))
            count = jnp.minimum(pos + 1, win).astype(F32)
            cur = dd / count
            sh = 1
            while sh < win:
                cur = cur + pltpu.roll(cur, ext - sh, 0)
                sh *= 2
            dn_parts.append(cur[:tm] - dd[:tm])
        dn = jnp.concatenate(dn_parts, axis=1)
        du = dn * g
        dh0_ref[...] = dh + r * (du - u * jnp.mean(du * u, axis=-1, keepdims=True))
        dgp = jnp.sum(dn * u, axis=0, keepdims=True)
        dsp = jnp.concatenate(ds_parts, axis=1)

        @pl.when(i == 0)
        def _():
            for gi in range(len(POOL_WINDOWS)):
                dw_ref[gi] = dw_parts[gi]
            ds_ref[...] = dsp
            dg_ref[...] = dgp

        @pl.when(i > 0)
        def _():
            for gi in range(len(POOL_WINDOWS)):
                dw_ref[gi] += dw_parts[gi]
            ds_ref[...] += dsp
            dg_ref[...] += dgp

    row = pl.BlockSpec((tm, d), lambda i: (i, 0))
    nxt = pl.BlockSpec((hb, d), lambda i: (jnp.minimum((i + 1) * (tm // hb), lp // hb - 1), 0))
    vec = pl.BlockSpec((1, d), lambda i: (0, 0))
    wsp = pl.BlockSpec(w.shape, lambda i: (0, 0, 0))
    return pl.pallas_call(
        body, name=name, grid=(nblk,),
        in_specs=[row, vec, wsp, vec, row, row, nxt],
        out_specs=[row, wsp, vec, vec],
        out_shape=[SDS((lp, d), F32), SDS(w.shape, F32), SDS((1, d), F32), SDS((1, d), F32)],
        compiler_params=_cp(1))(h, gain, w, scale, diff, dh1, dh1)


def _ffn_specs(tm, c, lp):
    blk = pl.BlockSpec((2, 1, tm, c), lambda g, i: (0, g, i, 0))
    halo = pl.BlockSpec((2, 1, CONV_HALO, c), lambda g, i: (0, g, jnp.maximum(i * (tm // CONV_HALO) - 1, 0), 0))
    cw = pl.BlockSpec((2, 1, 3, c), lambda g, i: (0, g, 0, 0))
    cb = pl.BlockSpec((2, 1, 1, c), lambda g, i: (0, g, 0, 0))
    return blk, halo, cw, cb


def _ffn_up_act(n2, w_up4, cw4, cb4, name, ride=None):
    lp, d = n2.shape
    _, ng, _, c = w_up4.shape
    tm = _row_tile(lp, FFN_ROWS_MAX, 16)
    hb = CONV_HALO

    def body(a_ref, w_ref, cw_ref, cb_ref, up_ref, act_ref, tail_ref):
        @pl.when(pl.program_id(1) == 0)
        def _():
            tail_ref[...] = jnp.zeros_like(tail_ref)

        a = a_ref[...]
        u = []
        for half in range(2):
            x = _dot(a, w_ref[half, 0])
            up_ref[half, 0] = x
            rows = jnp.concatenate([tail_ref[half], x], axis=0)
            u.append(cb_ref[half, 0] + cw_ref[half, 0, 0:1, :] * pltpu.roll(rows, 2, 0)[hb:]
                     + cw_ref[half, 0, 1:2, :] * pltpu.roll(rows, 1, 0)[hb:] + cw_ref[half, 0, 2:3, :] * x)
            tail_ref[half] = x[tm - hb:]
        gate, val = u
        sig = 1.0 / (1.0 + jnp.exp(-gate))
        act_ref[0] = (gate * sig * val).astype(BF16)

    blk, _, cw, cb = _ffn_specs(tm, c, lp)
    (up4, act), rode = _call_with_ride(
        body, ride, name=name, grid=(ng, lp // tm),
        in_specs=[pl.BlockSpec((tm, d), lambda g, i: (i, 0)), pl.BlockSpec((2, 1, d, c), lambda g, i: (0, g, 0, 0)), cw, cb],
        out_specs=[blk, pl.BlockSpec((1, tm, c), lambda g, i: (g, i, 0))],
        out_shape=[SDS((2, ng, lp, c), F32), SDS((ng, lp, c), BF16)],
        scratch_shapes=[pltpu.VMEM((2, hb, c), F32)], args=[n2, w_up4, cw4, cb4])
    return up4, act, rode


def _ffn_act_bwd(up4, cw4, cb4, dh, w_down4, name, ride=None):
    _, ng, lp, c = up4.shape
    d = dh.shape[1]
    tm = ROW_TILE
    hb = CONV_HALO
    nblk = lp // tm
    ext = tm + hb

    def body(up_ref, prev_ref, next_ref, cw_ref, cb_ref, dh_ref, dhn_ref, wd_ref, dup_ref, dcw_ref, dcb_ref):
        i = pl.program_id(1)
        first = i == 0
        last = i == nblk - 1
        dh_rows = jnp.concatenate([dh_ref[...], jnp.where(last, 0.0, dhn_ref[...])], axis=0)
        da = _dot(dh_rows.astype(BF16), wd_ref[0], NT)
        u, taps = [], []
        for half in range(2):
            rows = jnp.concatenate([jnp.where(first, 0.0, prev_ref[half, 0]), up_ref[half, 0],
                                    jnp.where(last, 0.0, next_ref[half, 0])], axis=0)
            x, xm1, xm2 = rows[hb:], pltpu.roll(rows, 1, 0)[hb:], pltpu.roll(rows, 2, 0)[hb:]
            u.append(cb_ref[half, 0] + cw_ref[half, 0, 0:1, :] * xm2 + cw_ref[half, 0, 1:2, :] * xm1
                     + cw_ref[half, 0, 2:3, :] * x)
            taps.append((xm2, xm1, x))
        gate, val = u
        sig = 1.0 / (1.0 + jnp.exp(-gate))
        dus = (da * val * (sig * (1.0 + gate * (1.0 - sig))), da * (gate * sig))
        sums = []
        for half in range(2):
            du = dus[half]
            dup_ref[half, 0] = (cw_ref[half, 0, 2:3, :] * du[:tm] + cw_ref[half, 0, 1:2, :] * pltpu.roll(du, ext - 1, 0)[:tm]
                                + cw_ref[half, 0, 0:1, :] * pltpu.roll(du, ext - 2, 0)[:tm]).astype(BF16)
            sums.append([jnp.sum(du[:tm] * t[:tm], axis=0, keepdims=True) for t in taps[half]]
                        + [jnp.sum(du[:tm], axis=0, keepdims=True)])

        @pl.when(first)
        def _():
            for half in range(2):
                for k in range(3):
                    dcw_ref[half, 0, k:k + 1, :] = sums[half][k]
                dcb_ref[half, 0] = sums[half][3]

        @pl.when(i > 0)
        def _():
            for half in range(2):
                for k in range(3):
                    dcw_ref[half, 0, k:k + 1, :] += sums[half][k]
                dcb_ref[half, 0] += sums[half][3]

    blk, prev, cw, cb = _ffn_specs(tm, c, lp)

    def next_rows(g, i):
        return jnp.minimum((i + 1) * (tm // hb), lp // hb - 1)

    (dup4, dcw4, dcb4), rode = _call_with_ride(
        body, ride, name=name, grid=(ng, nblk),
        in_specs=[blk, prev, pl.BlockSpec((2, 1, hb, c), lambda g, i: (0, g, next_rows(g, i), 0)), cw, cb,
                  pl.BlockSpec((tm, d), lambda g, i: (i, 0)),
                  pl.BlockSpec((hb, d), lambda g, i: (next_rows(g, i), 0)),
                  pl.BlockSpec((1, c, d), lambda g, i: (g, 0, 0))],
        out_specs=[blk, cw, cb],
        out_shape=[SDS(up4.shape, BF16), SDS(cw4.shape, F32), SDS(cb4.shape, F32)],
        args=[up4, up4, up4, cw4, cb4, dh, dh, w_down4])
    return dup4, dcw4, dcb4, rode


def _mm_tile(rows):
    return _row_tile(rows, MM_ROWS_MAX)


def _mm_group(a, b, dims, out_dtype, name):
    m, k = a.shape
    ng = b.shape[0]
    n = b.shape[2] if dims == NN else b.shape[1]
    tm = _mm_tile(m)

    def body(a_ref, b_ref, o_ref):
        o_ref[0] = _dot(a_ref[...].astype(BF16), b_ref[0], dims).astype(out_dtype)

    return pl.pallas_call(
        body, name=name, grid=(ng, m // tm),
        in_specs=[pl.BlockSpec((tm, k), lambda g, i: (i, 0)),
                  pl.BlockSpec((1,) + b.shape[1:], lambda g, i: (g, 0, 0))],
        out_specs=pl.BlockSpec((1, tm, n), lambda g, i: (g, i, 0)),
        out_shape=SDS((ng, m, n), out_dtype), compiler_params=_cp(2))(a, b)


def _mm_reduce(a, b, dims, res, name, ride=None, gains=None):
    ng, m, k = a.shape
    n = b.shape[2] if dims == NN else b.shape[1]
    tm = _mm_tile(m)
    has_res = res is not None
    n_norm = 0 if gains is None else gains.shape[0]

    def body(a_ref, b_ref, *refs):
        extra, (o_ref, *n_refs, acc_ref) = refs[:has_res + (n_norm > 0)], refs[has_res + (n_norm > 0):]
        g = pl.program_id(1)
        p = _dot(a_ref[0].astype(BF16), b_ref[0], dims)

        @pl.when(g == 0)
        def _():
            acc_ref[...] = p + extra[0][...] if has_res else p

        @pl.when(g > 0)
        def _():
            acc_ref[...] += p

        @pl.when(g == ng - 1)
        def _():
            x = acc_ref[...]
            o_ref[...] = x
            if n_norm:
                u = x * _rstd(x)
                for j in range(n_norm):
                    n_refs[j][...] = (u * extra[-1][j:j + 1, :]).astype(BF16)

    row = pl.BlockSpec((tm, n), lambda i, g: (i, 0))
    (out, *norms), rode = _call_with_ride(
        body, ride, name=name, grid=(m // tm, ng),
        in_specs=[pl.BlockSpec((1, tm, k), lambda i, g: (g, i, 0)),
                  pl.BlockSpec((1,) + b.shape[1:], lambda i, g: (g, 0, 0))] + ([row] if has_res else [])
        + ([pl.BlockSpec((n_norm, n), lambda i, g: (0, 0))] if n_norm else []),
        out_specs=[row] * (1 + n_norm), out_shape=[SDS((m, n), F32)] + [SDS((m, n), BF16)] * n_norm,
        scratch_shapes=[pltpu.VMEM((tm, n), F32)],
        args=[a, b] + ([res] if has_res else []) + ([gains] if n_norm else []))
    results = (out,) + ((norms,) if n_norm else ()) + ((rode,) if ride is not None else ())
    return results[0] if len(results) == 1 else results


def _mm_tn(a, b, name, ride=None):
    ga, m, ka = a.shape
    gb, _, n = b.shape
    ng = max(ga, gb)
    tk = _mm_tile(m)
    nk = m // tk

    def body(a_ref, b_ref, o_ref, acc_ref):
        s = pl.program_id(1)
        p = _dot(a_ref[0].astype(BF16), b_ref[0].astype(BF16), TN)

        @pl.when(s == 0)
        def _():
            acc_ref[...] = p

        @pl.when(s > 0)
        def _():
            acc_ref[...] += p

        @pl.when(s == nk - 1)
        def _():
            o_ref[0] = acc_ref[...].astype(BF16)

    (out,), rode = _call_with_ride(
        body, ride, name=name, grid=(ng, nk),
        in_specs=[pl.BlockSpec((1, tk, ka), (lambda g, s: (g, s, 0)) if ga > 1 else (lambda g, s: (0, s, 0))),
                  pl.BlockSpec((1, tk, n), (lambda g, s: (g, s, 0)) if gb > 1 else (lambda g, s: (0, s, 0)))],
        out_specs=[pl.BlockSpec((1, ka, n), lambda g, s: (g, 0, 0))], out_shape=[SDS((ng, ka, n), BF16)],
        scratch_shapes=[pltpu.VMEM((ka, n), F32)], args=[a, b])
    return out if ride is None else (out, rode)


def _mm_norm_bwd(a, b, h, gain, dh, name, ride=None):
    ng, m, k = a.shape
    d = b.shape[1]
    tm = _row_tile(m, FFN_ROWS_MAX, 16)
    nblk = m // tm

    def body(a_ref, b_ref, h_ref, g_ref, dh_ref, o_ref, dg_ref, acc_ref):
        i, g = pl.program_id(0), pl.program_id(1)
        p = _dot(a_ref[0], b_ref[0], NT)

        @pl.when(g == 0)
        def _():
            acc_ref[...] = p

        @pl.when(g > 0)
        def _():
            acc_ref[...] += p

        @pl.when(g == ng - 1)
        def _():
            dn = acc_ref[...]
            x = h_ref[...]
            r = _rstd(x)
            u = x * r
            du = dn * g_ref[...]
            o_ref[...] = dh_ref[...] + r * (du - u * jnp.mean(du * u, axis=-1, keepdims=True))
            dgp = jnp.sum(dn * u, axis=0, keepdims=True)

            @pl.when(i == 0)
            def _():
                dg_ref[...] = dgp

            @pl.when(i > 0)
            def _():
                dg_ref[...] += dgp

    row = pl.BlockSpec((tm, d), lambda i, g: (i, 0))
    vec = pl.BlockSpec((1, d), lambda i, g: (0, 0))
    (out, dgain), rode = _call_with_ride(
        body, ride, name=name, grid=(nblk, ng),
        in_specs=[pl.BlockSpec((1, tm, k), lambda i, g: (g, i, 0)), pl.BlockSpec((1, d, k), lambda i, g: (g, 0, 0)),
                  row, vec, row],
        out_specs=[row, vec], out_shape=[SDS((m, d), F32), SDS((1, d), F32)],
        scratch_shapes=[pltpu.VMEM((tm, d), F32)], args=[a, b, h, gain, dh])
    return out, dgain, rode


def _pair_tri(kind, sign):
    r = jnp.arange(2 * ATT_BLK)[:, None]
    c = jnp.arange(2 * ATT_BLK)[None, :]
    same = (r < ATT_BLK) == (c < ATT_BLK)
    rel = {"from": r >= c, "before": r < c}[kind]
    return ((same & rel) * sign).astype(BF16)


def _scan_dot(x, tri):
    hi = x.astype(BF16)
    lo = (x - hi.astype(F32)).astype(BF16)
    return _dot(hi, tri) + _dot(lo, tri)


def _split_heads(blk, lane_a):
    zero = jnp.zeros_like(blk)
    return jnp.concatenate([jnp.where(lane_a, blk, zero), jnp.where(lane_a, zero, blk)], axis=0)


def _softplus(z):
    return jnp.maximum(z, 0.0) + jnp.log(1.0 + jnp.exp2(jnp.abs(z) * (-LOG2_E)))


def _visible(qi, j, r0):
    t = qi * ATT_Q + r0 + lax.broadcasted_iota(jnp.int32, (ATT_Q - r0, 2 * ATT_BLK), 0)
    s = j * ATT_BLK + (lax.broadcasted_iota(jnp.int32, (ATT_Q - r0, 2 * ATT_BLK), 1) & (ATT_BLK - 1))
    return s < t


def _add_rows(x, r0, y):
    return x + y if r0 == 0 else jnp.concatenate([x[:r0], x[r0:] + y], axis=0)


def _diag_rows(n):
    return n * ATT_BLK


def _halves(x):
    return x[:, :ATT_BLK], x[:, ATT_BLK:]


def _rowsum(x):
    return jnp.sum(x, axis=1, keepdims=True)


def _still_visible(ca, cb):
    return (jnp.minimum(jnp.min(ca), jnp.min(cb)) < UNDERFLOW_AT).astype(jnp.int32)


def _attn_specs(lp):
    bk = ATT_BLK
    qblk = pl.BlockSpec((ATT_Q, bk), lambda p, i: (i, p))
    kblk = pl.BlockSpec((1, lp, bk), lambda p, i: (p // 2, 0, p % 2))
    vblk = pl.BlockSpec((1, lp, bk), lambda p, i: (HEAD_PAIRS // 2 + p // 2, 0, p % 2))
    tri = pl.BlockSpec((2 * bk, 2 * bk), lambda p, i: (0, 0))
    return qblk, kblk, vblk, tri


def _attn_fwd(q, kv, name, ride=None):
    lp, d = q.shape
    bk = ATT_BLK

    def body(q_ref, k_ref, v_ref, tri_ref, o_ref):
        qi = pl.program_id(1)
        qs = q_ref[...] * (HEAD_DIM ** -0.5)
        lane_a = lax.broadcasted_iota(jnp.int32, (1, bk), 1) < HEAD_DIM
        tri = tri_ref[...]

        def trip(js, carry, masked, live=None):
            oacc, ca, cb = carry
            r0s = [_diag_rows(len(js) - 1 - n) if masked else 0 for n in range(len(js))]
            rows = [pl.ds(pl.multiple_of(j * bk, bk), bk) for j in js]
            zs = [_dot(qs[r0:], _split_heads(k_ref[0, r, :], lane_a), NT) for r0, r in zip(r0s, rows)]
            ms = [_softplus(z) for z in zs]
            seen = [_visible(qi, j, r0) if masked else None for j, r0 in zip(js, r0s)] if live is None else live
            if masked or live is not None:
                ms = [jnp.where(v, m, 0.0) for v, m in zip(seen, ms)]
            ws = [_scan_dot(m, tri) for m in ms]
            for v, r0, r, z, m, w in zip(seen, r0s, rows, zs, ms, ws):
                exa, exb = _halves(z + w)
                a = jnp.concatenate([jnp.exp(exa - ca[r0:]), jnp.exp(exb - cb[r0:])], axis=1)
                if masked or live is not None:
                    a = jnp.where(v, a, 0.0)
                oacc = _add_rows(oacc, r0, _dot(a.astype(BF16), _split_heads(v_ref[0, r, :], lane_a)))
                ma, mb = _halves(m)
                ca, cb = _add_rows(ca, r0, _rowsum(ma)), _add_rows(cb, r0, _rowsum(mb))
            return oacc, ca, cb

        carry = (jnp.zeros((ATT_Q, bk), F32), jnp.zeros((ATT_Q, 1), F32), jnp.zeros((ATT_Q, 1), F32))
        top = (qi + 1) * ATT_UNROLL - 1
        carry = trip([top - u for u in range(ATT_UNROLL)], carry, True)
        def older(st):
            g, _, *c = st
            js = [top - ATT_UNROLL - g * ATT_OLD - u for u in range(ATT_OLD)]
            c = trip([jnp.maximum(j, 0) for j in js], tuple(c), False, [j >= 0 for j in js])
            return (g + 1, _still_visible(c[1], c[2]), *c)

        _, _, oacc, _, _ = lax.while_loop(
            lambda st: (top - ATT_UNROLL - st[0] * ATT_OLD >= 0) & (st[1] > 0), older,
            (jnp.int32(0), _still_visible(carry[1], carry[2]), *carry))
        o_ref[...] = oacc.astype(BF16)

    qblk, kblk, vblk, tri = _attn_specs(lp)
    (o,), rode = _call_with_ride(
        body, ride, name=name, grid=(HEAD_PAIRS, lp // ATT_Q), in_specs=[qblk, kblk, vblk, tri],
        out_specs=[qblk], out_shape=[SDS((lp, d), BF16)], args=[q, kv, kv, _pair_tri("from", -1)])
    return o, rode


def _attn_bwd(q, kv, do, name, ride=None):
    lp, d = q.shape
    bk = ATT_BLK
    scale = HEAD_DIM ** -0.5

    def body(q_ref, k_ref, v_ref, do_ref, tri_ref, dq_ref, dk_ref, dv_ref):
        qi = pl.program_id(1)

        @pl.when(qi == 0)
        def _():
            dk_ref[...] = jnp.zeros_like(dk_ref)
            dv_ref[...] = jnp.zeros_like(dv_ref)

        qs = q_ref[...] * scale
        do_blk = do_ref[...]
        lane_a = lax.broadcasted_iota(jnp.int32, (1, bk), 1) < HEAD_DIM
        tri = tri_ref[...]

        def sums(js, carry, masked, live=None):
            ca, cb = carry
            for n, j in enumerate(js):
                r0 = _diag_rows(n) if masked else 0
                m = _softplus(_dot(qs[r0:], _split_heads(k_ref[0, pl.ds(pl.multiple_of(j * bk, bk), bk), :], lane_a), NT))
                if masked or live is not None:
                    m = jnp.where(_visible(qi, j, r0) if live is None else live[n], m, 0.0)
                ma, mb = _halves(m)
                ca, cb = _add_rows(ca, r0, _rowsum(ma)), _add_rows(cb, r0, _rowsum(mb))
            return ca, cb

        def trip(js, carry, masked, live=None):
            dq, pa, pb, ea, eb = carry
            r0s = [_diag_rows(n) if masked else 0 for n in range(len(js))]
            rows = [pl.ds(pl.multiple_of(j * bk, bk), bk) for j in js]
            kks = [_split_heads(k_ref[0, r, :], lane_a) for r in rows]
            zs = [_dot(qs[r0:], kk, NT) for r0, kk in zip(r0s, kks)]
            das = [_dot(do_blk[r0:], _split_heads(v_ref[0, r, :], lane_a), NT) for r0, r in zip(r0s, rows)]
            ms = [_softplus(z) for z in zs]
            seen = [_visible(qi, j, r0) if masked else None for j, r0 in zip(js, r0s)] if live is None else live
            if masked or live is not None:
                ms = [jnp.where(v, m, 0.0) for v, m in zip(seen, ms)]
            xs = [_scan_dot(m, tri) for m in ms]
            es, a_bf = [], []
            for v, r0, z, m, x, da in zip(seen, r0s, zs, ms, xs, das):
                xa, xb = _halves(z + x)
                a = jnp.concatenate([jnp.exp(xa + pa[r0:]), jnp.exp(xb + pb[r0:])], axis=1)
                if masked or live is not None:
                    a = jnp.where(v, a, 0.0)
                a_bf.append(a.astype(BF16))
                es.append(a * da)
                ma, mb = _halves(m)
                pa, pb = _add_rows(pa, r0, _rowsum(ma)), _add_rows(pb, r0, _rowsum(mb))
            ss = [_dot(e.astype(BF16), tri) for e in es]
            for v, r0, r, kk, z, m, e, s, ab in zip(seen, r0s, rows, kks, zs, ms, es, ss, a_bf):
                sa, sb = _halves(s)
                e_before = jnp.concatenate([sa + ea[r0:], sb + eb[r0:]], axis=1)
                dz = e - jnp.exp(z - m) * (e + e_before)
                if masked or live is not None:
                    dz = jnp.where(v, dz, 0.0)
                dzb = dz.astype(BF16)
                dq = _add_rows(dq, r0, _dot(dzb, kk))
                rk = _dot(dzb, qs[r0:], TN)
                rv = _dot(ab, do_blk[r0:], TN)
                dk_ref[0, r, :] += jnp.where(lane_a, rk[:bk], rk[bk:])
                dv_ref[0, r, :] += jnp.where(lane_a, rv[:bk], rv[bk:])
                e_a, e_b = _halves(e)
                ea, eb = _add_rows(ea, r0, _rowsum(e_a)), _add_rows(eb, r0, _rowsum(e_b))
            return dq, pa, pb, ea, eb

        zcol = jnp.zeros((ATT_Q, 1), F32)
        diag = [qi * ATT_UNROLL + u for u in range(ATT_UNROLL)]
        newest_old = qi * ATT_UNROLL - 1

        def old_blocks(g):
            js = [newest_old - g * ATT_OLD - (ATT_OLD - 1 - u) for u in range(ATT_OLD)]
            return [jnp.maximum(j, 0) for j in js], [j >= 0 for j in js]

        def older(st):
            g, _, *c = st
            js, live = old_blocks(g)
            c = sums(js, tuple(c), False, live)
            return (g + 1, _still_visible(*c), *c)

        seen = sums(diag, (zcol, zcol), True)
        n_old, _, ta, tb = lax.while_loop(
            lambda st: (newest_old - st[0] * ATT_OLD >= 0) & (st[1] > 0), older,
            (jnp.int32(0), _still_visible(*seen), *seen))
        carry = lax.fori_loop(
            0, n_old, lambda g, c: trip(old_blocks(n_old - 1 - g)[0], c, False, old_blocks(n_old - 1 - g)[1]),
            (jnp.zeros((ATT_Q, bk), F32), -ta, -tb, zcol, zcol))
        carry = trip(diag, carry, True)
        dq_ref[...] = (carry[0] * scale).astype(BF16)

    qblk, kblk, vblk, tri = _attn_specs(lp)
    (dq, dk, dv), rode = _call_with_ride(
        body, ride, name=name, grid=(HEAD_PAIRS, lp // ATT_Q), in_specs=[qblk, kblk, vblk, qblk, tri],
        out_specs=[qblk, kblk, kblk],
        out_shape=[SDS((lp, d), BF16), SDS((HEAD_PAIRS // 2, lp, 2 * bk), F32), SDS((HEAD_PAIRS // 2, lp, 2 * bk), F32)],
        args=[q, kv, kv, do, _pair_tri("before", 1)])
    return dq, dk, dv, rode


def _mesh_pos():
    return lax.axis_index("x"), lax.axis_index("y"), lax.axis_index("c")


def _flip(pos, r):
    x, y, c = pos
    return (1 - x if r & 4 else x, 1 - y if r & 2 else y, 1 - c if r & 1 else c)


def _dev_index(pos):
    return 4 * pos[0] + 2 * pos[1] + pos[2]


class _Ride(NamedTuple):
    kind: str
    arrays: list


def _ride_arrays(ride):
    return [] if ride is None else ride.arrays


def _ride_args(ride):
    if ride is None:
        return [], [], [], []
    n = len(ride.arrays)
    hbm = pl.BlockSpec(memory_space=pl.ANY)
    shapes = [SDS(x.shape if ride.kind == "scatter" else (N_DEV,) + x.shape, x.dtype) for x in ride.arrays]
    sems = [pltpu.SemaphoreType.DMA((7 * n,)), pltpu.SemaphoreType.DMA((7 * n,)), pltpu.SemaphoreType.DMA((n,))]
    return [hbm] * n, [hbm] * n, shapes, sems


def _riding(body, n_in, n_out, ride, first, middle, last):
    if ride is None:
        return body
    n = len(ride.arrays)

    def wrapped(*refs):
        ins, srcs = refs[:n_in], refs[n_in:n_in + n]
        outs, dsts = refs[n_in + n:n_in + n + n_out], refs[n_in + n + n_out:n_in + 2 * n + n_out]
        scratch, (send_sems, recv_sems, local_sems) = refs[n_in + 2 * n + n_out:-3], refs[-3:]
        me = _mesh_pos()
        mi = _dev_index(me)

        def copy(a, k, src, dst, to):
            return pltpu.make_async_remote_copy(
                src_ref=src, dst_ref=dst, send_sem=send_sems.at[7 * a + k], recv_sem=recv_sems.at[7 * a + k],
                device_id=to, device_id_type=pl.DeviceIdType.MESH)

        local, sends, lands, arrived, passed = [], [], [], [], []
        for a in range(n):
            if ride.kind == "gather_by_chip":
                sibling, others = _flip(me, 1), [_flip(me, 4), _flip(me, 2), _flip(me, 6)]
                local.append(pltpu.make_async_copy(srcs[a], dsts[a].at[mi], local_sems.at[a]))
                sends.append(copy(a, 0, srcs[a], dsts[a].at[mi], sibling))
                lands.append(copy(a, 0, dsts[a].at[_dev_index(sibling)], dsts[a].at[_dev_index(sibling)], me))
                for j, o in enumerate(others):
                    oi, si = _dev_index(o), _dev_index(_flip(o, 1))
                    sends.append(copy(a, 1 + j, srcs[a], dsts[a].at[mi], o))
                    arrived.append(copy(a, 1 + j, dsts[a].at[oi], dsts[a].at[oi], me))
                    passed.append(copy(a, 4 + j, dsts[a].at[oi], dsts[a].at[oi], sibling))
                    lands.append(copy(a, 4 + j, dsts[a].at[si], dsts[a].at[si], me))
                continue
            gather = ride.kind == "gather"
            local.append(pltpu.make_async_copy(srcs[a] if gather else srcs[a].at[mi], dsts[a].at[mi], local_sems.at[a]))
            for r in range(1, N_DEV):
                peer = _flip(me, r)
                pi = _dev_index(peer)
                sends.append(copy(a, r - 1, srcs[a] if gather else srcs[a].at[pi], dsts[a].at[mi], peer))
                lands.append(copy(a, r - 1, dsts[a].at[pi], dsts[a].at[pi], peer))

        @pl.when(first())
        def _():
            for cp in local + sends:
                cp.start()

        if passed:
            @pl.when(middle())
            def _():
                for got, on in zip(arrived, passed):
                    got.wait_recv()
                    on.start()

        body(*ins, *outs, *scratch)

        @pl.when(last())
        def _():
            for cp in lands:
                cp.wait_recv()
            for cp in sends + passed:
                cp.wait_send()
            for cp in local:
                cp.wait()

    return wrapped


def _call_with_ride(body, ride, *, name, grid, in_specs, out_specs, out_shape, args, scratch_shapes=()):
    ride_in, ride_out, ride_shape, ride_sems = _ride_args(ride)
    axes = range(len(grid))
    assert ride is None or ride.kind != "gather_by_chip" or grid[0] >= 4, grid

    def at(step):
        return lambda: functools.reduce(lambda p, k: p & (pl.program_id(k) == step[k]), axes, True)

    ends = [(0,) * len(grid), (3 * grid[0] // 4,) + (0,) * (len(grid) - 1), tuple(g - 1 for g in grid)]
    out = pl.pallas_call(
        _riding(body, len(in_specs), len(out_specs), ride, *map(at, ends)), name=name, grid=grid,
        in_specs=list(in_specs) + ride_in, out_specs=list(out_specs) + ride_out,
        out_shape=list(out_shape) + ride_shape, scratch_shapes=list(scratch_shapes) + ride_sems,
        compiler_params=_cp(len(grid)))(*args, *_ride_arrays(ride))
    return out[:len(out_specs)], out[len(out_specs):]


def _all_gather(xs, name):
    n = len(xs)

    def body(*refs):
        x_refs, out_refs = refs[:n], refs[n:2 * n]
        send_sems, recv_sems, local_sems = refs[2 * n:]
        me = _mesh_pos()
        sibling = _flip(me, 1)
        others = [_flip(me, 4), _flip(me, 2), _flip(me, 6)]

        def copy(a, k, block, to, own=False):
            slab = out_refs[a].at[_dev_index(block)]
            return pltpu.make_async_remote_copy(
                src_ref=x_refs[a] if own else slab, dst_ref=slab,
                send_sem=send_sems.at[7 * a + k], recv_sem=recv_sems.at[7 * a + k],
                device_id=to, device_id_type=pl.DeviceIdType.MESH)

        mine = [pltpu.make_async_copy(x_refs[a], out_refs[a].at[_dev_index(me)], local_sems.at[a]) for a in range(n)]
        first = []
        for a in range(n):
            mine[a].start()
            first += [copy(a, 0, me, sibling, own=True)] + [copy(a, 1 + j, me, o, own=True) for j, o in enumerate(others)]
        for cp in first:
            cp.start()
        passed = []
        for a in range(n):
            for j, o in enumerate(others):
                copy(a, 1 + j, o, me).wait_recv()
                passed.append(copy(a, 4 + j, o, sibling))
                passed[-1].start()
        for a in range(n):
            copy(a, 0, sibling, me).wait_recv()
            for j, o in enumerate(others):
                copy(a, 4 + j, _flip(o, 1), me).wait_recv()
        for cp in first + passed:
            cp.wait_send()
        for cp in mine:
            cp.wait()

    hbm = pl.BlockSpec(memory_space=pl.ANY)
    return pl.pallas_call(
        body, name=name, out_shape=[SDS((N_DEV,) + x.shape, x.dtype) for x in xs],
        in_specs=[hbm] * n, out_specs=[hbm] * n,
        scratch_shapes=[pltpu.SemaphoreType.DMA((7 * n,)), pltpu.SemaphoreType.DMA((7 * n,)), pltpu.SemaphoreType.DMA((n,))],
    )(*xs)


def _sum_slabs(a, name, ride=None):
    n, rows, cols = a.shape
    tr = rows if a.size * a.dtype.itemsize <= SUM_WHOLE_BYTES else _row_tile(rows, SUM_ROWS_MAX, 16)

    def body(a_ref, o_ref):
        acc = a_ref[0].astype(F32)
        for k in range(1, n):
            acc = acc + a_ref[k].astype(F32)
        o_ref[...] = acc

    (out,), rode = _call_with_ride(
        body, ride, name=name, grid=(rows // tr,),
        in_specs=[pl.BlockSpec((n, tr, cols), lambda i: (0, i, 0))],
        out_specs=[pl.BlockSpec((tr, cols), lambda i: (i, 0))], out_shape=[SDS((rows, cols), F32)], args=[a])
    return out if ride is None else (out, rode)


def _adamw(w, g, m, v, name):
    rows, cols = w.shape
    tr = _row_tile(rows, 352)

    def body(w_ref, g_ref, m_ref, v_ref, d_ref, mo_ref, vo_ref):
        g_ = g_ref[...]
        m_ = ADAM_B1 * m_ref[...] + (1.0 - ADAM_B1) * g_
        v_ = ADAM_B2 * v_ref[...] + (1.0 - ADAM_B2) * (g_ * g_)
        m_hat = m_ / (1.0 - ADAM_B1 ** ADAM_STEP)
        v_hat = v_ / (1.0 - ADAM_B2 ** ADAM_STEP)
        d_ref[...] = -ADAM_LR * (m_hat / (jnp.sqrt(v_hat) + ADAM_EPS) + ADAM_WD * w_ref[...])
        mo_ref[...] = m_
        vo_ref[...] = v_

    blk = pl.BlockSpec((tr, cols), lambda i: (i, 0))
    return pl.pallas_call(
        body, name=name, grid=(rows // tr,),
        in_specs=[blk] * 4, out_specs=[blk] * 3, out_shape=[SDS((rows, cols), F32)] * 3,
        compiler_params=_cp(1))(w, g, m, v)


def _ffn_bwd(h, gain, w_up, cw4, cb4, w_down4, saved, dh, tag, ride_wup=None, scatter_own=False):
    n2, up4, act = saved
    d_w_down = _mm_tn(act, dh[None], f"ffn_dwdown_{tag}")
    ride_gate = _Ride("scatter", [d_w_down.reshape(N_DEV, -1, d_w_down.shape[-1])]) if scatter_own else None
    dup4, dcw4, dcb4, rode = _ffn_act_bwd(up4, cw4, cb4, dh, w_down4, f"ffn_dgate_{tag}", ride_gate)
    if scatter_own:
        (d_w_down,) = rode
    dup = dup4.reshape((8,) + dup4.shape[2:])
    d_w_up, rode_wup = _mm_tn(n2[None], dup, f"ffn_dwup_{tag}", ride_wup), []
    if ride_wup is not None:
        d_w_up, rode_wup = d_w_up
    dh_in, dgain, rode = _mm_norm_bwd(dup, w_up, h, gain, dh, f"ffn_dnorm_{tag}",
                                      _Ride("scatter", [d_w_up]) if scatter_own else None)
    if scatter_own:
        (d_w_up,) = rode
    return dh_in, dgain, d_w_up, d_w_down, dcw4, dcb4, rode_wup


def kernel(x, meta_tokens, mix_norm, ffn_norm, pool_w, pool_scale, kv_norm, w_kv, w_q, w_o, ffn_w_up, ffn_conv_w, ffn_conv_b, ffn_w_down, final_norm, loss_target, m_meta_tokens, m_mix_norm, m_ffn_norm, m_pool_w, m_pool_scale, m_kv_norm, m_w_kv, m_w_q, m_w_o, m_ffn_w_up, m_ffn_conv_w, m_ffn_conv_b, m_ffn_w_down, m_final_norm, v_meta_tokens, v_mix_norm, v_ffn_norm, v_pool_w, v_pool_scale, v_kv_norm, v_w_kv, v_w_q, v_w_o, v_ffn_w_up, v_ffn_conv_w, v_ffn_conv_b, v_ffn_w_down, v_final_norm):
    seq, d = x.shape[1], x.shape[2]
    n_tok = N_META + seq
    lp = -(-n_tok // ROW_TILE) * ROW_TILE
    fc = ffn_w_up.shape[2]
    me = _dev_index(_mesh_pos())

    def rows_of(parts):
        rows = [p.size // d for p in parts]
        return [sum(rows[:k]) for k in range(len(parts) + 1)]

    def bf16_rows(parts):
        return jnp.concatenate([p.reshape(-1, d) for p in parts], axis=0).astype(BF16)

    small_parts = [meta_tokens, pool_scale, ffn_conv_w]
    small_rows = [p.size // 128 for p in small_parts]
    small_pad = -sum(small_rows) % 8
    local_small = jnp.concatenate([p.reshape(-1, 128) for p in small_parts] + [jnp.zeros((small_pad, 128), F32)], axis=0)
    g_pw, gs = _all_gather([bf16_rows([pool_w]), local_small], "gather_first")
    pw = g_pw.reshape(N_DEV, 4, POOL_C // N_DEV, POOL_C).transpose(1, 0, 2, 3).reshape(4, POOL_C, POOL_C)
    early_parts, late_parts = [ffn_w_down[0], w_kv], [w_o, ffn_w_down[1]]
    early_off, late_off = rows_of(early_parts), rows_of(late_parts)
    r0, r1, r2 = small_rows[0], small_rows[0] + small_rows[1], sum(small_rows)
    meta_full = gs[:, :r0].transpose(1, 0, 2).reshape(N_META, d)
    pscale = gs[:, r0:r1].reshape(1, d)
    cw = gs[:, r1:r2].reshape(N_DEV, 2, 3, fc)
    cw4_l = [cw[:, l].reshape(2, 4, 3, fc) for l in range(2)]
    cb4_l = [ffn_conv_b[l].reshape(2, 4, 1, fc) for l in range(2)]

    h0 = jnp.concatenate([meta_full, x[0], jnp.zeros((lp - n_tok, d), F32)], axis=0)
    h1, diff, (wup0,) = _pool_fwd(h0, mix_norm[0:1], pw, pscale, "pool_fwd",
                                  _Ride("gather_by_chip", [ffn_w_up[0].astype(BF16)]))
    (n2_0,) = _rms_fwd(h1, ffn_norm[0:1], "ffn_norm_0")
    up4_0, act0, (g_early,) = _ffn_up_act(n2_0, wup0.reshape(2, 4, d, fc), cw4_l[0], cb4_l[0], "ffn_up_0",
                                          _Ride("gather_by_chip", [bf16_rows(early_parts)]))
    wdn0 = g_early[:, early_off[0]:early_off[1]].reshape(4, fc, d)
    wkv = g_early[:, early_off[1]:early_off[2]].reshape(N_DEV, d, 2 * d // N_DEV)
    gains_b = jnp.stack([kv_norm, mix_norm[1]], axis=0)
    h2, (kvn, n3), (wq,) = _mm_reduce(act0, wdn0, NN, h1, "ffn_down_0", _Ride("gather", [w_q[0].astype(BF16)]), gains_b)
    wq = wq.reshape(1, d, d)
    kv = _mm_group(kvn, wkv, NN, BF16, "kv_proj")
    q = _mm_group(n3, wq, NN, BF16, "q_proj")[0]
    o, (g_late, wup1) = _attn_fwd(
        q, kv, "attn_fwd", _Ride("gather_by_chip", [bf16_rows(late_parts), ffn_w_up[1].astype(BF16)]))
    wo = g_late[:, late_off[0]:late_off[1]].reshape(1, d, d)
    wdn1 = g_late[:, late_off[1]:late_off[2]].reshape(4, fc, d)
    h3, (n2_1,) = _mm_reduce(o[None], wo, NN, h2, "o_proj", gains=ffn_norm[1:2])
    up4_1, act1, _ = _ffn_up_act(n2_1, wup1.reshape(2, 4, d, fc), cw4_l[1], cb4_l[1], "ffn_up_1")
    h4 = _mm_reduce(act1, wdn1, NN, h3, "ffn_down_1")
    target = jnp.pad(loss_target[0], ((N_META, lp - n_tok), (0, 0)))
    dh4, loss_blk, dg_final = _loss_bwd(h4, final_norm[None], target, seq, "loss")
    loss = lax.psum(loss_blk[0, 0], MESH_AXES)

    dh3, dg_ffn1, d_wup1, d_wdn1, dcw4_1, dcb4_1, _ = _ffn_bwd(
        h3, ffn_norm[1:2], wup1, cw4_l[1], cb4_l[1], wdn1, (n2_1, up4_1, act1), dh4, "1")
    d_o = _mm_group(dh3, wo, NT, BF16, "o_proj_dx")[0]
    d_wo = _mm_tn(o[None], dh3[None], "o_proj_dw")
    ride_late = _Ride("scatter", [jnp.concatenate([d_wo.reshape(N_DEV, -1, d), d_wdn1.reshape(N_DEV, -1, d)], axis=1), d_wup1])
    dq, dk, dv, (p_late, p_up1) = _attn_bwd(q, kv, d_o, "attn_bwd", ride_late)
    dn3 = _mm_group(dq, wq, NT, F32, "q_proj_dx")[0]
    d_wq = _mm_tn(n3[None], dq[None], "q_proj_dw")
    dkv = jnp.concatenate([dk, dv], axis=0).astype(BF16)
    dkvn = _mm_reduce(dkv, wkv, NT, None, "kv_proj_dx")
    d_wkv = _mm_tn(kvn[None], dkv, "kv_proj_dw")
    dh2, dg_b = _rms_bwd(h2, gains_b, [dkvn, dn3], dh3, "attn_norms_bwd")
    ride_proj = _Ride("scatter", [jnp.concatenate([d_wkv.reshape(N_DEV, -1, d), d_wq.reshape(N_DEV, -1, d)], axis=1)])

    dh1, dg_ffn0, p_up0, p_dn0, dcw4_0, dcb4_0, (p_proj,) = _ffn_bwd(
        h1, ffn_norm[0:1], wup0, cw4_l[0], cb4_l[0], wdn0, (n2_0, up4_0, act0), dh2, "0", ride_proj, scatter_own=True)
    dh0, d_pw, d_pscale, dg_mix0 = _pool_bwd(h0, mix_norm[0:1], pw, pscale, diff, dh1, "pool_bwd")
    grad_x = dh0[N_META:n_tok][None]
    d_pw8 = d_pw.reshape(4, N_DEV, POOL_C // N_DEV, POOL_C).transpose(1, 0, 2, 3).reshape(N_DEV, -1, d).astype(BF16)
    s_up0, (p_pw,) = _sum_slabs(p_up0, "sum_up0", _Ride("scatter", [d_pw8]))
    s_late, s_up1, s_proj, s_dn0, s_pw = [_sum_slabs(p, "sum_" + n) for p, n in (
        (p_late, "late"), (p_up1, "up1"), (p_proj, "proj"), (p_dn0, "down0"), (p_pw, "pool"))]
    n_kv, n_o = w_kv.size // d, w_o.size // d

    rep_parts = [jnp.concatenate([dg_mix0, dg_b[1:2]], axis=0), jnp.concatenate([dg_ffn0, dg_ffn1], axis=0),
                 dg_b[0:1], dg_final, jnp.stack([dcb4_0.reshape(-1), dcb4_1.reshape(-1)], axis=0)]
    rep_shapes = [mix_norm.shape, ffn_norm.shape, kv_norm.shape, final_norm.shape, ffn_conv_b.shape]
    rep_rows = [p.size // 128 for p in rep_parts]
    d_meta8 = dh0[:N_META].reshape(N_META, N_DEV, d // N_DEV).transpose(1, 0, 2).reshape(N_DEV, -1, 128)
    d_cw8 = jnp.stack([dcw4_0.reshape(N_DEV, 3, fc), dcw4_1.reshape(N_DEV, 3, fc)], axis=1).reshape(N_DEV, -1, 128)
    shard_parts = jnp.concatenate([d_meta8, d_pscale.reshape(N_DEV, 1, 128), d_cw8], axis=1)
    n_rep = sum(rep_rows)
    partial_small = jnp.concatenate([p.reshape(-1, 128) for p in rep_parts] + [shard_parts.reshape(-1, 128)], axis=0)
    g_small = _sum_slabs(_all_gather([partial_small], "gather_vector_grads")[0], "sum_vectors")
    g_rep = [g_small[sum(rep_rows[:k]):sum(rep_rows[:k + 1])].reshape(s) for k, s in enumerate(rep_shapes)]
    g_shard = lax.dynamic_index_in_dim(g_small[n_rep:].reshape(N_DEV, -1, 128), me, 0, keepdims=False)
    g_meta = g_shard[:r0].reshape(meta_tokens.shape)
    g_pscale = g_shard[r0:r1].reshape(pool_scale.shape)
    g_cw = g_shard[r1:r2].reshape(ffn_conv_w.shape)

    grads = {
        "meta_tokens": g_meta, "mix_norm": g_rep[0], "ffn_norm": g_rep[1],
        "pool_w": s_pw.reshape(pool_w.shape), "pool_scale": g_pscale, "kv_norm": g_rep[2],
        "w_kv": s_proj[:n_kv].reshape(w_kv.shape), "w_q": s_proj[n_kv:].reshape(w_q.shape),
        "w_o": s_late[:n_o].reshape(w_o.shape),
        "ffn_w_up": jnp.stack([s_up0, s_up1], axis=0), "ffn_conv_w": g_cw, "ffn_conv_b": g_rep[4],
        "ffn_w_down": jnp.stack([s_dn0, s_late[n_o:]], axis=0), "final_norm": g_rep[3],
    }
    names = list(grads)
    weights = dict(zip(names, [meta_tokens, mix_norm, ffn_norm, pool_w, pool_scale, kv_norm, w_kv, w_q, w_o,
                               ffn_w_up, ffn_conv_w, ffn_conv_b, ffn_w_down, final_norm]))
    mom1 = dict(zip(names, [m_meta_tokens, m_mix_norm, m_ffn_norm, m_pool_w, m_pool_scale, m_kv_norm, m_w_kv, m_w_q,
                            m_w_o, m_ffn_w_up, m_ffn_conv_w, m_ffn_conv_b, m_ffn_w_down, m_final_norm]))
    mom2 = dict(zip(names, [v_meta_tokens, v_mix_norm, v_ffn_norm, v_pool_w, v_pool_scale, v_kv_norm, v_w_kv, v_w_q,
                            v_w_o, v_ffn_w_up, v_ffn_conv_w, v_ffn_conv_b, v_ffn_w_down, v_final_norm]))

    delta, new_m, new_v = {}, {}, {}
    for n in names:
        shape = weights[n].shape
        flat = (-1, shape[-1])
        dl, nm, nv = _adamw(weights[n].reshape(flat), grads[n].reshape(flat), mom1[n].reshape(flat),
                            mom2[n].reshape(flat), "adamw_" + n)
        delta[n], new_m[n], new_v[n] = dl.reshape(shape), nm.reshape(shape), nv.reshape(shape)
    return (loss, grad_x, *[grads[n] for n in names], *[delta[n] for n in names],
            *[new_m[n] for n in names], *[new_v[n] for n in names])
```
